```python
import math
import jax, jax.numpy as jnp
from jax import lax
import numpy as np

D_MODEL = 1024
BATCH = 32
SEQ = 2048
DEPTH = 4

CTX_LEN = 256
GRID_W = 64
HEAD_DIM = 64
N_Q_HEADS = 8
N_KV_HEADS = 2
Q_PER_KV = N_Q_HEADS // N_KV_HEADS
ATTN_WIDTH = N_Q_HEADS * HEAD_DIM
KV_WIDTH = N_KV_HEADS * HEAD_DIM
WINDOW = 128
Q_BLOCK = 128
ROPE_BASE = 10000.0
ROPE_PAIRS_PER_AXIS = HEAD_DIM // 4
CONV_WIDTH = 256
CONV_K = 3
SSM_WIDTH = 256
SSM_GROUP = 16
SSM_GROUPS = SSM_WIDTH // SSM_GROUP
SSM_STATE = 64
MIX_WIDTH = ATTN_WIDTH + CONV_WIDTH + SSM_WIDTH
IN_WIDTH = ATTN_WIDTH + 2 * KV_WIDTH + 3 * CONV_WIDTH + SSM_WIDTH
D_FF = 4 * D_MODEL
N_MOD = 6
EPS = 1e-6
NEG_INF = -1e30

kernel_name = 'hybrid_dit_parallel_groups'


def rms_norm(t, g):
    tf = t.astype(jnp.float32)
    y = tf * lax.rsqrt(jnp.mean(tf * tf, axis=-1, keepdims=True) + EPS)
    return y.astype(t.dtype) * g


def modulate(t, shift, scale):
    return t * (1 + scale) + shift


def split_in_proj(p):
    sizes = [ATTN_WIDTH, KV_WIDTH, KV_WIDTH, CONV_WIDTH, CONV_WIDTH, CONV_WIDTH]
    return jnp.split(p, list(np.cumsum(sizes)), axis=-1)


def axial_rope_tables(n_tokens):
    rows = n_tokens // GRID_W
    row = jnp.broadcast_to(jnp.arange(rows)[:, None], (rows, GRID_W)).reshape(-1)
    col = jnp.broadcast_to(jnp.arange(GRID_W)[None, :], (rows, GRID_W)).reshape(-1)
    freqs = ROPE_BASE ** (-jnp.arange(ROPE_PAIRS_PER_AXIS, dtype=jnp.float32) / ROPE_PAIRS_PER_AXIS)
    ang = jnp.concatenate([row[:, None].astype(jnp.float32) * freqs,
                           col[:, None].astype(jnp.float32) * freqs], axis=-1)
    return jnp.cos(ang), jnp.sin(ang)


def _rotate(t, c, s):
    t1, t2 = jnp.split(t, 2, axis=-1)
    return jnp.concatenate([t1 * c - t2 * s, t1 * s + t2 * c], axis=-1)


def apply_axial_rope(t, cos, sin):
    c = cos[:, None, :].astype(t.dtype)
    s = sin[:, None, :].astype(t.dtype)
    n = ROPE_PAIRS_PER_AXIS
    half = HEAD_DIM // 2
    return jnp.concatenate([_rotate(t[..., :half], c[..., :n], s[..., :n]),
                            _rotate(t[..., half:], c[..., n:], s[..., n:])], axis=-1)


def windowed_latent_attention(q, k, v, kc, vc, sink):
    bsz, n_lat = q.shape[0], q.shape[1]
    n_ctx = kc.shape[1]
    n_blocks = n_lat // Q_BLOCK
    span = Q_BLOCK + 2 * WINDOW
    scale = HEAD_DIM ** -0.5
    pad = ((0, 0), (WINDOW, WINDOW), (0, 0), (0, 0))
    kp = jnp.pad(k, pad)
    vp = jnp.pad(v, pad)
    s_ctx_all = None

    def block(i):
        start = i * Q_BLOCK
        qb = lax.dynamic_slice_in_dim(q, start, Q_BLOCK, axis=1)
        kb = lax.dynamic_slice_in_dim(kp, start, span, axis=1)
        vb = lax.dynamic_slice_in_dim(vp, start, span, axis=1)
        qpos = start + jnp.arange(Q_BLOCK)
        kpos = start - WINDOW + jnp.arange(span)
        mask = (jnp.abs(qpos[:, None] - kpos[None, :]) <= WINDOW) & (kpos >= 0) & (kpos < n_lat)
        s_lat = jnp.einsum('bqhgd,bkhd->bhgqk', qb, kb).astype(jnp.float32) * scale
        s_lat = jnp.where(mask, s_lat, NEG_INF)
        s_ctx = jnp.einsum('bqhgd,bkhd->bhgqk', qb, kc).astype(jnp.float32) * scale
        s_sink = jnp.broadcast_to(sink.astype(jnp.float32)[None, :, :, None, None],
                                  s_ctx.shape[:-1] + (1,))
        p = jax.nn.softmax(jnp.concatenate([s_lat, s_ctx, s_sink], axis=-1), axis=-1)
        p_lat = p[..., :span].astype(v.dtype)
        p_ctx = p[..., span:span + n_ctx].astype(v.dtype)
        return (jnp.einsum('bhgqk,bkhd->bqhgd', p_lat, vb)
                + jnp.einsum('bhgqk,bkhd->bqhgd', p_ctx, vc))

    o = lax.map(block, jnp.arange(n_blocks))
    return jnp.moveaxis(o, 0, 1).reshape(bsz, n_lat, ATTN_WIDTH)


def context_attention(qc, kc, vc, sink):
    bsz, n_ctx = qc.shape[0], qc.shape[1]
    s = jnp.einsum('bqhgd,bkhd->bhgqk', qc, kc).astype(jnp.float32) * (HEAD_DIM ** -0.5)
    s_sink = jnp.broadcast_to(sink.astype(jnp.float32)[None, :, :, None, None], s.shape[:-1] + (1,))
    p = jax.nn.softmax(jnp.concatenate([s, s_sink], axis=-1), axis=-1)[..., :n_ctx]
    o = jnp.einsum('bhgqk,bkhd->bqhgd', p.astype(vc.dtype), vc)
    return o.reshape(bsz, n_ctx, ATTN_WIDTH)


def centred_conv3(z, w):
    zp = jnp.pad(z, ((0, 0), (1, 1), (0, 0)))
    return zp[:, :-2] * w[0] + zp[:, 1:-1] * w[1] + zp[:, 2:] * w[2]


def diag_scan(lam_bar, drive, h0, reverse):
    if h0 is not None:
        edge = -1 if reverse else 0
        drive = drive.at[:, edge].add(lam_bar * h0)
    decay = jnp.broadcast_to(lam_bar, (1, drive.shape[1]) + lam_bar.shape)

    def combine(left, right):
        a_l, b_l = left
        a_r, b_r = right
        return a_l * a_r, a_r * b_l + b_r

    _, h = lax.associative_scan(combine, (decay, drive), reverse=reverse, axis=1)
    return h


def s5_bidirectional(u, uc, lam_re, lam_im, log_dt, b_re, b_im, c_re, c_im, d_skip, w_glu, b_glu,
                     with_ctx_out):
    f32 = jnp.float32
    lam = lax.complex(lam_re.astype(f32), lam_im.astype(f32))
    dt = jnp.exp(log_dt.astype(f32))[..., None]
    lam_bar = jnp.exp(lam * dt)
    b_bar = ((lam_bar - 1) / lam)[..., None] * lax.complex(b_re.astype(f32), b_im.astype(f32))
    c_mat = lax.complex(c_re.astype(f32), c_im.astype(f32))

    def drive(t, direction):
        tg = t.astype(f32).reshape(t.shape[0], t.shape[1], SSM_GROUPS, SSM_GROUP).astype(jnp.complex64)
        return jnp.einsum('blgi,gpi->blgp', tg, b_bar[direction])

    h_ctx_f = diag_scan(lam_bar[0], drive(uc, 0), None, False)
    h_ctx_b = diag_scan(lam_bar[1], drive(uc, 1), None, True)
    h_lat_f = diag_scan(lam_bar[0], drive(u, 0), h_ctx_f[:, -1], False)
    h_lat_b = diag_scan(lam_bar[1], drive(u, 1), h_ctx_b[:, 0], True)

    def readout(t, hf, hb):
        y = jnp.real(jnp.einsum('blgp,gip->blgi', hf, c_mat[0])
                     + jnp.einsum('blgp,gip->blgi', hb, c_mat[1])).reshape(t.shape)
        y = (y + d_skip.astype(f32) * t.astype(f32)).astype(t.dtype)
        g = jax.nn.gelu(y)
        return g * jax.nn.sigmoid(g @ w_glu + b_glu)

    out_ctx = readout(uc, h_ctx_f, h_ctx_b) if with_ctx_out else None
    return readout(u, h_lat_f, h_lat_b), out_ctx


def hybrid_mixer(h_lat, h_ctx, w_in, conv_w, sink, lam_re, lam_im, log_dt, b_re, b_im, c_re, c_im,
                 d_skip, w_glu, b_glu, w_out, cos, sin, with_ctx_out):
    bsz, n_lat, _ = h_lat.shape
    n_ctx = h_ctx.shape[1]
    q, k, v, cb, cc, cx, u = split_in_proj(h_lat @ w_in)
    qc, kc, vc, cbc, ccc, cxc, uc = split_in_proj(h_ctx @ w_in)
    sink = sink.reshape(N_KV_HEADS, Q_PER_KV)
    q = apply_axial_rope(q.reshape(bsz, n_lat, N_Q_HEADS, HEAD_DIM), cos, sin)
    q = q.reshape(bsz, n_lat, N_KV_HEADS, Q_PER_KV, HEAD_DIM)
    k = apply_axial_rope(k.reshape(bsz, n_lat, N_KV_HEADS, HEAD_DIM), cos, sin)
    v = v.reshape(bsz, n_lat, N_KV_HEADS, HEAD_DIM)
    kc = kc.reshape(bsz, n_ctx, N_KV_HEADS, HEAD_DIM)
    vc = vc.reshape(bsz, n_ctx, N_KV_HEADS, HEAD_DIM)
    attn = windowed_latent_attention(q, k, v, kc, vc, sink)
    conv = cb * centred_conv3(cc * cx, conv_w)
    ssm, ssm_c = s5_bidirectional(u, uc, lam_re, lam_im, log_dt, b_re, b_im, c_re, c_im, d_skip,
                                  w_glu, b_glu, with_ctx_out)
    out_lat = jnp.concatenate([attn, conv, ssm], axis=-1) @ w_out
    if not with_ctx_out:
        return out_lat, None
    qc = qc.reshape(bsz, n_ctx, N_KV_HEADS, Q_PER_KV, HEAD_DIM)
    attn_c = context_attention(qc, kc, vc, sink)
    conv_c = cbc * centred_conv3(ccc * cxc, conv_w)
    out_ctx = jnp.concatenate([attn_c, conv_c, ssm_c], axis=-1) @ w_out
    return out_lat, out_ctx


def squared_relu_mlp(t, w1, w2):
    return jnp.square(jax.nn.relu(t @ w1)) @ w2


def _fwd_setup_inputs(seed: int = 0) -> dict:
    key = jax.random.key(seed)
    ks = jax.random.split(key, 24)
    f32 = jnp.float32
    nrm = lambda k, shape, s: jax.random.normal(k, shape, f32) * s
    lam_im_base = jnp.pi * jnp.arange(SSM_STATE, dtype=f32)
    return {
        'x': nrm(ks[0], (BATCH, SEQ, D_MODEL), 1.0),
        'c': nrm(ks[1], (BATCH, D_MODEL), 1.0),
        'ctx': nrm(ks[2], (BATCH, CTX_LEN, D_MODEL), 1.0),
        'c_ctx': nrm(ks[3], (D_MODEL,), 1.0),
        'w_ada': nrm(ks[4], (DEPTH, D_MODEL, N_MOD * D_MODEL), 0.5 * D_MODEL ** -0.5),
        'b_ada': nrm(ks[5], (DEPTH, N_MOD * D_MODEL), 0.02),
        'norm_g': 1.0 + nrm(ks[6], (DEPTH, 4, D_MODEL), 0.02),
        'w_in': nrm(ks[7], (DEPTH, D_MODEL, IN_WIDTH), D_MODEL ** -0.5),
        'conv_w': nrm(ks[8], (DEPTH, CONV_K, CONV_WIDTH), CONV_K ** -0.5),
        'attn_sink': nrm(ks[9], (DEPTH, N_Q_HEADS), 0.5),
        'ssm_lam_re': -0.5 + nrm(ks[10], (DEPTH, 2, SSM_GROUPS, SSM_STATE), 0.01),
        'ssm_lam_im': lam_im_base + nrm(ks[11], (DEPTH, 2, SSM_GROUPS, SSM_STATE), 0.01),
        'ssm_log_dt': jax.random.uniform(ks[12], (DEPTH, 2, SSM_GROUPS), f32,
                                         minval=math.log(1e-3), maxval=math.log(1e-1)),
        'ssm_b_re': nrm(ks[13], (DEPTH, 2, SSM_GROUPS, SSM_STATE, SSM_GROUP), (2 * SSM_GROUP) ** -0.5),
        'ssm_b_im': nrm(ks[14], (DEPTH, 2, SSM_GROUPS, SSM_STATE, SSM_GROUP), (2 * SSM_GROUP) ** -0.5),
        'ssm_c_re': nrm(ks[15], (DEPTH, 2, SSM_GROUPS, SSM_GROUP, SSM_STATE), (2 * SSM_STATE) ** -0.5),
        'ssm_c_im': nrm(ks[16], (DEPTH, 2, SSM_GROUPS, SSM_GROUP, SSM_STATE), (2 * SSM_STATE) ** -0.5),
        'ssm_d': nrm(ks[17], (DEPTH, SSM_WIDTH), 1.0),
        'w_glu': nrm(ks[18], (DEPTH, SSM_WIDTH, SSM_WIDTH), SSM_WIDTH ** -0.5),
        'b_glu': nrm(ks[19], (DEPTH, SSM_WIDTH), 0.02),
        'w_out': nrm(ks[20], (DEPTH, MIX_WIDTH, D_MODEL), MIX_WIDTH ** -0.5),
        'w_mlp_in': nrm(ks[21], (DEPTH, D_MODEL, D_FF), D_MODEL ** -0.5),
        'w_mlp_out': nrm(ks[22], (DEPTH, D_FF, D_MODEL), D_FF ** -0.5),
    }


def _fwd_reference(x, c, ctx, c_ctx, w_ada, b_ada, norm_g, w_in, conv_w, attn_sink, ssm_lam_re, ssm_lam_im,
              ssm_log_dt, ssm_b_re, ssm_b_im, ssm_c_re, ssm_c_im, ssm_d, w_glu, b_glu, w_out,
              w_mlp_in, w_mlp_out):
    cos, sin = axial_rope_tables(x.shape[1])
    c_act = jax.nn.silu(c)
    c_ctx_act = jax.nn.silu(c_ctx)
    h, hc = x, ctx
    for l in range(DEPTH):
        with_ctx_out = l < DEPTH - 1
        mod = (c_act @ w_ada[l] + b_ada[l])[:, None, :]
        mod_c = c_ctx_act @ w_ada[l] + b_ada[l]
        sh1, sc1, g1, sh2, sc2, g2 = jnp.split(mod, N_MOD, axis=-1)
        sh1c, sc1c, g1c, sh2c, sc2c, g2c = jnp.split(mod_c, N_MOD, axis=-1)
        g_pre_mix, g_post_mix, g_pre_mlp, g_post_mlp = norm_g[l]
        a_lat = modulate(rms_norm(h, g_pre_mix), sh1, sc1)
        a_ctx = modulate(rms_norm(hc, g_pre_mix), sh1c, sc1c)
        m_lat, m_ctx = hybrid_mixer(a_lat, a_ctx, w_in[l], conv_w[l], attn_sink[l], ssm_lam_re[l],
                                    ssm_lam_im[l], ssm_log_dt[l], ssm_b_re[l], ssm_b_im[l], ssm_c_re[l],
                                    ssm_c_im[l], ssm_d[l], w_glu[l], b_glu[l], w_out[l], cos, sin,
                                    with_ctx_out)
        h = h + g1 * rms_norm(m_lat, g_post_mix)
        f_lat = squared_relu_mlp(modulate(rms_norm(h, g_pre_mlp), sh2, sc2), w_mlp_in[l], w_mlp_out[l])
        h = h + g2 * rms_norm(f_lat, g_post_mlp)
        if with_ctx_out:
            hc = hc + g1c * rms_norm(m_ctx, g_post_mix)
            f_ctx = squared_relu_mlp(modulate(rms_norm(hc, g_pre_mlp), sh2c, sc2c), w_mlp_in[l], w_mlp_out[l])
            hc = hc + g2c * rms_norm(f_ctx, g_post_mlp)
    return h


import jax as _jax
import jax.numpy as _jnp

TWIN_FORMAT = 'train_step'
FWD_PARAMS = ['x', 'c', 'ctx', 'c_ctx', 'w_ada', 'b_ada', 'norm_g', 'w_in', 'conv_w', 'attn_sink', 'ssm_lam_re', 'ssm_lam_im', 'ssm_log_dt', 'ssm_b_re', 'ssm_b_im', 'ssm_c_re', 'ssm_c_im', 'ssm_d', 'w_glu', 'b_glu', 'w_out', 'w_mlp_in', 'w_mlp_out']
TWIN_WEIGHTS = ['c_ctx', 'w_ada', 'b_ada', 'norm_g', 'w_in', 'conv_w', 'attn_sink', 'ssm_lam_re', 'ssm_lam_im', 'ssm_log_dt', 'ssm_b_re', 'ssm_b_im', 'ssm_c_re', 'ssm_c_im', 'ssm_d', 'w_glu', 'b_glu', 'w_out', 'w_mlp_in', 'w_mlp_out']
TWIN_DIFF_INPUT = 'x'
TWIN_INPUTS = ['x', 'c', 'ctx', 'c_ctx', 'w_ada', 'b_ada', 'norm_g', 'w_in', 'conv_w', 'attn_sink', 'ssm_lam_re', 'ssm_lam_im', 'ssm_log_dt', 'ssm_b_re', 'ssm_b_im', 'ssm_c_re', 'ssm_c_im', 'ssm_d', 'w_glu', 'b_glu', 'w_out', 'w_mlp_in', 'w_mlp_out', 'loss_target', 'm_c_ctx', 'm_w_ada', 'm_b_ada', 'm_norm_g', 'm_w_in', 'm_conv_w', 'm_attn_sink', 'm_ssm_lam_re', 'm_ssm_lam_im', 'm_ssm_log_dt', 'm_ssm_b_re', 'm_ssm_b_im', 'm_ssm_c_re', 'm_ssm_c_im', 'm_ssm_d', 'm_w_glu', 'm_b_glu', 'm_w_out', 'm_w_mlp_in', 'm_w_mlp_out', 'v_c_ctx', 'v_w_ada', 'v_b_ada', 'v_norm_g', 'v_w_in', 'v_conv_w', 'v_attn_sink', 'v_ssm_lam_re', 'v_ssm_lam_im', 'v_ssm_log_dt', 'v_ssm_b_re', 'v_ssm_b_im', 'v_ssm_c_re', 'v_ssm_c_im', 'v_ssm_d', 'v_w_glu', 'v_b_glu', 'v_w_out', 'v_w_mlp_in', 'v_w_mlp_out']
TWIN_OUTPUTS = ['loss', 'grad_x', 'grad_c_ctx', 'grad_w_ada', 'grad_b_ada', 'grad_norm_g', 'grad_w_in', 'grad_conv_w', 'grad_attn_sink', 'grad_ssm_lam_re', 'grad_ssm_lam_im', 'grad_ssm_log_dt', 'grad_ssm_b_re', 'grad_ssm_b_im', 'grad_ssm_c_re', 'grad_ssm_c_im', 'grad_ssm_d', 'grad_w_glu', 'grad_b_glu', 'grad_w_out', 'grad_w_mlp_in', 'grad_w_mlp_out', 'delta_c_ctx', 'delta_w_ada', 'delta_b_ada', 'delta_norm_g', 'delta_w_in', 'delta_conv_w', 'delta_attn_sink', 'delta_ssm_lam_re', 'delta_ssm_lam_im', 'delta_ssm_log_dt', 'delta_ssm_b_re', 'delta_ssm_b_im', 'delta_ssm_c_re', 'delta_ssm_c_im', 'delta_ssm_d', 'delta_w_glu', 'delta_b_glu', 'delta_w_out', 'delta_w_mlp_in', 'delta_w_mlp_out', 'new_m_c_ctx', 'new_m_w_ada', 'new_m_b_ada', 'new_m_norm_g', 'new_m_w_in', 'new_m_conv_w', 'new_m_attn_sink', 'new_m_ssm_lam_re', 'new_m_ssm_lam_im', 'new_m_ssm_log_dt', 'new_m_ssm_b_re', 'new_m_ssm_b_im', 'new_m_ssm_c_re', 'new_m_ssm_c_im', 'new_m_ssm_d', 'new_m_w_glu', 'new_m_b_glu', 'new_m_w_out', 'new_m_w_mlp_in', 'new_m_w_mlp_out', 'new_v_c_ctx', 'new_v_w_ada', 'new_v_b_ada', 'new_v_norm_g', 'new_v_w_in', 'new_v_conv_w', 'new_v_attn_sink', 'new_v_ssm_lam_re', 'new_v_ssm_lam_im', 'new_v_ssm_log_dt', 'new_v_ssm_b_re', 'new_v_ssm_b_im', 'new_v_ssm_c_re', 'new_v_ssm_c_im', 'new_v_ssm_d', 'new_v_w_glu', 'new_v_b_glu', 'new_v_w_out', 'new_v_w_mlp_in', 'new_v_w_mlp_out']
TWIN_LEAF_KINDS = {'loss': 'loss', 'grad_x': 'grad_x', 'grad_c_ctx': 'grad_w', 'grad_w_ada': 'grad_w', 'grad_b_ada': 'grad_w', 'grad_norm_g': 'grad_w', 'grad_w_in': 'grad_w', 'grad_conv_w': 'grad_w', 'grad_attn_sink': 'grad_w', 'grad_ssm_lam_re': 'grad_w', 'grad_ssm_lam_im': 'grad_w', 'grad_ssm_log_dt': 'grad_w', 'grad_ssm_b_re': 'grad_w', 'grad_ssm_b_im': 'grad_w', 'grad_ssm_c_re': 'grad_w', 'grad_ssm_c_im': 'grad_w', 'grad_ssm_d': 'grad_w', 'grad_w_glu': 'grad_w', 'grad_b_glu': 'grad_w', 'grad_w_out': 'grad_w', 'grad_w_mlp_in': 'grad_w', 'grad_w_mlp_out': 'grad_w', 'delta_c_ctx': 'delta_w', 'delta_w_ada': 'delta_w', 'delta_b_ada': 'delta_w', 'delta_norm_g': 'delta_w', 'delta_w_in': 'delta_w', 'delta_conv_w': 'delta_w', 'delta_attn_sink': 'delta_w', 'delta_ssm_lam_re': 'delta_w', 'delta_ssm_lam_im': 'delta_w', 'delta_ssm_log_dt': 'delta_w', 'delta_ssm_b_re': 'delta_w', 'delta_ssm_b_im': 'delta_w', 'delta_ssm_c_re': 'delta_w', 'delta_ssm_c_im': 'delta_w', 'delta_ssm_d': 'delta_w', 'delta_w_glu': 'delta_w', 'delta_b_glu': 'delta_w', 'delta_w_out': 'delta_w', 'delta_w_mlp_in': 'delta_w', 'delta_w_mlp_out': 'delta_w', 'new_m_c_ctx': 'new_m', 'new_m_w_ada': 'new_m', 'new_m_b_ada': 'new_m', 'new_m_norm_g': 'new_m', 'new_m_w_in': 'new_m', 'new_m_conv_w': 'new_m', 'new_m_attn_sink': 'new_m', 'new_m_ssm_lam_re': 'new_m', 'new_m_ssm_lam_im': 'new_m', 'new_m_ssm_log_dt': 'new_m', 'new_m_ssm_b_re': 'new_m', 'new_m_ssm_b_im': 'new_m', 'new_m_ssm_c_re': 'new_m', 'new_m_ssm_c_im': 'new_m', 'new_m_ssm_d': 'new_m', 'new_m_w_glu': 'new_m', 'new_m_b_glu': 'new_m', 'new_m_w_out': 'new_m', 'new_m_w_mlp_in': 'new_m', 'new_m_w_mlp_out': 'new_m', 'new_v_c_ctx': 'new_v', 'new_v_w_ada': 'new_v', 'new_v_b_ada': 'new_v', 'new_v_norm_g': 'new_v', 'new_v_w_in': 'new_v', 'new_v_conv_w': 'new_v', 'new_v_attn_sink': 'new_v', 'new_v_ssm_lam_re': 'new_v', 'new_v_ssm_lam_im': 'new_v', 'new_v_ssm_log_dt': 'new_v', 'new_v_ssm_b_re': 'new_v', 'new_v_ssm_b_im': 'new_v', 'new_v_ssm_c_re': 'new_v', 'new_v_ssm_c_im': 'new_v', 'new_v_ssm_d': 'new_v', 'new_v_w_glu': 'new_v', 'new_v_b_glu': 'new_v', 'new_v_w_out': 'new_v', 'new_v_w_mlp_in': 'new_v', 'new_v_w_mlp_out': 'new_v'}


def _forward(args):
    return _fwd_reference(*[args[k] for k in FWD_PARAMS])


def _output_shape():
    out = _jax.eval_shape(lambda: _forward(_fwd_setup_inputs(0)))
    return out.shape, out.dtype

N_MICROBATCH = 1
ADAM_LR = 0.001
ADAM_B1 = 0.9
ADAM_B2 = 0.999
ADAM_EPS = 1e-08
ADAM_WD = 0.01
ADAM_STEP = 10
PER_EXAMPLE_BATCH_AXIS = {'x': 0, 'c': 0, 'ctx': 0, 'loss_target': 0}
SHARED_INPUTS = []
_WEIGHT_DTYPES = {'c_ctx': _jnp.float32, 'w_ada': _jnp.float32, 'b_ada': _jnp.float32, 'norm_g': _jnp.float32, 'w_in': _jnp.float32, 'conv_w': _jnp.float32, 'attn_sink': _jnp.float32, 'ssm_lam_re': _jnp.float32, 'ssm_lam_im': _jnp.float32, 'ssm_log_dt': _jnp.float32, 'ssm_b_re': _jnp.float32, 'ssm_b_im': _jnp.float32, 'ssm_c_re': _jnp.float32, 'ssm_c_im': _jnp.float32, 'ssm_d': _jnp.float32, 'w_glu': _jnp.float32, 'b_glu': _jnp.float32, 'w_out': _jnp.float32, 'w_mlp_in': _jnp.float32, 'w_mlp_out': _jnp.float32}
MOMENT_SCALE = {'c_ctx': 1.180441e+00, 'w_ada': 3.785272e+00, 'b_ada': 7.098590e+00, 'norm_g': 5.262910e+00, 'w_in': 4.601875e-01, 'conv_w': 3.467905e-01, 'attn_sink': 9.494807e-03, 'ssm_lam_re': 6.374980e-02, 'ssm_lam_im': 4.678362e-02, 'ssm_log_dt': 2.432651e+00, 'ssm_b_re': 3.800587e-02, 'ssm_b_im': 4.263809e-02, 'ssm_c_re': 7.017374e-02, 'ssm_c_im': 7.349192e-02, 'ssm_d': 1.000710e+00, 'w_glu': 1.367498e-01, 'b_glu': 3.979694e-01, 'w_out': 8.110330e-01, 'w_mlp_in': 3.128086e-01, 'w_mlp_out': 1.555554e+00}


def _to_microbatches(a, axis):
    t = _jnp.moveaxis(a, axis, 0)
    t = t.reshape((N_MICROBATCH, t.shape[0] // N_MICROBATCH) + t.shape[1:])
    return _jnp.moveaxis(t, 1, axis + 1)


def setup_inputs(seed: int = 0) -> dict:
    inp = _fwd_setup_inputs(seed)
    key = _jax.random.fold_in(_jax.random.key(seed), 7919)
    shape, _ = _output_shape()
    out = dict(inp)
    out["loss_target"] = _jax.random.normal(_jax.random.fold_in(key, 0), shape, _jnp.float32)
    for i, name in enumerate(TWIN_WEIGHTS):
        w = inp[name].astype(_jnp.float32)
        if MOMENT_SCALE is None:
            s = _jnp.sqrt(_jnp.mean(_jnp.square(w)) + 1e-30)
        else:
            s = MOMENT_SCALE[name]
        km, kv = _jax.random.split(_jax.random.fold_in(key, i + 1))
        out[name] = w
        out["m_" + name] = s * _jax.random.normal(km, w.shape, _jnp.float32)
        out["v_" + name] = (s * s) * _jax.random.uniform(kv, w.shape, _jnp.float32, 0.5, 1.5)
    if N_MICROBATCH > 1:
        for name, axis in PER_EXAMPLE_BATCH_AXIS.items():
            out[name] = _to_microbatches(out[name], axis)
    return {'x': out['x'], 'c': out['c'], 'ctx': out['ctx'], 'c_ctx': out['c_ctx'], 'w_ada': out['w_ada'], 'b_ada': out['b_ada'], 'norm_g': out['norm_g'], 'w_in': out['w_in'], 'conv_w': out['conv_w'], 'attn_sink': out['attn_sink'], 'ssm_lam_re': out['ssm_lam_re'], 'ssm_lam_im': out['ssm_lam_im'], 'ssm_log_dt': out['ssm_log_dt'], 'ssm_b_re': out['ssm_b_re'], 'ssm_b_im': out['ssm_b_im'], 'ssm_c_re': out['ssm_c_re'], 'ssm_c_im': out['ssm_c_im'], 'ssm_d': out['ssm_d'], 'w_glu': out['w_glu'], 'b_glu': out['b_glu'], 'w_out': out['w_out'], 'w_mlp_in': out['w_mlp_in'], 'w_mlp_out': out['w_mlp_out'], 'loss_target': out['loss_target'], 'm_c_ctx': out['m_c_ctx'], 'm_w_ada': out['m_w_ada'], 'm_b_ada': out['m_b_ada'], 'm_norm_g': out['m_norm_g'], 'm_w_in': out['m_w_in'], 'm_conv_w': out['m_conv_w'], 'm_attn_sink': out['m_attn_sink'], 'm_ssm_lam_re': out['m_ssm_lam_re'], 'm_ssm_lam_im': out['m_ssm_lam_im'], 'm_ssm_log_dt': out['m_ssm_log_dt'], 'm_ssm_b_re': out['m_ssm_b_re'], 'm_ssm_b_im': out['m_ssm_b_im'], 'm_ssm_c_re': out['m_ssm_c_re'], 'm_ssm_c_im': out['m_ssm_c_im'], 'm_ssm_d': out['m_ssm_d'], 'm_w_glu': out['m_w_glu'], 'm_b_glu': out['m_b_glu'], 'm_w_out': out['m_w_out'], 'm_w_mlp_in': out['m_w_mlp_in'], 'm_w_mlp_out': out['m_w_mlp_out'], 'v_c_ctx': out['v_c_ctx'], 'v_w_ada': out['v_w_ada'], 'v_b_ada': out['v_b_ada'], 'v_norm_g': out['v_norm_g'], 'v_w_in': out['v_w_in'], 'v_conv_w': out['v_conv_w'], 'v_attn_sink': out['v_attn_sink'], 'v_ssm_lam_re': out['v_ssm_lam_re'], 'v_ssm_lam_im': out['v_ssm_lam_im'], 'v_ssm_log_dt': out['v_ssm_log_dt'], 'v_ssm_b_re': out['v_ssm_b_re'], 'v_ssm_b_im': out['v_ssm_b_im'], 'v_ssm_c_re': out['v_ssm_c_re'], 'v_ssm_c_im': out['v_ssm_c_im'], 'v_ssm_d': out['v_ssm_d'], 'v_w_glu': out['v_w_glu'], 'v_b_glu': out['v_b_glu'], 'v_w_out': out['v_w_out'], 'v_w_mlp_in': out['v_w_mlp_in'], 'v_w_mlp_out': out['v_w_mlp_out']}


def _loss(weights, diff, rest, loss_target):
    with _jax.named_scope("forward"):
        args = {**rest, TWIN_DIFF_INPUT: diff, **{k: w.astype(_WEIGHT_DTYPES[k]) for k, w in weights.items()}}
        y = _forward(args)
    with _jax.named_scope("loss_head"):
        err = _jnp.square(y.astype(_jnp.float32) - loss_target)
        return 0.5 * _jnp.sum(_jnp.mean(err, axis=-1)) if err.ndim else 0.5 * err


def _adamw(w, g, m, v):
    m = ADAM_B1 * m + (1.0 - ADAM_B1) * g
    v = ADAM_B2 * v + (1.0 - ADAM_B2) * _jnp.square(g)
    m_hat = m / (1.0 - ADAM_B1 ** ADAM_STEP)
    v_hat = v / (1.0 - ADAM_B2 ** ADAM_STEP)
    delta = -ADAM_LR * (m_hat / (_jnp.sqrt(v_hat) + ADAM_EPS) + ADAM_WD * w)
    return delta, m, v


def reference(x, c, ctx, c_ctx, w_ada, b_ada, norm_g, w_in, conv_w, attn_sink, ssm_lam_re, ssm_lam_im, ssm_log_dt, ssm_b_re, ssm_b_im, ssm_c_re, ssm_c_im, ssm_d, w_glu, b_glu, w_out, w_mlp_in, w_mlp_out, loss_target, m_c_ctx, m_w_ada, m_b_ada, m_norm_g, m_w_in, m_conv_w, m_attn_sink, m_ssm_lam_re, m_ssm_lam_im, m_ssm_log_dt, m_ssm_b_re, m_ssm_b_im, m_ssm_c_re, m_ssm_c_im, m_ssm_d, m_w_glu, m_b_glu, m_w_out, m_w_mlp_in, m_w_mlp_out, v_c_ctx, v_w_ada, v_b_ada, v_norm_g, v_w_in, v_conv_w, v_attn_sink, v_ssm_lam_re, v_ssm_lam_im, v_ssm_log_dt, v_ssm_b_re, v_ssm_b_im, v_ssm_c_re, v_ssm_c_im, v_ssm_d, v_w_glu, v_b_glu, v_w_out, v_w_mlp_in, v_w_mlp_out):
    given = dict(x=x, c=c, ctx=ctx, c_ctx=c_ctx, w_ada=w_ada, b_ada=b_ada, norm_g=norm_g, w_in=w_in, conv_w=conv_w, attn_sink=attn_sink, ssm_lam_re=ssm_lam_re, ssm_lam_im=ssm_lam_im, ssm_log_dt=ssm_log_dt, ssm_b_re=ssm_b_re, ssm_b_im=ssm_b_im, ssm_c_re=ssm_c_re, ssm_c_im=ssm_c_im, ssm_d=ssm_d, w_glu=w_glu, b_glu=b_glu, w_out=w_out, w_mlp_in=w_mlp_in, w_mlp_out=w_mlp_out, loss_target=loss_target, m_c_ctx=m_c_ctx, m_w_ada=m_w_ada, m_b_ada=m_b_ada, m_norm_g=m_norm_g, m_w_in=m_w_in, m_conv_w=m_conv_w, m_attn_sink=m_attn_sink, m_ssm_lam_re=m_ssm_lam_re, m_ssm_lam_im=m_ssm_lam_im, m_ssm_log_dt=m_ssm_log_dt, m_ssm_b_re=m_ssm_b_re, m_ssm_b_im=m_ssm_b_im, m_ssm_c_re=m_ssm_c_re, m_ssm_c_im=m_ssm_c_im, m_ssm_d=m_ssm_d, m_w_glu=m_w_glu, m_b_glu=m_b_glu, m_w_out=m_w_out, m_w_mlp_in=m_w_mlp_in, m_w_mlp_out=m_w_mlp_out, v_c_ctx=v_c_ctx, v_w_ada=v_w_ada, v_b_ada=v_b_ada, v_norm_g=v_norm_g, v_w_in=v_w_in, v_conv_w=v_conv_w, v_attn_sink=v_attn_sink, v_ssm_lam_re=v_ssm_lam_re, v_ssm_lam_im=v_ssm_lam_im, v_ssm_log_dt=v_ssm_log_dt, v_ssm_b_re=v_ssm_b_re, v_ssm_b_im=v_ssm_b_im, v_ssm_c_re=v_ssm_c_re, v_ssm_c_im=v_ssm_c_im, v_ssm_d=v_ssm_d, v_w_glu=v_w_glu, v_b_glu=v_b_glu, v_w_out=v_w_out, v_w_mlp_in=v_w_mlp_in, v_w_mlp_out=v_w_mlp_out)
    weights = {n: given[n] for n in TWIN_WEIGHTS}
    shared = {n: given[n] for n in SHARED_INPUTS}
    per_example = {n: given[n] for n in ['x', 'c', 'ctx']}
    grad_fn = _jax.value_and_grad(_loss, argnums=(0, 1))

    def one_microbatch(ex, loss_target):
        ex = dict(ex)
        diff = ex.pop(TWIN_DIFF_INPUT)
        return grad_fn(weights, diff, {**shared, **ex}, loss_target)

    if N_MICROBATCH == 1:
        loss, (grad_w, grad_x) = one_microbatch(per_example, given["loss_target"])
    else:
        def body(carry, xs):
            loss_sum, grad_sum = carry
            l_k, (gw_k, gx_k) = one_microbatch(xs[0], xs[1])
            with _jax.named_scope("update"):
                return (loss_sum + l_k, _jax.tree.map(_jnp.add, grad_sum, gw_k)), gx_k

        init = (_jnp.zeros((), _jnp.float32), _jax.tree.map(_jnp.zeros_like, weights))
        (loss, grad_w), grad_x = _jax.lax.scan(body, init, (per_example, given["loss_target"]))
    with _jax.named_scope("update"):
        delta_w, new_m, new_v = {}, {}, {}
        for n in TWIN_WEIGHTS:
            delta_w[n], new_m[n], new_v[n] = _adamw(weights[n], grad_w[n], given["m_" + n], given["v_" + n])
    return (loss, grad_x, *[grad_w[n] for n in TWIN_WEIGHTS], *[delta_w[n] for n in TWIN_WEIGHTS],
            *[new_m[n] for n in TWIN_WEIGHTS], *[new_v[n] for n in TWIN_WEIGHTS])
```

```python
import functools

import jax
import jax.numpy as jnp
from jax import lax
from jax.experimental import pallas as pl
from jax.experimental.pallas import tpu as pltpu

F32 = jnp.float32
MXU_DTYPE = jnp.bfloat16
SDS = jax.ShapeDtypeStruct

N_DEV = 8
HEAD_DIM = 64
N_Q_HEADS = 8
Q_PER_KV = 4
ATTN_WIDTH = 512
KV_WIDTH = 128
WINDOW = 128
Q_BLOCK = 128
GRID_W = 64
ROPE_BASE = 10000.0
ROPE_PAIRS = 16
CONV_WIDTH = 256
SSM_WIDTH = 256
SSM_GROUP = 16
SSM_GROUPS = 16
SSM_STATE = 64
SSM_N = SSM_GROUPS * SSM_STATE
IN_WIDTH = 1792
N_MOD = 6
EPS = 1e-6
NEG_INF = -1e30
ADAM_LR = 0.001
ADAM_B1 = 0.9
ADAM_B2 = 0.999
ADAM_EPS = 1e-08
ADAM_WD = 0.01
ADAM_STEP = 10
MOD_ROWS = 128
ROW_TILE = 256
LANES = 128
MESH = pl.DeviceIdType.MESH
_SCALE = HEAD_DIM ** -0.5

WEIGHTS = ['c_ctx', 'w_ada', 'b_ada', 'norm_g', 'w_in', 'conv_w', 'attn_sink', 'ssm_lam_re', 'ssm_lam_im',
           'ssm_log_dt', 'ssm_b_re', 'ssm_b_im', 'ssm_c_re', 'ssm_c_im', 'ssm_d', 'w_glu', 'b_glu', 'w_out',
           'w_mlp_in', 'w_mlp_out']
SHARD_AXIS = {'w_ada': 2, 'norm_g': 2, 'w_in': 2, 'conv_w': 2, 'w_glu': 1, 'w_out': 1, 'w_mlp_in': 2, 'w_mlp_out': 1}
BIG = ['w_ada', 'w_in', 'w_out', 'w_mlp_in', 'w_mlp_out']
SMALL = ['norm_g', 'conv_w', 'w_glu']
SHARDED = BIG + SMALL
REPLICATED = [n for n in WEIGHTS if n not in SHARD_AXIS]


def _pick(n, cands):
    for c in cands:
        if n % c == 0:
            return c
    return n


def _mx(x):
    return x.astype(MXU_DTYPE)


def _dg_nt(a, b):
    return lax.dot_general(a, b, (((1,), (1,)), ((), ())), preferred_element_type=F32)


def _dg_tn(a, b):
    return lax.dot_general(a, b, (((0,), (0,)), ((), ())), preferred_element_type=F32)


def _all_gather(x2d):
    r, cdim = x2d.shape

    def body(x_ref, out_ref, send_sems, recv_sems, local_sem):
        x, y, c = lax.axis_index("x"), lax.axis_index("y"), lax.axis_index("c")
        me, sibling = (x, y, c), (x, y, 1 - c)
        chips = [(1 - x, y), (x, 1 - y), (1 - x, 1 - y)]

        def slot(px, py, pc):
            return out_ref.at[4 * px + 2 * py + pc]

        def copy(k, block, to, src=None):
            return pltpu.make_async_remote_copy(
                src_ref=slot(*block) if src is None else src, dst_ref=slot(*block),
                send_sem=send_sems.at[k], recv_sem=recv_sems.at[k], device_id=to, device_id_type=MESH)

        mine = pltpu.make_async_copy(x_ref, slot(*me), local_sem)
        mine.start()
        first = [copy(0, me, sibling, src=x_ref)]
        first += [copy(1 + j, me, (*chip, c), src=x_ref) for j, chip in enumerate(chips)]
        for cp in first:
            cp.start()
        passed = [copy(4 + j, (*chip, c), sibling) for j, chip in enumerate(chips)]
        for j, chip in enumerate(chips):
            copy(1 + j, (*chip, c), me).wait_recv()
            passed[j].start()
        copy(0, sibling, me).wait_recv()
        for j, chip in enumerate(chips):
            copy(4 + j, (*chip, 1 - c), me).wait_recv()
        for cp in first + passed:
            cp.wait_send()
        mine.wait()

    return pl.pallas_call(
        body, name=f"all_gather_{r}x{cdim}_{jnp.dtype(x2d.dtype).name}",
        out_shape=SDS((N_DEV, r, cdim), x2d.dtype),
        in_specs=[pl.BlockSpec(memory_space=pl.ANY)],
        out_specs=pl.BlockSpec(memory_space=pl.ANY),
        scratch_shapes=[pltpu.SemaphoreType.DMA((7,)), pltpu.SemaphoreType.DMA((7,)), pltpu.SemaphoreType.DMA],
    )(x2d)


def _all_to_all(g3):
    _, r, cdim = g3.shape

    def body(g_ref, out_ref, send_sems, recv_sems, local_sem):
        x, y, c = lax.axis_index("x"), lax.axis_index("y"), lax.axis_index("c")
        me = 4 * x + 2 * y + c

        def peer(k):
            px = 1 - x if k & 4 else x
            py = 1 - y if k & 2 else y
            pc = 1 - c if k & 1 else c
            return (px, py, pc), 4 * px + 2 * py + pc

        def copy(k):
            to, to_idx = peer(k)
            return pltpu.make_async_remote_copy(
                src_ref=g_ref.at[to_idx], dst_ref=out_ref.at[me],
                send_sem=send_sems.at[k - 1], recv_sem=recv_sems.at[k - 1], device_id=to, device_id_type=MESH)

        def arrival(k):
            to, to_idx = peer(k)
            return pltpu.make_async_remote_copy(
                src_ref=g_ref.at[to_idx], dst_ref=out_ref.at[to_idx],
                send_sem=send_sems.at[k - 1], recv_sem=recv_sems.at[k - 1], device_id=to, device_id_type=MESH)

        mine = pltpu.make_async_copy(g_ref.at[me], out_ref.at[me], local_sem)
        mine.start()
        sends = [copy(k) for k in range(1, N_DEV)]
        for cp in sends:
            cp.start()
        for k in range(1, N_DEV):
            arrival(k).wait_recv()
        for cp in sends:
            cp.wait_send()
        mine.wait()

    return pl.pallas_call(
        body, name=f"all_to_all_{r}x{cdim}",
        out_shape=SDS(g3.shape, g3.dtype),
        in_specs=[pl.BlockSpec(memory_space=pl.ANY)],
        out_specs=pl.BlockSpec(memory_space=pl.ANY),
        scratch_shapes=[pltpu.SemaphoreType.DMA((7,)), pltpu.SemaphoreType.DMA((7,)), pltpu.SemaphoreType.DMA],
    )(g3)


def _sum_slots(g3):
    n, r, cdim = g3.shape
    tr = _pick(r, (1024, 512, 256, 128, 64, 32, 16, 8))

    def body(g_ref, o_ref):
        acc = g_ref[0]
        for s in range(1, n):
            acc = acc + g_ref[s]
        o_ref[...] = acc

    return pl.pallas_call(
        body, name=f"sum_slots_{r}", grid=(r // tr,),
        out_shape=SDS((r, cdim), g3.dtype),
        in_specs=[pl.BlockSpec((n, tr, cdim), lambda i: (0, i, 0))],
        out_specs=pl.BlockSpec((tr, cdim), lambda i: (i, 0)),
    )(g3)


def _pack(arrs, dtype, row_mult):
    flat = jnp.concatenate([a.reshape(-1).astype(dtype) for a in arrs])
    pad = -flat.size % (LANES * row_mult)
    return jnp.pad(flat, (0, pad)).reshape(-1, LANES)


def _pack8(arrs, row_mult):
    flat = jnp.concatenate([a.reshape(N_DEV, -1) for a in arrs], axis=1)
    pad = -flat.shape[1] % (LANES * row_mult)
    return jnp.pad(flat, ((0, 0), (0, pad))).reshape(N_DEV, -1, LANES)


def _unpack(mat, shapes, lead=()):
    flat = mat.reshape(lead + (-1,))
    out, off = [], 0
    for s in shapes:
        n = 1
        for d in s:
            n *= d
        out.append(flat[..., off:off + n].reshape(lead + tuple(s)))
        off += n
    return out


def _to_pieces(g, axis):
    s = g.shape
    g = g.reshape(s[:axis] + (N_DEV, s[axis] // N_DEV) + s[axis + 1:])
    return jnp.moveaxis(g, axis, 0)


def _from_pieces(p, axis):
    p = jnp.moveaxis(p, 0, axis)
    s = p.shape
    return p.reshape(s[:axis] + (s[axis] * s[axis + 1],) + s[axis + 2:])


def _mm_call(kind, a, b, z=None, sqrelu=False, out_dtype=F32):
    if kind == 'nn':
        (m, k), n = a.shape, b.shape[1]
    elif kind == 'nt':
        (m, k), n = a.shape, b.shape[0]
    else:
        (k, m), n = a.shape, b.shape[1]
    bm = _pick(m, (512, 256, 128))
    bn = _pick(n, (512, 256, 128))
    bk = _pick(k, (512, 256, 128))
    nk = k // bk

    def body(a_ref, b_ref, *rest):
        rest = list(rest)
        z_ref = rest.pop(0) if z is not None else None
        o_ref = rest.pop(0)
        act_ref = rest.pop(0) if sqrelu else None
        acc = rest.pop(0)
        kk = pl.program_id(2)

        @pl.when(kk == 0)
        def _():
            acc[...] = jnp.zeros_like(acc)

        av, bv = _mx(a_ref[...]), _mx(b_ref[...])
        if kind == 'nn':
            acc[...] += jnp.dot(av, bv, preferred_element_type=F32)
        elif kind == 'nt':
            acc[...] += _dg_nt(av, bv)
        else:
            acc[...] += _dg_tn(av, bv)

        @pl.when(kk == nk - 1)
        def _():
            r = acc[...]
            if z_ref is not None:
                r = r * (2.0 * jnp.maximum(z_ref[...], 0.0))
            o_ref[...] = r.astype(o_ref.dtype)
            if act_ref is not None:
                rr = jnp.maximum(r, 0.0)
                act_ref[...] = (rr * rr).astype(act_ref.dtype)

    if kind == 'nn':
        a_spec = pl.BlockSpec((bm, bk), lambda i, j, kk: (i, kk))
        b_spec = pl.BlockSpec((bk, bn), lambda i, j, kk: (kk, j))
    elif kind == 'nt':
        a_spec = pl.BlockSpec((bm, bk), lambda i, j, kk: (i, kk))
        b_spec = pl.BlockSpec((bn, bk), lambda i, j, kk: (j, kk))
    else:
        a_spec = pl.BlockSpec((bk, bm), lambda i, j, kk: (kk, i))
        b_spec = pl.BlockSpec((bk, bn), lambda i, j, kk: (kk, j))
    o_spec = pl.BlockSpec((bm, bn), lambda i, j, kk: (i, j))
    in_specs, args = [a_spec, b_spec], [a, b]
    if z is not None:
        in_specs.append(o_spec)
        args.append(z)
    out_shape, out_specs = [SDS((m, n), out_dtype)], [o_spec]
    if sqrelu:
        out_shape.append(SDS((m, n), MXU_DTYPE))
        out_specs.append(o_spec)
    tag = kind + ('_sq' if sqrelu else '') + ('_z' if z is not None else '')
    res = pl.pallas_call(
        body, name=f"mm_{tag}_{m}x{k}x{n}", grid=(m // bm, n // bn, nk),
        out_shape=out_shape, in_specs=in_specs, out_specs=out_specs,
        scratch_shapes=[pltpu.VMEM((bm, bn), F32)],
        compiler_params=pltpu.CompilerParams(dimension_semantics=("parallel", "parallel", "arbitrary")),
    )(*args)
    return res if sqrelu else res[0]


@jax.custom_vjp
def linear(a, w):
    return _mm_call('nn', a, w)


def _linear_fwd(a, w):
    return _mm_call('nn', a, w), (a, w)


def _linear_bwd(res, g):
    a, w = res
    return _mm_call('nt', g, w), _mm_call('tn', a, g)


linear.defvjp(_linear_fwd, _linear_bwd)


@jax.custom_vjp
def linear_d(a, w, wd):
    return _mm_call('nn', a, w)


def _linear_d_fwd(a, w, wd):
    return _mm_call('nn', a, w), (a, w)


def _linear_d_bwd(res, g):
    a, w = res
    return _mm_call('nt', g, w), jnp.zeros_like(w), _mm_call('tn', a, g)


linear_d.defvjp(_linear_d_fwd, _linear_d_bwd)


@jax.custom_vjp
def mlp(a, w1, w1d, w2, w2d):
    _, act = _mm_call('nn', a, w1, sqrelu=True)
    return _mm_call('nn', act, w2)


def _mlp_fwd(a, w1, w1d, w2, w2d):
    z, act = _mm_call('nn', a, w1, sqrelu=True)
    return _mm_call('nn', act, w2), (a, w1, w2, z, act)


def _mlp_bwd(res, g):
    a, w1, w2, z, act = res
    dz = _mm_call('nt', g, w2, z=z, out_dtype=MXU_DTYPE)
    dw2 = _mm_call('tn', act, g)
    da = _mm_call('nt', dz, w1)
    dw1 = _mm_call('tn', a, dz)
    return da, jnp.zeros_like(w1), dw1, jnp.zeros_like(w2), dw2


mlp.defvjp(_mlp_fwd, _mlp_bwd)


def _rowwise(name, f, out_widths, seg_len):
    def specs(rows, segs, globs, tr):
        nseg = segs[0].shape[0] if segs else 1

        def seg_of(i):
            return jnp.minimum((i * tr) // seg_len, nseg - 1)

        row_specs = [pl.BlockSpec((tr, r.shape[1]), lambda i: (i, 0)) for r in rows]
        seg_specs = [pl.BlockSpec((1, 1, s.shape[2]), lambda i: (seg_of(i), 0, 0)) for s in segs]
        glob_specs = [pl.BlockSpec(g.shape, lambda i: (0, 0)) for g in globs]
        return seg_of, row_specs, seg_specs, glob_specs

    def fwd_call(rows, segs, globs):
        t = rows[0].shape[0]
        tr = _pick(t, (ROW_TILE, 128, 64, 32, 16, 8))
        _, row_specs, seg_specs, glob_specs = specs(rows, segs, globs, tr)
        nr, ns = len(rows), len(segs)

        def body(*refs):
            ins, outs = refs[:nr + ns + len(globs)], refs[nr + ns + len(globs):]
            vals = [r[...] for r in ins[:nr]] + [r[0] for r in ins[nr:nr + ns]] + [r[...] for r in ins[nr + ns:]]
            for o_ref, v in zip(outs, f(*vals)):
                o_ref[...] = v

        return pl.pallas_call(
            body, name=f"{name}_fwd_{t}", grid=(t // tr,),
            out_shape=[SDS((t, w), F32) for w in out_widths],
            in_specs=row_specs + seg_specs + glob_specs,
            out_specs=[pl.BlockSpec((tr, w), lambda i: (i, 0)) for w in out_widths],
        )(*rows, *segs, *globs)

    def bwd_call(rows, segs, globs, douts):
        t = rows[0].shape[0]
        tr = _pick(t, (ROW_TILE, 128, 64, 32, 16, 8))
        seg_of, row_specs, seg_specs, glob_specs = specs(rows, segs, globs, tr)
        nr, ns, ng, no = len(rows), len(segs), len(globs), len(out_widths)

        def body(*refs):
            ins = refs[:nr + ns + ng]
            dos = refs[nr + ns + ng:nr + ns + ng + no]
            outs = refs[nr + ns + ng + no:]
            i = pl.program_id(0)
            first_of_seg = jnp.logical_or(i == 0, seg_of(i) != seg_of(jnp.maximum(i - 1, 0)))
            vals = [r[...] for r in ins[:nr]] + [r[0] for r in ins[nr:nr + ns]] + [r[...] for r in ins[nr + ns:]]
            _, vjp = jax.vjp(f, *vals)
            grads = vjp(tuple(d[...] for d in dos))
            for o_ref, gval in zip(outs[:nr], grads[:nr]):
                o_ref[...] = gval
            for o_ref, gval in zip(outs[nr:nr + ns], grads[nr:nr + ns]):
                @pl.when(first_of_seg)
                def _(o_ref=o_ref):
                    o_ref[...] = jnp.zeros_like(o_ref)
                o_ref[0] += gval
            for o_ref, gval in zip(outs[nr + ns:], grads[nr + ns:]):
                @pl.when(i == 0)
                def _(o_ref=o_ref):
                    o_ref[...] = jnp.zeros_like(o_ref)
                o_ref[...] += gval

        do_specs = [pl.BlockSpec((tr, w), lambda i: (i, 0)) for w in out_widths]
        return pl.pallas_call(
            body, name=f"{name}_bwd_{t}", grid=(t // tr,),
            out_shape=[SDS(r.shape, F32) for r in rows] + [SDS(s.shape, F32) for s in segs]
            + [SDS(g.shape, F32) for g in globs],
            in_specs=row_specs + seg_specs + glob_specs + do_specs,
            out_specs=row_specs + seg_specs + glob_specs,
        )(*rows, *segs, *globs, *douts)

    return fwd_call, bwd_call


def _norm_mod_f(x, shift, scale, g):
    r = lax.rsqrt(jnp.mean(x * x, axis=-1, keepdims=True) + EPS)
    return ((x * r) * g * (1.0 + scale) + shift,)


def _resid_gate_f(h, m, gate, g):
    r = lax.rsqrt(jnp.mean(m * m, axis=-1, keepdims=True) + EPS)
    return (h + gate * ((m * r) * g),)


def _glu_f(y0, y1, u, d, w, b):
    y = y0 + y1 + d * u
    g = 0.5 * y * (1.0 + jnp.tanh(0.7978845608028654 * (y + 0.044715 * (y * y * y))))
    zz = jnp.dot(_mx(g), _mx(w), preferred_element_type=F32) + b
    return (g * (1.0 / (1.0 + jnp.exp(-zz))),)


def _make_rowwise_op(name, f, out_width, n_rows, n_segs, seg_len):
    fwd_call, bwd_call = _rowwise(name, f, (out_width,), seg_len)

    @jax.custom_vjp
    def op(*args):
        return fwd_call(args[:n_rows], args[n_rows:n_rows + n_segs], args[n_rows + n_segs:])[0]

    def op_fwd(*args):
        return fwd_call(args[:n_rows], args[n_rows:n_rows + n_segs], args[n_rows + n_segs:])[0], args

    def op_bwd(args, g):
        return tuple(bwd_call(args[:n_rows], args[n_rows:n_rows + n_segs], args[n_rows + n_segs:], (g,)))

    op.defvjp(op_fwd, op_bwd)
    return op


def _lane():
    return lax.broadcasted_iota(jnp.int32, (1, LANES), 1)


def _hm(off):
    lane = _lane()
    return jnp.logical_and(lane >= off, lane < off + HEAD_DIM)


def _align_in(pair, off_from, off_to):
    x = pair if off_from == off_to else pltpu.roll(pair, HEAD_DIM, 1)
    return jnp.where(_hm(off_to), x, 0.0)


def _align_out(x, off_from, off_to):
    x = jnp.where(_hm(off_from), x, 0.0)
    return x if off_from == off_to else pltpu.roll(x, HEAD_DIM, 1)


def _col(tile, hq):
    return jnp.sum(jnp.where(_lane() == hq, tile, 0.0), axis=1, keepdims=True)


def _setcol(tile, hq, col):
    return jnp.where(_lane() == hq, col, tile)


def _head_fwd(qa, groups, sk):
    ss, m = [], None
    for kmat, _, mask in groups:
        s = _dg_nt(qa, kmat) * _SCALE
        if mask is not None:
            s = jnp.where(mask, s, NEG_INF)
        ss.append(s)
        mm = jnp.max(s, axis=1, keepdims=True)
        m = mm if m is None else jnp.maximum(m, mm)
    m = jnp.maximum(m, sk)
    l = jnp.exp(sk - m)
    pv = None
    for s, (_, vmat, _) in zip(ss, groups):
        p = jnp.exp(s - m)
        l = l + jnp.sum(p, axis=1, keepdims=True)
        t = jnp.dot(_mx(p), vmat, preferred_element_type=F32)
        pv = t if pv is None else pv + t
    return pv / l, m + jnp.log(l)


def _head_bwd(qa, doa, groups, sk, lse_h, delta_h):
    dq, outs = None, []
    for kmat, vmat, mask in groups:
        s = _dg_nt(qa, kmat) * _SCALE
        if mask is not None:
            s = jnp.where(mask, s, NEG_INF)
        p = jnp.exp(s - lse_h)
        dp = _dg_nt(doa, vmat)
        ds = p * (dp - delta_h) * _SCALE
        t = jnp.dot(_mx(ds), kmat, preferred_element_type=F32)
        dq = t if dq is None else dq + t
        outs.append((_mx(ds), _mx(p)))
    return dq, outs, jnp.exp(sk - lse_h)


def _heads():
    for pair in range(N_Q_HEADS // 2):
        for sub in range(2):
            hq = pair * 2 + sub
            yield pair, hq, sub * HEAD_DIM, (hq // Q_PER_KV) * HEAD_DIM


def _partner(t):
    lane = lax.broadcasted_iota(jnp.int32, t.shape, 1)
    return jnp.where((lane & ROPE_PAIRS) == 0, pltpu.roll(t, LANES - ROPE_PAIRS, 1), pltpu.roll(t, ROPE_PAIRS, 1))


def _rope_tables(n_tokens):
    rows = n_tokens // GRID_W
    row = jnp.broadcast_to(jnp.arange(rows)[:, None], (rows, GRID_W)).reshape(-1)
    col = jnp.broadcast_to(jnp.arange(GRID_W)[None, :], (rows, GRID_W)).reshape(-1)
    freqs = ROPE_BASE ** (-jnp.arange(ROPE_PAIRS, dtype=F32) / ROPE_PAIRS)
    ang = jnp.concatenate([row[:, None].astype(F32) * freqs, col[:, None].astype(F32) * freqs], axis=-1)
    c, s = jnp.cos(ang), jnp.sin(ang)
    n = ROPE_PAIRS
    cos64 = jnp.concatenate([c[:, :n], c[:, :n], c[:, n:], c[:, n:]], axis=1)
    sin64 = jnp.concatenate([-s[:, :n], s[:, :n], -s[:, n:], s[:, n:]], axis=1)
    return jnp.tile(cos64, (1, 2)), jnp.tile(sin64, (1, 2))


def _rope_call(q_src, k_src, cos, sin, dims, inverse):
    b, l, _ = dims
    t_lat = b * l
    tr = _pick(l, (ROW_TILE, 128))
    per = l // tr
    out_dtype = F32 if inverse else MXU_DTYPE

    def body(q_ref, k_ref, c_ref, s_ref, qo_ref, ko_ref):
        c, s = c_ref[...], s_ref[...]

        def rot(t):
            if inverse:
                return t * c + _partner(t * s)
            return t * c + _partner(t) * s

        for j in range(ATTN_WIDTH // LANES):
            qo_ref[:, j * LANES:(j + 1) * LANES] = rot(q_ref[:, j * LANES:(j + 1) * LANES]).astype(out_dtype)
        ko_ref[...] = rot(k_ref[...]).astype(out_dtype)

    kcol = 0 if inverse else ATTN_WIDTH // KV_WIDTH
    return pl.pallas_call(
        body, name=f"rope_{'inv' if inverse else 'fwd'}_{t_lat}", grid=(t_lat // tr,),
        out_shape=[SDS((t_lat, ATTN_WIDTH), out_dtype), SDS((t_lat, KV_WIDTH), out_dtype)],
        in_specs=[pl.BlockSpec((tr, ATTN_WIDTH), lambda i: (i, 0)),
                  pl.BlockSpec((tr, KV_WIDTH), lambda i: (i, kcol)),
                  pl.BlockSpec((tr, LANES), lambda i: (i % per, 0)),
                  pl.BlockSpec((tr, LANES), lambda i: (i % per, 0))],
        out_specs=[pl.BlockSpec((tr, ATTN_WIDTH), lambda i: (i, 0)), pl.BlockSpec((tr, KV_WIDTH), lambda i: (i, 0))],
    )(q_src, k_src, cos, sin)


_SMEM_SPEC = pl.BlockSpec(memory_space=pltpu.SMEM)
_KCOL = ATTN_WIDTH // KV_WIDTH
_VCOL = _KCOL + 1


def _win_specs(arr_cols, nb, col):
    del arr_cols
    return [pl.BlockSpec((Q_BLOCK, KV_WIDTH), lambda b, i, d=d: (b * nb + jnp.clip(i + d, 0, nb - 1), col))
            for d in (-1, 0, 1)]


def _win_mask(i, l):
    qpos = i * Q_BLOCK + lax.broadcasted_iota(jnp.int32, (Q_BLOCK, 3 * Q_BLOCK), 0)
    kpos = (i - 1) * Q_BLOCK + lax.broadcasted_iota(jnp.int32, (Q_BLOCK, 3 * Q_BLOCK), 1)
    ok = jnp.logical_and(jnp.abs(qpos - kpos) <= WINDOW, jnp.logical_and(kpos >= 0, kpos < l))
    return ok


def _attn_lat_fwd(qr, kr, p, sink, dims):
    b, l, lc = dims
    nb = l // Q_BLOCK
    t_lat = b * l
    cbase = t_lat // lc

    def body(sink_ref, q_ref, kp, kc, kn, vp, vc, vn, ck_ref, cv_ref, o_ref, lse_ref):
        i = pl.program_id(1)
        mask = _win_mask(i, l)
        kwin = _mx(jnp.concatenate([kp[...], kc[...], kn[...]], axis=0))
        vwin = _mx(jnp.concatenate([vp[...], vc[...], vn[...]], axis=0))
        groups = [(kwin, vwin, mask), (_mx(ck_ref[...]), _mx(cv_ref[...]), None)]
        lse_t = jnp.zeros((Q_BLOCK, LANES), F32)
        o_pair = None
        for pair, hq, off_q, off_k in _heads():
            qa = _mx(_align_in(q_ref[:, pair * LANES:(pair + 1) * LANES].astype(F32), off_q, off_k))
            o, lse_h = _head_fwd(qa, groups, sink_ref[0, hq])
            o = _align_out(o, off_k, off_q)
            o_pair = o if off_q == 0 else o_pair + o
            lse_t = _setcol(lse_t, hq, lse_h)
            if off_q:
                o_ref[:, pair * LANES:(pair + 1) * LANES] = o_pair
        lse_ref[...] = lse_t

    return pl.pallas_call(
        body, name=f"attn_lat_fwd_{t_lat}", grid=(b, nb),
        out_shape=[SDS((t_lat, ATTN_WIDTH), F32), SDS((t_lat, LANES), F32)],
        in_specs=[_SMEM_SPEC, pl.BlockSpec((Q_BLOCK, ATTN_WIDTH), lambda bb, i: (bb * nb + i, 0))]
        + _win_specs(None, nb, 0) + _win_specs(None, nb, _VCOL)
        + [pl.BlockSpec((lc, KV_WIDTH), lambda bb, i: (cbase + bb, _KCOL)),
           pl.BlockSpec((lc, KV_WIDTH), lambda bb, i: (cbase + bb, _VCOL))],
        out_specs=[pl.BlockSpec((Q_BLOCK, ATTN_WIDTH), lambda bb, i: (bb * nb + i, 0)),
                   pl.BlockSpec((Q_BLOCK, LANES), lambda bb, i: (bb * nb + i, 0))],
    )(sink, qr, kr, kr, kr, p, p, p, p, p)


def _attn_lat_bwd_dq(qr, kr, p, sink, o, lse, dout, dims):
    b, l, lc = dims
    nb = l // Q_BLOCK
    t_lat = b * l
    cbase = t_lat // lc

    def body(sink_ref, q_ref, kp, kc, kn, vp, vc, vn, ck_ref, cv_ref, o_ref, lse_ref, do_ref,
             dq_ref, delta_ref, dkc_ref, dvc_ref, dsk_ref):
        bb, i = pl.program_id(0), pl.program_id(1)

        @pl.when(i == 0)
        def _():
            dkc_ref[...] = jnp.zeros_like(dkc_ref)
            dvc_ref[...] = jnp.zeros_like(dvc_ref)

        @pl.when(jnp.logical_and(bb == 0, i == 0))
        def _():
            dsk_ref[...] = jnp.zeros_like(dsk_ref)

        mask = _win_mask(i, l)
        kwin = _mx(jnp.concatenate([kp[...], kc[...], kn[...]], axis=0))
        vwin = _mx(jnp.concatenate([vp[...], vc[...], vn[...]], axis=0))
        groups = [(kwin, vwin, mask), (_mx(ck_ref[...]), _mx(cv_ref[...]), None)]
        lse_t = lse_ref[...]
        delta_t = jnp.zeros((Q_BLOCK, LANES), F32)
        dsk = jnp.zeros((1, LANES), F32)
        dkc = jnp.zeros((lc, KV_WIDTH), F32)
        dvc = jnp.zeros((lc, KV_WIDTH), F32)
        dq_pair = None
        for pair, hq, off_q, off_k in _heads():
            sl = slice(pair * LANES, (pair + 1) * LANES)
            qa = _mx(_align_in(q_ref[:, sl].astype(F32), off_q, off_k))
            do_p = do_ref[:, sl]
            delta_h = jnp.sum(jnp.where(_hm(off_q), do_p * o_ref[:, sl], 0.0), axis=1, keepdims=True)
            doa = _mx(_align_in(do_p, off_q, off_k))
            sk = sink_ref[0, hq]
            dq, outs, p_s = _head_bwd(qa, doa, groups, sk, _col(lse_t, hq), delta_h)
            dq = _align_out(dq, off_k, off_q)
            dq_pair = dq if off_q == 0 else dq_pair + dq
            ds_c, p_c = outs[1]
            dkc = dkc + _dg_tn(ds_c, qa)
            dvc = dvc + _dg_tn(p_c, doa)
            dsk = dsk + jnp.where(_lane() == hq, jnp.sum(-p_s * delta_h, axis=0, keepdims=True), 0.0)
            delta_t = _setcol(delta_t, hq, delta_h)
            if off_q:
                dq_ref[:, sl] = dq_pair
        delta_ref[...] = delta_t
        dkc_ref[...] += dkc
        dvc_ref[...] += dvc
        dsk_ref[0:1, :] += dsk

    qspec = pl.BlockSpec((Q_BLOCK, ATTN_WIDTH), lambda bb, i: (bb * nb + i, 0))
    tspec = pl.BlockSpec((Q_BLOCK, LANES), lambda bb, i: (bb * nb + i, 0))
    cspec = pl.BlockSpec((lc, KV_WIDTH), lambda bb, i: (bb, 0))
    return pl.pallas_call(
        body, name=f"attn_lat_bwd_dq_{t_lat}", grid=(b, nb),
        out_shape=[SDS((t_lat, ATTN_WIDTH), F32), SDS((t_lat, LANES), F32), SDS((b * lc, KV_WIDTH), F32),
                   SDS((b * lc, KV_WIDTH), F32), SDS((8, LANES), F32)],
        in_specs=[_SMEM_SPEC, qspec] + _win_specs(None, nb, 0) + _win_specs(None, nb, _VCOL)
        + [pl.BlockSpec((lc, KV_WIDTH), lambda bb, i: (cbase + bb, _KCOL)),
           pl.BlockSpec((lc, KV_WIDTH), lambda bb, i: (cbase + bb, _VCOL)), qspec, tspec, qspec],
        out_specs=[qspec, tspec, cspec, cspec, pl.BlockSpec((8, LANES), lambda bb, i: (0, 0))],
    )(sink, qr, kr, kr, kr, p, p, p, p, p, o, lse, dout)


def _attn_lat_bwd_dkv(qr, kr, p, lse, delta, dout, dims):
    b, l, _ = dims
    nb = l // Q_BLOCK
    t_lat = b * l

    def body(k_ref, v_ref, *refs):
        j = pl.program_id(1)
        kj, vj = _mx(k_ref[...]), _mx(v_ref[...])
        dk = jnp.zeros((Q_BLOCK, KV_WIDTH), F32)
        dv = jnp.zeros((Q_BLOCK, KV_WIDTH), F32)
        for n, d in enumerate((-1, 0, 1)):
            q_ref, do_ref, lse_ref, delta_ref = refs[4 * n:4 * n + 4]
            i = j + d
            valid = jnp.logical_and(i >= 0, i < nb)
            qpos = i * Q_BLOCK + lax.broadcasted_iota(jnp.int32, (Q_BLOCK, Q_BLOCK), 0)
            kpos = j * Q_BLOCK + lax.broadcasted_iota(jnp.int32, (Q_BLOCK, Q_BLOCK), 1)
            mask = jnp.logical_and(jnp.abs(qpos - kpos) <= WINDOW, valid)
            lse_t, delta_t = lse_ref[...], delta_ref[...]
            for pair, hq, off_q, off_k in _heads():
                sl = slice(pair * LANES, (pair + 1) * LANES)
                qa = _mx(_align_in(q_ref[:, sl].astype(F32), off_q, off_k))
                doa = _mx(_align_in(do_ref[:, sl], off_q, off_k))
                _, outs, _ = _head_bwd(qa, doa, [(kj, vj, mask)], 0.0, _col(lse_t, hq), _col(delta_t, hq))
                ds, pp = outs[0]
                dk = dk + _dg_tn(ds, qa)
                dv = dv + _dg_tn(pp, doa)
        dk_ref, dv_ref = refs[12], refs[13]
        dk_ref[...] = dk
        dv_ref[...] = dv

    def blk(width, d, col=0):
        return pl.BlockSpec((Q_BLOCK, width), lambda bb, j: (bb * nb + jnp.clip(j + d, 0, nb - 1), col))

    in_specs = [blk(KV_WIDTH, 0), blk(KV_WIDTH, 0, _VCOL)]
    args = [kr, p]
    for d in (-1, 0, 1):
        in_specs += [blk(ATTN_WIDTH, d), blk(ATTN_WIDTH, d), blk(LANES, d), blk(LANES, d)]
        args += [qr, dout, lse, delta]
    return pl.pallas_call(
        body, name=f"attn_lat_bwd_dkv_{t_lat}", grid=(b, nb),
        out_shape=[SDS((t_lat, KV_WIDTH), F32), SDS((t_lat, KV_WIDTH), F32)],
        in_specs=in_specs, out_specs=[blk(KV_WIDTH, 0), blk(KV_WIDTH, 0)],
    )(*args)


def _attn_ctx_fwd(p, sink, dims):
    b, l, lc = dims
    cbase = b * l // lc

    def body(sink_ref, q_ref, k_ref, v_ref, o_ref, lse_ref):
        groups = [(_mx(k_ref[...]), _mx(v_ref[...]), None)]
        lse_t = jnp.zeros((lc, LANES), F32)
        o_pair = None
        for pair, hq, off_q, off_k in _heads():
            qa = _mx(_align_in(q_ref[:, pair * LANES:(pair + 1) * LANES], off_q, off_k))
            o, lse_h = _head_fwd(qa, groups, sink_ref[0, hq])
            o = _align_out(o, off_k, off_q)
            o_pair = o if off_q == 0 else o_pair + o
            lse_t = _setcol(lse_t, hq, lse_h)
            if off_q:
                o_ref[:, pair * LANES:(pair + 1) * LANES] = o_pair
        lse_ref[...] = lse_t

    return pl.pallas_call(
        body, name=f"attn_ctx_fwd_{b * lc}", grid=(b,),
        out_shape=[SDS((b * lc, ATTN_WIDTH), F32), SDS((b * lc, LANES), F32)],
        in_specs=[_SMEM_SPEC, pl.BlockSpec((lc, ATTN_WIDTH), lambda bb: (cbase + bb, 0)),
                  pl.BlockSpec((lc, KV_WIDTH), lambda bb: (cbase + bb, _KCOL)),
                  pl.BlockSpec((lc, KV_WIDTH), lambda bb: (cbase + bb, _VCOL))],
        out_specs=[pl.BlockSpec((lc, ATTN_WIDTH), lambda bb: (bb, 0)), pl.BlockSpec((lc, LANES), lambda bb: (bb, 0))],
    )(sink, p, p, p)


def _attn_ctx_bwd(p, sink, o, lse, dout, dims):
    b, l, lc = dims
    cbase = b * l // lc

    def body(sink_ref, q_ref, k_ref, v_ref, o_ref, lse_ref, do_ref, dq_ref, dk_ref, dv_ref, dsk_ref):
        bb = pl.program_id(0)

        @pl.when(bb == 0)
        def _():
            dsk_ref[...] = jnp.zeros_like(dsk_ref)

        groups = [(_mx(k_ref[...]), _mx(v_ref[...]), None)]
        lse_t = lse_ref[...]
        dsk = jnp.zeros((1, LANES), F32)
        dk = jnp.zeros((lc, KV_WIDTH), F32)
        dv = jnp.zeros((lc, KV_WIDTH), F32)
        dq_pair = None
        for pair, hq, off_q, off_k in _heads():
            sl = slice(pair * LANES, (pair + 1) * LANES)
            qa = _mx(_align_in(q_ref[:, sl], off_q, off_k))
            do_p = do_ref[:, sl]
            delta_h = jnp.sum(jnp.where(_hm(off_q), do_p * o_ref[:, sl], 0.0), axis=1, keepdims=True)
            doa = _mx(_align_in(do_p, off_q, off_k))
            dq, outs, p_s = _head_bwd(qa, doa, groups, sink_ref[0, hq], _col(lse_t, hq), delta_h)
            dq = _align_out(dq, off_k, off_q)
            dq_pair = dq if off_q == 0 else dq_pair + dq
            ds, pp = outs[0]
            dk = dk + _dg_tn(ds, qa)
            dv = dv + _dg_tn(pp, doa)
            dsk = dsk + jnp.where(_lane() == hq, jnp.sum(-p_s * delta_h, axis=0, keepdims=True), 0.0)
            if off_q:
                dq_ref[:, sl] = dq_pair
        dk_ref[...] = dk
        dv_ref[...] = dv
        dsk_ref[0:1, :] += dsk

    qs = pl.BlockSpec((lc, ATTN_WIDTH), lambda bb: (bb, 0))
    ks = pl.BlockSpec((lc, KV_WIDTH), lambda bb: (bb, 0))
    return pl.pallas_call(
        body, name=f"attn_ctx_bwd_{b * lc}", grid=(b,),
        out_shape=[SDS((b * lc, ATTN_WIDTH), F32), SDS((b * lc, KV_WIDTH), F32), SDS((b * lc, KV_WIDTH), F32),
                   SDS((8, LANES), F32)],
        in_specs=[_SMEM_SPEC, pl.BlockSpec((lc, ATTN_WIDTH), lambda bb: (cbase + bb, 0)),
                  pl.BlockSpec((lc, KV_WIDTH), lambda bb: (cbase + bb, _KCOL)),
                  pl.BlockSpec((lc, KV_WIDTH), lambda bb: (cbase + bb, _VCOL)),
                  qs, pl.BlockSpec((lc, LANES), lambda bb: (bb, 0)),
                  pl.BlockSpec((lc, ATTN_WIDTH), lambda bb: (cbase + bb, 0))],
        out_specs=[qs, ks, ks, pl.BlockSpec((8, LANES), lambda bb: (0, 0))],
    )(sink, p, p, p, o, lse, dout)


_CONV_COL = (ATTN_WIDTH + 2 * KV_WIDTH) // CONV_WIDTH


def _shift_prev(z, n):
    rows = lax.broadcasted_iota(jnp.int32, z.shape, 0)
    return jnp.where(rows == 0, 0.0, pltpu.roll(z, 1, 0))


def _shift_next(z, n):
    rows = lax.broadcasted_iota(jnp.int32, z.shape, 0)
    return jnp.where(rows == n - 1, 0.0, pltpu.roll(z, n - 1, 0))


def _conv_fwd(p, w, base, n_seq, ls):
    def body(cb_ref, cc_ref, cx_ref, w_ref, o_ref):
        z = cc_ref[...] * cx_ref[...]
        c3 = _shift_prev(z, ls) * w_ref[0:1, :] + z * w_ref[1:2, :] + _shift_next(z, ls) * w_ref[2:3, :]
        o_ref[...] = cb_ref[...] * c3

    return pl.pallas_call(
        body, name=f"conv_fwd_{n_seq}x{ls}", grid=(n_seq,),
        out_shape=SDS((n_seq * ls, CONV_WIDTH), F32),
        in_specs=[pl.BlockSpec((ls, CONV_WIDTH), lambda s, c=c: (base + s, _CONV_COL + c)) for c in range(3)]
        + [pl.BlockSpec(w.shape, lambda s: (0, 0))],
        out_specs=pl.BlockSpec((ls, CONV_WIDTH), lambda s: (s, 0)),
    )(p, p, p, w)


def _conv_bwd(p, w, dout, base, n_seq, ls):
    dcol = ATTN_WIDTH // CONV_WIDTH

    def body(cb_ref, cc_ref, cx_ref, w_ref, do_ref, dcb_ref, dcc_ref, dcx_ref, dw_ref):
        @pl.when(pl.program_id(0) == 0)
        def _():
            dw_ref[...] = jnp.zeros_like(dw_ref)

        cc, cx = cc_ref[...], cx_ref[...]
        z = cc * cx
        zp, zn = _shift_prev(z, ls), _shift_next(z, ls)
        c3 = zp * w_ref[0:1, :] + z * w_ref[1:2, :] + zn * w_ref[2:3, :]
        do = do_ref[...]
        dcb_ref[...] = do * c3
        e = do * cb_ref[...]
        dz = _shift_next(e, ls) * w_ref[0:1, :] + e * w_ref[1:2, :] + _shift_prev(e, ls) * w_ref[2:3, :]
        dcc_ref[...] = dz * cx
        dcx_ref[...] = dz * cc
        dw_ref[0:1, :] += jnp.sum(e * zp, axis=0, keepdims=True)
        dw_ref[1:2, :] += jnp.sum(e * z, axis=0, keepdims=True)
        dw_ref[2:3, :] += jnp.sum(e * zn, axis=0, keepdims=True)

    ospec = pl.BlockSpec((ls, CONV_WIDTH), lambda s: (s, 0))
    return pl.pallas_call(
        body, name=f"conv_bwd_{n_seq}x{ls}", grid=(n_seq,),
        out_shape=[SDS((n_seq * ls, CONV_WIDTH), F32)] * 3 + [SDS((8, CONV_WIDTH), F32)],
        in_specs=[pl.BlockSpec((ls, CONV_WIDTH), lambda s, c=c: (base + s, _CONV_COL + c)) for c in range(3)]
        + [pl.BlockSpec(w.shape, lambda s: (0, 0)), pl.BlockSpec((ls, CONV_WIDTH), lambda s: (base + s, dcol))],
        out_specs=[ospec, ospec, ospec, pl.BlockSpec((8, CONV_WIDTH), lambda s: (0, 0))],
    )(p, p, p, w, dout)


def _make_att_conv(dims, cos, sin):
    b, l, lc = dims
    t_lat = b * l

    def forward(p, conv_w, sink):
        qr, kr = _rope_call(p, p, cos, sin, dims, inverse=False)
        o_lat, lse_lat = _attn_lat_fwd(qr, kr, p, sink, dims)
        o_ctx, lse_ctx = _attn_ctx_fwd(p, sink, dims)
        conv_lat = _conv_fwd(p, conv_w, 0, b, l)
        conv_ctx = _conv_fwd(p, conv_w, t_lat // lc, b, lc)
        out = jnp.concatenate([jnp.concatenate([o_lat, conv_lat], axis=1),
                               jnp.concatenate([o_ctx, conv_ctx], axis=1)], axis=0)
        return out, (p, conv_w, sink, qr, kr, o_lat, lse_lat, o_ctx, lse_ctx)

    @jax.custom_vjp
    def op(p, conv_w, sink):
        return forward(p, conv_w, sink)[0]

    def op_bwd(res, dout):
        p, conv_w, sink, qr, kr, o_lat, lse_lat, o_ctx, lse_ctx = res
        dqr, delta, dkc1, dvc1, dsk1 = _attn_lat_bwd_dq(qr, kr, p, sink, o_lat, lse_lat, dout, dims)
        dkr, dv = _attn_lat_bwd_dkv(qr, kr, p, lse_lat, delta, dout, dims)
        dq, dk = _rope_call(dqr, dkr, cos, sin, dims, inverse=True)
        dqc, dkc2, dvc2, dsk2 = _attn_ctx_bwd(p, sink, o_ctx, lse_ctx, dout, dims)
        dcb_l, dcc_l, dcx_l, dw_l = _conv_bwd(p, conv_w, dout, 0, b, l)
        dcb_c, dcc_c, dcx_c, dw_c = _conv_bwd(p, conv_w, dout, t_lat // lc, b, lc)
        zeros_u = jnp.zeros((p.shape[0], SSM_WIDTH), F32)
        lat = jnp.concatenate([dq, dk, dv, dcb_l, dcc_l, dcx_l], axis=1)
        ctx = jnp.concatenate([dqc, dkc1 + dkc2, dvc1 + dvc2, dcb_c, dcc_c, dcx_c], axis=1)
        dp = jnp.concatenate([jnp.concatenate([lat, ctx], axis=0), zeros_u], axis=1)
        dsink = (dsk1 + dsk2)[0:1, :N_Q_HEADS]
        return dp, (dw_l + dw_c)[:3], dsink

    op.defvjp(forward, op_bwd)
    return op


def _scan_call(x, lam_r, lam_i, dims, ctx_first, dec, h=None):
    b, l, lc = dims
    n = SSM_N
    tc = _pick(lc, (256, 128, 64, 32, 16, 8))
    nc_c, nc_l = lc // tc, l // tc
    n_chunks = nc_c + nc_l
    ctx_base = b * l // tc
    g8 = tc // 8
    with_h = h is not None

    def chunk(bb, j):
        if ctx_first:
            is_ctx = j < nc_c
            jj = jnp.where(is_ctx, j, j - nc_c)
        else:
            is_ctx = j >= nc_l
            jj = jnp.where(is_ctx, j - nc_l, j)
        ic = nc_c - 1 - jj if dec else jj
        il = nc_l - 1 - jj if dec else jj
        return jnp.where(is_ctx, ctx_base + bb * nc_c + ic, bb * nc_l + il)

    def edge(bb, j):
        jn = jnp.minimum(j + 1, n_chunks - 1)
        return chunk(bb, jn) * g8 + (g8 - 1 if dec else 0)

    def body(x_ref, lr_ref, li_ref, *rest):
        rest = list(rest)
        h_ref, hb_ref = (rest.pop(0), rest.pop(0)) if with_h else (None, None)
        o_ref = rest.pop(0)
        dl_ref = rest.pop(0) if with_h else None
        carry = rest.pop(0)
        bb, j = pl.program_id(0), pl.program_id(1)

        @pl.when(j == 0)
        def _():
            carry[...] = jnp.zeros_like(carry)

        if with_h:
            @pl.when(jnp.logical_and(bb == 0, j == 0))
            def _():
                dl_ref[...] = jnp.zeros_like(dl_ref)

        rows = lax.broadcasted_iota(jnp.int32, (tc, LANES), 0)
        first = rows == (tc - 1 if dec else 0)
        has_next = j + 1 < n_chunks
        for s in range(n // LANES):
            re, im = slice(s * LANES, (s + 1) * LANES), slice(n + s * LANES, n + (s + 1) * LANES)
            ar, ai = lr_ref[:, re], li_ref[:, re]
            cr, ci = carry[0:1, re], carry[0:1, im]
            dr = x_ref[:, re] + jnp.where(first, ar * cr - ai * ci, 0.0)
            di = x_ref[:, im] + jnp.where(first, ar * ci + ai * cr, 0.0)
            pr, pi = ar, ai
            sft = 1
            while sft < tc:
                if dec:
                    keep = rows < tc - sft
                    sr = jnp.where(keep, pltpu.roll(dr, tc - sft, 0), 0.0)
                    si = jnp.where(keep, pltpu.roll(di, tc - sft, 0), 0.0)
                else:
                    keep = rows >= sft
                    sr = jnp.where(keep, pltpu.roll(dr, sft, 0), 0.0)
                    si = jnp.where(keep, pltpu.roll(di, sft, 0), 0.0)
                dr, di = dr + pr * sr - pi * si, di + pr * si + pi * sr
                pr, pi = pr * pr - pi * pi, 2.0 * pr * pi
                sft *= 2
            o_ref[:, re] = dr
            o_ref[:, im] = di
            last = 0 if dec else tc - 1
            carry[0:1, re] = o_ref[last:last + 1, re]
            carry[0:1, im] = o_ref[last:last + 1, im]
            if with_h:
                hr, hi = h_ref[:, re], h_ref[:, im]
                er = 7 if dec else 0
                br = jnp.where(has_next, hb_ref[er:er + 1, re], 0.0)
                bi = jnp.where(has_next, hb_ref[er:er + 1, im], 0.0)
                if dec:
                    nr = jnp.where(rows == 0, br, pltpu.roll(hr, 1, 0))
                    ni = jnp.where(rows == 0, bi, pltpu.roll(hi, 1, 0))
                else:
                    nr = jnp.where(rows == tc - 1, br, pltpu.roll(hr, tc - 1, 0))
                    ni = jnp.where(rows == tc - 1, bi, pltpu.roll(hi, tc - 1, 0))
                dl_ref[0:1, re] += jnp.sum(dr * nr + di * ni, axis=0, keepdims=True)
                dl_ref[0:1, im] += jnp.sum(di * nr - dr * ni, axis=0, keepdims=True)

    xspec = pl.BlockSpec((tc, 2 * n), lambda bb, j: (chunk(bb, j), 0))
    lspec = pl.BlockSpec((1, n), lambda bb, j: (0, 0))
    in_specs, args = [xspec, lspec, lspec], [x, lam_r, lam_i]
    out_shape, out_specs = [SDS(x.shape, F32)], [xspec]
    if with_h:
        in_specs += [xspec, pl.BlockSpec((8, 2 * n), lambda bb, j: (edge(bb, j), 0))]
        args += [h, h]
        out_shape.append(SDS((8, 2 * n), F32))
        out_specs.append(pl.BlockSpec((8, 2 * n), lambda bb, j: (0, 0)))
    tag = ('c' if ctx_first else 'l') + ('d' if dec else 'u') + ('h' if with_h else '')
    res = pl.pallas_call(
        body, name=f"scan_{tag}_{x.shape[0]}", grid=(b, n_chunks),
        out_shape=out_shape, in_specs=in_specs, out_specs=out_specs,
        scratch_shapes=[pltpu.VMEM((8, 2 * n), F32)],
    )(*args)
    return res


def _make_scan(dims, direction):
    @jax.custom_vjp
    def scan(x, lam_r, lam_i):
        return _scan_call(x, lam_r, lam_i, dims, True, direction == 1)[0]

    def scan_fwd(x, lam_r, lam_i):
        h = _scan_call(x, lam_r, lam_i, dims, True, direction == 1)[0]
        return h, (h, lam_r, lam_i)

    def scan_bwd(res, g):
        h, lam_r, lam_i = res
        gd, dl = _scan_call(g, lam_r, -lam_i, dims, False, direction == 0, h=h)
        return gd, dl[0:1, :SSM_N], dl[0:1, SSM_N:]

    scan.defvjp(scan_fwd, scan_bwd)
    return scan


def _block_diag(m):
    g, a, bdim = m.shape
    eye = jnp.eye(g, dtype=m.dtype)
    full = m[:, :, None, :] * eye[:, None, :, None]
    return full.reshape(g * a, g * bdim)


def _ssm_mats(lam_re, lam_im, log_dt, b_re, b_im, c_re, c_im):
    lam = lax.complex(lam_re, lam_im)
    dt = jnp.exp(log_dt)[..., None]
    lam_bar = jnp.exp(lam * dt)
    b_bar = ((lam_bar - 1) / lam)[..., None] * lax.complex(b_re, b_im)
    out = []
    for d in range(2):
        bt = jnp.swapaxes(b_bar[d], 1, 2)
        bmat = jnp.concatenate([_block_diag(jnp.real(bt)), _block_diag(jnp.imag(bt))], axis=1)
        ct_re = jnp.swapaxes(c_re[d], 1, 2)
        ct_im = jnp.swapaxes(c_im[d], 1, 2)
        cmat = jnp.concatenate([_block_diag(ct_re), _block_diag(-ct_im)], axis=0)
        out.append((bmat, cmat, jnp.real(lam_bar[d]).reshape(1, SSM_N), jnp.imag(lam_bar[d]).reshape(1, SSM_N)))
    return out


def _loss_call(y, target):
    t, d = y.shape
    tr = _pick(t, (ROW_TILE, 128, 64, 32, 16, 8))

    def body(y_ref, t_ref, acc_ref, dy_ref):
        @pl.when(pl.program_id(0) == 0)
        def _():
            acc_ref[...] = jnp.zeros_like(acc_ref)

        diff = y_ref[...] - t_ref[...]
        dy_ref[...] = diff * (1.0 / d)
        acc_ref[0:1, :] += jnp.sum(diff * diff, axis=0, keepdims=True)

    spec = pl.BlockSpec((tr, d), lambda i: (i, 0))
    return pl.pallas_call(
        body, name=f"loss_{t}", grid=(t // tr,),
        out_shape=[SDS((8, d), F32), SDS((t, d), F32)],
        in_specs=[spec, spec], out_specs=[pl.BlockSpec((8, d), lambda i: (0, 0)), spec],
    )(y, target)


def _adamw_call(w, g, m, v):
    r, c = w.shape
    tr = _pick(r, (1024, 512, 256, 128, 64, 32, 16, 8))

    def body(w_ref, g_ref, m_ref, v_ref, d_ref, mo_ref, vo_ref):
        gv = g_ref[...]
        mn = ADAM_B1 * m_ref[...] + (1.0 - ADAM_B1) * gv
        vn = ADAM_B2 * v_ref[...] + (1.0 - ADAM_B2) * (gv * gv)
        m_hat = mn / (1.0 - ADAM_B1 ** ADAM_STEP)
        v_hat = vn / (1.0 - ADAM_B2 ** ADAM_STEP)
        d_ref[...] = -ADAM_LR * (m_hat / (jnp.sqrt(v_hat) + ADAM_EPS) + ADAM_WD * w_ref[...])
        mo_ref[...] = mn
        vo_ref[...] = vn

    spec = pl.BlockSpec((tr, c), lambda i: (i, 0))
    return pl.pallas_call(
        body, name=f"adamw_{r}", grid=(r // tr,),
        out_shape=[SDS((r, c), F32)] * 3, in_specs=[spec] * 4, out_specs=[spec] * 3,
    )(w, g, m, v)


def _forward(diff, gathered, dims, depth):
    b, l, lc = dims
    t_lat = b * l
    d_model = diff['x'].shape[-1]
    cos, sin = _rope_tables(l)
    norm_mod = _make_rowwise_op("norm_mod", _norm_mod_f, d_model, 1, 2, l)
    resid_gate = _make_rowwise_op("resid_gate", _resid_gate_f, d_model, 2, 1, l)
    glu = _make_rowwise_op("glu", _glu_f, SSM_WIDTH, 3, 0, l)
    att_conv = _make_att_conv(dims, cos, sin)
    scans = [_make_scan(dims, 0), _make_scan(dims, 1)]

    c_act = jax.nn.silu(diff['c'])
    c_ctx_act = jax.nn.silu(diff['c_ctx'])
    mod_in = jnp.concatenate([c_act, c_ctx_act[None, :], jnp.zeros((MOD_ROWS - b - 1, d_model), F32)], axis=0)
    h = jnp.concatenate([diff['x'].reshape(t_lat, d_model), diff['ctx'].reshape(b * lc, d_model)], axis=0)

    for layer in range(depth):
        last = layer == depth - 1
        mod = linear_d(mod_in, gathered['w_ada'][layer], diff['w_ada'][layer])[:b + 1] + diff['b_ada'][layer][None, :]
        sh1, sc1, g1, sh2, sc2, g2 = [m.reshape(b + 1, 1, d_model) for m in jnp.split(mod, N_MOD, axis=-1)]
        ng = diff['norm_g'][layer]
        a = norm_mod(h, sh1, sc1, ng[0:1])
        p = linear_d(a, gathered['w_in'][layer], diff['w_in'][layer])
        ac = att_conv(p, diff['conv_w'][layer], diff['attn_sink'][layer][None, :])
        u = p[:, IN_WIDTH - SSM_WIDTH:]
        mats = _ssm_mats(diff['ssm_lam_re'][layer], diff['ssm_lam_im'][layer], diff['ssm_log_dt'][layer],
                         diff['ssm_b_re'][layer], diff['ssm_b_im'][layer], diff['ssm_c_re'][layer],
                         diff['ssm_c_im'][layer])
        ys = []
        for d in range(2):
            bmat, cmat, lr, li = mats[d]
            hs = scans[d](linear(u, bmat), lr, li)
            ys.append(linear(hs, cmat))
        s = glu(ys[0], ys[1], u, diff['ssm_d'][layer][None, :], diff['w_glu'][layer], diff['b_glu'][layer][None, :])
        mix = jnp.concatenate([ac, s], axis=1)
        if last:
            mix, h = mix[:t_lat], h[:t_lat]
            g1, sh2, sc2, g2 = g1[:b], sh2[:b], sc2[:b], g2[:b]
        m = linear_d(mix, gathered['w_out'][layer], diff['w_out'][layer])
        h = resid_gate(h, m, g1, ng[1:2])
        a2 = norm_mod(h, sh2, sc2, ng[2:3])
        f = mlp(a2, gathered['w_mlp_in'][layer], diff['w_mlp_in'][layer],
                gathered['w_mlp_out'][layer], diff['w_mlp_out'][layer])
        h = resid_gate(h, f, g2, ng[3:4])
    return h


def kernel(x, c, ctx, c_ctx, w_ada, b_ada, norm_g, w_in, conv_w, attn_sink, ssm_lam_re, ssm_lam_im, ssm_log_dt, ssm_b_re, ssm_b_im, ssm_c_re, ssm_c_im, ssm_d, w_glu, b_glu, w_out, w_mlp_in, w_mlp_out, loss_target, m_c_ctx, m_w_ada, m_b_ada, m_norm_g, m_w_in, m_conv_w, m_attn_sink, m_ssm_lam_re, m_ssm_lam_im, m_ssm_log_dt, m_ssm_b_re, m_ssm_b_im, m_ssm_c_re, m_ssm_c_im, m_ssm_d, m_w_glu, m_b_glu, m_w_out, m_w_mlp_in, m_w_mlp_out, v_c_ctx, v_w_ada, v_b_ada, v_norm_g, v_w_in, v_conv_w, v_attn_sink, v_ssm_lam_re, v_ssm_lam_im, v_ssm_log_dt, v_ssm_b_re, v_ssm_b_im, v_ssm_c_re, v_ssm_c_im, v_ssm_d, v_w_glu, v_b_glu, v_w_out, v_w_mlp_in, v_w_mlp_out):
    given = dict(locals())
    weights = {n: given[n] for n in WEIGHTS}
    moms = {n: given['m_' + n] for n in WEIGHTS}
    vars_ = {n: given['v_' + n] for n in WEIGHTS}
    b, l, d_model = x.shape
    lc = ctx.shape[1]
    dims = (b, l, lc)
    depth = w_ada.shape[0]

    big_shapes = [weights[n].shape for n in BIG]
    small_shapes = [weights[n].shape for n in SMALL]
    big_all = _all_gather(_pack([weights[n] for n in BIG], MXU_DTYPE, 16))
    small_all = _all_gather(_pack([weights[n] for n in SMALL], F32, 8))
    gathered = {n: _from_pieces(pc, SHARD_AXIS[n]) for n, pc in zip(BIG, _unpack(big_all, big_shapes, (N_DEV,)))}
    small_full = {n: _from_pieces(pc, SHARD_AXIS[n])
                  for n, pc in zip(SMALL, _unpack(small_all, small_shapes, (N_DEV,)))}

    diff = {'x': x, 'c': c, 'ctx': ctx}
    for n in REPLICATED:
        diff[n] = weights[n]
    for n in SMALL:
        diff[n] = small_full[n]
    for n in BIG:
        diff[n] = jnp.zeros(gathered[n].shape, F32)
    y, vjp = jax.vjp(lambda dd: _forward(dd, gathered, dims, depth), diff)
    sq, dy = _loss_call(y, loss_target.reshape(b * l, d_model))
    loss = lax.psum(0.5 * jnp.sum(sq) / d_model, ('x', 'y', 'c'))
    grads = vjp(dy)[0]
    grad_x = grads['x']

    shard_shapes = [weights[n].shape for n in SHARDED]
    pieces = _pack8([_to_pieces(grads[n], SHARD_AXIS[n]) for n in SHARDED], 1024)
    g_sh = _sum_slots(_all_to_all(pieces))
    rep_shapes = [weights[n].shape for n in REPLICATED]
    g_rep = _sum_slots(_all_gather(_pack([grads[n] for n in REPLICATED], F32, 8)))

    out = {}
    for names, shapes, g_mat, mult in ((SHARDED, shard_shapes, g_sh, 1024), (REPLICATED, rep_shapes, g_rep, 8)):
        w_mat = _pack([weights[n] for n in names], F32, mult)
        m_mat = _pack([moms[n] for n in names], F32, mult)
        v_mat = _pack([vars_[n] for n in names], F32, mult)
        d_mat, mn_mat, vn_mat = _adamw_call(w_mat, g_mat, m_mat, v_mat)
        for kind, mat in (('grad', g_mat), ('delta', d_mat), ('new_m', mn_mat), ('new_v', vn_mat)):
            for n, arr in zip(names, _unpack(mat, shapes)):
                out[(kind, n)] = arr
    return (loss, grad_x, *[out[(kind, n)] for kind in ('grad', 'delta', 'new_m', 'new_v') for n in WEIGHTS])
```

```python
import functools

import jax
import jax.numpy as jnp
from jax import lax
from jax.experimental import pallas as pl
from jax.experimental.pallas import tpu as pltpu

F32 = jnp.float32
MXU_DTYPE = jnp.bfloat16
SDS = jax.ShapeDtypeStruct

N_DEV = 8
HEAD_DIM = 64
N_Q_HEADS = 8
Q_PER_KV = 4
ATTN_WIDTH = 512
KV_WIDTH = 128
WINDOW = 128
Q_BLOCK = 128
GRID_W = 64
ROPE_BASE = 10000.0
ROPE_PAIRS = 16
CONV_WIDTH = 256
SSM_WIDTH = 256
SSM_GROUP = 16
SSM_GROUPS = 16
SSM_STATE = 64
SSM_N = SSM_GROUPS * SSM_STATE
IN_WIDTH = 1792
N_MOD = 6
EPS = 1e-6
NEG_INF = -1e30
ADAM_LR = 0.001
ADAM_B1 = 0.9
ADAM_B2 = 0.999
ADAM_EPS = 1e-08
ADAM_WD = 0.01
ADAM_STEP = 10
MOD_ROWS = 128
ROW_TILE = 256
LANES = 128
MESH = pl.DeviceIdType.MESH
_SCALE = HEAD_DIM ** -0.5
MIX_SPLIT = ATTN_WIDTH + CONV_WIDTH
MM_TILE = 1024
MM_VMEM_BYTES = 56 * 1024 * 1024

WEIGHTS = ['c_ctx', 'w_ada', 'b_ada', 'norm_g', 'w_in', 'conv_w', 'attn_sink', 'ssm_lam_re', 'ssm_lam_im',
           'ssm_log_dt', 'ssm_b_re', 'ssm_b_im', 'ssm_c_re', 'ssm_c_im', 'ssm_d', 'w_glu', 'b_glu', 'w_out',
           'w_mlp_in', 'w_mlp_out']
SHARD_AXIS = {'w_ada': 2, 'norm_g': 2, 'w_in': 2, 'conv_w': 2, 'w_glu': 1, 'w_out': 1, 'w_mlp_in': 2, 'w_mlp_out': 1}
BIG = ['w_ada', 'w_in', 'w_out', 'w_mlp_in', 'w_mlp_out']
SMALL = ['norm_g', 'conv_w', 'w_glu']
SHARDED = BIG + SMALL
REPLICATED = [n for n in WEIGHTS if n not in SHARD_AXIS]


def _pick(n, cands):
    for c in cands:
        if n % c == 0:
            return c
    return n


def _div_tile(n, cap):
    if n <= cap:
        return n
    for c in range(cap, LANES - 1, -LANES):
        if n % c == 0:
            return c
    return n


def _mx(x):
    return x.astype(MXU_DTYPE)


def _dg_nt(a, b):
    return lax.dot_general(a, b, (((1,), (1,)), ((), ())), preferred_element_type=F32)


def _dg_tn(a, b):
    return lax.dot_general(a, b, (((0,), (0,)), ((), ())), preferred_element_type=F32)


def _all_gather(x2d):
    r, cdim = x2d.shape

    def body(x_ref, out_ref, send_sems, recv_sems, local_sem):
        x, y, c = lax.axis_index("x"), lax.axis_index("y"), lax.axis_index("c")
        me, sibling = (x, y, c), (x, y, 1 - c)
        chips = [(1 - x, y), (x, 1 - y), (1 - x, 1 - y)]

        def slot(px, py, pc):
            return out_ref.at[4 * px + 2 * py + pc]

        def copy(k, block, to, src=None):
            return pltpu.make_async_remote_copy(
                src_ref=slot(*block) if src is None else src, dst_ref=slot(*block),
                send_sem=send_sems.at[k], recv_sem=recv_sems.at[k], device_id=to, device_id_type=MESH)

        mine = pltpu.make_async_copy(x_ref, slot(*me), local_sem)
        mine.start()
        first = [copy(0, me, sibling, src=x_ref)]
        first += [copy(1 + j, me, (*chip, c), src=x_ref) for j, chip in enumerate(chips)]
        for cp in first:
            cp.start()
        passed = [copy(4 + j, (*chip, c), sibling) for j, chip in enumerate(chips)]
        for j, chip in enumerate(chips):
            copy(1 + j, (*chip, c), me).wait_recv()
            passed[j].start()
        copy(0, sibling, me).wait_recv()
        for j, chip in enumerate(chips):
            copy(4 + j, (*chip, 1 - c), me).wait_recv()
        for cp in first + passed:
            cp.wait_send()
        mine.wait()

    return pl.pallas_call(
        body, name=f"all_gather_{r}x{cdim}_{jnp.dtype(x2d.dtype).name}",
        out_shape=SDS((N_DEV, r, cdim), x2d.dtype),
        in_specs=[pl.BlockSpec(memory_space=pl.ANY)],
        out_specs=pl.BlockSpec(memory_space=pl.ANY),
        scratch_shapes=[pltpu.SemaphoreType.DMA((7,)), pltpu.SemaphoreType.DMA((7,)), pltpu.SemaphoreType.DMA],
    )(x2d)


def _rs_sibling(g8):
    _, r, cdim = g8.shape

    def body(g_ref, out_ref, send_sems, recv_sems):
        x, y, c = lax.axis_index("x"), lax.axis_index("y"), lax.axis_index("c")
        copies = [pltpu.make_async_remote_copy(
            src_ref=g_ref.at[2 * k + (1 - c)], dst_ref=out_ref.at[k], send_sem=send_sems.at[k],
            recv_sem=recv_sems.at[k], device_id=(x, y, 1 - c), device_id_type=MESH) for k in range(4)]
        for cp in copies:
            cp.start()
        for cp in copies:
            cp.wait()

    return pl.pallas_call(
        body, name=f"rs_sibling_{r}x{cdim}",
        out_shape=SDS((4, r, cdim), g8.dtype),
        in_specs=[pl.BlockSpec(memory_space=pl.ANY)],
        out_specs=pl.BlockSpec(memory_space=pl.ANY),
        scratch_shapes=[pltpu.SemaphoreType.DMA((4,)), pltpu.SemaphoreType.DMA((4,))],
    )(g8)


def _rs_add(g8, sib, c_idx):
    _, r, cdim = g8.shape
    tr = _pick(r, (256, 128, 64, 32, 16))

    def body(c_ref, g_ref, s_ref, o_ref):
        del c_ref
        o_ref[...] = (g_ref[...] + s_ref[...]).astype(o_ref.dtype)

    return pl.pallas_call(
        body, name=f"rs_add_{r}x{cdim}",
        grid_spec=pltpu.PrefetchScalarGridSpec(
            num_scalar_prefetch=1, grid=(4, r // tr),
            in_specs=[pl.BlockSpec((1, tr, cdim), lambda k, i, c: (2 * k + c[0], i, 0)),
                      pl.BlockSpec((1, tr, cdim), lambda k, i, c: (k, i, 0))],
            out_specs=pl.BlockSpec((1, tr, cdim), lambda k, i, c: (k, i, 0))),
        out_shape=SDS((4, r, cdim), MXU_DTYPE),
    )(c_idx, g8, sib)


def _rs_chips(s4):
    _, r, cdim = s4.shape

    def body(s_ref, out_ref, send_sems, recv_sems, local_sem):
        x, y, c = lax.axis_index("x"), lax.axis_index("y"), lax.axis_index("c")
        me = 2 * x + y

        def peer(j):
            px = 1 - x if j & 2 else x
            py = 1 - y if j & 1 else y
            return (px, py, c), 2 * px + py

        def copy(j, landing):
            to, to_chip = peer(j)
            return pltpu.make_async_remote_copy(
                src_ref=s_ref.at[to_chip], dst_ref=out_ref.at[to_chip if landing else me],
                send_sem=send_sems.at[j - 1], recv_sem=recv_sems.at[j - 1], device_id=to, device_id_type=MESH)

        mine = pltpu.make_async_copy(s_ref.at[me], out_ref.at[me], local_sem)
        mine.start()
        sends = [copy(j, False) for j in range(1, 4)]
        for cp in sends:
            cp.start()
        for j in range(1, 4):
            copy(j, True).wait_recv()
        for cp in sends:
            cp.wait_send()
        mine.wait()

    return pl.pallas_call(
        body, name=f"rs_chips_{r}x{cdim}",
        out_shape=SDS(s4.shape, s4.dtype),
        in_specs=[pl.BlockSpec(memory_space=pl.ANY)],
        out_specs=pl.BlockSpec(memory_space=pl.ANY),
        scratch_shapes=[pltpu.SemaphoreType.DMA((3,)), pltpu.SemaphoreType.DMA((3,)), pltpu.SemaphoreType.DMA],
    )(s4)


def _sum_slots(g3):
    n, r, cdim = g3.shape
    tr = _pick(r, (256, 128, 64, 32, 16, 8))

    def body(g_ref, o_ref):
        acc = g_ref[0].astype(F32)
        for s in range(1, n):
            acc = acc + g_ref[s].astype(F32)
        o_ref[...] = acc

    return pl.pallas_call(
        body, name=f"sum_slots_{n}x{r}x{cdim}", grid=(r // tr,),
        out_shape=SDS((r, cdim), F32),
        in_specs=[pl.BlockSpec((n, tr, cdim), lambda i: (0, i, 0))],
        out_specs=pl.BlockSpec((tr, cdim), lambda i: (i, 0)),
    )(g3)


def _pack(arrs, dtype, row_mult):
    flat = jnp.concatenate([a.reshape(-1).astype(dtype) for a in arrs])
    pad = -flat.size % (LANES * row_mult)
    return jnp.pad(flat, (0, pad)).reshape(-1, LANES)


def _unpack(mat, shapes, lead=()):
    flat = mat.reshape(lead + (-1,))
    out, off = [], 0
    for s in shapes:
        n = 1
        for d in s:
            n *= d
        out.append(flat[..., off:off + n].reshape(lead + tuple(s)))
        off += n
    return out


def _to_pieces(g, axis):
    s = g.shape
    g = g.reshape(s[:axis] + (N_DEV, s[axis] // N_DEV) + s[axis + 1:])
    return jnp.moveaxis(g, axis, 0)


def _from_pieces(p, axis):
    p = jnp.moveaxis(p, 0, axis)
    s = p.shape
    return p.reshape(s[:axis] + (s[axis] * s[axis + 1],) + s[axis + 2:])


def _mm_call(kind, a, b, z=None, sqrelu=False, out_dtype=F32):
    if kind == 'nn':
        (m, k), n = a.shape, b.shape[1]
    elif kind == 'nt':
        (m, k), n = a.shape, b.shape[0]
    else:
        (k, m), n = a.shape, b.shape[1]
    bm = _div_tile(m, MM_TILE)
    bn = _div_tile(n, MM_TILE)
    bk = _div_tile(k, MM_TILE if kind == 'tn' else 2 * MM_TILE)
    nk = k // bk
    if sqrelu:
        out_dtype = MXU_DTYPE
    plain = not sqrelu and z is None and out_dtype == F32
    use_acc = nk > 1 and not plain

    def body(a_ref, b_ref, *rest):
        rest = list(rest)
        z_ref = rest.pop(0) if z is not None else None
        o_ref = rest.pop(0)
        act_ref = rest.pop(0) if sqrelu else None
        acc = rest.pop(0) if use_acc else None
        kk = pl.program_id(2)
        av, bv = _mx(a_ref[...]), _mx(b_ref[...])
        if kind == 'nn':
            prod = jnp.dot(av, bv, preferred_element_type=F32)
        elif kind == 'nt':
            prod = _dg_nt(av, bv)
        else:
            prod = _dg_tn(av, bv)

        def finish(r):
            if z_ref is not None:
                r = r * (2.0 * jnp.maximum(z_ref[...].astype(F32), 0.0))
            o_ref[...] = r.astype(o_ref.dtype)
            if act_ref is not None:
                rr = jnp.maximum(r, 0.0)
                act_ref[...] = (rr * rr).astype(act_ref.dtype)

        if nk == 1:
            finish(prod)
        else:
            tgt = acc if use_acc else o_ref

            @pl.when(kk == 0)
            def _():
                tgt[...] = prod

            @pl.when(kk > 0)
            def _():
                tgt[...] += prod

            if use_acc:
                @pl.when(kk == nk - 1)
                def _():
                    finish(acc[...])

    if kind == 'nn':
        a_spec = pl.BlockSpec((bm, bk), lambda i, j, kk: (i, kk))
        b_spec = pl.BlockSpec((bk, bn), lambda i, j, kk: (kk, j))
    elif kind == 'nt':
        a_spec = pl.BlockSpec((bm, bk), lambda i, j, kk: (i, kk))
        b_spec = pl.BlockSpec((bn, bk), lambda i, j, kk: (j, kk))
    else:
        a_spec = pl.BlockSpec((bk, bm), lambda i, j, kk: (kk, i))
        b_spec = pl.BlockSpec((bk, bn), lambda i, j, kk: (kk, j))
    o_spec = pl.BlockSpec((bm, bn), lambda i, j, kk: (i, j))
    in_specs, args = [a_spec, b_spec], [a, b]
    if z is not None:
        in_specs.append(o_spec)
        args.append(z)
    out_shape, out_specs = [SDS((m, n), out_dtype)], [o_spec]
    if sqrelu:
        out_shape.append(SDS((m, n), MXU_DTYPE))
        out_specs.append(o_spec)
    tag = kind + ('_sq' if sqrelu else '') + ('_z' if z is not None else '')
    res = pl.pallas_call(
        body, name=f"mm_{tag}_{m}x{k}x{n}", grid=(m // bm, n // bn, nk),
        out_shape=out_shape, in_specs=in_specs, out_specs=out_specs,
        scratch_shapes=[pltpu.VMEM((bm, bn), F32)] if use_acc else [],
        compiler_params=pltpu.CompilerParams(dimension_semantics=("parallel", "parallel", "arbitrary"),
                                             vmem_limit_bytes=MM_VMEM_BYTES),
    )(*args)
    return res if sqrelu else res[0]


@jax.custom_vjp
def linear(a, w):
    return _mm_call('nn', a, w)


def _linear_fwd(a, w):
    return _mm_call('nn', a, w), (a, w)


def _linear_bwd(res, g):
    a, w = res
    return _mm_call('nt', g, w), _mm_call('tn', a, g)


linear.defvjp(_linear_fwd, _linear_bwd)


@jax.custom_vjp
def linear_t(a, wt, wtd):
    return _mm_call('nt', a, wt)


def _linear_t_fwd(a, wt, wtd):
    return _mm_call('nt', a, wt), (a, wt)


def _linear_t_bwd(res, g):
    a, wt = res
    return _mm_call('nn', g, wt), jnp.zeros_like(wt), _mm_call('tn', g, a)


linear_t.defvjp(_linear_t_fwd, _linear_t_bwd)


@jax.custom_vjp
def mix_linear(ac, s, w, wd):
    return _mm_call('nn', _mx(jnp.concatenate([ac, s], axis=1)), w)


def _mix_linear_fwd(ac, s, w, wd):
    mix = _mx(jnp.concatenate([ac, s], axis=1))
    return _mm_call('nn', mix, w), (mix, w)


def _mix_linear_bwd(res, g):
    mix, w = res
    dmix = _mm_call('nt', g, w)
    return dmix[:, :MIX_SPLIT], dmix[:, MIX_SPLIT:], jnp.zeros_like(w), _mm_call('tn', mix, g)


mix_linear.defvjp(_mix_linear_fwd, _mix_linear_bwd)


def _make_nm_ops(nm_fwd, nm_bwd):
    @jax.custom_vjp
    def nm_linear(h, sh, sc, g, wt, wtd):
        return _mm_call('nt', nm_fwd((h,), (sh, sc), (g,), MXU_DTYPE)[0], wt)

    def nm_linear_fwd(h, sh, sc, g, wt, wtd):
        a = nm_fwd((h,), (sh, sc), (g,), MXU_DTYPE)[0]
        return _mm_call('nt', a, wt), (h, sh, sc, g, a, wt)

    def nm_linear_bwd(res, gp):
        h, sh, sc, g, a, wt = res
        da = _mm_call('nn', gp, wt)
        dwt = _mm_call('tn', gp, a)
        dh, dsh, dsc, dg = nm_bwd((h,), (sh, sc), (g,), (da,))
        return dh, dsh, dsc, dg, jnp.zeros_like(wt), dwt

    nm_linear.defvjp(nm_linear_fwd, nm_linear_bwd)

    @jax.custom_vjp
    def nm_mlp(h, sh, sc, g, w1t, w1td, w2, w2d):
        a = nm_fwd((h,), (sh, sc), (g,), MXU_DTYPE)[0]
        _, act = _mm_call('nt', a, w1t, sqrelu=True)
        return _mm_call('nn', act, w2)

    def nm_mlp_fwd(h, sh, sc, g, w1t, w1td, w2, w2d):
        a = nm_fwd((h,), (sh, sc), (g,), MXU_DTYPE)[0]
        zb, act = _mm_call('nt', a, w1t, sqrelu=True)
        return _mm_call('nn', act, w2), (h, sh, sc, g, a, w1t, w2, zb, act)

    def nm_mlp_bwd(res, gf):
        h, sh, sc, g, a, w1t, w2, zb, act = res
        dz = _mm_call('nt', gf, w2, z=zb, out_dtype=MXU_DTYPE)
        dw2 = _mm_call('tn', act, gf)
        da = _mm_call('nn', dz, w1t)
        dw1t = _mm_call('tn', dz, a)
        dh, dsh, dsc, dg = nm_bwd((h,), (sh, sc), (g,), (da,))
        return dh, dsh, dsc, dg, jnp.zeros_like(w1t), dw1t, jnp.zeros_like(w2), dw2

    nm_mlp.defvjp(nm_mlp_fwd, nm_mlp_bwd)
    return nm_linear, nm_mlp


def _rowwise(name, f, out_widths, seg_len):
    def specs(rows, segs, globs, tr):
        nseg = segs[0].shape[0] if segs else 1

        def seg_of(i):
            return jnp.minimum((i * tr) // seg_len, nseg - 1)

        row_specs = [pl.BlockSpec((tr, r.shape[1]), lambda i: (i, 0)) for r in rows]
        seg_specs = [pl.BlockSpec((1, 1, s.shape[2]), lambda i: (seg_of(i), 0, 0)) for s in segs]
        glob_specs = [pl.BlockSpec(g.shape, lambda i: (0, 0)) for g in globs]
        return seg_of, row_specs, seg_specs, glob_specs

    def fwd_call(rows, segs, globs, out_dtype=F32):
        t = rows[0].shape[0]
        tr = _pick(t, (ROW_TILE, 128, 64, 32, 16, 8))
        _, row_specs, seg_specs, glob_specs = specs(rows, segs, globs, tr)
        nr, ns = len(rows), len(segs)

        def body(*refs):
            ins, outs = refs[:nr + ns + len(globs)], refs[nr + ns + len(globs):]
            vals = [r[...] for r in ins[:nr]] + [r[0] for r in ins[nr:nr + ns]] + [r[...] for r in ins[nr + ns:]]
            for o_ref, v in zip(outs, f(*vals)):
                o_ref[...] = v.astype(o_ref.dtype)

        return pl.pallas_call(
            body, name=f"{name}_fwd_{t}_{jnp.dtype(out_dtype).name}", grid=(t // tr,),
            out_shape=[SDS((t, w), out_dtype) for w in out_widths],
            in_specs=row_specs + seg_specs + glob_specs,
            out_specs=[pl.BlockSpec((tr, w), lambda i: (i, 0)) for w in out_widths],
        )(*rows, *segs, *globs)

    def bwd_call(rows, segs, globs, douts):
        t = rows[0].shape[0]
        tr = _pick(t, (ROW_TILE, 128, 64, 32, 16, 8))
        seg_of, row_specs, seg_specs, glob_specs = specs(rows, segs, globs, tr)
        nr, ns, ng, no = len(rows), len(segs), len(globs), len(out_widths)

        def body(*refs):
            ins = refs[:nr + ns + ng]
            dos = refs[nr + ns + ng:nr + ns + ng + no]
            outs = refs[nr + ns + ng + no:]
            i = pl.program_id(0)
            first_of_seg = jnp.logical_or(i == 0, seg_of(i) != seg_of(jnp.maximum(i - 1, 0)))
            vals = [r[...] for r in ins[:nr]] + [r[0] for r in ins[nr:nr + ns]] + [r[...] for r in ins[nr + ns:]]
            _, vjp = jax.vjp(f, *vals)
            grads = vjp(tuple(d[...] for d in dos))
            for o_ref, gval in zip(outs[:nr], grads[:nr]):
                o_ref[...] = gval
            for o_ref, gval in zip(outs[nr:nr + ns], grads[nr:nr + ns]):
                @pl.when(first_of_seg)
                def _(o_ref=o_ref):
                    o_ref[...] = jnp.zeros_like(o_ref)
                o_ref[0] += gval
            for o_ref, gval in zip(outs[nr + ns:], grads[nr + ns:]):
                @pl.when(i == 0)
                def _(o_ref=o_ref):
                    o_ref[...] = jnp.zeros_like(o_ref)
                o_ref[...] += gval

        do_specs = [pl.BlockSpec((tr, w), lambda i: (i, 0)) for w in out_widths]
        return pl.pallas_call(
            body, name=f"{name}_bwd_{t}", grid=(t // tr,),
            out_shape=[SDS(r.shape, F32) for r in rows] + [SDS(s.shape, F32) for s in segs]
            + [SDS(g.shape, F32) for g in globs],
            in_specs=row_specs + seg_specs + glob_specs + do_specs,
            out_specs=row_specs + seg_specs + glob_specs,
        )(*rows, *segs, *globs, *douts)

    return fwd_call, bwd_call


def _norm_mod_f(x, shift, scale, g):
    r = lax.rsqrt(jnp.mean(x * x, axis=-1, keepdims=True) + EPS)
    return ((x * r) * g * (1.0 + scale) + shift,)


def _resid_gate_f(h, m, gate, g):
    r = lax.rsqrt(jnp.mean(m * m, axis=-1, keepdims=True) + EPS)
    return (h + gate * ((m * r) * g),)


def _glu_f(y0, y1, u, d, w, b):
    y = y0 + y1 + d * u
    g = 0.5 * y * (1.0 + jnp.tanh(0.7978845608028654 * (y + 0.044715 * (y * y * y))))
    zz = jnp.dot(_mx(g), _mx(w), preferred_element_type=F32) + b
    return (g * (1.0 / (1.0 + jnp.exp(-zz))),)


def _make_rowwise_op(name, f, out_width, n_rows, n_segs, seg_len):
    fwd_call, bwd_call = _rowwise(name, f, (out_width,), seg_len)

    @jax.custom_vjp
    def op(*args):
        return fwd_call(args[:n_rows], args[n_rows:n_rows + n_segs], args[n_rows + n_segs:])[0]

    def op_fwd(*args):
        return fwd_call(args[:n_rows], args[n_rows:n_rows + n_segs], args[n_rows + n_segs:])[0], args

    def op_bwd(args, g):
        return tuple(bwd_call(args[:n_rows], args[n_rows:n_rows + n_segs], args[n_rows + n_segs:], (g,)))

    op.defvjp(op_fwd, op_bwd)
    return op


def _lane():
    return lax.broadcasted_iota(jnp.int32, (1, LANES), 1)


def _hm(off):
    lane = _lane()
    return jnp.logical_and(lane >= off, lane < off + HEAD_DIM)


def _align_in(pair, off_from, off_to):
    x = pair if off_from == off_to else pltpu.roll(pair, HEAD_DIM, 1)
    return jnp.where(_hm(off_to), x, 0.0)


def _align_out(x, off_from, off_to):
    x = jnp.where(_hm(off_from), x, 0.0)
    return x if off_from == off_to else pltpu.roll(x, HEAD_DIM, 1)


def _col(tile, hq):
    return jnp.sum(jnp.where(_lane() == hq, tile, 0.0), axis=1, keepdims=True)


def _setcol(tile, hq, col):
    return jnp.where(_lane() == hq, col, tile)


def _head_fwd(qa, groups, sk):
    ss, m = [], None
    for kmat, _, mask in groups:
        s = _dg_nt(qa, kmat) * _SCALE
        if mask is not None:
            s = jnp.where(mask, s, NEG_INF)
        ss.append(s)
        mm = jnp.max(s, axis=1, keepdims=True)
        m = mm if m is None else jnp.maximum(m, mm)
    m = jnp.maximum(m, sk)
    l = jnp.exp(sk - m)
    pv = None
    for s, (_, vmat, _) in zip(ss, groups):
        p = jnp.exp(s - m)
        l = l + jnp.sum(p, axis=1, keepdims=True)
        t = jnp.dot(_mx(p), vmat, preferred_element_type=F32)
        pv = t if pv is None else pv + t
    return pv / l, m + jnp.log(l)


def _head_bwd(qa, doa, groups, sk, lse_h, delta_h):
    dq, outs = None, []
    for kmat, vmat, mask in groups:
        s = _dg_nt(qa, kmat) * _SCALE
        if mask is not None:
            s = jnp.where(mask, s, NEG_INF)
        p = jnp.exp(s - lse_h)
        dp = _dg_nt(doa, vmat)
        ds = p * (dp - delta_h) * _SCALE
        t = jnp.dot(_mx(ds), kmat, preferred_element_type=F32)
        dq = t if dq is None else dq + t
        outs.append((_mx(ds), _mx(p)))
    return dq, outs, jnp.exp(sk - lse_h)


def _heads():
    for pair in range(N_Q_HEADS // 2):
        for sub in range(2):
            hq = pair * 2 + sub
            yield pair, hq, sub * HEAD_DIM, (hq // Q_PER_KV) * HEAD_DIM


def _partner(t):
    lane = lax.broadcasted_iota(jnp.int32, t.shape, 1)
    return jnp.where((lane & ROPE_PAIRS) == 0, pltpu.roll(t, LANES - ROPE_PAIRS, 1), pltpu.roll(t, ROPE_PAIRS, 1))


def _rope_tables(n_tokens):
    rows = n_tokens // GRID_W
    row = jnp.broadcast_to(jnp.arange(rows)[:, None], (rows, GRID_W)).reshape(-1)
    col = jnp.broadcast_to(jnp.arange(GRID_W)[None, :], (rows, GRID_W)).reshape(-1)
    freqs = ROPE_BASE ** (-jnp.arange(ROPE_PAIRS, dtype=F32) / ROPE_PAIRS)
    ang = jnp.concatenate([row[:, None].astype(F32) * freqs, col[:, None].astype(F32) * freqs], axis=-1)
    c, s = jnp.cos(ang), jnp.sin(ang)
    n = ROPE_PAIRS
    cos64 = jnp.concatenate([c[:, :n], c[:, :n], c[:, n:], c[:, n:]], axis=1)
    sin64 = jnp.concatenate([-s[:, :n], s[:, :n], -s[:, n:], s[:, n:]], axis=1)
    return jnp.tile(cos64, (1, 2)), jnp.tile(sin64, (1, 2))


def _rope_call(q_src, k_src, cos, sin, dims, inverse):
    b, l, _ = dims
    t_lat = b * l
    tr = _pick(l, (ROW_TILE, 128))
    per = l // tr
    out_dtype = F32 if inverse else MXU_DTYPE

    def body(q_ref, k_ref, c_ref, s_ref, qo_ref, ko_ref):
        c, s = c_ref[...], s_ref[...]

        def rot(t):
            if inverse:
                return t * c + _partner(t * s)
            return t * c + _partner(t) * s

        for j in range(ATTN_WIDTH // LANES):
            qo_ref[:, j * LANES:(j + 1) * LANES] = rot(q_ref[:, j * LANES:(j + 1) * LANES]).astype(out_dtype)
        ko_ref[...] = rot(k_ref[...]).astype(out_dtype)

    kcol = 0 if inverse else ATTN_WIDTH // KV_WIDTH
    return pl.pallas_call(
        body, name=f"rope_{'inv' if inverse else 'fwd'}_{t_lat}", grid=(t_lat // tr,),
        out_shape=[SDS((t_lat, ATTN_WIDTH), out_dtype), SDS((t_lat, KV_WIDTH), out_dtype)],
        in_specs=[pl.BlockSpec((tr, ATTN_WIDTH), lambda i: (i, 0)),
                  pl.BlockSpec((tr, KV_WIDTH), lambda i: (i, kcol)),
                  pl.BlockSpec((tr, LANES), lambda i: (i % per, 0)),
                  pl.BlockSpec((tr, LANES), lambda i: (i % per, 0))],
        out_specs=[pl.BlockSpec((tr, ATTN_WIDTH), lambda i: (i, 0)), pl.BlockSpec((tr, KV_WIDTH), lambda i: (i, 0))],
    )(q_src, k_src, cos, sin)


_SMEM_SPEC = pl.BlockSpec(memory_space=pltpu.SMEM)
_KCOL = ATTN_WIDTH // KV_WIDTH
_VCOL = _KCOL + 1


def _win_specs(arr_cols, nb, col):
    del arr_cols
    return [pl.BlockSpec((Q_BLOCK, KV_WIDTH), lambda b, i, d=d: (b * nb + jnp.clip(i + d, 0, nb - 1), col))
            for d in (-1, 0, 1)]


def _win_mask(i, l):
    qpos = i * Q_BLOCK + lax.broadcasted_iota(jnp.int32, (Q_BLOCK, 3 * Q_BLOCK), 0)
    kpos = (i - 1) * Q_BLOCK + lax.broadcasted_iota(jnp.int32, (Q_BLOCK, 3 * Q_BLOCK), 1)
    ok = jnp.logical_and(jnp.abs(qpos - kpos) <= WINDOW, jnp.logical_and(kpos >= 0, kpos < l))
    return ok


def _attn_lat_fwd(qr, kr, p, sink, dims):
    b, l, lc = dims
    nb = l // Q_BLOCK
    t_lat = b * l
    cbase = t_lat // lc

    def body(sink_ref, q_ref, kp, kc, kn, vp, vc, vn, ck_ref, cv_ref, o_ref, lse_ref):
        i = pl.program_id(1)
        mask = _win_mask(i, l)
        kwin = _mx(jnp.concatenate([kp[...], kc[...], kn[...]], axis=0))
        vwin = _mx(jnp.concatenate([vp[...], vc[...], vn[...]], axis=0))
        groups = [(kwin, vwin, mask), (_mx(ck_ref[...]), _mx(cv_ref[...]), None)]
        lse_t = jnp.zeros((Q_BLOCK, LANES), F32)
        o_pair = None
        for pair, hq, off_q, off_k in _heads():
            qa = _mx(_align_in(q_ref[:, pair * LANES:(pair + 1) * LANES].astype(F32), off_q, off_k))
            o, lse_h = _head_fwd(qa, groups, sink_ref[0, hq])
            o = _align_out(o, off_k, off_q)
            o_pair = o if off_q == 0 else o_pair + o
            lse_t = _setcol(lse_t, hq, lse_h)
            if off_q:
                o_ref[:, pair * LANES:(pair + 1) * LANES] = o_pair
        lse_ref[...] = lse_t

    return pl.pallas_call(
        body, name=f"attn_lat_fwd_{t_lat}", grid=(b, nb),
        out_shape=[SDS((t_lat, ATTN_WIDTH), F32), SDS((t_lat, LANES), F32)],
        in_specs=[_SMEM_SPEC, pl.BlockSpec((Q_BLOCK, ATTN_WIDTH), lambda bb, i: (bb * nb + i, 0))]
        + _win_specs(None, nb, 0) + _win_specs(None, nb, _VCOL)
        + [pl.BlockSpec((lc, KV_WIDTH), lambda bb, i: (cbase + bb, _KCOL)),
           pl.BlockSpec((lc, KV_WIDTH), lambda bb, i: (cbase + bb, _VCOL))],
        out_specs=[pl.BlockSpec((Q_BLOCK, ATTN_WIDTH), lambda bb, i: (bb * nb + i, 0)),
                   pl.BlockSpec((Q_BLOCK, LANES), lambda bb, i: (bb * nb + i, 0))],
    )(sink, qr, kr, kr, kr, p, p, p, p, p)


def _attn_lat_bwd_dq(qr, kr, p, sink, o, lse, dout, dims):
    b, l, lc = dims
    nb = l // Q_BLOCK
    t_lat = b * l
    cbase = t_lat // lc

    def body(sink_ref, q_ref, kp, kc, kn, vp, vc, vn, ck_ref, cv_ref, o_ref, lse_ref, do_ref,
             dq_ref, delta_ref, dkc_ref, dvc_ref, dsk_ref):
        bb, i = pl.program_id(0), pl.program_id(1)

        @pl.when(i == 0)
        def _():
            dkc_ref[...] = jnp.zeros_like(dkc_ref)
            dvc_ref[...] = jnp.zeros_like(dvc_ref)

        @pl.when(jnp.logical_and(bb == 0, i == 0))
        def _():
            dsk_ref[...] = jnp.zeros_like(dsk_ref)

        mask = _win_mask(i, l)
        kwin = _mx(jnp.concatenate([kp[...], kc[...], kn[...]], axis=0))
        vwin = _mx(jnp.concatenate([vp[...], vc[...], vn[...]], axis=0))
        groups = [(kwin, vwin, mask), (_mx(ck_ref[...]), _mx(cv_ref[...]), None)]
        lse_t = lse_ref[...]
        delta_t = jnp.zeros((Q_BLOCK, LANES), F32)
        dsk = jnp.zeros((1, LANES), F32)
        dkc = jnp.zeros((lc, KV_WIDTH), F32)
        dvc = jnp.zeros((lc, KV_WIDTH), F32)
        dq_pair = None
        for pair, hq, off_q, off_k in _heads():
            sl = slice(pair * LANES, (pair + 1) * LANES)
            qa = _mx(_align_in(q_ref[:, sl].astype(F32), off_q, off_k))
            do_p = do_ref[:, sl]
            delta_h = jnp.sum(jnp.where(_hm(off_q), do_p * o_ref[:, sl], 0.0), axis=1, keepdims=True)
            doa = _mx(_align_in(do_p, off_q, off_k))
            sk = sink_ref[0, hq]
            dq, outs, p_s = _head_bwd(qa, doa, groups, sk, _col(lse_t, hq), delta_h)
            dq = _align_out(dq, off_k, off_q)
            dq_pair = dq if off_q == 0 else dq_pair + dq
            ds_c, p_c = outs[1]
            dkc = dkc + _dg_tn(ds_c, qa)
            dvc = dvc + _dg_tn(p_c, doa)
            dsk = dsk + jnp.where(_lane() == hq, jnp.sum(-p_s * delta_h, axis=0, keepdims=True), 0.0)
            delta_t = _setcol(delta_t, hq, delta_h)
            if off_q:
                dq_ref[:, sl] = dq_pair
        delta_ref[...] = delta_t
        dkc_ref[...] += dkc
        dvc_ref[...] += dvc
        dsk_ref[0:1, :] += dsk

    qspec = pl.BlockSpec((Q_BLOCK, ATTN_WIDTH), lambda bb, i: (bb * nb + i, 0))
    tspec = pl.BlockSpec((Q_BLOCK, LANES), lambda bb, i: (bb * nb + i, 0))
    cspec = pl.BlockSpec((lc, KV_WIDTH), lambda bb, i: (bb, 0))
    return pl.pallas_call(
        body, name=f"attn_lat_bwd_dq_{t_lat}", grid=(b, nb),
        out_shape=[SDS((t_lat, ATTN_WIDTH), F32), SDS((t_lat, LANES), F32), SDS((b * lc, KV_WIDTH), F32),
                   SDS((b * lc, KV_WIDTH), F32), SDS((8, LANES), F32)],
        in_specs=[_SMEM_SPEC, qspec] + _win_specs(None, nb, 0) + _win_specs(None, nb, _VCOL)
        + [pl.BlockSpec((lc, KV_WIDTH), lambda bb, i: (cbase + bb, _KCOL)),
           pl.BlockSpec((lc, KV_WIDTH), lambda bb, i: (cbase + bb, _VCOL)), qspec, tspec, qspec],
        out_specs=[qspec, tspec, cspec, cspec, pl.BlockSpec((8, LANES), lambda bb, i: (0, 0))],
    )(sink, qr, kr, kr, kr, p, p, p, p, p, o, lse, dout)


def _attn_lat_bwd_dkv(qr, kr, p, lse, delta, dout, dims):
    b, l, _ = dims
    nb = l // Q_BLOCK
    t_lat = b * l

    def body(k_ref, v_ref, *refs):
        j = pl.program_id(1)
        kj, vj = _mx(k_ref[...]), _mx(v_ref[...])
        dk = jnp.zeros((Q_BLOCK, KV_WIDTH), F32)
        dv = jnp.zeros((Q_BLOCK, KV_WIDTH), F32)
        for n, d in enumerate((-1, 0, 1)):
            q_ref, do_ref, lse_ref, delta_ref = refs[4 * n:4 * n + 4]
            i = j + d
            valid = jnp.logical_and(i >= 0, i < nb)
            qpos = i * Q_BLOCK + lax.broadcasted_iota(jnp.int32, (Q_BLOCK, Q_BLOCK), 0)
            kpos = j * Q_BLOCK + lax.broadcasted_iota(jnp.int32, (Q_BLOCK, Q_BLOCK), 1)
            mask = jnp.logical_and(jnp.abs(qpos - kpos) <= WINDOW, valid)
            lse_t, delta_t = lse_ref[...], delta_ref[...]
            for pair, hq, off_q, off_k in _heads():
                sl = slice(pair * LANES, (pair + 1) * LANES)
                qa = _mx(_align_in(q_ref[:, sl].astype(F32), off_q, off_k))
                doa = _mx(_align_in(do_ref[:, sl], off_q, off_k))
                _, outs, _ = _head_bwd(qa, doa, [(kj, vj, mask)], 0.0, _col(lse_t, hq), _col(delta_t, hq))
                ds, pp = outs[0]
                dk = dk + _dg_tn(ds, qa)
                dv = dv + _dg_tn(pp, doa)
        dk_ref, dv_ref = refs[12], refs[13]
        dk_ref[...] = dk
        dv_ref[...] = dv

    def blk(width, d, col=0):
        return pl.BlockSpec((Q_BLOCK, width), lambda bb, j: (bb * nb + jnp.clip(j + d, 0, nb - 1), col))

    in_specs = [blk(KV_WIDTH, 0), blk(KV_WIDTH, 0, _VCOL)]
    args = [kr, p]
    for d in (-1, 0, 1):
        in_specs += [blk(ATTN_WIDTH, d), blk(ATTN_WIDTH, d), blk(LANES, d), blk(LANES, d)]
        args += [qr, dout, lse, delta]
    return pl.pallas_call(
        body, name=f"attn_lat_bwd_dkv_{t_lat}", grid=(b, nb),
        out_shape=[SDS((t_lat, KV_WIDTH), F32), SDS((t_lat, KV_WIDTH), F32)],
        in_specs=in_specs, out_specs=[blk(KV_WIDTH, 0), blk(KV_WIDTH, 0)],
    )(*args)


def _attn_ctx_fwd(p, sink, dims):
    b, l, lc = dims
    cbase = b * l // lc

    def body(sink_ref, q_ref, k_ref, v_ref, o_ref, lse_ref):
        groups = [(_mx(k_ref[...]), _mx(v_ref[...]), None)]
        lse_t = jnp.zeros((lc, LANES), F32)
        o_pair = None
        for pair, hq, off_q, off_k in _heads():
            qa = _mx(_align_in(q_ref[:, pair * LANES:(pair + 1) * LANES], off_q, off_k))
            o, lse_h = _head_fwd(qa, groups, sink_ref[0, hq])
            o = _align_out(o, off_k, off_q)
            o_pair = o if off_q == 0 else o_pair + o
            lse_t = _setcol(lse_t, hq, lse_h)
            if off_q:
                o_ref[:, pair * LANES:(pair + 1) * LANES] = o_pair
        lse_ref[...] = lse_t

    return pl.pallas_call(
        body, name=f"attn_ctx_fwd_{b * lc}", grid=(b,),
        out_shape=[SDS((b * lc, ATTN_WIDTH), F32), SDS((b * lc, LANES), F32)],
        in_specs=[_SMEM_SPEC, pl.BlockSpec((lc, ATTN_WIDTH), lambda bb: (cbase + bb, 0)),
                  pl.BlockSpec((lc, KV_WIDTH), lambda bb: (cbase + bb, _KCOL)),
                  pl.BlockSpec((lc, KV_WIDTH), lambda bb: (cbase + bb, _VCOL))],
        out_specs=[pl.BlockSpec((lc, ATTN_WIDTH), lambda bb: (bb, 0)), pl.BlockSpec((lc, LANES), lambda bb: (bb, 0))],
    )(sink, p, p, p)


def _attn_ctx_bwd(p, sink, o, lse, dout, dims):
    b, l, lc = dims
    cbase = b * l // lc

    def body(sink_ref, q_ref, k_ref, v_ref, o_ref, lse_ref, do_ref, dq_ref, dk_ref, dv_ref, dsk_ref):
        bb = pl.program_id(0)

        @pl.when(bb == 0)
        def _():
            dsk_ref[...] = jnp.zeros_like(dsk_ref)

        groups = [(_mx(k_ref[...]), _mx(v_ref[...]), None)]
        lse_t = lse_ref[...]
        dsk = jnp.zeros((1, LANES), F32)
        dk = jnp.zeros((lc, KV_WIDTH), F32)
        dv = jnp.zeros((lc, KV_WIDTH), F32)
        dq_pair = None
        for pair, hq, off_q, off_k in _heads():
            sl = slice(pair * LANES, (pair + 1) * LANES)
            qa = _mx(_align_in(q_ref[:, sl], off_q, off_k))
            do_p = do_ref[:, sl]
            delta_h = jnp.sum(jnp.where(_hm(off_q), do_p * o_ref[:, sl], 0.0), axis=1, keepdims=True)
            doa = _mx(_align_in(do_p, off_q, off_k))
            dq, outs, p_s = _head_bwd(qa, doa, groups, sink_ref[0, hq], _col(lse_t, hq), delta_h)
            dq = _align_out(dq, off_k, off_q)
            dq_pair = dq if off_q == 0 else dq_pair + dq
            ds, pp = outs[0]
            dk = dk + _dg_tn(ds, qa)
            dv = dv + _dg_tn(pp, doa)
            dsk = dsk + jnp.where(_lane() == hq, jnp.sum(-p_s * delta_h, axis=0, keepdims=True), 0.0)
            if off_q:
                dq_ref[:, sl] = dq_pair
        dk_ref[...] = dk
        dv_ref[...] = dv
        dsk_ref[0:1, :] += dsk

    qs = pl.BlockSpec((lc, ATTN_WIDTH), lambda bb: (bb, 0))
    ks = pl.BlockSpec((lc, KV_WIDTH), lambda bb: (bb, 0))
    return pl.pallas_call(
        body, name=f"attn_ctx_bwd_{b * lc}", grid=(b,),
        out_shape=[SDS((b * lc, ATTN_WIDTH), F32), SDS((b * lc, KV_WIDTH), F32), SDS((b * lc, KV_WIDTH), F32),
                   SDS((8, LANES), F32)],
        in_specs=[_SMEM_SPEC, pl.BlockSpec((lc, ATTN_WIDTH), lambda bb: (cbase + bb, 0)),
                  pl.BlockSpec((lc, KV_WIDTH), lambda bb: (cbase + bb, _KCOL)),
                  pl.BlockSpec((lc, KV_WIDTH), lambda bb: (cbase + bb, _VCOL)),
                  qs, pl.BlockSpec((lc, LANES), lambda bb: (bb, 0)),
                  pl.BlockSpec((lc, ATTN_WIDTH), lambda bb: (cbase + bb, 0))],
        out_specs=[qs, ks, ks, pl.BlockSpec((8, LANES), lambda bb: (0, 0))],
    )(sink, p, p, p, o, lse, dout)


_CONV_COL = (ATTN_WIDTH + 2 * KV_WIDTH) // CONV_WIDTH


def _shift_prev(z, n):
    rows = lax.broadcasted_iota(jnp.int32, z.shape, 0)
    return jnp.where(rows == 0, 0.0, pltpu.roll(z, 1, 0))


def _shift_next(z, n):
    rows = lax.broadcasted_iota(jnp.int32, z.shape, 0)
    return jnp.where(rows == n - 1, 0.0, pltpu.roll(z, n - 1, 0))


def _conv_fwd(p, w, base, n_seq, ls):
    def body(cb_ref, cc_ref, cx_ref, w_ref, o_ref):
        z = cc_ref[...] * cx_ref[...]
        c3 = _shift_prev(z, ls) * w_ref[0:1, :] + z * w_ref[1:2, :] + _shift_next(z, ls) * w_ref[2:3, :]
        o_ref[...] = cb_ref[...] * c3

    return pl.pallas_call(
        body, name=f"conv_fwd_{n_seq}x{ls}", grid=(n_seq,),
        out_shape=SDS((n_seq * ls, CONV_WIDTH), F32),
        in_specs=[pl.BlockSpec((ls, CONV_WIDTH), lambda s, c=c: (base + s, _CONV_COL + c)) for c in range(3)]
        + [pl.BlockSpec(w.shape, lambda s: (0, 0))],
        out_specs=pl.BlockSpec((ls, CONV_WIDTH), lambda s: (s, 0)),
    )(p, p, p, w)


def _conv_bwd(p, w, dout, base, n_seq, ls):
    dcol = ATTN_WIDTH // CONV_WIDTH

    def body(cb_ref, cc_ref, cx_ref, w_ref, do_ref, dcb_ref, dcc_ref, dcx_ref, dw_ref):
        @pl.when(pl.program_id(0) == 0)
        def _():
            dw_ref[...] = jnp.zeros_like(dw_ref)

        cc, cx = cc_ref[...], cx_ref[...]
        z = cc * cx
        zp, zn = _shift_prev(z, ls), _shift_next(z, ls)
        c3 = zp * w_ref[0:1, :] + z * w_ref[1:2, :] + zn * w_ref[2:3, :]
        do = do_ref[...]
        dcb_ref[...] = do * c3
        e = do * cb_ref[...]
        dz = _shift_next(e, ls) * w_ref[0:1, :] + e * w_ref[1:2, :] + _shift_prev(e, ls) * w_ref[2:3, :]
        dcc_ref[...] = dz * cx
        dcx_ref[...] = dz * cc
        dw_ref[0:1, :] += jnp.sum(e * zp, axis=0, keepdims=True)
        dw_ref[1:2, :] += jnp.sum(e * z, axis=0, keepdims=True)
        dw_ref[2:3, :] += jnp.sum(e * zn, axis=0, keepdims=True)

    ospec = pl.BlockSpec((ls, CONV_WIDTH), lambda s: (s, 0))
    return pl.pallas_call(
        body, name=f"conv_bwd_{n_seq}x{ls}", grid=(n_seq,),
        out_shape=[SDS((n_seq * ls, CONV_WIDTH), F32)] * 3 + [SDS((8, CONV_WIDTH), F32)],
        in_specs=[pl.BlockSpec((ls, CONV_WIDTH), lambda s, c=c: (base + s, _CONV_COL + c)) for c in range(3)]
        + [pl.BlockSpec(w.shape, lambda s: (0, 0)), pl.BlockSpec((ls, CONV_WIDTH), lambda s: (base + s, dcol))],
        out_specs=[ospec, ospec, ospec, pl.BlockSpec((8, CONV_WIDTH), lambda s: (0, 0))],
    )(p, p, p, w, dout)


def _make_att_conv(dims, cos, sin):
    b, l, lc = dims
    t_lat = b * l

    def forward(p, conv_w, sink):
        qr, kr = _rope_call(p, p, cos, sin, dims, inverse=False)
        o_lat, lse_lat = _attn_lat_fwd(qr, kr, p, sink, dims)
        o_ctx, lse_ctx = _attn_ctx_fwd(p, sink, dims)
        conv_lat = _conv_fwd(p, conv_w, 0, b, l)
        conv_ctx = _conv_fwd(p, conv_w, t_lat // lc, b, lc)
        out = jnp.concatenate([jnp.concatenate([o_lat, conv_lat], axis=1),
                               jnp.concatenate([o_ctx, conv_ctx], axis=1)], axis=0)
        return out, (p, conv_w, sink, qr, kr, o_lat, lse_lat, o_ctx, lse_ctx)

    @jax.custom_vjp
    def op(p, conv_w, sink):
        return forward(p, conv_w, sink)[0]

    def op_bwd(res, dout):
        p, conv_w, sink, qr, kr, o_lat, lse_lat, o_ctx, lse_ctx = res
        dqr, delta, dkc1, dvc1, dsk1 = _attn_lat_bwd_dq(qr, kr, p, sink, o_lat, lse_lat, dout, dims)
        dkr, dv = _attn_lat_bwd_dkv(qr, kr, p, lse_lat, delta, dout, dims)
        dq, dk = _rope_call(dqr, dkr, cos, sin, dims, inverse=True)
        dqc, dkc2, dvc2, dsk2 = _attn_ctx_bwd(p, sink, o_ctx, lse_ctx, dout, dims)
        dcb_l, dcc_l, dcx_l, dw_l = _conv_bwd(p, conv_w, dout, 0, b, l)
        dcb_c, dcc_c, dcx_c, dw_c = _conv_bwd(p, conv_w, dout, t_lat // lc, b, lc)
        zeros_u = jnp.zeros((p.shape[0], SSM_WIDTH), F32)
        lat = jnp.concatenate([dq, dk, dv, dcb_l, dcc_l, dcx_l], axis=1)
        ctx = jnp.concatenate([dqc, dkc1 + dkc2, dvc1 + dvc2, dcb_c, dcc_c, dcx_c], axis=1)
        dp = jnp.concatenate([jnp.concatenate([lat, ctx], axis=0), zeros_u], axis=1)
        dsink = (dsk1 + dsk2)[0:1, :N_Q_HEADS]
        return dp, (dw_l + dw_c)[:3], dsink

    op.defvjp(forward, op_bwd)
    return op


def _scan_call(x, lam_r, lam_i, dims, ctx_first, dec, h=None):
    b, l, lc = dims
    n = SSM_N
    tc = _pick(lc, (256, 128, 64, 32, 16, 8))
    nc_c, nc_l = lc // tc, l // tc
    n_chunks = nc_c + nc_l
    ctx_base = b * l // tc
    g8 = tc // 8
    with_h = h is not None

    def chunk(bb, j):
        if ctx_first:
            is_ctx = j < nc_c
            jj = jnp.where(is_ctx, j, j - nc_c)
        else:
            is_ctx = j >= nc_l
            jj = jnp.where(is_ctx, j - nc_l, j)
        ic = nc_c - 1 - jj if dec else jj
        il = nc_l - 1 - jj if dec else jj
        return jnp.where(is_ctx, ctx_base + bb * nc_c + ic, bb * nc_l + il)

    def edge(bb, j):
        jn = jnp.minimum(j + 1, n_chunks - 1)
        return chunk(bb, jn) * g8 + (g8 - 1 if dec else 0)

    def body(x_ref, lr_ref, li_ref, *rest):
        rest = list(rest)
        h_ref, hb_ref = (rest.pop(0), rest.pop(0)) if with_h else (None, None)
        o_ref = rest.pop(0)
        dl_ref = rest.pop(0) if with_h else None
        carry = rest.pop(0)
        bb, j = pl.program_id(0), pl.program_id(1)

        @pl.when(j == 0)
        def _():
            carry[...] = jnp.zeros_like(carry)

        if with_h:
            @pl.when(jnp.logical_and(bb == 0, j == 0))
            def _():
                dl_ref[...] = jnp.zeros_like(dl_ref)

        rows = lax.broadcasted_iota(jnp.int32, (tc, LANES), 0)
        first = rows == (tc - 1 if dec else 0)
        has_next = j + 1 < n_chunks
        for s in range(n // LANES):
            re, im = slice(s * LANES, (s + 1) * LANES), slice(n + s * LANES, n + (s + 1) * LANES)
            ar, ai = lr_ref[:, re], li_ref[:, re]
            cr, ci = carry[0:1, re], carry[0:1, im]
            dr = x_ref[:, re] + jnp.where(first, ar * cr - ai * ci, 0.0)
            di = x_ref[:, im] + jnp.where(first, ar * ci + ai * cr, 0.0)
            pr, pi = ar, ai
            sft = 1
            while sft < tc:
                if dec:
                    keep = rows < tc - sft
                    sr = jnp.where(keep, pltpu.roll(dr, tc - sft, 0), 0.0)
                    si = jnp.where(keep, pltpu.roll(di, tc - sft, 0), 0.0)
                else:
                    keep = rows >= sft
                    sr = jnp.where(keep, pltpu.roll(dr, sft, 0), 0.0)
                    si = jnp.where(keep, pltpu.roll(di, sft, 0), 0.0)
                dr, di = dr + pr * sr - pi * si, di + pr * si + pi * sr
                pr, pi = pr * pr - pi * pi, 2.0 * pr * pi
                sft *= 2
            o_ref[:, re] = dr
            o_ref[:, im] = di
            last = 0 if dec else tc - 1
            carry[0:1, re] = o_ref[last:last + 1, re]
            carry[0:1, im] = o_ref[last:last + 1, im]
            if with_h:
                hr, hi = h_ref[:, re], h_ref[:, im]
                er = 7 if dec else 0
                br = jnp.where(has_next, hb_ref[er:er + 1, re], 0.0)
                bi = jnp.where(has_next, hb_ref[er:er + 1, im], 0.0)
                if dec:
                    nr = jnp.where(rows == 0, br, pltpu.roll(hr, 1, 0))
                    ni = jnp.where(rows == 0, bi, pltpu.roll(hi, 1, 0))
                else:
                    nr = jnp.where(rows == tc - 1, br, pltpu.roll(hr, tc - 1, 0))
                    ni = jnp.where(rows == tc - 1, bi, pltpu.roll(hi, tc - 1, 0))
                dl_ref[0:1, re] += jnp.sum(dr * nr + di * ni, axis=0, keepdims=True)
                dl_ref[0:1, im] += jnp.sum(di * nr - dr * ni, axis=0, keepdims=True)

    xspec = pl.BlockSpec((tc, 2 * n), lambda bb, j: (chunk(bb, j), 0))
    lspec = pl.BlockSpec((1, n), lambda bb, j: (0, 0))
    in_specs, args = [xspec, lspec, lspec], [x, lam_r, lam_i]
    out_shape, out_specs = [SDS(x.shape, F32)], [xspec]
    if with_h:
        in_specs += [xspec, pl.BlockSpec((8, 2 * n), lambda bb, j: (edge(bb, j), 0))]
        args += [h, h]
        out_shape.append(SDS((8, 2 * n), F32))
        out_specs.append(pl.BlockSpec((8, 2 * n), lambda bb, j: (0, 0)))
    tag = ('c' if ctx_first else 'l') + ('d' if dec else 'u') + ('h' if with_h else '')
    res = pl.pallas_call(
        body, name=f"scan_{tag}_{x.shape[0]}", grid=(b, n_chunks),
        out_shape=out_shape, in_specs=in_specs, out_specs=out_specs,
        scratch_shapes=[pltpu.VMEM((8, 2 * n), F32)],
    )(*args)
    return res


def _make_scan(dims, direction):
    @jax.custom_vjp
    def scan(x, lam_r, lam_i):
        return _scan_call(x, lam_r, lam_i, dims, True, direction == 1)[0]

    def scan_fwd(x, lam_r, lam_i):
        h = _scan_call(x, lam_r, lam_i, dims, True, direction == 1)[0]
        return h, (h, lam_r, lam_i)

    def scan_bwd(res, g):
        h, lam_r, lam_i = res
        gd, dl = _scan_call(g, lam_r, -lam_i, dims, False, direction == 0, h=h)
        return gd, dl[0:1, :SSM_N], dl[0:1, SSM_N:]

    scan.defvjp(scan_fwd, scan_bwd)
    return scan


def _block_diag(m):
    g, a, bdim = m.shape
    eye = jnp.eye(g, dtype=m.dtype)
    full = m[:, :, None, :] * eye[:, None, :, None]
    return full.reshape(g * a, g * bdim)


def _ssm_mats(lam_re, lam_im, log_dt, b_re, b_im, c_re, c_im):
    lam = lax.complex(lam_re, lam_im)
    dt = jnp.exp(log_dt)[..., None]
    lam_bar = jnp.exp(lam * dt)
    b_bar = ((lam_bar - 1) / lam)[..., None] * lax.complex(b_re, b_im)
    out = []
    for d in range(2):
        bt = jnp.swapaxes(b_bar[d], 1, 2)
        bmat = jnp.concatenate([_block_diag(jnp.real(bt)), _block_diag(jnp.imag(bt))], axis=1)
        ct_re = jnp.swapaxes(c_re[d], 1, 2)
        ct_im = jnp.swapaxes(c_im[d], 1, 2)
        cmat = jnp.concatenate([_block_diag(ct_re), _block_diag(-ct_im)], axis=0)
        out.append((bmat, cmat, jnp.real(lam_bar[d]).reshape(1, SSM_N), jnp.imag(lam_bar[d]).reshape(1, SSM_N)))
    return out


def _loss_call(y, target):
    t, d = y.shape
    tr = _pick(t, (ROW_TILE, 128, 64, 32, 16, 8))

    def body(y_ref, t_ref, acc_ref, dy_ref):
        @pl.when(pl.program_id(0) == 0)
        def _():
            acc_ref[...] = jnp.zeros_like(acc_ref)

        diff = y_ref[...] - t_ref[...]
        dy_ref[...] = diff * (1.0 / d)
        acc_ref[0:1, :] += jnp.sum(diff * diff, axis=0, keepdims=True)

    spec = pl.BlockSpec((tr, d), lambda i: (i, 0))
    return pl.pallas_call(
        body, name=f"loss_{t}", grid=(t // tr,),
        out_shape=[SDS((8, d), F32), SDS((t, d), F32)],
        in_specs=[spec, spec], out_specs=[pl.BlockSpec((8, d), lambda i: (0, 0)), spec],
    )(y, target)


def _adamw_call(w, g, m, v):
    r, c = w.shape
    tr = _pick(r, [t for t in (1024, 512, 256, 128, 64, 32, 16, 8) if t * c <= 256 * 1024])

    def body(w_ref, g_ref, m_ref, v_ref, d_ref, mo_ref, vo_ref):
        gv = g_ref[...]
        mn = ADAM_B1 * m_ref[...] + (1.0 - ADAM_B1) * gv
        vn = ADAM_B2 * v_ref[...] + (1.0 - ADAM_B2) * (gv * gv)
        m_hat = mn / (1.0 - ADAM_B1 ** ADAM_STEP)
        v_hat = vn / (1.0 - ADAM_B2 ** ADAM_STEP)
        d_ref[...] = -ADAM_LR * (m_hat / (jnp.sqrt(v_hat) + ADAM_EPS) + ADAM_WD * w_ref[...])
        mo_ref[...] = mn
        vo_ref[...] = vn

    spec = pl.BlockSpec((tr, c), lambda i: (i, 0))
    return pl.pallas_call(
        body, name=f"adamw_{r}", grid=(r // tr,),
        out_shape=[SDS((r, c), F32)] * 3, in_specs=[spec] * 4, out_specs=[spec] * 3,
    )(w, g, m, v)


def _forward(diff, gathered, dims, depth):
    b, l, lc = dims
    t_lat = b * l
    d_model = diff['x'].shape[-1]
    cos, sin = _rope_tables(l)
    nm_linear, nm_mlp = _make_nm_ops(*_rowwise("norm_mod", _norm_mod_f, (d_model,), l))
    resid_gate = _make_rowwise_op("resid_gate", _resid_gate_f, d_model, 2, 1, l)
    glu = _make_rowwise_op("glu", _glu_f, SSM_WIDTH, 3, 0, l)
    att_conv = _make_att_conv(dims, cos, sin)
    scans = [_make_scan(dims, 0), _make_scan(dims, 1)]

    c_act = jax.nn.silu(diff['c'])
    c_ctx_act = jax.nn.silu(diff['c_ctx'])
    mod_in = jnp.concatenate([c_act, c_ctx_act[None, :], jnp.zeros((MOD_ROWS - b - 1, d_model), F32)], axis=0)
    h = jnp.concatenate([diff['x'].reshape(t_lat, d_model), diff['ctx'].reshape(b * lc, d_model)], axis=0)

    for layer in range(depth):
        last = layer == depth - 1
        mod = linear_t(mod_in, gathered['w_ada'][layer], diff['w_ada'][layer])[:b + 1] + diff['b_ada'][layer][None, :]
        sh1, sc1, g1, sh2, sc2, g2 = [m.reshape(b + 1, 1, d_model) for m in jnp.split(mod, N_MOD, axis=-1)]
        ng = diff['norm_g'][layer]
        p = nm_linear(h, sh1, sc1, ng[0:1], gathered['w_in'][layer], diff['w_in'][layer])
        ac = att_conv(p, diff['conv_w'][layer], diff['attn_sink'][layer][None, :])
        u = p[:, IN_WIDTH - SSM_WIDTH:]
        mats = _ssm_mats(diff['ssm_lam_re'][layer], diff['ssm_lam_im'][layer], diff['ssm_log_dt'][layer],
                         diff['ssm_b_re'][layer], diff['ssm_b_im'][layer], diff['ssm_c_re'][layer],
                         diff['ssm_c_im'][layer])
        ys = []
        for d in range(2):
            bmat, cmat, lr, li = mats[d]
            hs = scans[d](linear(u, bmat), lr, li)
            ys.append(linear(hs, cmat))
        s = glu(ys[0], ys[1], u, diff['ssm_d'][layer][None, :], diff['w_glu'][layer], diff['b_glu'][layer][None, :])
        if last:
            ac, s, h = ac[:t_lat], s[:t_lat], h[:t_lat]
            g1, sh2, sc2, g2 = g1[:b], sh2[:b], sc2[:b], g2[:b]
        m = mix_linear(ac, s, gathered['w_out'][layer], diff['w_out'][layer])
        h = resid_gate(h, m, g1, ng[1:2])
        f = nm_mlp(h, sh2, sc2, ng[2:3], gathered['w_mlp_in'][layer], diff['w_mlp_in'][layer],
                   gathered['w_mlp_out'][layer], diff['w_mlp_out'][layer])
        h = resid_gate(h, f, g2, ng[3:4])
    return h


BIG_TRANSPOSED = {'w_ada': True, 'w_in': True, 'w_out': False, 'w_mlp_in': True, 'w_mlp_out': False}


def _big_rows(weights):
    blocks = []
    depth = weights['w_ada'].shape[0]
    for layer in range(depth):
        for n in BIG:
            w = weights[n][layer]
            blocks.append((layer, n, w.shape[1] if BIG_TRANSPOSED[n] else w.shape[0]))
    return blocks


def _stack_big(arrs, blocks, dtype):
    parts = []
    for layer, n, _ in blocks:
        w = arrs[n][layer]
        parts.append((w.T if BIG_TRANSPOSED[n] else w).astype(dtype))
    return jnp.concatenate(parts, axis=0)


def kernel(x, c, ctx, c_ctx, w_ada, b_ada, norm_g, w_in, conv_w, attn_sink, ssm_lam_re, ssm_lam_im, ssm_log_dt, ssm_b_re, ssm_b_im, ssm_c_re, ssm_c_im, ssm_d, w_glu, b_glu, w_out, w_mlp_in, w_mlp_out, loss_target, m_c_ctx, m_w_ada, m_b_ada, m_norm_g, m_w_in, m_conv_w, m_attn_sink, m_ssm_lam_re, m_ssm_lam_im, m_ssm_log_dt, m_ssm_b_re, m_ssm_b_im, m_ssm_c_re, m_ssm_c_im, m_ssm_d, m_w_glu, m_b_glu, m_w_out, m_w_mlp_in, m_w_mlp_out, v_c_ctx, v_w_ada, v_b_ada, v_norm_g, v_w_in, v_conv_w, v_attn_sink, v_ssm_lam_re, v_ssm_lam_im, v_ssm_log_dt, v_ssm_b_re, v_ssm_b_im, v_ssm_c_re, v_ssm_c_im, v_ssm_d, v_w_glu, v_b_glu, v_w_out, v_w_mlp_in, v_w_mlp_out):
    given = dict(locals())
    weights = {n: given[n] for n in WEIGHTS}
    moms = {n: given['m_' + n] for n in WEIGHTS}
    vars_ = {n: given['v_' + n] for n in WEIGHTS}
    b, l, d_model = x.shape
    lc = ctx.shape[1]
    dims = (b, l, lc)
    depth = w_ada.shape[0]

    core = lax.axis_index('c')
    me = 4 * lax.axis_index('x') + 2 * lax.axis_index('y') + core

    blocks = _big_rows(weights)
    big_all = _all_gather(_stack_big(weights, blocks, MXU_DTYPE))
    gathered = {n: [None] * depth for n in BIG}
    off = 0
    for layer, n, rows in blocks:
        gathered[n][layer] = big_all[:, off:off + rows].reshape(N_DEV * rows, -1)
        off += rows
    small_shapes = [weights[n].shape for n in SMALL]
    small_all = _all_gather(_pack([weights[n] for n in SMALL], F32, 8))
    small_full = {n: _from_pieces(pc, SHARD_AXIS[n])
                  for n, pc in zip(SMALL, _unpack(small_all, small_shapes, (N_DEV,)))}

    diff = {'x': x, 'c': c, 'ctx': ctx}
    for n in REPLICATED:
        diff[n] = weights[n]
    for n in SMALL:
        diff[n] = small_full[n]
    for n in BIG:
        diff[n] = [jnp.zeros(gathered[n][layer].shape, F32) for layer in range(depth)]
    y, vjp = jax.vjp(lambda dd: _forward(dd, gathered, dims, depth), diff)
    sq, dy = _loss_call(y, loss_target.reshape(b * l, d_model))
    loss = lax.psum(0.5 * jnp.sum(sq) / d_model, ('x', 'y', 'c'))
    grads = vjp(dy)[0]
    grad_x = grads['x']

    g8 = jnp.concatenate([grads[n][layer].reshape(N_DEV, rows, -1) for layer, n, rows in blocks], axis=1)
    chip_sums = _rs_add(g8, _rs_sibling(g8), core.reshape(1).astype(jnp.int32))
    g_big = _sum_slots(_rs_chips(chip_sums))
    big_grad = {n: [] for n in BIG}
    off = 0
    for layer, n, rows in blocks:
        blk = g_big[off:off + rows]
        big_grad[n].append(blk.T if BIG_TRANSPOSED[n] else blk)
        off += rows
    big_grad = {n: jnp.stack(v) for n, v in big_grad.items()}
    rest = SMALL + REPLICATED
    g_rest = _sum_slots(_all_gather(_pack([grads[n] for n in rest], F32, 8)))
    rest_grad = dict(zip(rest, _unpack(g_rest, [grads[n].shape for n in rest])))
    for n in SMALL:
        width = weights[n].shape[SHARD_AXIS[n]]
        rest_grad[n] = lax.dynamic_slice_in_dim(rest_grad[n], me * width, width, axis=SHARD_AXIS[n])

    out = {}
    for n in BIG:
        shape = weights[n].shape
        flat = [a.reshape(-1, shape[-1]) for a in (weights[n], big_grad[n], moms[n], vars_[n])]
        for kind, mat in zip(('delta', 'new_m', 'new_v'), _adamw_call(*flat)):
            out[(kind, n)] = mat.reshape(shape)
        out[('grad', n)] = big_grad[n]
    shapes = [weights[n].shape for n in rest]
    g_mat = _pack([rest_grad[n] for n in rest], F32, 8)
    packed = [_pack([src[n] for n in rest], F32, 8) for src in (weights, moms, vars_)]
    d_mat, mn_mat, vn_mat = _adamw_call(packed[0], g_mat, packed[1], packed[2])
    for kind, mat in (('delta', d_mat), ('new_m', mn_mat), ('new_v', vn_mat)):
        for n, arr in zip(rest, _unpack(mat, shapes)):
            out[(kind, n)] = arr
    for n in rest:
        out[('grad', n)] = rest_grad[n]
    return (loss, grad_x, *[out[(kind, n)] for kind in ('grad', 'delta', 'new_m', 'new_v') for n in WEIGHTS])
```

```python
import functools

import jax
import jax.numpy as jnp
from jax import lax
from jax.experimental import pallas as pl
from jax.experimental.pallas import tpu as pltpu

F32 = jnp.float32
MXU_DTYPE = jnp.bfloat16
SDS = jax.ShapeDtypeStruct

N_DEV = 8
HEAD_DIM = 64
N_Q_HEADS = 8
Q_PER_KV = 4
ATTN_WIDTH = 512
KV_WIDTH = 128
WINDOW = 128
Q_BLOCK = 128
GRID_W = 64
ROPE_BASE = 10000.0
ROPE_PAIRS = 16
CONV_WIDTH = 256
SSM_WIDTH = 256
SSM_GROUP = 16
SSM_GROUPS = 16
SSM_STATE = 64
SSM_N = SSM_GROUPS * SSM_STATE
IN_WIDTH = 1792
N_MOD = 6
EPS = 1e-6
NEG_INF = -1e30
ADAM_LR = 0.001
ADAM_B1 = 0.9
ADAM_B2 = 0.999
ADAM_EPS = 1e-08
ADAM_WD = 0.01
ADAM_STEP = 10
MOD_ROWS = 128
ROW_TILE = 256
PACK_ROWS = 256
LANES = 128
MESH = pl.DeviceIdType.MESH
_SCALE = HEAD_DIM ** -0.5
MIX_SPLIT = ATTN_WIDTH + CONV_WIDTH
MM_TILE = 1024
MM_VMEM_BYTES = 56 * 1024 * 1024

WEIGHTS = ['c_ctx', 'w_ada', 'b_ada', 'norm_g', 'w_in', 'conv_w', 'attn_sink', 'ssm_lam_re', 'ssm_lam_im',
           'ssm_log_dt', 'ssm_b_re', 'ssm_b_im', 'ssm_c_re', 'ssm_c_im', 'ssm_d', 'w_glu', 'b_glu', 'w_out',
           'w_mlp_in', 'w_mlp_out']
SHARD_AXIS = {'w_ada': 2, 'norm_g': 2, 'w_in': 2, 'conv_w': 2, 'w_glu': 1, 'w_out': 1, 'w_mlp_in': 2, 'w_mlp_out': 1}
BIG = ['w_ada', 'w_in', 'w_out', 'w_mlp_in', 'w_mlp_out']
SMALL = ['norm_g', 'conv_w', 'w_glu']
SHARDED = BIG + SMALL
REPLICATED = [n for n in WEIGHTS if n not in SHARD_AXIS]


def _pick(n, cands):
    for c in cands:
        if n % c == 0:
            return c
    return n


def _div_tile(n, cap):
    if n <= cap:
        return n
    for c in range(cap, LANES - 1, -LANES):
        if n % c == 0:
            return c
    return n


def _mx(x):
    return x.astype(MXU_DTYPE)


def _dg_nt(a, b):
    return lax.dot_general(a, b, (((1,), (1,)), ((), ())), preferred_element_type=F32)


def _dg_tn(a, b):
    return lax.dot_general(a, b, (((0,), (0,)), ((), ())), preferred_element_type=F32)


def _all_gather(x2d):
    r, cdim = x2d.shape

    def body(x_ref, out_ref, send_sems, recv_sems, local_sem):
        x, y, c = lax.axis_index("x"), lax.axis_index("y"), lax.axis_index("c")
        me, sibling = (x, y, c), (x, y, 1 - c)
        chips = [(1 - x, y), (x, 1 - y), (1 - x, 1 - y)]

        def slot(px, py, pc):
            return out_ref.at[4 * px + 2 * py + pc]

        def copy(k, block, to, src=None):
            return pltpu.make_async_remote_copy(
                src_ref=slot(*block) if src is None else src, dst_ref=slot(*block),
                send_sem=send_sems.at[k], recv_sem=recv_sems.at[k], device_id=to, device_id_type=MESH)

        mine = pltpu.make_async_copy(x_ref, slot(*me), local_sem)
        mine.start()
        first = [copy(0, me, sibling, src=x_ref)]
        first += [copy(1 + j, me, (*chip, c), src=x_ref) for j, chip in enumerate(chips)]
        for cp in first:
            cp.start()
        passed = [copy(4 + j, (*chip, c), sibling) for j, chip in enumerate(chips)]
        for j, chip in enumerate(chips):
            copy(1 + j, (*chip, c), me).wait_recv()
            passed[j].start()
        copy(0, sibling, me).wait_recv()
        for j, chip in enumerate(chips):
            copy(4 + j, (*chip, 1 - c), me).wait_recv()
        for cp in first + passed:
            cp.wait_send()
        mine.wait()

    return pl.pallas_call(
        body, name=f"all_gather_{r}x{cdim}_{jnp.dtype(x2d.dtype).name}",
        out_shape=SDS((N_DEV, r, cdim), x2d.dtype),
        in_specs=[pl.BlockSpec(memory_space=pl.ANY)],
        out_specs=pl.BlockSpec(memory_space=pl.ANY),
        scratch_shapes=[pltpu.SemaphoreType.DMA((7,)), pltpu.SemaphoreType.DMA((7,)), pltpu.SemaphoreType.DMA],
    )(x2d)


def _rs_sibling(g8):
    _, r, cdim = g8.shape

    def body(g_ref, out_ref, send_sems, recv_sems):
        x, y, c = lax.axis_index("x"), lax.axis_index("y"), lax.axis_index("c")
        copies = [pltpu.make_async_remote_copy(
            src_ref=g_ref.at[2 * k + (1 - c)], dst_ref=out_ref.at[k], send_sem=send_sems.at[k],
            recv_sem=recv_sems.at[k], device_id=(x, y, 1 - c), device_id_type=MESH) for k in range(4)]
        for cp in copies:
            cp.start()
        for cp in copies:
            cp.wait()

    return pl.pallas_call(
        body, name=f"rs_sibling_{r}x{cdim}",
        out_shape=SDS((4, r, cdim), g8.dtype),
        in_specs=[pl.BlockSpec(memory_space=pl.ANY)],
        out_specs=pl.BlockSpec(memory_space=pl.ANY),
        scratch_shapes=[pltpu.SemaphoreType.DMA((4,)), pltpu.SemaphoreType.DMA((4,))],
    )(g8)


def _rs_add(g8, sib, c_idx):
    _, r, cdim = g8.shape
    tr = _pick(r, (256, 128, 64, 32, 16))

    def body(c_ref, g_ref, s_ref, o_ref):
        del c_ref
        o_ref[...] = (g_ref[...] + s_ref[...]).astype(o_ref.dtype)

    return pl.pallas_call(
        body, name=f"rs_add_{r}x{cdim}",
        grid_spec=pltpu.PrefetchScalarGridSpec(
            num_scalar_prefetch=1, grid=(4, r // tr),
            in_specs=[pl.BlockSpec((1, tr, cdim), lambda k, i, c: (2 * k + c[0], i, 0)),
                      pl.BlockSpec((1, tr, cdim), lambda k, i, c: (k, i, 0))],
            out_specs=pl.BlockSpec((1, tr, cdim), lambda k, i, c: (k, i, 0))),
        out_shape=SDS((4, r, cdim), MXU_DTYPE),
    )(c_idx, g8, sib)


def _rs_chips(s4):
    _, r, cdim = s4.shape

    def body(s_ref, out_ref, send_sems, recv_sems, local_sem):
        x, y, c = lax.axis_index("x"), lax.axis_index("y"), lax.axis_index("c")
        me = 2 * x + y

        def peer(j):
            px = 1 - x if j & 2 else x
            py = 1 - y if j & 1 else y
            return (px, py, c), 2 * px + py

        def copy(j, landing):
            to, to_chip = peer(j)
            return pltpu.make_async_remote_copy(
                src_ref=s_ref.at[to_chip], dst_ref=out_ref.at[to_chip if landing else me],
                send_sem=send_sems.at[j - 1], recv_sem=recv_sems.at[j - 1], device_id=to, device_id_type=MESH)

        mine = pltpu.make_async_copy(s_ref.at[me], out_ref.at[me], local_sem)
        mine.start()
        sends = [copy(j, False) for j in range(1, 4)]
        for cp in sends:
            cp.start()
        for j in range(1, 4):
            copy(j, True).wait_recv()
        for cp in sends:
            cp.wait_send()
        mine.wait()

    return pl.pallas_call(
        body, name=f"rs_chips_{r}x{cdim}",
        out_shape=SDS(s4.shape, s4.dtype),
        in_specs=[pl.BlockSpec(memory_space=pl.ANY)],
        out_specs=pl.BlockSpec(memory_space=pl.ANY),
        scratch_shapes=[pltpu.SemaphoreType.DMA((3,)), pltpu.SemaphoreType.DMA((3,)), pltpu.SemaphoreType.DMA],
    )(s4)


def _sum_slots(g3):
    n, r, cdim = g3.shape
    tr = _pick(r, (256, 128, 64, 32, 16, 8))

    def body(g_ref, o_ref):
        acc = g_ref[0].astype(F32)
        for s in range(1, n):
            acc = acc + g_ref[s].astype(F32)
        o_ref[...] = acc

    return pl.pallas_call(
        body, name=f"sum_slots_{n}x{r}x{cdim}", grid=(r // tr,),
        out_shape=SDS((r, cdim), F32),
        in_specs=[pl.BlockSpec((n, tr, cdim), lambda i: (0, i, 0))],
        out_specs=pl.BlockSpec((tr, cdim), lambda i: (i, 0)),
    )(g3)


def _rows_of(n):
    return -(-n // (8 * LANES)) * 8


def _pack(arrs, dtype, row_mult):
    parts = []
    for a in arrs:
        flat = a.reshape(-1).astype(dtype)
        parts.append(jnp.pad(flat, (0, _rows_of(flat.size) * LANES - flat.size)).reshape(-1, LANES))
    rows = sum(p.shape[0] for p in parts)
    if rows % row_mult:
        parts.append(jnp.zeros((row_mult - rows % row_mult, LANES), dtype))
    return jnp.concatenate(parts, axis=0)


def _unpack(mat, shapes, lead=()):
    out, off = [], 0
    for s in shapes:
        n = 1
        for d in s:
            n *= d
        rows = _rows_of(n)
        part = mat[..., off:off + rows, :].reshape(lead + (rows * LANES,))
        out.append(part[..., :n].reshape(lead + tuple(s)))
        off += rows
    return out


def _to_pieces(g, axis):
    s = g.shape
    g = g.reshape(s[:axis] + (N_DEV, s[axis] // N_DEV) + s[axis + 1:])
    return jnp.moveaxis(g, axis, 0)


def _from_pieces(p, axis):
    p = jnp.moveaxis(p, 0, axis)
    s = p.shape
    return p.reshape(s[:axis] + (s[axis] * s[axis + 1],) + s[axis + 2:])


def _mm_call(kind, a, b, z=None, sqrelu=False, out_dtype=F32):
    if kind == 'nn':
        (m, k), n = a.shape, b.shape[1]
    elif kind == 'nt':
        (m, k), n = a.shape, b.shape[0]
    else:
        (k, m), n = a.shape, b.shape[1]
    bm = _div_tile(m, MM_TILE)
    bn = _div_tile(n, MM_TILE)
    bk = _div_tile(k, MM_TILE if kind == 'tn' else 2 * MM_TILE)
    nk = k // bk
    if sqrelu:
        out_dtype = MXU_DTYPE
    plain = not sqrelu and z is None and out_dtype == F32
    use_acc = nk > 1 and not plain

    def body(a_ref, b_ref, *rest):
        rest = list(rest)
        z_ref = rest.pop(0) if z is not None else None
        o_ref = rest.pop(0)
        act_ref = rest.pop(0) if sqrelu else None
        acc = rest.pop(0) if use_acc else None
        kk = pl.program_id(2)
        av, bv = _mx(a_ref[...]), _mx(b_ref[...])
        if kind == 'nn':
            prod = jnp.dot(av, bv, preferred_element_type=F32)
        elif kind == 'nt':
            prod = _dg_nt(av, bv)
        else:
            prod = _dg_tn(av, bv)

        def finish(r):
            if z_ref is not None:
                r = r * (2.0 * jnp.maximum(z_ref[...].astype(F32), 0.0))
            o_ref[...] = r.astype(o_ref.dtype)
            if act_ref is not None:
                rr = jnp.maximum(r, 0.0)
                act_ref[...] = (rr * rr).astype(act_ref.dtype)

        if nk == 1:
            finish(prod)
        else:
            tgt = acc if use_acc else o_ref

            @pl.when(kk == 0)
            def _():
                tgt[...] = prod

            @pl.when(kk > 0)
            def _():
                tgt[...] += prod

            if use_acc:
                @pl.when(kk == nk - 1)
                def _():
                    finish(acc[...])

    if kind == 'nn':
        a_spec = pl.BlockSpec((bm, bk), lambda i, j, kk: (i, kk))
        b_spec = pl.BlockSpec((bk, bn), lambda i, j, kk: (kk, j))
    elif kind == 'nt':
        a_spec = pl.BlockSpec((bm, bk), lambda i, j, kk: (i, kk))
        b_spec = pl.BlockSpec((bn, bk), lambda i, j, kk: (j, kk))
    else:
        a_spec = pl.BlockSpec((bk, bm), lambda i, j, kk: (kk, i))
        b_spec = pl.BlockSpec((bk, bn), lambda i, j, kk: (kk, j))
    o_spec = pl.BlockSpec((bm, bn), lambda i, j, kk: (i, j))
    in_specs, args = [a_spec, b_spec], [a, b]
    if z is not None:
        in_specs.append(o_spec)
        args.append(z)
    out_shape, out_specs = [SDS((m, n), out_dtype)], [o_spec]
    if sqrelu:
        out_shape.append(SDS((m, n), MXU_DTYPE))
        out_specs.append(o_spec)
    tag = kind + ('_sq' if sqrelu else '') + ('_z' if z is not None else '')
    res = pl.pallas_call(
        body, name=f"mm_{tag}_{m}x{k}x{n}", grid=(m // bm, n // bn, nk),
        out_shape=out_shape, in_specs=in_specs, out_specs=out_specs,
        scratch_shapes=[pltpu.VMEM((bm, bn), F32)] if use_acc else [],
        compiler_params=pltpu.CompilerParams(dimension_semantics=("parallel", "parallel", "arbitrary"),
                                             vmem_limit_bytes=MM_VMEM_BYTES),
    )(*args)
    return res if sqrelu else res[0]


@jax.custom_vjp
def linear(a, w):
    return _mm_call('nn', a, w)


def _linear_fwd(a, w):
    return _mm_call('nn', a, w), (a, w)


def _linear_bwd(res, g):
    a, w = res
    return _mm_call('nt', g, w), _mm_call('tn', a, g)


linear.defvjp(_linear_fwd, _linear_bwd)


@jax.custom_vjp
def linear_t(a, wt, wtd):
    return _mm_call('nt', a, wt)


def _linear_t_fwd(a, wt, wtd):
    return _mm_call('nt', a, wt), (a, wt)


def _linear_t_bwd(res, g):
    a, wt = res
    return _mm_call('nn', g, wt), jnp.zeros_like(wt), _mm_call('tn', g, a)


linear_t.defvjp(_linear_t_fwd, _linear_t_bwd)


@jax.custom_vjp
def mix_linear(ac, s, w, wd):
    return _mm_call('nn', _mx(jnp.concatenate([ac, s], axis=1)), w)


def _mix_linear_fwd(ac, s, w, wd):
    mix = _mx(jnp.concatenate([ac, s], axis=1))
    return _mm_call('nn', mix, w), (mix, w)


def _mix_linear_bwd(res, g):
    mix, w = res
    dmix = _mm_call('nt', g, w)
    return dmix[:, :MIX_SPLIT], dmix[:, MIX_SPLIT:], jnp.zeros_like(w), _mm_call('tn', mix, g)


mix_linear.defvjp(_mix_linear_fwd, _mix_linear_bwd)


def _make_nm_ops(nm_fwd, nm_bwd):
    @jax.custom_vjp
    def nm_linear(h, sh, sc, g, wt, wtd):
        return _mm_call('nt', nm_fwd((h,), (sh, sc), (g,), MXU_DTYPE)[0], wt)

    def nm_linear_fwd(h, sh, sc, g, wt, wtd):
        a = nm_fwd((h,), (sh, sc), (g,), MXU_DTYPE)[0]
        return _mm_call('nt', a, wt), (h, sh, sc, g, a, wt)

    def nm_linear_bwd(res, gp):
        h, sh, sc, g, a, wt = res
        da = _mm_call('nn', gp, wt)
        dwt = _mm_call('tn', gp, a)
        dh, dsh, dsc, dg = nm_bwd((h,), (sh, sc), (g,), (da,))
        return dh, dsh, dsc, dg, jnp.zeros_like(wt), dwt

    nm_linear.defvjp(nm_linear_fwd, nm_linear_bwd)

    @jax.custom_vjp
    def nm_mlp(h, sh, sc, g, w1t, w1td, w2, w2d):
        a = nm_fwd((h,), (sh, sc), (g,), MXU_DTYPE)[0]
        _, act = _mm_call('nt', a, w1t, sqrelu=True)
        return _mm_call('nn', act, w2)

    def nm_mlp_fwd(h, sh, sc, g, w1t, w1td, w2, w2d):
        a = nm_fwd((h,), (sh, sc), (g,), MXU_DTYPE)[0]
        zb, act = _mm_call('nt', a, w1t, sqrelu=True)
        return _mm_call('nn', act, w2), (h, sh, sc, g, a, w1t, w2, zb, act)

    def nm_mlp_bwd(res, gf):
        h, sh, sc, g, a, w1t, w2, zb, act = res
        dz = _mm_call('nt', gf, w2, z=zb, out_dtype=MXU_DTYPE)
        dw2 = _mm_call('tn', act, gf)
        da = _mm_call('nn', dz, w1t)
        dw1t = _mm_call('tn', dz, a)
        dh, dsh, dsc, dg = nm_bwd((h,), (sh, sc), (g,), (da,))
        return dh, dsh, dsc, dg, jnp.zeros_like(w1t), dw1t, jnp.zeros_like(w2), dw2

    nm_mlp.defvjp(nm_mlp_fwd, nm_mlp_bwd)
    return nm_linear, nm_mlp


def _rowwise(name, f, out_widths, seg_len):
    def specs(rows, segs, globs, tr):
        nseg = segs[0].shape[0] if segs else 1

        def seg_of(i):
            return jnp.minimum((i * tr) // seg_len, nseg - 1)

        row_specs = [pl.BlockSpec((tr, r.shape[1]), lambda i: (i, 0)) for r in rows]
        seg_specs = [pl.BlockSpec((1, 1, s.shape[2]), lambda i: (seg_of(i), 0, 0)) for s in segs]
        glob_specs = [pl.BlockSpec(g.shape, lambda i: (0, 0)) for g in globs]
        return seg_of, row_specs, seg_specs, glob_specs

    def fwd_call(rows, segs, globs, out_dtype=F32):
        t = rows[0].shape[0]
        tr = _pick(t, (ROW_TILE, 128, 64, 32, 16, 8))
        _, row_specs, seg_specs, glob_specs = specs(rows, segs, globs, tr)
        nr, ns = len(rows), len(segs)

        def body(*refs):
            ins, outs = refs[:nr + ns + len(globs)], refs[nr + ns + len(globs):]
            vals = [r[...] for r in ins[:nr]] + [r[0] for r in ins[nr:nr + ns]] + [r[...] for r in ins[nr + ns:]]
            for o_ref, v in zip(outs, f(*vals)):
                o_ref[...] = v.astype(o_ref.dtype)

        return pl.pallas_call(
            body, name=f"{name}_fwd_{t}_{jnp.dtype(out_dtype).name}", grid=(t // tr,),
            out_shape=[SDS((t, w), out_dtype) for w in out_widths],
            in_specs=row_specs + seg_specs + glob_specs,
            out_specs=[pl.BlockSpec((tr, w), lambda i: (i, 0)) for w in out_widths],
        )(*rows, *segs, *globs)

    def bwd_call(rows, segs, globs, douts):
        t = rows[0].shape[0]
        tr = _pick(t, (ROW_TILE, 128, 64, 32, 16, 8))
        seg_of, row_specs, seg_specs, glob_specs = specs(rows, segs, globs, tr)
        nr, ns, ng, no = len(rows), len(segs), len(globs), len(out_widths)

        def body(*refs):
            ins = refs[:nr + ns + ng]
            dos = refs[nr + ns + ng:nr + ns + ng + no]
            outs = refs[nr + ns + ng + no:]
            i = pl.program_id(0)
            first_of_seg = jnp.logical_or(i == 0, seg_of(i) != seg_of(jnp.maximum(i - 1, 0)))
            vals = [r[...] for r in ins[:nr]] + [r[0] for r in ins[nr:nr + ns]] + [r[...] for r in ins[nr + ns:]]
            _, vjp = jax.vjp(f, *vals)
            grads = vjp(tuple(d[...] for d in dos))
            for o_ref, gval in zip(outs[:nr], grads[:nr]):
                o_ref[...] = gval
            for o_ref, gval in zip(outs[nr:nr + ns], grads[nr:nr + ns]):
                @pl.when(first_of_seg)
                def _(o_ref=o_ref):
                    o_ref[...] = jnp.zeros_like(o_ref)
                o_ref[0] += gval
            for o_ref, gval in zip(outs[nr + ns:], grads[nr + ns:]):
                @pl.when(i == 0)
                def _(o_ref=o_ref):
                    o_ref[...] = jnp.zeros_like(o_ref)
                o_ref[...] += gval

        do_specs = [pl.BlockSpec((tr, w), lambda i: (i, 0)) for w in out_widths]
        return pl.pallas_call(
            body, name=f"{name}_bwd_{t}", grid=(t // tr,),
            out_shape=[SDS(r.shape, F32) for r in rows] + [SDS(s.shape, F32) for s in segs]
            + [SDS(g.shape, F32) for g in globs],
            in_specs=row_specs + seg_specs + glob_specs + do_specs,
            out_specs=row_specs + seg_specs + glob_specs,
        )(*rows, *segs, *globs, *douts)

    return fwd_call, bwd_call


def _norm_mod_f(x, shift, scale, g):
    r = lax.rsqrt(jnp.mean(x * x, axis=-1, keepdims=True) + EPS)
    return ((x * r) * g * (1.0 + scale) + shift,)


def _resid_gate_f(h, m, gate, g):
    r = lax.rsqrt(jnp.mean(m * m, axis=-1, keepdims=True) + EPS)
    return (h + gate * ((m * r) * g),)


def _glu_f(y0, y1, u, d, w, b):
    y = y0 + y1 + d * u
    g = 0.5 * y * (1.0 + jnp.tanh(0.7978845608028654 * (y + 0.044715 * (y * y * y))))
    zz = jnp.dot(_mx(g), _mx(w), preferred_element_type=F32) + b
    return (g * (1.0 / (1.0 + jnp.exp(-zz))),)


def _make_rowwise_op(name, f, out_width, n_rows, n_segs, seg_len):
    fwd_call, bwd_call = _rowwise(name, f, (out_width,), seg_len)

    @jax.custom_vjp
    def op(*args):
        return fwd_call(args[:n_rows], args[n_rows:n_rows + n_segs], args[n_rows + n_segs:])[0]

    def op_fwd(*args):
        return fwd_call(args[:n_rows], args[n_rows:n_rows + n_segs], args[n_rows + n_segs:])[0], args

    def op_bwd(args, g):
        return tuple(bwd_call(args[:n_rows], args[n_rows:n_rows + n_segs], args[n_rows + n_segs:], (g,)))

    op.defvjp(op_fwd, op_bwd)
    return op


def _lane():
    return lax.broadcasted_iota(jnp.int32, (1, LANES), 1)


def _hm(off):
    lane = _lane()
    return jnp.logical_and(lane >= off, lane < off + HEAD_DIM)


def _col(tile, hq):
    return jnp.sum(jnp.where(_lane() == hq, tile, 0.0), axis=1, keepdims=True)


def _setcol(tile, hq, col):
    return jnp.where(_lane() == hq, col, tile)


def _head_fwd(qa, groups, sk):
    ss, m = [], None
    for kmat, _, mask in groups:
        s = _dg_nt(qa, kmat) * _SCALE
        if mask is not None:
            s = jnp.where(mask, s, NEG_INF)
        ss.append(s)
        mm = jnp.max(s, axis=1, keepdims=True)
        m = mm if m is None else jnp.maximum(m, mm)
    m = jnp.maximum(m, sk)
    l = jnp.exp(sk - m)
    pv = None
    for s, (_, vmat, _) in zip(ss, groups):
        p = jnp.exp(s - m)
        l = l + jnp.sum(p, axis=1, keepdims=True)
        t = jnp.dot(_mx(p), vmat, preferred_element_type=F32)
        pv = t if pv is None else pv + t
    return pv / l, m + jnp.log(l)


def _head_bwd(qa, doa, groups, sk, lse_h, delta_h):
    dq, outs = None, []
    for kmat, vmat, mask in groups:
        s = _dg_nt(qa, kmat) * _SCALE
        if mask is not None:
            s = jnp.where(mask, s, NEG_INF)
        p = jnp.exp(s - lse_h)
        dp = _dg_nt(doa, vmat)
        ds = p * (dp - delta_h) * _SCALE
        t = jnp.dot(_mx(ds), kmat, preferred_element_type=F32)
        dq = t if dq is None else dq + t
        outs.append((_mx(ds), _mx(p)))
    return dq, outs, jnp.exp(sk - lse_h)


def _kv_group(h):
    return [(g, Q_PER_KV * h + g, (Q_PER_KV * h + g) // 2, (g % 2) != h) for g in range(Q_PER_KV)]


def _to_kv_lanes(pair_tile, roll):
    if not roll:
        return _mx(pair_tile)
    return _mx(pltpu.roll(pair_tile.astype(F32), HEAD_DIM, 1))


def _kv_masked(x, h):
    return _mx(jnp.where(_hm(h * HEAD_DIM), x.astype(F32), 0.0))


def _attn_fwd_block(sink_ref, q_ref, kv, n, o_ref, lse_ref):
    lse_t = jnp.zeros((n, LANES), F32)
    for h in range(2):
        groups = [(_kv_masked(k, h), _kv_masked(v, h), mask) for k, v, mask in kv]
        heads = _kv_group(h)
        q = jnp.concatenate([_to_kv_lanes(q_ref[:, pair * LANES:(pair + 1) * LANES], roll)
                             for _, _, pair, roll in heads], axis=0)
        sk = jnp.concatenate([jnp.full((n, 1), sink_ref[0, hq], F32) for _, hq, _, _ in heads], axis=0)
        o, lse = _head_fwd(q, groups, sk)
        o_pair = None
        for g, hq, pair, roll in heads:
            og = o[g * n:(g + 1) * n]
            og = pltpu.roll(og, HEAD_DIM, 1) if roll else og
            lse_t = _setcol(lse_t, hq, lse[g * n:(g + 1) * n])
            if g % 2 == 0:
                o_pair = og
            else:
                o_ref[:, pair * LANES:(pair + 1) * LANES] = o_pair + og
    lse_ref[...] = lse_t


def _attn_bwd_block(sink_ref, q_ref, do_ref, o_ref, lse_t, kv, want, n, dq_ref):
    delta_t = jnp.zeros((n, LANES), F32)
    dsk = jnp.zeros((1, LANES), F32)
    dkv = [None] * len(kv)
    for h in range(2):
        hm = _hm(h * HEAD_DIM)
        groups = [(_kv_masked(k, h), _kv_masked(v, h), mask) for k, v, mask in kv]
        heads = _kv_group(h)
        qs, dos, sks, lses, deltas = [], [], [], [], []
        for g, hq, pair, roll in heads:
            sl = slice(pair * LANES, (pair + 1) * LANES)
            do_p = do_ref[:, sl]
            delta_h = jnp.sum(jnp.where(_hm((hq % 2) * HEAD_DIM), do_p * o_ref[:, sl], 0.0), axis=1, keepdims=True)
            delta_t = _setcol(delta_t, hq, delta_h)
            qs.append(_to_kv_lanes(q_ref[:, sl], roll))
            dos.append(_to_kv_lanes(do_p, roll))
            sks.append(jnp.full((n, 1), sink_ref[0, hq], F32))
            lses.append(_col(lse_t, hq))
            deltas.append(delta_h)
        q, do = jnp.concatenate(qs, axis=0), jnp.concatenate(dos, axis=0)
        delta = jnp.concatenate(deltas, axis=0)
        dq, outs, p_s = _head_bwd(q, do, groups, jnp.concatenate(sks, axis=0), jnp.concatenate(lses, axis=0), delta)
        dq_pair = None
        for g, hq, pair, roll in heads:
            dg = dq[g * n:(g + 1) * n]
            dg = pltpu.roll(dg, HEAD_DIM, 1) if roll else dg
            dsk = dsk + jnp.where(_lane() == hq,
                                  jnp.sum(-p_s[g * n:(g + 1) * n] * deltas[g], axis=0, keepdims=True), 0.0)
            if g % 2 == 0:
                dq_pair = dg
            else:
                dq_ref[:, pair * LANES:(pair + 1) * LANES] = dq_pair + dg
        for gi, (ds, pp) in enumerate(outs):
            if want[gi]:
                dk_h = jnp.where(hm, _dg_tn(ds, q), 0.0)
                dv_h = jnp.where(hm, _dg_tn(pp, do), 0.0)
                dkv[gi] = (dk_h, dv_h) if dkv[gi] is None else (dkv[gi][0] + dk_h, dkv[gi][1] + dv_h)
    return delta_t, dsk, dkv


def _partner(t):
    lane = lax.broadcasted_iota(jnp.int32, t.shape, 1)
    return jnp.where((lane & ROPE_PAIRS) == 0, pltpu.roll(t, LANES - ROPE_PAIRS, 1), pltpu.roll(t, ROPE_PAIRS, 1))


def _rope_tables(n_tokens):
    rows = n_tokens // GRID_W
    row = jnp.broadcast_to(jnp.arange(rows)[:, None], (rows, GRID_W)).reshape(-1)
    col = jnp.broadcast_to(jnp.arange(GRID_W)[None, :], (rows, GRID_W)).reshape(-1)
    freqs = ROPE_BASE ** (-jnp.arange(ROPE_PAIRS, dtype=F32) / ROPE_PAIRS)
    ang = jnp.concatenate([row[:, None].astype(F32) * freqs, col[:, None].astype(F32) * freqs], axis=-1)
    c, s = jnp.cos(ang), jnp.sin(ang)
    n = ROPE_PAIRS
    cos64 = jnp.concatenate([c[:, :n], c[:, :n], c[:, n:], c[:, n:]], axis=1)
    sin64 = jnp.concatenate([-s[:, :n], s[:, :n], -s[:, n:], s[:, n:]], axis=1)
    return jnp.tile(cos64, (1, 2)), jnp.tile(sin64, (1, 2))


def _rope_call(q_src, k_src, cos, sin, dims, inverse):
    b, l, _ = dims
    t_lat = b * l
    tr = _pick(l, (ROW_TILE, 128))
    per = l // tr
    out_dtype = F32 if inverse else MXU_DTYPE

    def body(q_ref, k_ref, c_ref, s_ref, qo_ref, ko_ref):
        c, s = c_ref[...], s_ref[...]

        def rot(t):
            if inverse:
                return t * c + _partner(t * s)
            return t * c + _partner(t) * s

        for j in range(ATTN_WIDTH // LANES):
            qo_ref[:, j * LANES:(j + 1) * LANES] = rot(q_ref[:, j * LANES:(j + 1) * LANES]).astype(out_dtype)
        ko_ref[...] = rot(k_ref[...]).astype(out_dtype)

    kcol = 0 if inverse else ATTN_WIDTH // KV_WIDTH
    return pl.pallas_call(
        body, name=f"rope_{'inv' if inverse else 'fwd'}_{t_lat}", grid=(t_lat // tr,),
        out_shape=[SDS((t_lat, ATTN_WIDTH), out_dtype), SDS((t_lat, KV_WIDTH), out_dtype)],
        in_specs=[pl.BlockSpec((tr, ATTN_WIDTH), lambda i: (i, 0)),
                  pl.BlockSpec((tr, KV_WIDTH), lambda i: (i, kcol)),
                  pl.BlockSpec((tr, LANES), lambda i: (i % per, 0)),
                  pl.BlockSpec((tr, LANES), lambda i: (i % per, 0))],
        out_specs=[pl.BlockSpec((tr, ATTN_WIDTH), lambda i: (i, 0)), pl.BlockSpec((tr, KV_WIDTH), lambda i: (i, 0))],
    )(q_src, k_src, cos, sin)


_SMEM_SPEC = pl.BlockSpec(memory_space=pltpu.SMEM)
_KCOL = ATTN_WIDTH // KV_WIDTH
_VCOL = _KCOL + 1


def _win_specs(arr_cols, nb, col):
    del arr_cols
    return [pl.BlockSpec((Q_BLOCK, KV_WIDTH), lambda b, i, d=d: (b * nb + jnp.clip(i + d, 0, nb - 1), col))
            for d in (-1, 0, 1)]


def _win_mask(i, l):
    shape = (Q_PER_KV * Q_BLOCK, 3 * Q_BLOCK)
    qpos = i * Q_BLOCK + (lax.broadcasted_iota(jnp.int32, shape, 0) & (Q_BLOCK - 1))
    kpos = (i - 1) * Q_BLOCK + lax.broadcasted_iota(jnp.int32, shape, 1)
    return jnp.logical_and(jnp.abs(qpos - kpos) <= WINDOW, jnp.logical_and(kpos >= 0, kpos < l))


def _attn_lat_fwd(qr, kr, p, sink, dims):
    b, l, lc = dims
    nb = l // Q_BLOCK
    t_lat = b * l
    cbase = t_lat // lc

    def body(sink_ref, q_ref, kp, kc, kn, vp, vc, vn, ck_ref, cv_ref, o_ref, lse_ref):
        kwin = jnp.concatenate([kp[...], kc[...], kn[...]], axis=0)
        vwin = jnp.concatenate([vp[...], vc[...], vn[...]], axis=0)
        kv = [(kwin, vwin, _win_mask(pl.program_id(1), l)), (ck_ref[...], cv_ref[...], None)]
        _attn_fwd_block(sink_ref, q_ref, kv, Q_BLOCK, o_ref, lse_ref)

    return pl.pallas_call(
        body, name=f"attn_lat_fwd_{t_lat}", grid=(b, nb),
        out_shape=[SDS((t_lat, ATTN_WIDTH), F32), SDS((t_lat, LANES), F32)],
        in_specs=[_SMEM_SPEC, pl.BlockSpec((Q_BLOCK, ATTN_WIDTH), lambda bb, i: (bb * nb + i, 0))]
        + _win_specs(None, nb, 0) + _win_specs(None, nb, _VCOL)
        + [pl.BlockSpec((lc, KV_WIDTH), lambda bb, i: (cbase + bb, _KCOL)),
           pl.BlockSpec((lc, KV_WIDTH), lambda bb, i: (cbase + bb, _VCOL))],
        out_specs=[pl.BlockSpec((Q_BLOCK, ATTN_WIDTH), lambda bb, i: (bb * nb + i, 0)),
                   pl.BlockSpec((Q_BLOCK, LANES), lambda bb, i: (bb * nb + i, 0))],
    )(sink, qr, kr, kr, kr, p, p, p, p, p)


def _attn_lat_bwd_dq(qr, kr, p, sink, o, lse, dout, dims):
    b, l, lc = dims
    nb = l // Q_BLOCK
    t_lat = b * l
    cbase = t_lat // lc

    def body(sink_ref, q_ref, kp, kc, kn, vp, vc, vn, ck_ref, cv_ref, o_ref, lse_ref, do_ref,
             dq_ref, delta_ref, dkc_ref, dvc_ref, dsk_ref):
        bb, i = pl.program_id(0), pl.program_id(1)

        @pl.when(i == 0)
        def _():
            dkc_ref[...] = jnp.zeros_like(dkc_ref)
            dvc_ref[...] = jnp.zeros_like(dvc_ref)

        @pl.when(jnp.logical_and(bb == 0, i == 0))
        def _():
            dsk_ref[...] = jnp.zeros_like(dsk_ref)

        kwin = jnp.concatenate([kp[...], kc[...], kn[...]], axis=0)
        vwin = jnp.concatenate([vp[...], vc[...], vn[...]], axis=0)
        kv = [(kwin, vwin, _win_mask(i, l)), (ck_ref[...], cv_ref[...], None)]
        delta_t, dsk, dkv = _attn_bwd_block(sink_ref, q_ref, do_ref, o_ref, lse_ref[...], kv, [False, True], Q_BLOCK,
                                            dq_ref)
        delta_ref[...] = delta_t
        dkc_ref[...] += dkv[1][0]
        dvc_ref[...] += dkv[1][1]
        dsk_ref[0:1, :] += dsk

    qspec = pl.BlockSpec((Q_BLOCK, ATTN_WIDTH), lambda bb, i: (bb * nb + i, 0))
    tspec = pl.BlockSpec((Q_BLOCK, LANES), lambda bb, i: (bb * nb + i, 0))
    cspec = pl.BlockSpec((lc, KV_WIDTH), lambda bb, i: (bb, 0))
    return pl.pallas_call(
        body, name=f"attn_lat_bwd_dq_{t_lat}", grid=(b, nb),
        out_shape=[SDS((t_lat, ATTN_WIDTH), F32), SDS((t_lat, LANES), F32), SDS((b * lc, KV_WIDTH), F32),
                   SDS((b * lc, KV_WIDTH), F32), SDS((8, LANES), F32)],
        in_specs=[_SMEM_SPEC, qspec] + _win_specs(None, nb, 0) + _win_specs(None, nb, _VCOL)
        + [pl.BlockSpec((lc, KV_WIDTH), lambda bb, i: (cbase + bb, _KCOL)),
           pl.BlockSpec((lc, KV_WIDTH), lambda bb, i: (cbase + bb, _VCOL)), qspec, tspec, qspec],
        out_specs=[qspec, tspec, cspec, cspec, pl.BlockSpec((8, LANES), lambda bb, i: (0, 0))],
    )(sink, qr, kr, kr, kr, p, p, p, p, p, o, lse, dout)


def _attn_lat_bwd_dkv(qr, kr, p, lse, delta, dout, dims):
    b, l, _ = dims
    nb = l // Q_BLOCK
    t_lat = b * l

    def body(k_ref, v_ref, *refs):
        j = pl.program_id(1)
        kj, vj = k_ref[...], v_ref[...]
        n_q = 3 * Q_PER_KV * Q_BLOCK
        sub = lax.broadcasted_iota(jnp.int32, (8, LANES), 0)
        col = lax.broadcasted_iota(jnp.int32, (Q_BLOCK, n_q), 1)
        i_of = j + col // (Q_PER_KV * Q_BLOCK) - 1
        qpos = i_of * Q_BLOCK + (col & (Q_BLOCK - 1))
        kpos = j * Q_BLOCK + lax.broadcasted_iota(jnp.int32, (Q_BLOCK, n_q), 0)
        mask = jnp.logical_and(jnp.abs(qpos - kpos) <= WINDOW, jnp.logical_and(i_of >= 0, i_of < nb))
        lse_rows = [refs[4 * n + 2][...].T[0:8, :] for n in range(3)]
        delta_rows = [refs[4 * n + 3][...].T[0:8, :] for n in range(3)]
        dk = jnp.zeros((Q_BLOCK, KV_WIDTH), F32)
        dv = jnp.zeros((Q_BLOCK, KV_WIDTH), F32)
        for h in range(2):
            hm = _hm(h * HEAD_DIM)
            kh, vh = _kv_masked(kj, h), _kv_masked(vj, h)
            qs, dos, lrow, drow = [], [], [], []
            for n in range(3):
                q_ref, do_ref = refs[4 * n], refs[4 * n + 1]
                for _, hq, pair, roll in _kv_group(h):
                    sl = slice(pair * LANES, (pair + 1) * LANES)
                    qs.append(_to_kv_lanes(q_ref[:, sl], roll))
                    dos.append(_to_kv_lanes(do_ref[:, sl], roll))
                    lrow.append(jnp.sum(jnp.where(sub == hq, lse_rows[n], 0.0), axis=0, keepdims=True))
                    drow.append(jnp.sum(jnp.where(sub == hq, delta_rows[n], 0.0), axis=0, keepdims=True))
            q, do = jnp.concatenate(qs, axis=0), jnp.concatenate(dos, axis=0)
            lse_r, delta_r = jnp.concatenate(lrow, axis=1), jnp.concatenate(drow, axis=1)
            s_t = jnp.where(mask, _dg_nt(kh, q) * _SCALE, NEG_INF)
            p_t = jnp.exp(s_t - lse_r)
            ds_t = p_t * (_dg_nt(vh, do) - delta_r) * _SCALE
            dv = dv + jnp.where(hm, jnp.dot(_mx(p_t), do, preferred_element_type=F32), 0.0)
            dk = dk + jnp.where(hm, jnp.dot(_mx(ds_t), q, preferred_element_type=F32), 0.0)
        dk_ref, dv_ref = refs[12], refs[13]
        dk_ref[...] = dk
        dv_ref[...] = dv

    def blk(width, d, col=0):
        return pl.BlockSpec((Q_BLOCK, width), lambda bb, j: (bb * nb + jnp.clip(j + d, 0, nb - 1), col))

    in_specs = [blk(KV_WIDTH, 0), blk(KV_WIDTH, 0, _VCOL)]
    args = [kr, p]
    for d in (-1, 0, 1):
        in_specs += [blk(ATTN_WIDTH, d), blk(ATTN_WIDTH, d), blk(LANES, d), blk(LANES, d)]
        args += [qr, dout, lse, delta]
    return pl.pallas_call(
        body, name=f"attn_lat_bwd_dkv_{t_lat}", grid=(b, nb),
        out_shape=[SDS((t_lat, KV_WIDTH), F32), SDS((t_lat, KV_WIDTH), F32)],
        in_specs=in_specs, out_specs=[blk(KV_WIDTH, 0), blk(KV_WIDTH, 0)],
    )(*args)


def _attn_ctx_fwd(p, sink, dims):
    b, l, lc = dims
    cbase = b * l // lc

    def body(sink_ref, q_ref, k_ref, v_ref, o_ref, lse_ref):
        _attn_fwd_block(sink_ref, q_ref, [(k_ref[...], v_ref[...], None)], lc, o_ref, lse_ref)

    return pl.pallas_call(
        body, name=f"attn_ctx_fwd_{b * lc}", grid=(b,),
        out_shape=[SDS((b * lc, ATTN_WIDTH), F32), SDS((b * lc, LANES), F32)],
        in_specs=[_SMEM_SPEC, pl.BlockSpec((lc, ATTN_WIDTH), lambda bb: (cbase + bb, 0)),
                  pl.BlockSpec((lc, KV_WIDTH), lambda bb: (cbase + bb, _KCOL)),
                  pl.BlockSpec((lc, KV_WIDTH), lambda bb: (cbase + bb, _VCOL))],
        out_specs=[pl.BlockSpec((lc, ATTN_WIDTH), lambda bb: (bb, 0)), pl.BlockSpec((lc, LANES), lambda bb: (bb, 0))],
    )(sink, p, p, p)


def _attn_ctx_bwd(p, sink, o, lse, dout, dims):
    b, l, lc = dims
    cbase = b * l // lc

    def body(sink_ref, q_ref, k_ref, v_ref, o_ref, lse_ref, do_ref, dq_ref, dk_ref, dv_ref, dsk_ref):
        bb = pl.program_id(0)

        @pl.when(bb == 0)
        def _():
            dsk_ref[...] = jnp.zeros_like(dsk_ref)

        _, dsk, dkv = _attn_bwd_block(sink_ref, q_ref, do_ref, o_ref, lse_ref[...], [(k_ref[...], v_ref[...], None)],
                                      [True], lc, dq_ref)
        dk_ref[...] = dkv[0][0]
        dv_ref[...] = dkv[0][1]
        dsk_ref[0:1, :] += dsk

    qs = pl.BlockSpec((lc, ATTN_WIDTH), lambda bb: (bb, 0))
    ks = pl.BlockSpec((lc, KV_WIDTH), lambda bb: (bb, 0))
    return pl.pallas_call(
        body, name=f"attn_ctx_bwd_{b * lc}", grid=(b,),
        out_shape=[SDS((b * lc, ATTN_WIDTH), F32), SDS((b * lc, KV_WIDTH), F32), SDS((b * lc, KV_WIDTH), F32),
                   SDS((8, LANES), F32)],
        in_specs=[_SMEM_SPEC, pl.BlockSpec((lc, ATTN_WIDTH), lambda bb: (cbase + bb, 0)),
                  pl.BlockSpec((lc, KV_WIDTH), lambda bb: (cbase + bb, _KCOL)),
                  pl.BlockSpec((lc, KV_WIDTH), lambda bb: (cbase + bb, _VCOL)),
                  qs, pl.BlockSpec((lc, LANES), lambda bb: (bb, 0)),
                  pl.BlockSpec((lc, ATTN_WIDTH), lambda bb: (cbase + bb, 0))],
        out_specs=[qs, ks, ks, pl.BlockSpec((8, LANES), lambda bb: (0, 0))],
    )(sink, p, p, p, o, lse, dout)


_CONV_COL = (ATTN_WIDTH + 2 * KV_WIDTH) // CONV_WIDTH


def _shift_prev(z, n):
    rows = lax.broadcasted_iota(jnp.int32, z.shape, 0)
    return jnp.where(rows == 0, 0.0, pltpu.roll(z, 1, 0))


def _shift_next(z, n):
    rows = lax.broadcasted_iota(jnp.int32, z.shape, 0)
    return jnp.where(rows == n - 1, 0.0, pltpu.roll(z, n - 1, 0))


def _conv_fwd(p, w, base, n_seq, ls):
    def body(cb_ref, cc_ref, cx_ref, w_ref, o_ref):
        z = cc_ref[...] * cx_ref[...]
        c3 = _shift_prev(z, ls) * w_ref[0:1, :] + z * w_ref[1:2, :] + _shift_next(z, ls) * w_ref[2:3, :]
        o_ref[...] = cb_ref[...] * c3

    return pl.pallas_call(
        body, name=f"conv_fwd_{n_seq}x{ls}", grid=(n_seq,),
        out_shape=SDS((n_seq * ls, CONV_WIDTH), F32),
        in_specs=[pl.BlockSpec((ls, CONV_WIDTH), lambda s, c=c: (base + s, _CONV_COL + c)) for c in range(3)]
        + [pl.BlockSpec(w.shape, lambda s: (0, 0))],
        out_specs=pl.BlockSpec((ls, CONV_WIDTH), lambda s: (s, 0)),
    )(p, p, p, w)


def _conv_bwd(p, w, dout, base, n_seq, ls):
    dcol = ATTN_WIDTH // CONV_WIDTH

    def body(cb_ref, cc_ref, cx_ref, w_ref, do_ref, dcb_ref, dcc_ref, dcx_ref, dw_ref):
        @pl.when(pl.program_id(0) == 0)
        def _():
            dw_ref[...] = jnp.zeros_like(dw_ref)

        cc, cx = cc_ref[...], cx_ref[...]
        z = cc * cx
        zp, zn = _shift_prev(z, ls), _shift_next(z, ls)
        c3 = zp * w_ref[0:1, :] + z * w_ref[1:2, :] + zn * w_ref[2:3, :]
        do = do_ref[...]
        dcb_ref[...] = do * c3
        e = do * cb_ref[...]
        dz = _shift_next(e, ls) * w_ref[0:1, :] + e * w_ref[1:2, :] + _shift_prev(e, ls) * w_ref[2:3, :]
        dcc_ref[...] = dz * cx
        dcx_ref[...] = dz * cc
        dw_ref[0:1, :] += jnp.sum(e * zp, axis=0, keepdims=True)
        dw_ref[1:2, :] += jnp.sum(e * z, axis=0, keepdims=True)
        dw_ref[2:3, :] += jnp.sum(e * zn, axis=0, keepdims=True)

    ospec = pl.BlockSpec((ls, CONV_WIDTH), lambda s: (s, 0))
    return pl.pallas_call(
        body, name=f"conv_bwd_{n_seq}x{ls}", grid=(n_seq,),
        out_shape=[SDS((n_seq * ls, CONV_WIDTH), F32)] * 3 + [SDS((8, CONV_WIDTH), F32)],
        in_specs=[pl.BlockSpec((ls, CONV_WIDTH), lambda s, c=c: (base + s, _CONV_COL + c)) for c in range(3)]
        + [pl.BlockSpec(w.shape, lambda s: (0, 0)), pl.BlockSpec((ls, CONV_WIDTH), lambda s: (base + s, dcol))],
        out_specs=[ospec, ospec, ospec, pl.BlockSpec((8, CONV_WIDTH), lambda s: (0, 0))],
    )(p, p, p, w, dout)


def _make_att_conv(dims, cos, sin):
    b, l, lc = dims
    t_lat = b * l

    def forward(p, conv_w, sink):
        qr, kr = _rope_call(p, p, cos, sin, dims, inverse=False)
        o_lat, lse_lat = _attn_lat_fwd(qr, kr, p, sink, dims)
        o_ctx, lse_ctx = _attn_ctx_fwd(p, sink, dims)
        conv_lat = _conv_fwd(p, conv_w, 0, b, l)
        conv_ctx = _conv_fwd(p, conv_w, t_lat // lc, b, lc)
        out = jnp.concatenate([jnp.concatenate([o_lat, conv_lat], axis=1),
                               jnp.concatenate([o_ctx, conv_ctx], axis=1)], axis=0)
        return out, (p, conv_w, sink, qr, kr, o_lat, lse_lat, o_ctx, lse_ctx)

    @jax.custom_vjp
    def op(p, conv_w, sink):
        return forward(p, conv_w, sink)[0]

    def op_bwd(res, dout):
        p, conv_w, sink, qr, kr, o_lat, lse_lat, o_ctx, lse_ctx = res
        dqr, delta, dkc1, dvc1, dsk1 = _attn_lat_bwd_dq(qr, kr, p, sink, o_lat, lse_lat, dout, dims)
        dkr, dv = _attn_lat_bwd_dkv(qr, kr, p, lse_lat, delta, dout, dims)
        dq, dk = _rope_call(dqr, dkr, cos, sin, dims, inverse=True)
        dqc, dkc2, dvc2, dsk2 = _attn_ctx_bwd(p, sink, o_ctx, lse_ctx, dout, dims)
        dcb_l, dcc_l, dcx_l, dw_l = _conv_bwd(p, conv_w, dout, 0, b, l)
        dcb_c, dcc_c, dcx_c, dw_c = _conv_bwd(p, conv_w, dout, t_lat // lc, b, lc)
        zeros_u = jnp.zeros((p.shape[0], SSM_WIDTH), F32)
        lat = jnp.concatenate([dq, dk, dv, dcb_l, dcc_l, dcx_l], axis=1)
        ctx = jnp.concatenate([dqc, dkc1 + dkc2, dvc1 + dvc2, dcb_c, dcc_c, dcx_c], axis=1)
        dp = jnp.concatenate([jnp.concatenate([lat, ctx], axis=0), zeros_u], axis=1)
        dsink = (dsk1 + dsk2)[0:1, :N_Q_HEADS]
        return dp, (dw_l + dw_c)[:3], dsink

    op.defvjp(forward, op_bwd)
    return op


def _scan_call(x, lam_r, lam_i, dims, ctx_first, dec, h=None):
    b, l, lc = dims
    n = SSM_N
    tc = _pick(lc, (256, 128, 64, 32, 16, 8))
    nc_c, nc_l = lc // tc, l // tc
    n_chunks = nc_c + nc_l
    ctx_base = b * l // tc
    g8 = tc // 8
    with_h = h is not None

    def chunk(bb, j):
        if ctx_first:
            is_ctx = j < nc_c
            jj = jnp.where(is_ctx, j, j - nc_c)
        else:
            is_ctx = j >= nc_l
            jj = jnp.where(is_ctx, j - nc_l, j)
        ic = nc_c - 1 - jj if dec else jj
        il = nc_l - 1 - jj if dec else jj
        return jnp.where(is_ctx, ctx_base + bb * nc_c + ic, bb * nc_l + il)

    def edge(bb, j):
        jn = jnp.minimum(j + 1, n_chunks - 1)
        return chunk(bb, jn) * g8 + (g8 - 1 if dec else 0)

    def body(x_ref, lr_ref, li_ref, *rest):
        rest = list(rest)
        h_ref, hb_ref = (rest.pop(0), rest.pop(0)) if with_h else (None, None)
        o_ref = rest.pop(0)
        dl_ref = rest.pop(0) if with_h else None
        carry = rest.pop(0)
        bb, j = pl.program_id(0), pl.program_id(1)

        @pl.when(j == 0)
        def _():
            carry[...] = jnp.zeros_like(carry)

        if with_h:
            @pl.when(jnp.logical_and(bb == 0, j == 0))
            def _():
                dl_ref[...] = jnp.zeros_like(dl_ref)

        rows = lax.broadcasted_iota(jnp.int32, (tc, LANES), 0)
        first = rows == (tc - 1 if dec else 0)
        has_next = j + 1 < n_chunks
        for s in range(n // LANES):
            re, im = slice(s * LANES, (s + 1) * LANES), slice(n + s * LANES, n + (s + 1) * LANES)
            ar, ai = lr_ref[:, re], li_ref[:, re]
            cr, ci = carry[0:1, re], carry[0:1, im]
            dr = x_ref[:, re] + jnp.where(first, ar * cr - ai * ci, 0.0)
            di = x_ref[:, im] + jnp.where(first, ar * ci + ai * cr, 0.0)
            pr, pi = ar, ai
            sft = 1
            while sft < tc:
                if dec:
                    keep = rows < tc - sft
                    sr = jnp.where(keep, pltpu.roll(dr, tc - sft, 0), 0.0)
                    si = jnp.where(keep, pltpu.roll(di, tc - sft, 0), 0.0)
                else:
                    keep = rows >= sft
                    sr = jnp.where(keep, pltpu.roll(dr, sft, 0), 0.0)
                    si = jnp.where(keep, pltpu.roll(di, sft, 0), 0.0)
                dr, di = dr + pr * sr - pi * si, di + pr * si + pi * sr
                pr, pi = pr * pr - pi * pi, 2.0 * pr * pi
                sft *= 2
            o_ref[:, re] = dr
            o_ref[:, im] = di
            last = 0 if dec else tc - 1
            carry[0:1, re] = o_ref[last:last + 1, re]
            carry[0:1, im] = o_ref[last:last + 1, im]
            if with_h:
                hr, hi = h_ref[:, re], h_ref[:, im]
                er = 7 if dec else 0
                br = jnp.where(has_next, hb_ref[er:er + 1, re], 0.0)
                bi = jnp.where(has_next, hb_ref[er:er + 1, im], 0.0)
                if dec:
                    nr = jnp.where(rows == 0, br, pltpu.roll(hr, 1, 0))
                    ni = jnp.where(rows == 0, bi, pltpu.roll(hi, 1, 0))
                else:
                    nr = jnp.where(rows == tc - 1, br, pltpu.roll(hr, tc - 1, 0))
                    ni = jnp.where(rows == tc - 1, bi, pltpu.roll(hi, tc - 1, 0))
                dl_ref[0:1, re] += jnp.sum(dr * nr + di * ni, axis=0, keepdims=True)
                dl_ref[0:1, im] += jnp.sum(di * nr - dr * ni, axis=0, keepdims=True)

    xspec = pl.BlockSpec((tc, 2 * n), lambda bb, j: (chunk(bb, j), 0))
    lspec = pl.BlockSpec((1, n), lambda bb, j: (0, 0))
    in_specs, args = [xspec, lspec, lspec], [x, lam_r, lam_i]
    out_shape, out_specs = [SDS(x.shape, F32)], [xspec]
    if with_h:
        in_specs += [xspec, pl.BlockSpec((8, 2 * n), lambda bb, j: (edge(bb, j), 0))]
        args += [h, h]
        out_shape.append(SDS((8, 2 * n), F32))
        out_specs.append(pl.BlockSpec((8, 2 * n), lambda bb, j: (0, 0)))
    tag = ('c' if ctx_first else 'l') + ('d' if dec else 'u') + ('h' if with_h else '')
    res = pl.pallas_call(
        body, name=f"scan_{tag}_{x.shape[0]}", grid=(b, n_chunks),
        out_shape=out_shape, in_specs=in_specs, out_specs=out_specs,
        scratch_shapes=[pltpu.VMEM((8, 2 * n), F32)],
    )(*args)
    return res


def _make_scan(dims, direction):
    @jax.custom_vjp
    def scan(x, lam_r, lam_i):
        return _scan_call(x, lam_r, lam_i, dims, True, direction == 1)[0]

    def scan_fwd(x, lam_r, lam_i):
        h = _scan_call(x, lam_r, lam_i, dims, True, direction == 1)[0]
        return h, (h, lam_r, lam_i)

    def scan_bwd(res, g):
        h, lam_r, lam_i = res
        gd, dl = _scan_call(g, lam_r, -lam_i, dims, False, direction == 0, h=h)
        return gd, dl[0:1, :SSM_N], dl[0:1, SSM_N:]

    scan.defvjp(scan_fwd, scan_bwd)
    return scan


def _cmul(ar, ai, br, bi):
    return ar * br - ai * bi, ar * bi + ai * br


def _scan_chunk(buf, lr_ref, li_ref, carry, dec, tc):
    sub = lax.broadcasted_iota(jnp.int32, (8, LANES), 0)
    groups = range(tc // 8 - 1, -1, -1) if dec else range(tc // 8)
    for s in range(SSM_N // LANES):
        re, im = slice(s * LANES, (s + 1) * LANES), slice(SSM_N + s * LANES, SSM_N + (s + 1) * LANES)
        pw = [(lr_ref[:, re], li_ref[:, re])]
        for _ in range(7):
            pw.append(_cmul(*pw[-1], *pw[0]))
        p8r, p8i = jnp.zeros((8, LANES), F32), jnp.zeros((8, LANES), F32)
        for k in range(8):
            row = 7 - k if dec else k
            p8r = jnp.where(sub == row, pw[k][0], p8r)
            p8i = jnp.where(sub == row, pw[k][1], p8i)
        cr, ci = carry[0:1, re], carry[0:1, im]
        for r in groups:
            rows = slice(8 * r, 8 * r + 8)
            xr, xi = buf[rows, re], buf[rows, im]
            for sft in (1, 2, 4):
                mr, mi = pw[sft - 1]
                if dec:
                    keep = sub < 8 - sft
                    sr, si = pltpu.roll(xr, 8 - sft, 0), pltpu.roll(xi, 8 - sft, 0)
                else:
                    keep = sub >= sft
                    sr, si = pltpu.roll(xr, sft, 0), pltpu.roll(xi, sft, 0)
                sr, si = jnp.where(keep, sr, 0.0), jnp.where(keep, si, 0.0)
                xr, xi = xr + mr * sr - mi * si, xi + mr * si + mi * sr
            xr, xi = xr + p8r * cr - p8i * ci, xi + p8r * ci + p8i * cr
            buf[rows, re] = xr
            buf[rows, im] = xi
            last = 8 * r if dec else 8 * r + 7
            cr, ci = buf[last:last + 1, re], buf[last:last + 1, im]
        carry[0:1, re] = cr
        carry[0:1, im] = ci


def _ssm_chunks(dims, ctx_first, dec):
    b, l, lc = dims
    tc = _pick(lc, (256, 128, 64, 32, 16, 8))
    nc_c, nc_l = lc // tc, l // tc
    n_chunks = nc_c + nc_l
    ctx_base = b * l // tc
    g8 = tc // 8

    def chunk(bb, j):
        if ctx_first:
            is_ctx = j < nc_c
            jj = jnp.where(is_ctx, j, j - nc_c)
        else:
            is_ctx = j >= nc_l
            jj = jnp.where(is_ctx, j - nc_l, j)
        ic = nc_c - 1 - jj if dec else jj
        il = nc_l - 1 - jj if dec else jj
        return jnp.where(is_ctx, ctx_base + bb * nc_c + ic, bb * nc_l + il)

    def edge(bb, j):
        jn = jnp.minimum(j + 1, n_chunks - 1)
        return chunk(bb, jn) * g8 + (g8 - 1 if dec else 0)

    return tc, n_chunks, chunk, edge


def _ssm_fwd_call(u, bmat, cmat, lam_r, lam_i, dims, dec):
    n2 = 2 * SSM_N
    tc, n_chunks, chunk, _ = _ssm_chunks(dims, True, dec)

    def body(u_ref, b_ref, c_ref, lr_ref, li_ref, h_ref, y_ref, carry):
        @pl.when(pl.program_id(1) == 0)
        def _():
            carry[...] = jnp.zeros_like(carry)

        h_ref[...] = jnp.dot(_mx(u_ref[...]), b_ref[...], preferred_element_type=F32)
        _scan_chunk(h_ref, lr_ref, li_ref, carry, dec, tc)
        y_ref[...] = jnp.dot(_mx(h_ref[...]), c_ref[...], preferred_element_type=F32)

    def full(shape):
        return pl.BlockSpec(shape, lambda bb, j: (0, 0))

    return pl.pallas_call(
        body, name=f"ssm_fwd_{'d' if dec else 'u'}_{u.shape[0]}", grid=(dims[0], n_chunks),
        out_shape=[SDS((u.shape[0], n2), F32), SDS(u.shape, F32)],
        in_specs=[pl.BlockSpec((tc, SSM_WIDTH), lambda bb, j: (chunk(bb, j), 0)), full((SSM_WIDTH, n2)),
                  full((n2, SSM_WIDTH)), full((1, SSM_N)), full((1, SSM_N))],
        out_specs=[pl.BlockSpec((tc, n2), lambda bb, j: (chunk(bb, j), 0)),
                   pl.BlockSpec((tc, SSM_WIDTH), lambda bb, j: (chunk(bb, j), 0))],
        scratch_shapes=[pltpu.VMEM((8, n2), F32)],
    )(u, bmat, cmat, lam_r, lam_i)


def _ssm_bwd_call(gy, h, u, bmat, cmat, lam_r, lam_i, dims, dec):
    n2 = 2 * SSM_N
    tc, n_chunks, chunk, edge = _ssm_chunks(dims, False, dec)

    def body(gy_ref, h_ref, hb_ref, u_ref, b_ref, c_ref, lr_ref, li_ref, du_ref, db_ref, dct_ref, dl_ref, gd, carry):
        bb, j = pl.program_id(0), pl.program_id(1)

        @pl.when(j == 0)
        def _():
            carry[...] = jnp.zeros_like(carry)

        @pl.when(jnp.logical_and(bb == 0, j == 0))
        def _():
            db_ref[...] = jnp.zeros_like(db_ref)
            dct_ref[...] = jnp.zeros_like(dct_ref)
            dl_ref[...] = jnp.zeros_like(dl_ref)

        gyv = _mx(gy_ref[...])
        gd[...] = _dg_nt(gyv, c_ref[...])
        _scan_chunk(gd, lr_ref, li_ref, carry, dec, tc)
        gdv = _mx(gd[...])
        du_ref[...] = _dg_nt(gdv, b_ref[...])
        db_ref[...] += _dg_tn(_mx(u_ref[...]), gdv)
        dct_ref[...] += _dg_tn(gyv, _mx(h_ref[...]))
        rows = lax.broadcasted_iota(jnp.int32, (tc, LANES), 0)
        has_next = j + 1 < n_chunks
        er = 7 if dec else 0
        for s in range(SSM_N // LANES):
            re, im = slice(s * LANES, (s + 1) * LANES), slice(SSM_N + s * LANES, SSM_N + (s + 1) * LANES)
            dr, di, hr, hi = gd[:, re], gd[:, im], h_ref[:, re], h_ref[:, im]
            br = jnp.where(has_next, hb_ref[er:er + 1, re], 0.0)
            bi = jnp.where(has_next, hb_ref[er:er + 1, im], 0.0)
            if dec:
                nr = jnp.where(rows == 0, br, pltpu.roll(hr, 1, 0))
                ni = jnp.where(rows == 0, bi, pltpu.roll(hi, 1, 0))
            else:
                nr = jnp.where(rows == tc - 1, br, pltpu.roll(hr, tc - 1, 0))
                ni = jnp.where(rows == tc - 1, bi, pltpu.roll(hi, tc - 1, 0))
            dl_ref[0:1, re] += jnp.sum(dr * nr + di * ni, axis=0, keepdims=True)
            dl_ref[0:1, im] += jnp.sum(di * nr - dr * ni, axis=0, keepdims=True)

    def full(shape):
        return pl.BlockSpec(shape, lambda bb, j: (0, 0))

    def at_chunk(width):
        return pl.BlockSpec((tc, width), lambda bb, j: (chunk(bb, j), 0))

    return pl.pallas_call(
        body, name=f"ssm_bwd_{'d' if dec else 'u'}_{u.shape[0]}", grid=(dims[0], n_chunks),
        out_shape=[SDS(u.shape, F32), SDS((SSM_WIDTH, n2), F32), SDS((SSM_WIDTH, n2), F32), SDS((8, n2), F32)],
        in_specs=[at_chunk(SSM_WIDTH), at_chunk(n2), pl.BlockSpec((8, n2), lambda bb, j: (edge(bb, j), 0)),
                  at_chunk(SSM_WIDTH), full((SSM_WIDTH, n2)), full((n2, SSM_WIDTH)), full((1, SSM_N)),
                  full((1, SSM_N))],
        out_specs=[at_chunk(SSM_WIDTH), full((SSM_WIDTH, n2)), full((SSM_WIDTH, n2)), full((8, n2))],
        scratch_shapes=[pltpu.VMEM((tc, n2), F32), pltpu.VMEM((8, n2), F32)],
    )(gy, h, h, u, bmat, cmat, lam_r, lam_i)


def _make_ssm(dims, direction):
    @jax.custom_vjp
    def ssm(u, bmat, cmat, lam_r, lam_i):
        return _ssm_fwd_call(u, _mx(bmat), _mx(cmat), lam_r, lam_i, dims, direction == 1)[1]

    def ssm_fwd(u, bmat, cmat, lam_r, lam_i):
        h, y = _ssm_fwd_call(u, _mx(bmat), _mx(cmat), lam_r, lam_i, dims, direction == 1)
        return y, (u, bmat, cmat, lam_r, lam_i, h)

    def ssm_bwd(res, gy):
        u, bmat, cmat, lam_r, lam_i, h = res
        du, db, dct, dl = _ssm_bwd_call(gy, h, u, _mx(bmat), _mx(cmat), lam_r, -lam_i, dims, direction == 0)
        return du, db, dct.T, dl[0:1, :SSM_N], dl[0:1, SSM_N:]

    ssm.defvjp(ssm_fwd, ssm_bwd)
    return ssm


def _block_diag(m):
    g, a, bdim = m.shape
    eye = jnp.eye(g, dtype=m.dtype)
    full = m[:, :, None, :] * eye[:, None, :, None]
    return full.reshape(g * a, g * bdim)


def _ssm_mats(lam_re, lam_im, log_dt, b_re, b_im, c_re, c_im):
    lam = lax.complex(lam_re, lam_im)
    dt = jnp.exp(log_dt)[..., None]
    lam_bar = jnp.exp(lam * dt)
    b_bar = ((lam_bar - 1) / lam)[..., None] * lax.complex(b_re, b_im)
    out = []
    for d in range(2):
        bt = jnp.swapaxes(b_bar[d], 1, 2)
        bmat = jnp.concatenate([_block_diag(jnp.real(bt)), _block_diag(jnp.imag(bt))], axis=1)
        ct_re = jnp.swapaxes(c_re[d], 1, 2)
        ct_im = jnp.swapaxes(c_im[d], 1, 2)
        cmat = jnp.concatenate([_block_diag(ct_re), _block_diag(-ct_im)], axis=0)
        out.append((bmat, cmat, jnp.real(lam_bar[d]).reshape(1, SSM_N), jnp.imag(lam_bar[d]).reshape(1, SSM_N)))
    return out


def _loss_call(y, target):
    t, d = y.shape
    tr = _pick(t, (ROW_TILE, 128, 64, 32, 16, 8))

    def body(y_ref, t_ref, acc_ref, dy_ref):
        @pl.when(pl.program_id(0) == 0)
        def _():
            acc_ref[...] = jnp.zeros_like(acc_ref)

        diff = y_ref[...] - t_ref[...]
        dy_ref[...] = diff * (1.0 / d)
        acc_ref[0:1, :] += jnp.sum(diff * diff, axis=0, keepdims=True)

    spec = pl.BlockSpec((tr, d), lambda i: (i, 0))
    return pl.pallas_call(
        body, name=f"loss_{t}", grid=(t // tr,),
        out_shape=[SDS((8, d), F32), SDS((t, d), F32)],
        in_specs=[spec, spec], out_specs=[pl.BlockSpec((8, d), lambda i: (0, 0)), spec],
    )(y, target)


def _adamw_call(w, g, m, v):
    r, c = w.shape
    tr = _pick(r, [t for t in (1024, 512, 256, 128, 64, 32, 16, 8) if t * c <= 256 * 1024])

    def body(w_ref, g_ref, m_ref, v_ref, d_ref, mo_ref, vo_ref):
        gv = g_ref[...]
        mn = ADAM_B1 * m_ref[...] + (1.0 - ADAM_B1) * gv
        vn = ADAM_B2 * v_ref[...] + (1.0 - ADAM_B2) * (gv * gv)
        m_hat = mn / (1.0 - ADAM_B1 ** ADAM_STEP)
        v_hat = vn / (1.0 - ADAM_B2 ** ADAM_STEP)
        d_ref[...] = -ADAM_LR * (m_hat / (jnp.sqrt(v_hat) + ADAM_EPS) + ADAM_WD * w_ref[...])
        mo_ref[...] = mn
        vo_ref[...] = vn

    spec = pl.BlockSpec((tr, c), lambda i: (i, 0))
    return pl.pallas_call(
        body, name=f"adamw_{r}", grid=(r // tr,),
        out_shape=[SDS((r, c), F32)] * 3, in_specs=[spec] * 4, out_specs=[spec] * 3,
    )(w, g, m, v)


def _forward(diff, gathered, dims, depth):
    b, l, lc = dims
    t_lat = b * l
    d_model = diff['x'].shape[-1]
    cos, sin = _rope_tables(l)
    nm_linear, nm_mlp = _make_nm_ops(*_rowwise("norm_mod", _norm_mod_f, (d_model,), l))
    resid_gate = _make_rowwise_op("resid_gate", _resid_gate_f, d_model, 2, 1, l)
    glu = _make_rowwise_op("glu", _glu_f, SSM_WIDTH, 3, 0, l)
    att_conv = _make_att_conv(dims, cos, sin)
    ssms = [_make_ssm(dims, 0), _make_ssm(dims, 1)]

    c_act = jax.nn.silu(diff['c'])
    c_ctx_act = jax.nn.silu(diff['c_ctx'])
    mod_in = jnp.concatenate([c_act, c_ctx_act[None, :], jnp.zeros((MOD_ROWS - b - 1, d_model), F32)], axis=0)
    h = jnp.concatenate([diff['x'].reshape(t_lat, d_model), diff['ctx'].reshape(b * lc, d_model)], axis=0)

    for layer in range(depth):
        last = layer == depth - 1
        mod = linear_t(mod_in, gathered['w_ada'][layer], diff['w_ada'][layer])[:b + 1] + diff['b_ada'][layer][None, :]
        sh1, sc1, g1, sh2, sc2, g2 = [m.reshape(b + 1, 1, d_model) for m in jnp.split(mod, N_MOD, axis=-1)]
        ng = diff['norm_g'][layer]
        p = nm_linear(h, sh1, sc1, ng[0:1], gathered['w_in'][layer], diff['w_in'][layer])
        ac = att_conv(p, diff['conv_w'][layer], diff['attn_sink'][layer][None, :])
        u = p[:, IN_WIDTH - SSM_WIDTH:]
        mats = _ssm_mats(diff['ssm_lam_re'][layer], diff['ssm_lam_im'][layer], diff['ssm_log_dt'][layer],
                         diff['ssm_b_re'][layer], diff['ssm_b_im'][layer], diff['ssm_c_re'][layer],
                         diff['ssm_c_im'][layer])
        ys = []
        for d in range(2):
            bmat, cmat, lr, li = mats[d]
            ys.append(ssms[d](u, bmat, cmat, lr, li))
        s = glu(ys[0], ys[1], u, diff['ssm_d'][layer][None, :], diff['w_glu'][layer], diff['b_glu'][layer][None, :])
        if last:
            ac, s, h = ac[:t_lat], s[:t_lat], h[:t_lat]
            g1, sh2, sc2, g2 = g1[:b], sh2[:b], sc2[:b], g2[:b]
        m = mix_linear(ac, s, gathered['w_out'][layer], diff['w_out'][layer])
        h = resid_gate(h, m, g1, ng[1:2])
        f = nm_mlp(h, sh2, sc2, ng[2:3], gathered['w_mlp_in'][layer], diff['w_mlp_in'][layer],
                   gathered['w_mlp_out'][layer], diff['w_mlp_out'][layer])
        h = resid_gate(h, f, g2, ng[3:4])
    return h


BIG_TRANSPOSED = {'w_ada': True, 'w_in': True, 'w_out': False, 'w_mlp_in': True, 'w_mlp_out': False}


def _big_rows(weights):
    blocks = []
    depth = weights['w_ada'].shape[0]
    for layer in range(depth):
        for n in BIG:
            w = weights[n][layer]
            blocks.append((layer, n, w.shape[1] if BIG_TRANSPOSED[n] else w.shape[0]))
    return blocks


def _stack_big(arrs, blocks, dtype):
    parts = []
    for layer, n, _ in blocks:
        w = arrs[n][layer]
        parts.append((w.T if BIG_TRANSPOSED[n] else w).astype(dtype))
    return jnp.concatenate(parts, axis=0)


def kernel(x, c, ctx, c_ctx, w_ada, b_ada, norm_g, w_in, conv_w, attn_sink, ssm_lam_re, ssm_lam_im, ssm_log_dt, ssm_b_re, ssm_b_im, ssm_c_re, ssm_c_im, ssm_d, w_glu, b_glu, w_out, w_mlp_in, w_mlp_out, loss_target, m_c_ctx, m_w_ada, m_b_ada, m_norm_g, m_w_in, m_conv_w, m_attn_sink, m_ssm_lam_re, m_ssm_lam_im, m_ssm_log_dt, m_ssm_b_re, m_ssm_b_im, m_ssm_c_re, m_ssm_c_im, m_ssm_d, m_w_glu, m_b_glu, m_w_out, m_w_mlp_in, m_w_mlp_out, v_c_ctx, v_w_ada, v_b_ada, v_norm_g, v_w_in, v_conv_w, v_attn_sink, v_ssm_lam_re, v_ssm_lam_im, v_ssm_log_dt, v_ssm_b_re, v_ssm_b_im, v_ssm_c_re, v_ssm_c_im, v_ssm_d, v_w_glu, v_b_glu, v_w_out, v_w_mlp_in, v_w_mlp_out):
    given = dict(locals())
    weights = {n: given[n] for n in WEIGHTS}
    moms = {n: given['m_' + n] for n in WEIGHTS}
    vars_ = {n: given['v_' + n] for n in WEIGHTS}
    b, l, d_model = x.shape
    lc = ctx.shape[1]
    dims = (b, l, lc)
    depth = w_ada.shape[0]

    core = lax.axis_index('c')
    me = 4 * lax.axis_index('x') + 2 * lax.axis_index('y') + core

    blocks = _big_rows(weights)
    big_all = _all_gather(_stack_big(weights, blocks, MXU_DTYPE))
    gathered = {n: [None] * depth for n in BIG}
    off = 0
    for layer, n, rows in blocks:
        gathered[n][layer] = big_all[:, off:off + rows].reshape(N_DEV * rows, -1)
        off += rows
    small_shapes = [weights[n].shape for n in SMALL]
    small_all = _all_gather(_pack([weights[n] for n in SMALL], F32, PACK_ROWS))
    small_full = {n: _from_pieces(pc, SHARD_AXIS[n])
                  for n, pc in zip(SMALL, _unpack(small_all, small_shapes, (N_DEV,)))}

    diff = {'x': x, 'c': c, 'ctx': ctx}
    for n in REPLICATED:
        diff[n] = weights[n]
    for n in SMALL:
        diff[n] = small_full[n]
    for n in BIG:
        diff[n] = [jnp.zeros(gathered[n][layer].shape, F32) for layer in range(depth)]
    y, vjp = jax.vjp(lambda dd: _forward(dd, gathered, dims, depth), diff)
    sq, dy = _loss_call(y, loss_target.reshape(b * l, d_model))
    loss = lax.psum(0.5 * jnp.sum(sq) / d_model, ('x', 'y', 'c'))
    grads = vjp(dy)[0]
    grad_x = grads['x']

    g8 = jnp.concatenate([grads[n][layer].reshape(N_DEV, rows, -1) for layer, n, rows in blocks], axis=1)
    chip_sums = _rs_add(g8, _rs_sibling(g8), core.reshape(1).astype(jnp.int32))
    g_big = _sum_slots(_rs_chips(chip_sums))
    big_grad = {n: [] for n in BIG}
    off = 0
    for layer, n, rows in blocks:
        blk = g_big[off:off + rows]
        big_grad[n].append(blk.T if BIG_TRANSPOSED[n] else blk)
        off += rows
    big_grad = {n: jnp.stack(v) for n, v in big_grad.items()}
    rest = SMALL + REPLICATED
    g_rest = _sum_slots(_all_gather(_pack([grads[n] for n in rest], F32, PACK_ROWS)))
    rest_grad = dict(zip(rest, _unpack(g_rest, [grads[n].shape for n in rest])))
    for n in SMALL:
        width = weights[n].shape[SHARD_AXIS[n]]
        rest_grad[n] = lax.dynamic_slice_in_dim(rest_grad[n], me * width, width, axis=SHARD_AXIS[n])

    out = {}
    for n in BIG:
        shape = weights[n].shape
        flat = [a.reshape(-1, shape[-1]) for a in (weights[n], big_grad[n], moms[n], vars_[n])]
        for kind, mat in zip(('delta', 'new_m', 'new_v'), _adamw_call(*flat)):
            out[(kind, n)] = mat.reshape(shape)
        out[('grad', n)] = big_grad[n]
    shapes = [weights[n].shape for n in rest]
    g_mat = _pack([rest_grad[n] for n in rest], F32, PACK_ROWS)
    packed = [_pack([src[n] for n in rest], F32, PACK_ROWS) for src in (weights, moms, vars_)]
    d_mat, mn_mat, vn_mat = _adamw_call(packed[0], g_mat, packed[1], packed[2])
    for kind, mat in (('delta', d_mat), ('new_m', mn_mat), ('new_v', vn_mat)):
        for n, arr in zip(rest, _unpack(mat, shapes)):
            out[(kind, n)] = arr
    for n in rest:
        out[('grad', n)] = rest_grad[n]
    return (loss, grad_x, *[out[(kind, n)] for kind in ('grad', 'delta', 'new_m', 'new_v') for n in WEIGHTS])
```

```python
import functools

import jax
import jax.numpy as jnp
from jax import lax
from jax.experimental import pallas as pl
from jax.experimental.pallas import tpu as pltpu

F32 = jnp.float32
MXU_DTYPE = jnp.bfloat16
SDS = jax.ShapeDtypeStruct

N_DEV = 8
HEAD_DIM = 64
N_Q_HEADS = 8
Q_PER_KV = 4
ATTN_WIDTH = 512
KV_WIDTH = 128
WINDOW = 128
Q_BLOCK = 128
GRID_W = 64
ROPE_BASE = 10000.0
ROPE_PAIRS = 16
CONV_WIDTH = 256
SSM_WIDTH = 256
SSM_GROUP = 16
SSM_GROUPS = 16
SSM_STATE = 64
SSM_N = SSM_GROUPS * SSM_STATE
IN_WIDTH = 1792
N_MOD = 6
EPS = 1e-6
NEG_INF = -1e30
ADAM_LR = 0.001
ADAM_B1 = 0.9
ADAM_B2 = 0.999
ADAM_EPS = 1e-08
ADAM_WD = 0.01
ADAM_STEP = 10
MOD_ROWS = 128
ROW_TILE = 256
PACK_ROWS = 256
LANES = 128
MESH = pl.DeviceIdType.MESH
_SCALE = HEAD_DIM ** -0.5
MIX_SPLIT = ATTN_WIDTH + CONV_WIDTH
MM_TILE = 1024
MM_VMEM_BYTES = 56 * 1024 * 1024

WEIGHTS = ['c_ctx', 'w_ada', 'b_ada', 'norm_g', 'w_in', 'conv_w', 'attn_sink', 'ssm_lam_re', 'ssm_lam_im',
           'ssm_log_dt', 'ssm_b_re', 'ssm_b_im', 'ssm_c_re', 'ssm_c_im', 'ssm_d', 'w_glu', 'b_glu', 'w_out',
           'w_mlp_in', 'w_mlp_out']
SHARD_AXIS = {'w_ada': 2, 'norm_g': 2, 'w_in': 2, 'conv_w': 2, 'w_glu': 1, 'w_out': 1, 'w_mlp_in': 2, 'w_mlp_out': 1}
BIG = ['w_ada', 'w_in', 'w_out', 'w_mlp_in', 'w_mlp_out']
SMALL = ['norm_g', 'conv_w', 'w_glu']
SHARDED = BIG + SMALL
REPLICATED = [n for n in WEIGHTS if n not in SHARD_AXIS]


def _pick(n, cands):
    for c in cands:
        if n % c == 0:
            return c
    return n


def _div_tile(n, cap):
    if n <= cap:
        return n
    for c in range(cap, LANES - 1, -LANES):
        if n % c == 0:
            return c
    return n


def _mx(x):
    return x.astype(MXU_DTYPE)


def _dg_nt(a, b):
    return lax.dot_general(a, b, (((1,), (1,)), ((), ())), preferred_element_type=F32)


def _dg_tn(a, b):
    return lax.dot_general(a, b, (((0,), (0,)), ((), ())), preferred_element_type=F32)


def _all_gather(x2d):
    r, cdim = x2d.shape

    def body(x_ref, out_ref, send_sems, recv_sems, local_sem):
        x, y, c = lax.axis_index("x"), lax.axis_index("y"), lax.axis_index("c")
        me, sibling = (x, y, c), (x, y, 1 - c)
        chips = [(1 - x, y), (x, 1 - y), (1 - x, 1 - y)]

        def slot(px, py, pc):
            return out_ref.at[4 * px + 2 * py + pc]

        def copy(k, block, to, src=None):
            return pltpu.make_async_remote_copy(
                src_ref=slot(*block) if src is None else src, dst_ref=slot(*block),
                send_sem=send_sems.at[k], recv_sem=recv_sems.at[k], device_id=to, device_id_type=MESH)

        mine = pltpu.make_async_copy(x_ref, slot(*me), local_sem)
        mine.start()
        first = [copy(0, me, sibling, src=x_ref)]
        first += [copy(1 + j, me, (*chip, c), src=x_ref) for j, chip in enumerate(chips)]
        for cp in first:
            cp.start()
        passed = [copy(4 + j, (*chip, c), sibling) for j, chip in enumerate(chips)]
        for j, chip in enumerate(chips):
            copy(1 + j, (*chip, c), me).wait_recv()
            passed[j].start()
        copy(0, sibling, me).wait_recv()
        for j, chip in enumerate(chips):
            copy(4 + j, (*chip, 1 - c), me).wait_recv()
        for cp in first + passed:
            cp.wait_send()
        mine.wait()

    return pl.pallas_call(
        body, name=f"all_gather_{r}x{cdim}_{jnp.dtype(x2d.dtype).name}",
        out_shape=SDS((N_DEV, r, cdim), x2d.dtype),
        in_specs=[pl.BlockSpec(memory_space=pl.ANY)],
        out_specs=pl.BlockSpec(memory_space=pl.ANY),
        scratch_shapes=[pltpu.SemaphoreType.DMA((7,)), pltpu.SemaphoreType.DMA((7,)), pltpu.SemaphoreType.DMA],
    )(x2d)


def _rs_sibling(g8):
    _, r, cdim = g8.shape

    def body(g_ref, out_ref, send_sems, recv_sems):
        x, y, c = lax.axis_index("x"), lax.axis_index("y"), lax.axis_index("c")
        copies = [pltpu.make_async_remote_copy(
            src_ref=g_ref.at[2 * k + (1 - c)], dst_ref=out_ref.at[k], send_sem=send_sems.at[k],
            recv_sem=recv_sems.at[k], device_id=(x, y, 1 - c), device_id_type=MESH) for k in range(4)]
        for cp in copies:
            cp.start()
        for cp in copies:
            cp.wait()

    return pl.pallas_call(
        body, name=f"rs_sibling_{r}x{cdim}",
        out_shape=SDS((4, r, cdim), g8.dtype),
        in_specs=[pl.BlockSpec(memory_space=pl.ANY)],
        out_specs=pl.BlockSpec(memory_space=pl.ANY),
        scratch_shapes=[pltpu.SemaphoreType.DMA((4,)), pltpu.SemaphoreType.DMA((4,))],
    )(g8)


def _rs_add(g8, sib, c_idx):
    _, r, cdim = g8.shape
    tr = _pick(r, (256, 128, 64, 32, 16))

    def body(c_ref, g_ref, s_ref, o_ref):
        del c_ref
        o_ref[...] = (g_ref[...] + s_ref[...]).astype(o_ref.dtype)

    return pl.pallas_call(
        body, name=f"rs_add_{r}x{cdim}",
        grid_spec=pltpu.PrefetchScalarGridSpec(
            num_scalar_prefetch=1, grid=(4, r // tr),
            in_specs=[pl.BlockSpec((1, tr, cdim), lambda k, i, c: (2 * k + c[0], i, 0)),
                      pl.BlockSpec((1, tr, cdim), lambda k, i, c: (k, i, 0))],
            out_specs=pl.BlockSpec((1, tr, cdim), lambda k, i, c: (k, i, 0))),
        out_shape=SDS((4, r, cdim), MXU_DTYPE),
    )(c_idx, g8, sib)


def _rs_chips(s4):
    _, r, cdim = s4.shape

    def body(s_ref, out_ref, send_sems, recv_sems, local_sem):
        x, y, c = lax.axis_index("x"), lax.axis_index("y"), lax.axis_index("c")
        me = 2 * x + y

        def peer(j):
            px = 1 - x if j & 2 else x
            py = 1 - y if j & 1 else y
            return (px, py, c), 2 * px + py

        def copy(j, landing):
            to, to_chip = peer(j)
            return pltpu.make_async_remote_copy(
                src_ref=s_ref.at[to_chip], dst_ref=out_ref.at[to_chip if landing else me],
                send_sem=send_sems.at[j - 1], recv_sem=recv_sems.at[j - 1], device_id=to, device_id_type=MESH)

        mine = pltpu.make_async_copy(s_ref.at[me], out_ref.at[me], local_sem)
        mine.start()
        sends = [copy(j, False) for j in range(1, 4)]
        for cp in sends:
            cp.start()
        for j in range(1, 4):
            copy(j, True).wait_recv()
        for cp in sends:
            cp.wait_send()
        mine.wait()

    return pl.pallas_call(
        body, name=f"rs_chips_{r}x{cdim}",
        out_shape=SDS(s4.shape, s4.dtype),
        in_specs=[pl.BlockSpec(memory_space=pl.ANY)],
        out_specs=pl.BlockSpec(memory_space=pl.ANY),
        scratch_shapes=[pltpu.SemaphoreType.DMA((3,)), pltpu.SemaphoreType.DMA((3,)), pltpu.SemaphoreType.DMA],
    )(s4)


def _sum_slots(g3):
    n, r, cdim = g3.shape
    tr = _pick(r, (256, 128, 64, 32, 16, 8))

    def body(g_ref, o_ref):
        acc = g_ref[0].astype(F32)
        for s in range(1, n):
            acc = acc + g_ref[s].astype(F32)
        o_ref[...] = acc

    return pl.pallas_call(
        body, name=f"sum_slots_{n}x{r}x{cdim}", grid=(r // tr,),
        out_shape=SDS((r, cdim), F32),
        in_specs=[pl.BlockSpec((n, tr, cdim), lambda i: (0, i, 0))],
        out_specs=pl.BlockSpec((tr, cdim), lambda i: (i, 0)),
    )(g3)


def _rows_of(n):
    return -(-n // (8 * LANES)) * 8


def _pack(arrs, dtype, row_mult):
    parts = []
    for a in arrs:
        flat = a.reshape(-1).astype(dtype)
        parts.append(jnp.pad(flat, (0, _rows_of(flat.size) * LANES - flat.size)).reshape(-1, LANES))
    rows = sum(p.shape[0] for p in parts)
    if rows % row_mult:
        parts.append(jnp.zeros((row_mult - rows % row_mult, LANES), dtype))
    return jnp.concatenate(parts, axis=0)


def _unpack(mat, shapes, lead=()):
    out, off = [], 0
    for s in shapes:
        n = 1
        for d in s:
            n *= d
        rows = _rows_of(n)
        part = mat[..., off:off + rows, :].reshape(lead + (rows * LANES,))
        out.append(part[..., :n].reshape(lead + tuple(s)))
        off += rows
    return out


def _to_pieces(g, axis):
    s = g.shape
    g = g.reshape(s[:axis] + (N_DEV, s[axis] // N_DEV) + s[axis + 1:])
    return jnp.moveaxis(g, axis, 0)


def _from_pieces(p, axis):
    p = jnp.moveaxis(p, 0, axis)
    s = p.shape
    return p.reshape(s[:axis] + (s[axis] * s[axis + 1],) + s[axis + 2:])


def _mm_call(kind, a, b, z=None, sqrelu=False, out_dtype=F32):
    if kind == 'nn':
        (m, k), n = a.shape, b.shape[1]
    elif kind == 'nt':
        (m, k), n = a.shape, b.shape[0]
    else:
        (k, m), n = a.shape, b.shape[1]
    bm = _div_tile(m, MM_TILE)
    bn = _div_tile(n, MM_TILE)
    bk = _div_tile(k, MM_TILE if kind == 'tn' else 2 * MM_TILE)
    nk = k // bk
    if sqrelu:
        out_dtype = MXU_DTYPE
    plain = not sqrelu and z is None and out_dtype == F32
    use_acc = nk > 1 and not plain

    def body(a_ref, b_ref, *rest):
        rest = list(rest)
        z_ref = rest.pop(0) if z is not None else None
        o_ref = rest.pop(0)
        act_ref = rest.pop(0) if sqrelu else None
        acc = rest.pop(0) if use_acc else None
        kk = pl.program_id(2)
        av, bv = _mx(a_ref[...]), _mx(b_ref[...])
        if kind == 'nn':
            prod = jnp.dot(av, bv, preferred_element_type=F32)
        elif kind == 'nt':
            prod = _dg_nt(av, bv)
        else:
            prod = _dg_tn(av, bv)

        def finish(r):
            if z_ref is not None:
                r = r * (2.0 * jnp.maximum(z_ref[...].astype(F32), 0.0))
            o_ref[...] = r.astype(o_ref.dtype)
            if act_ref is not None:
                rr = jnp.maximum(r, 0.0)
                act_ref[...] = (rr * rr).astype(act_ref.dtype)

        if nk == 1:
            finish(prod)
        else:
            tgt = acc if use_acc else o_ref

            @pl.when(kk == 0)
            def _():
                tgt[...] = prod

            @pl.when(kk > 0)
            def _():
                tgt[...] += prod

            if use_acc:
                @pl.when(kk == nk - 1)
                def _():
                    finish(acc[...])

    if kind == 'nn':
        a_spec = pl.BlockSpec((bm, bk), lambda i, j, kk: (i, kk))
        b_spec = pl.BlockSpec((bk, bn), lambda i, j, kk: (kk, j))
    elif kind == 'nt':
        a_spec = pl.BlockSpec((bm, bk), lambda i, j, kk: (i, kk))
        b_spec = pl.BlockSpec((bn, bk), lambda i, j, kk: (j, kk))
    else:
        a_spec = pl.BlockSpec((bk, bm), lambda i, j, kk: (kk, i))
        b_spec = pl.BlockSpec((bk, bn), lambda i, j, kk: (kk, j))
    o_spec = pl.BlockSpec((bm, bn), lambda i, j, kk: (i, j))
    in_specs, args = [a_spec, b_spec], [a, b]
    if z is not None:
        in_specs.append(o_spec)
        args.append(z)
    out_shape, out_specs = [SDS((m, n), out_dtype)], [o_spec]
    if sqrelu:
        out_shape.append(SDS((m, n), MXU_DTYPE))
        out_specs.append(o_spec)
    tag = kind + ('_sq' if sqrelu else '') + ('_z' if z is not None else '')
    res = pl.pallas_call(
        body, name=f"mm_{tag}_{m}x{k}x{n}", grid=(m // bm, n // bn, nk),
        out_shape=out_shape, in_specs=in_specs, out_specs=out_specs,
        scratch_shapes=[pltpu.VMEM((bm, bn), F32)] if use_acc else [],
        compiler_params=pltpu.CompilerParams(dimension_semantics=("parallel", "parallel", "arbitrary"),
                                             vmem_limit_bytes=MM_VMEM_BYTES),
    )(*args)
    return res if sqrelu else res[0]


@jax.custom_vjp
def linear(a, w):
    return _mm_call('nn', a, w)


def _linear_fwd(a, w):
    return _mm_call('nn', a, w), (a, w)


def _linear_bwd(res, g):
    a, w = res
    return _mm_call('nt', g, w), _mm_call('tn', a, g)


linear.defvjp(_linear_fwd, _linear_bwd)


@jax.custom_vjp
def linear_t(a, wt, wtd):
    return _mm_call('nt', a, wt)


def _linear_t_fwd(a, wt, wtd):
    return _mm_call('nt', a, wt), (a, wt)


def _linear_t_bwd(res, g):
    a, wt = res
    return _mm_call('nn', g, wt), jnp.zeros_like(wt), _mm_call('tn', g, a)


linear_t.defvjp(_linear_t_fwd, _linear_t_bwd)


@jax.custom_vjp
def mix_linear(ac, s, w, wd):
    return _mm_call('nn', _mx(jnp.concatenate([ac, s], axis=1)), w)


def _mix_linear_fwd(ac, s, w, wd):
    mix = _mx(jnp.concatenate([ac, s], axis=1))
    return _mm_call('nn', mix, w), (mix, w)


def _mix_linear_bwd(res, g):
    mix, w = res
    dmix = _mm_call('nt', g, w)
    return dmix[:, :MIX_SPLIT], dmix[:, MIX_SPLIT:], jnp.zeros_like(w), _mm_call('tn', mix, g)


mix_linear.defvjp(_mix_linear_fwd, _mix_linear_bwd)


def _make_nm_ops(nm_fwd, nm_bwd):
    @jax.custom_vjp
    def nm_linear(h, sh, sc, g, wt, wtd):
        return _mm_call('nt', nm_fwd((h,), (sh, sc), (g,), MXU_DTYPE)[0], wt)

    def nm_linear_fwd(h, sh, sc, g, wt, wtd):
        a = nm_fwd((h,), (sh, sc), (g,), MXU_DTYPE)[0]
        return _mm_call('nt', a, wt), (h, sh, sc, g, a, wt)

    def nm_linear_bwd(res, gp):
        h, sh, sc, g, a, wt = res
        da = _mm_call('nn', gp, wt)
        dwt = _mm_call('tn', gp, a)
        dh, dsh, dsc, dg = nm_bwd((h,), (sh, sc), (g,), (da,))
        return dh, dsh, dsc, dg, jnp.zeros_like(wt), dwt

    nm_linear.defvjp(nm_linear_fwd, nm_linear_bwd)

    @jax.custom_vjp
    def nm_mlp(h, sh, sc, g, w1t, w1td, w2, w2d):
        a = nm_fwd((h,), (sh, sc), (g,), MXU_DTYPE)[0]
        _, act = _mm_call('nt', a, w1t, sqrelu=True)
        return _mm_call('nn', act, w2)

    def nm_mlp_fwd(h, sh, sc, g, w1t, w1td, w2, w2d):
        a = nm_fwd((h,), (sh, sc), (g,), MXU_DTYPE)[0]
        zb, act = _mm_call('nt', a, w1t, sqrelu=True)
        return _mm_call('nn', act, w2), (h, sh, sc, g, a, w1t, w2, zb, act)

    def nm_mlp_bwd(res, gf):
        h, sh, sc, g, a, w1t, w2, zb, act = res
        dz = _mm_call('nt', gf, w2, z=zb, out_dtype=MXU_DTYPE)
        dw2 = _mm_call('tn', act, gf)
        da = _mm_call('nn', dz, w1t)
        dw1t = _mm_call('tn', dz, a)
        dh, dsh, dsc, dg = nm_bwd((h,), (sh, sc), (g,), (da,))
        return dh, dsh, dsc, dg, jnp.zeros_like(w1t), dw1t, jnp.zeros_like(w2), dw2

    nm_mlp.defvjp(nm_mlp_fwd, nm_mlp_bwd)
    return nm_linear, nm_mlp


def _rowwise(name, f, out_widths, seg_len):
    def specs(rows, segs, globs, tr):
        nseg = segs[0].shape[0] if segs else 1

        def seg_of(i):
            return jnp.minimum((i * tr) // seg_len, nseg - 1)

        row_specs = [pl.BlockSpec((tr, r.shape[1]), lambda i: (i, 0)) for r in rows]
        seg_specs = [pl.BlockSpec((1, 1, s.shape[2]), lambda i: (seg_of(i), 0, 0)) for s in segs]
        glob_specs = [pl.BlockSpec(g.shape, lambda i: (0, 0)) for g in globs]
        return seg_of, row_specs, seg_specs, glob_specs

    def fwd_call(rows, segs, globs, out_dtype=F32):
        t = rows[0].shape[0]
        tr = _pick(t, (ROW_TILE, 128, 64, 32, 16, 8))
        _, row_specs, seg_specs, glob_specs = specs(rows, segs, globs, tr)
        nr, ns = len(rows), len(segs)

        def body(*refs):
            ins, outs = refs[:nr + ns + len(globs)], refs[nr + ns + len(globs):]
            vals = [r[...] for r in ins[:nr]] + [r[0] for r in ins[nr:nr + ns]] + [r[...] for r in ins[nr + ns:]]
            for o_ref, v in zip(outs, f(*vals)):
                o_ref[...] = v.astype(o_ref.dtype)

        return pl.pallas_call(
            body, name=f"{name}_fwd_{t}_{jnp.dtype(out_dtype).name}", grid=(t // tr,),
            out_shape=[SDS((t, w), out_dtype) for w in out_widths],
            in_specs=row_specs + seg_specs + glob_specs,
            out_specs=[pl.BlockSpec((tr, w), lambda i: (i, 0)) for w in out_widths],
        )(*rows, *segs, *globs)

    def bwd_call(rows, segs, globs, douts):
        t = rows[0].shape[0]
        tr = _pick(t, (ROW_TILE, 128, 64, 32, 16, 8))
        seg_of, row_specs, seg_specs, glob_specs = specs(rows, segs, globs, tr)
        nr, ns, ng, no = len(rows), len(segs), len(globs), len(out_widths)

        def body(*refs):
            ins = refs[:nr + ns + ng]
            dos = refs[nr + ns + ng:nr + ns + ng + no]
            outs = refs[nr + ns + ng + no:]
            i = pl.program_id(0)
            first_of_seg = jnp.logical_or(i == 0, seg_of(i) != seg_of(jnp.maximum(i - 1, 0)))
            vals = [r[...] for r in ins[:nr]] + [r[0] for r in ins[nr:nr + ns]] + [r[...] for r in ins[nr + ns:]]
            _, vjp = jax.vjp(f, *vals)
            grads = vjp(tuple(d[...] for d in dos))
            for o_ref, gval in zip(outs[:nr], grads[:nr]):
                o_ref[...] = gval
            for o_ref, gval in zip(outs[nr:nr + ns], grads[nr:nr + ns]):
                @pl.when(first_of_seg)
                def _(o_ref=o_ref):
                    o_ref[...] = jnp.zeros_like(o_ref)
                o_ref[0] += gval
            for o_ref, gval in zip(outs[nr + ns:], grads[nr + ns:]):
                @pl.when(i == 0)
                def _(o_ref=o_ref):
                    o_ref[...] = jnp.zeros_like(o_ref)
                o_ref[...] += gval

        do_specs = [pl.BlockSpec((tr, w), lambda i: (i, 0)) for w in out_widths]
        return pl.pallas_call(
            body, name=f"{name}_bwd_{t}", grid=(t // tr,),
            out_shape=[SDS(r.shape, F32) for r in rows] + [SDS(s.shape, F32) for s in segs]
            + [SDS(g.shape, F32) for g in globs],
            in_specs=row_specs + seg_specs + glob_specs + do_specs,
            out_specs=row_specs + seg_specs + glob_specs,
        )(*rows, *segs, *globs, *douts)

    return fwd_call, bwd_call


def _norm_mod_f(x, shift, scale, g):
    r = lax.rsqrt(jnp.mean(x * x, axis=-1, keepdims=True) + EPS)
    return ((x * r) * g * (1.0 + scale) + shift,)


def _resid_gate_f(h, m, gate, g):
    r = lax.rsqrt(jnp.mean(m * m, axis=-1, keepdims=True) + EPS)
    return (h + gate * ((m * r) * g),)


def _glu_f(y0, y1, u, d, w, b):
    y = y0 + y1 + d * u
    g = 0.5 * y * (1.0 + jnp.tanh(0.7978845608028654 * (y + 0.044715 * (y * y * y))))
    zz = jnp.dot(_mx(g), _mx(w), preferred_element_type=F32) + b
    return (g * (1.0 / (1.0 + jnp.exp(-zz))),)


def _make_rowwise_op(name, f, out_width, n_rows, n_segs, seg_len):
    fwd_call, bwd_call = _rowwise(name, f, (out_width,), seg_len)

    @jax.custom_vjp
    def op(*args):
        return fwd_call(args[:n_rows], args[n_rows:n_rows + n_segs], args[n_rows + n_segs:])[0]

    def op_fwd(*args):
        return fwd_call(args[:n_rows], args[n_rows:n_rows + n_segs], args[n_rows + n_segs:])[0], args

    def op_bwd(args, g):
        return tuple(bwd_call(args[:n_rows], args[n_rows:n_rows + n_segs], args[n_rows + n_segs:], (g,)))

    op.defvjp(op_fwd, op_bwd)
    return op


def _lane():
    return lax.broadcasted_iota(jnp.int32, (1, LANES), 1)


def _hm(off):
    lane = _lane()
    return jnp.logical_and(lane >= off, lane < off + HEAD_DIM)


def _col(tile, hq):
    return jnp.sum(jnp.where(_lane() == hq, tile, 0.0), axis=1, keepdims=True)


def _setcol(tile, hq, col):
    return jnp.where(_lane() == hq, col, tile)


def _head_fwd(qa, groups, sk):
    ss, m = [], None
    for kmat, _, mask in groups:
        s = _dg_nt(qa, kmat) * _SCALE
        if mask is not None:
            s = jnp.where(mask, s, NEG_INF)
        ss.append(s)
        mm = jnp.max(s, axis=1, keepdims=True)
        m = mm if m is None else jnp.maximum(m, mm)
    m = jnp.maximum(m, sk)
    l = jnp.exp(sk - m)
    pv = None
    for s, (_, vmat, _) in zip(ss, groups):
        p = jnp.exp(s - m)
        l = l + jnp.sum(p, axis=1, keepdims=True)
        t = jnp.dot(_mx(p), vmat, preferred_element_type=F32)
        pv = t if pv is None else pv + t
    return pv / l, m + jnp.log(l)


def _head_bwd(qa, doa, groups, sk, lse_h, delta_h):
    dq, outs = None, []
    for kmat, vmat, mask in groups:
        s = _dg_nt(qa, kmat) * _SCALE
        if mask is not None:
            s = jnp.where(mask, s, NEG_INF)
        p = jnp.exp(s - lse_h)
        dp = _dg_nt(doa, vmat)
        ds = p * (dp - delta_h) * _SCALE
        t = jnp.dot(_mx(ds), kmat, preferred_element_type=F32)
        dq = t if dq is None else dq + t
        outs.append((_mx(ds), _mx(p)))
    return dq, outs, jnp.exp(sk - lse_h)


def _kv_group(h):
    return [(g, Q_PER_KV * h + g, (Q_PER_KV * h + g) // 2, (g % 2) != h) for g in range(Q_PER_KV)]


def _to_kv_lanes(pair_tile, roll):
    if not roll:
        return _mx(pair_tile)
    return _mx(pltpu.roll(pair_tile.astype(F32), HEAD_DIM, 1))


def _kv_masked(x, h):
    return _mx(jnp.where(_hm(h * HEAD_DIM), x.astype(F32), 0.0))


def _attn_fwd_block(sink_ref, q_ref, kv, n, o_ref, lse_ref):
    lse_t = jnp.zeros((n, LANES), F32)
    for h in range(2):
        groups = [(_kv_masked(k, h), _kv_masked(v, h), mask) for k, v, mask in kv]
        heads = _kv_group(h)
        q = jnp.concatenate([_to_kv_lanes(q_ref[:, pair * LANES:(pair + 1) * LANES], roll)
                             for _, _, pair, roll in heads], axis=0)
        sk = jnp.concatenate([jnp.full((n, 1), sink_ref[0, hq], F32) for _, hq, _, _ in heads], axis=0)
        o, lse = _head_fwd(q, groups, sk)
        o_pair = None
        for g, hq, pair, roll in heads:
            og = o[g * n:(g + 1) * n]
            og = pltpu.roll(og, HEAD_DIM, 1) if roll else og
            lse_t = _setcol(lse_t, hq, lse[g * n:(g + 1) * n])
            if g % 2 == 0:
                o_pair = og
            else:
                o_ref[:, pair * LANES:(pair + 1) * LANES] = o_pair + og
    lse_ref[...] = lse_t


def _attn_bwd_block(sink_ref, q_ref, do_ref, o_ref, lse_t, kv, want, n, dq_ref):
    delta_t = jnp.zeros((n, LANES), F32)
    dsk = jnp.zeros((1, LANES), F32)
    dkv = [None] * len(kv)
    for h in range(2):
        hm = _hm(h * HEAD_DIM)
        groups = [(_kv_masked(k, h), _kv_masked(v, h), mask) for k, v, mask in kv]
        heads = _kv_group(h)
        qs, dos, sks, lses, deltas = [], [], [], [], []
        for g, hq, pair, roll in heads:
            sl = slice(pair * LANES, (pair + 1) * LANES)
            do_p = do_ref[:, sl]
            delta_h = jnp.sum(jnp.where(_hm((hq % 2) * HEAD_DIM), do_p * o_ref[:, sl], 0.0), axis=1, keepdims=True)
            delta_t = _setcol(delta_t, hq, delta_h)
            qs.append(_to_kv_lanes(q_ref[:, sl], roll))
            dos.append(_to_kv_lanes(do_p, roll))
            sks.append(jnp.full((n, 1), sink_ref[0, hq], F32))
            lses.append(_col(lse_t, hq))
            deltas.append(delta_h)
        q, do = jnp.concatenate(qs, axis=0), jnp.concatenate(dos, axis=0)
        delta = jnp.concatenate(deltas, axis=0)
        dq, outs, p_s = _head_bwd(q, do, groups, jnp.concatenate(sks, axis=0), jnp.concatenate(lses, axis=0), delta)
        dq_pair = None
        for g, hq, pair, roll in heads:
            dg = dq[g * n:(g + 1) * n]
            dg = pltpu.roll(dg, HEAD_DIM, 1) if roll else dg
            dsk = dsk + jnp.where(_lane() == hq,
                                  jnp.sum(-p_s[g * n:(g + 1) * n] * deltas[g], axis=0, keepdims=True), 0.0)
            if g % 2 == 0:
                dq_pair = dg
            else:
                dq_ref[:, pair * LANES:(pair + 1) * LANES] = dq_pair + dg
        for gi, (ds, pp) in enumerate(outs):
            if want[gi]:
                dk_h = jnp.where(hm, _dg_tn(ds, q), 0.0)
                dv_h = jnp.where(hm, _dg_tn(pp, do), 0.0)
                dkv[gi] = (dk_h, dv_h) if dkv[gi] is None else (dkv[gi][0] + dk_h, dkv[gi][1] + dv_h)
    return delta_t, dsk, dkv


def _partner(t):
    lane = lax.broadcasted_iota(jnp.int32, t.shape, 1)
    return jnp.where((lane & ROPE_PAIRS) == 0, pltpu.roll(t, LANES - ROPE_PAIRS, 1), pltpu.roll(t, ROPE_PAIRS, 1))


def _rope_tables(n_tokens):
    rows = n_tokens // GRID_W
    row = jnp.broadcast_to(jnp.arange(rows)[:, None], (rows, GRID_W)).reshape(-1)
    col = jnp.broadcast_to(jnp.arange(GRID_W)[None, :], (rows, GRID_W)).reshape(-1)
    freqs = ROPE_BASE ** (-jnp.arange(ROPE_PAIRS, dtype=F32) / ROPE_PAIRS)
    ang = jnp.concatenate([row[:, None].astype(F32) * freqs, col[:, None].astype(F32) * freqs], axis=-1)
    c, s = jnp.cos(ang), jnp.sin(ang)
    n = ROPE_PAIRS
    cos64 = jnp.concatenate([c[:, :n], c[:, :n], c[:, n:], c[:, n:]], axis=1)
    sin64 = jnp.concatenate([-s[:, :n], s[:, :n], -s[:, n:], s[:, n:]], axis=1)
    return jnp.tile(cos64, (1, 2)), jnp.tile(sin64, (1, 2))


def _rope_call(q_src, k_src, cos, sin, dims, inverse):
    b, l, _ = dims
    t_lat = b * l
    tr = _pick(l, (ROW_TILE, 128))
    per = l // tr
    out_dtype = F32 if inverse else MXU_DTYPE

    def body(q_ref, k_ref, c_ref, s_ref, qo_ref, ko_ref):
        c, s = c_ref[...], s_ref[...]

        def rot(t):
            if inverse:
                return t * c + _partner(t * s)
            return t * c + _partner(t) * s

        for j in range(ATTN_WIDTH // LANES):
            qo_ref[:, j * LANES:(j + 1) * LANES] = rot(q_ref[:, j * LANES:(j + 1) * LANES]).astype(out_dtype)
        ko_ref[...] = rot(k_ref[...]).astype(out_dtype)

    kcol = 0 if inverse else ATTN_WIDTH // KV_WIDTH
    return pl.pallas_call(
        body, name=f"rope_{'inv' if inverse else 'fwd'}_{t_lat}", grid=(t_lat // tr,),
        out_shape=[SDS((t_lat, ATTN_WIDTH), out_dtype), SDS((t_lat, KV_WIDTH), out_dtype)],
        in_specs=[pl.BlockSpec((tr, ATTN_WIDTH), lambda i: (i, 0)),
                  pl.BlockSpec((tr, KV_WIDTH), lambda i: (i, kcol)),
                  pl.BlockSpec((tr, LANES), lambda i: (i % per, 0)),
                  pl.BlockSpec((tr, LANES), lambda i: (i % per, 0))],
        out_specs=[pl.BlockSpec((tr, ATTN_WIDTH), lambda i: (i, 0)), pl.BlockSpec((tr, KV_WIDTH), lambda i: (i, 0))],
    )(q_src, k_src, cos, sin)


_SMEM_SPEC = pl.BlockSpec(memory_space=pltpu.SMEM)
_KCOL = ATTN_WIDTH // KV_WIDTH
_VCOL = _KCOL + 1


def _win_specs(arr_cols, nb, col):
    del arr_cols
    return [pl.BlockSpec((Q_BLOCK, KV_WIDTH), lambda b, i, d=d: (b * nb + jnp.clip(i + d, 0, nb - 1), col))
            for d in (-1, 0, 1)]


def _win_mask(i, l):
    shape = (Q_PER_KV * Q_BLOCK, 3 * Q_BLOCK)
    qpos = i * Q_BLOCK + (lax.broadcasted_iota(jnp.int32, shape, 0) & (Q_BLOCK - 1))
    kpos = (i - 1) * Q_BLOCK + lax.broadcasted_iota(jnp.int32, shape, 1)
    return jnp.logical_and(jnp.abs(qpos - kpos) <= WINDOW, jnp.logical_and(kpos >= 0, kpos < l))


def _attn_lat_fwd(qr, kr, p, sink, dims):
    b, l, lc = dims
    nb = l // Q_BLOCK
    t_lat = b * l
    cbase = t_lat // lc

    def body(sink_ref, q_ref, kp, kc, kn, vp, vc, vn, ck_ref, cv_ref, o_ref, lse_ref):
        kwin = jnp.concatenate([kp[...], kc[...], kn[...]], axis=0)
        vwin = jnp.concatenate([vp[...], vc[...], vn[...]], axis=0)
        kv = [(kwin, vwin, _win_mask(pl.program_id(1), l)), (ck_ref[...], cv_ref[...], None)]
        _attn_fwd_block(sink_ref, q_ref, kv, Q_BLOCK, o_ref, lse_ref)

    return pl.pallas_call(
        body, name=f"attn_lat_fwd_{t_lat}", grid=(b, nb),
        out_shape=[SDS((t_lat, ATTN_WIDTH), F32), SDS((t_lat, LANES), F32)],
        in_specs=[_SMEM_SPEC, pl.BlockSpec((Q_BLOCK, ATTN_WIDTH), lambda bb, i: (bb * nb + i, 0))]
        + _win_specs(None, nb, 0) + _win_specs(None, nb, _VCOL)
        + [pl.BlockSpec((lc, KV_WIDTH), lambda bb, i: (cbase + bb, _KCOL)),
           pl.BlockSpec((lc, KV_WIDTH), lambda bb, i: (cbase + bb, _VCOL))],
        out_specs=[pl.BlockSpec((Q_BLOCK, ATTN_WIDTH), lambda bb, i: (bb * nb + i, 0)),
                   pl.BlockSpec((Q_BLOCK, LANES), lambda bb, i: (bb * nb + i, 0))],
    )(sink, qr, kr, kr, kr, p, p, p, p, p)


def _attn_lat_bwd_dq(qr, kr, p, sink, o, lse, dout, dims):
    b, l, lc = dims
    nb = l // Q_BLOCK
    t_lat = b * l
    cbase = t_lat // lc

    def body(sink_ref, q_ref, kp, kc, kn, vp, vc, vn, ck_ref, cv_ref, o_ref, lse_ref, do_ref,
             dq_ref, delta_ref, dkc_ref, dvc_ref, dsk_ref):
        bb, i = pl.program_id(0), pl.program_id(1)

        @pl.when(i == 0)
        def _():
            dkc_ref[...] = jnp.zeros_like(dkc_ref)
            dvc_ref[...] = jnp.zeros_like(dvc_ref)

        @pl.when(jnp.logical_and(bb == 0, i == 0))
        def _():
            dsk_ref[...] = jnp.zeros_like(dsk_ref)

        kwin = jnp.concatenate([kp[...], kc[...], kn[...]], axis=0)
        vwin = jnp.concatenate([vp[...], vc[...], vn[...]], axis=0)
        kv = [(kwin, vwin, _win_mask(i, l)), (ck_ref[...], cv_ref[...], None)]
        delta_t, dsk, dkv = _attn_bwd_block(sink_ref, q_ref, do_ref, o_ref, lse_ref[...], kv, [False, True], Q_BLOCK,
                                            dq_ref)
        delta_ref[...] = delta_t
        dkc_ref[...] += dkv[1][0]
        dvc_ref[...] += dkv[1][1]
        dsk_ref[0:1, :] += dsk

    qspec = pl.BlockSpec((Q_BLOCK, ATTN_WIDTH), lambda bb, i: (bb * nb + i, 0))
    tspec = pl.BlockSpec((Q_BLOCK, LANES), lambda bb, i: (bb * nb + i, 0))
    cspec = pl.BlockSpec((lc, KV_WIDTH), lambda bb, i: (bb, 0))
    return pl.pallas_call(
        body, name=f"attn_lat_bwd_dq_{t_lat}", grid=(b, nb),
        out_shape=[SDS((t_lat, ATTN_WIDTH), F32), SDS((t_lat, LANES), F32), SDS((b * lc, KV_WIDTH), F32),
                   SDS((b * lc, KV_WIDTH), F32), SDS((8, LANES), F32)],
        in_specs=[_SMEM_SPEC, qspec] + _win_specs(None, nb, 0) + _win_specs(None, nb, _VCOL)
        + [pl.BlockSpec((lc, KV_WIDTH), lambda bb, i: (cbase + bb, _KCOL)),
           pl.BlockSpec((lc, KV_WIDTH), lambda bb, i: (cbase + bb, _VCOL)), qspec, tspec, qspec],
        out_specs=[qspec, tspec, cspec, cspec, pl.BlockSpec((8, LANES), lambda bb, i: (0, 0))],
    )(sink, qr, kr, kr, kr, p, p, p, p, p, o, lse, dout)


def _attn_lat_bwd_dkv(qr, kr, p, lse, delta, dout, dims):
    b, l, _ = dims
    nb = l // Q_BLOCK
    t_lat = b * l

    def body(k_ref, v_ref, *refs):
        j = pl.program_id(1)
        kj, vj = k_ref[...], v_ref[...]
        n_q = 3 * Q_PER_KV * Q_BLOCK
        sub = lax.broadcasted_iota(jnp.int32, (8, LANES), 0)
        col = lax.broadcasted_iota(jnp.int32, (Q_BLOCK, n_q), 1)
        i_of = j + col // (Q_PER_KV * Q_BLOCK) - 1
        qpos = i_of * Q_BLOCK + (col & (Q_BLOCK - 1))
        kpos = j * Q_BLOCK + lax.broadcasted_iota(jnp.int32, (Q_BLOCK, n_q), 0)
        mask = jnp.logical_and(jnp.abs(qpos - kpos) <= WINDOW, jnp.logical_and(i_of >= 0, i_of < nb))
        lse_rows = [refs[4 * n + 2][...].T[0:8, :] for n in range(3)]
        delta_rows = [refs[4 * n + 3][...].T[0:8, :] for n in range(3)]
        dk = jnp.zeros((Q_BLOCK, KV_WIDTH), F32)
        dv = jnp.zeros((Q_BLOCK, KV_WIDTH), F32)
        for h in range(2):
            hm = _hm(h * HEAD_DIM)
            kh, vh = _kv_masked(kj, h), _kv_masked(vj, h)
            qs, dos, lrow, drow = [], [], [], []
            for n in range(3):
                q_ref, do_ref = refs[4 * n], refs[4 * n + 1]
                for _, hq, pair, roll in _kv_group(h):
                    sl = slice(pair * LANES, (pair + 1) * LANES)
                    qs.append(_to_kv_lanes(q_ref[:, sl], roll))
                    dos.append(_to_kv_lanes(do_ref[:, sl], roll))
                    lrow.append(jnp.sum(jnp.where(sub == hq, lse_rows[n], 0.0), axis=0, keepdims=True))
                    drow.append(jnp.sum(jnp.where(sub == hq, delta_rows[n], 0.0), axis=0, keepdims=True))
            q, do = jnp.concatenate(qs, axis=0), jnp.concatenate(dos, axis=0)
            lse_r, delta_r = jnp.concatenate(lrow, axis=1), jnp.concatenate(drow, axis=1)
            s_t = jnp.where(mask, _dg_nt(kh, q) * _SCALE, NEG_INF)
            p_t = jnp.exp(s_t - lse_r)
            ds_t = p_t * (_dg_nt(vh, do) - delta_r) * _SCALE
            dv = dv + jnp.where(hm, jnp.dot(_mx(p_t), do, preferred_element_type=F32), 0.0)
            dk = dk + jnp.where(hm, jnp.dot(_mx(ds_t), q, preferred_element_type=F32), 0.0)
        dk_ref, dv_ref = refs[12], refs[13]
        dk_ref[...] = dk
        dv_ref[...] = dv

    def blk(width, d, col=0):
        return pl.BlockSpec((Q_BLOCK, width), lambda bb, j: (bb * nb + jnp.clip(j + d, 0, nb - 1), col))

    in_specs = [blk(KV_WIDTH, 0), blk(KV_WIDTH, 0, _VCOL)]
    args = [kr, p]
    for d in (-1, 0, 1):
        in_specs += [blk(ATTN_WIDTH, d), blk(ATTN_WIDTH, d), blk(LANES, d), blk(LANES, d)]
        args += [qr, dout, lse, delta]
    return pl.pallas_call(
        body, name=f"attn_lat_bwd_dkv_{t_lat}", grid=(b, nb),
        out_shape=[SDS((t_lat, KV_WIDTH), F32), SDS((t_lat, KV_WIDTH), F32)],
        in_specs=in_specs, out_specs=[blk(KV_WIDTH, 0), blk(KV_WIDTH, 0)],
    )(*args)


def _attn_ctx_fwd(p, sink, dims):
    b, l, lc = dims
    cbase = b * l // lc

    def body(sink_ref, q_ref, k_ref, v_ref, o_ref, lse_ref):
        _attn_fwd_block(sink_ref, q_ref, [(k_ref[...], v_ref[...], None)], lc, o_ref, lse_ref)

    return pl.pallas_call(
        body, name=f"attn_ctx_fwd_{b * lc}", grid=(b,),
        out_shape=[SDS((b * lc, ATTN_WIDTH), F32), SDS((b * lc, LANES), F32)],
        in_specs=[_SMEM_SPEC, pl.BlockSpec((lc, ATTN_WIDTH), lambda bb: (cbase + bb, 0)),
                  pl.BlockSpec((lc, KV_WIDTH), lambda bb: (cbase + bb, _KCOL)),
                  pl.BlockSpec((lc, KV_WIDTH), lambda bb: (cbase + bb, _VCOL))],
        out_specs=[pl.BlockSpec((lc, ATTN_WIDTH), lambda bb: (bb, 0)), pl.BlockSpec((lc, LANES), lambda bb: (bb, 0))],
    )(sink, p, p, p)


def _attn_ctx_bwd(p, sink, o, lse, dout, dims):
    b, l, lc = dims
    cbase = b * l // lc

    def body(sink_ref, q_ref, k_ref, v_ref, o_ref, lse_ref, do_ref, dq_ref, dk_ref, dv_ref, dsk_ref):
        bb = pl.program_id(0)

        @pl.when(bb == 0)
        def _():
            dsk_ref[...] = jnp.zeros_like(dsk_ref)

        _, dsk, dkv = _attn_bwd_block(sink_ref, q_ref, do_ref, o_ref, lse_ref[...], [(k_ref[...], v_ref[...], None)],
                                      [True], lc, dq_ref)
        dk_ref[...] = dkv[0][0]
        dv_ref[...] = dkv[0][1]
        dsk_ref[0:1, :] += dsk

    qs = pl.BlockSpec((lc, ATTN_WIDTH), lambda bb: (bb, 0))
    ks = pl.BlockSpec((lc, KV_WIDTH), lambda bb: (bb, 0))
    return pl.pallas_call(
        body, name=f"attn_ctx_bwd_{b * lc}", grid=(b,),
        out_shape=[SDS((b * lc, ATTN_WIDTH), F32), SDS((b * lc, KV_WIDTH), F32), SDS((b * lc, KV_WIDTH), F32),
                   SDS((8, LANES), F32)],
        in_specs=[_SMEM_SPEC, pl.BlockSpec((lc, ATTN_WIDTH), lambda bb: (cbase + bb, 0)),
                  pl.BlockSpec((lc, KV_WIDTH), lambda bb: (cbase + bb, _KCOL)),
                  pl.BlockSpec((lc, KV_WIDTH), lambda bb: (cbase + bb, _VCOL)),
                  qs, pl.BlockSpec((lc, LANES), lambda bb: (bb, 0)),
                  pl.BlockSpec((lc, ATTN_WIDTH), lambda bb: (cbase + bb, 0))],
        out_specs=[qs, ks, ks, pl.BlockSpec((8, LANES), lambda bb: (0, 0))],
    )(sink, p, p, p, o, lse, dout)


_CONV_COL = (ATTN_WIDTH + 2 * KV_WIDTH) // CONV_WIDTH


def _shift_prev(z, n):
    rows = lax.broadcasted_iota(jnp.int32, z.shape, 0)
    return jnp.where(rows == 0, 0.0, pltpu.roll(z, 1, 0))


def _shift_next(z, n):
    rows = lax.broadcasted_iota(jnp.int32, z.shape, 0)
    return jnp.where(rows == n - 1, 0.0, pltpu.roll(z, n - 1, 0))


def _conv_fwd(p, w, base, n_seq, ls):
    def body(cb_ref, cc_ref, cx_ref, w_ref, o_ref):
        z = cc_ref[...] * cx_ref[...]
        c3 = _shift_prev(z, ls) * w_ref[0:1, :] + z * w_ref[1:2, :] + _shift_next(z, ls) * w_ref[2:3, :]
        o_ref[...] = cb_ref[...] * c3

    return pl.pallas_call(
        body, name=f"conv_fwd_{n_seq}x{ls}", grid=(n_seq,),
        out_shape=SDS((n_seq * ls, CONV_WIDTH), F32),
        in_specs=[pl.BlockSpec((ls, CONV_WIDTH), lambda s, c=c: (base + s, _CONV_COL + c)) for c in range(3)]
        + [pl.BlockSpec(w.shape, lambda s: (0, 0))],
        out_specs=pl.BlockSpec((ls, CONV_WIDTH), lambda s: (s, 0)),
    )(p, p, p, w)


def _conv_bwd(p, w, dout, base, n_seq, ls):
    dcol = ATTN_WIDTH // CONV_WIDTH

    def body(cb_ref, cc_ref, cx_ref, w_ref, do_ref, dcb_ref, dcc_ref, dcx_ref, dw_ref):
        @pl.when(pl.program_id(0) == 0)
        def _():
            dw_ref[...] = jnp.zeros_like(dw_ref)

        cc, cx = cc_ref[...], cx_ref[...]
        z = cc * cx
        zp, zn = _shift_prev(z, ls), _shift_next(z, ls)
        c3 = zp * w_ref[0:1, :] + z * w_ref[1:2, :] + zn * w_ref[2:3, :]
        do = do_ref[...]
        dcb_ref[...] = do * c3
        e = do * cb_ref[...]
        dz = _shift_next(e, ls) * w_ref[0:1, :] + e * w_ref[1:2, :] + _shift_prev(e, ls) * w_ref[2:3, :]
        dcc_ref[...] = dz * cx
        dcx_ref[...] = dz * cc
        dw_ref[0:1, :] += jnp.sum(e * zp, axis=0, keepdims=True)
        dw_ref[1:2, :] += jnp.sum(e * z, axis=0, keepdims=True)
        dw_ref[2:3, :] += jnp.sum(e * zn, axis=0, keepdims=True)

    ospec = pl.BlockSpec((ls, CONV_WIDTH), lambda s: (s, 0))
    return pl.pallas_call(
        body, name=f"conv_bwd_{n_seq}x{ls}", grid=(n_seq,),
        out_shape=[SDS((n_seq * ls, CONV_WIDTH), F32)] * 3 + [SDS((8, CONV_WIDTH), F32)],
        in_specs=[pl.BlockSpec((ls, CONV_WIDTH), lambda s, c=c: (base + s, _CONV_COL + c)) for c in range(3)]
        + [pl.BlockSpec(w.shape, lambda s: (0, 0)), pl.BlockSpec((ls, CONV_WIDTH), lambda s: (base + s, dcol))],
        out_specs=[ospec, ospec, ospec, pl.BlockSpec((8, CONV_WIDTH), lambda s: (0, 0))],
    )(p, p, p, w, dout)


def _make_att_conv(dims, cos, sin):
    b, l, lc = dims
    t_lat = b * l

    def forward(p, conv_w, sink):
        qr, kr = _rope_call(p, p, cos, sin, dims, inverse=False)
        o_lat, lse_lat = _attn_lat_fwd(qr, kr, p, sink, dims)
        o_ctx, lse_ctx = _attn_ctx_fwd(p, sink, dims)
        conv_lat = _conv_fwd(p, conv_w, 0, b, l)
        conv_ctx = _conv_fwd(p, conv_w, t_lat // lc, b, lc)
        out = jnp.concatenate([jnp.concatenate([o_lat, conv_lat], axis=1),
                               jnp.concatenate([o_ctx, conv_ctx], axis=1)], axis=0)
        return out, (p, conv_w, sink, qr, kr, o_lat, lse_lat, o_ctx, lse_ctx)

    @jax.custom_vjp
    def op(p, conv_w, sink):
        return forward(p, conv_w, sink)[0]

    def op_bwd(res, dout):
        p, conv_w, sink, qr, kr, o_lat, lse_lat, o_ctx, lse_ctx = res
        dqr, delta, dkc1, dvc1, dsk1 = _attn_lat_bwd_dq(qr, kr, p, sink, o_lat, lse_lat, dout, dims)
        dkr, dv = _attn_lat_bwd_dkv(qr, kr, p, lse_lat, delta, dout, dims)
        dq, dk = _rope_call(dqr, dkr, cos, sin, dims, inverse=True)
        dqc, dkc2, dvc2, dsk2 = _attn_ctx_bwd(p, sink, o_ctx, lse_ctx, dout, dims)
        dcb_l, dcc_l, dcx_l, dw_l = _conv_bwd(p, conv_w, dout, 0, b, l)
        dcb_c, dcc_c, dcx_c, dw_c = _conv_bwd(p, conv_w, dout, t_lat // lc, b, lc)
        zeros_u = jnp.zeros((p.shape[0], SSM_WIDTH), F32)
        lat = jnp.concatenate([dq, dk, dv, dcb_l, dcc_l, dcx_l], axis=1)
        ctx = jnp.concatenate([dqc, dkc1 + dkc2, dvc1 + dvc2, dcb_c, dcc_c, dcx_c], axis=1)
        dp = jnp.concatenate([jnp.concatenate([lat, ctx], axis=0), zeros_u], axis=1)
        dsink = (dsk1 + dsk2)[0:1, :N_Q_HEADS]
        return dp, (dw_l + dw_c)[:3], dsink

    op.defvjp(forward, op_bwd)
    return op


def _scan_call(x, lam_r, lam_i, dims, ctx_first, dec, h=None):
    b, l, lc = dims
    n = SSM_N
    tc = _pick(lc, (256, 128, 64, 32, 16, 8))
    nc_c, nc_l = lc // tc, l // tc
    n_chunks = nc_c + nc_l
    ctx_base = b * l // tc
    g8 = tc // 8
    with_h = h is not None

    def chunk(bb, j):
        if ctx_first:
            is_ctx = j < nc_c
            jj = jnp.where(is_ctx, j, j - nc_c)
        else:
            is_ctx = j >= nc_l
            jj = jnp.where(is_ctx, j - nc_l, j)
        ic = nc_c - 1 - jj if dec else jj
        il = nc_l - 1 - jj if dec else jj
        return jnp.where(is_ctx, ctx_base + bb * nc_c + ic, bb * nc_l + il)

    def edge(bb, j):
        jn = jnp.minimum(j + 1, n_chunks - 1)
        return chunk(bb, jn) * g8 + (g8 - 1 if dec else 0)

    def body(x_ref, lr_ref, li_ref, *rest):
        rest = list(rest)
        h_ref, hb_ref = (rest.pop(0), rest.pop(0)) if with_h else (None, None)
        o_ref = rest.pop(0)
        dl_ref = rest.pop(0) if with_h else None
        carry = rest.pop(0)
        bb, j = pl.program_id(0), pl.program_id(1)

        @pl.when(j == 0)
        def _():
            carry[...] = jnp.zeros_like(carry)

        if with_h:
            @pl.when(jnp.logical_and(bb == 0, j == 0))
            def _():
                dl_ref[...] = jnp.zeros_like(dl_ref)

        rows = lax.broadcasted_iota(jnp.int32, (tc, LANES), 0)
        first = rows == (tc - 1 if dec else 0)
        has_next = j + 1 < n_chunks
        for s in range(n // LANES):
            re, im = slice(s * LANES, (s + 1) * LANES), slice(n + s * LANES, n + (s + 1) * LANES)
            ar, ai = lr_ref[:, re], li_ref[:, re]
            cr, ci = carry[0:1, re], carry[0:1, im]
            dr = x_ref[:, re] + jnp.where(first, ar * cr - ai * ci, 0.0)
            di = x_ref[:, im] + jnp.where(first, ar * ci + ai * cr, 0.0)
            pr, pi = ar, ai
            sft = 1
            while sft < tc:
                if dec:
                    keep = rows < tc - sft
                    sr = jnp.where(keep, pltpu.roll(dr, tc - sft, 0), 0.0)
                    si = jnp.where(keep, pltpu.roll(di, tc - sft, 0), 0.0)
                else:
                    keep = rows >= sft
                    sr = jnp.where(keep, pltpu.roll(dr, sft, 0), 0.0)
                    si = jnp.where(keep, pltpu.roll(di, sft, 0), 0.0)
                dr, di = dr + pr * sr - pi * si, di + pr * si + pi * sr
                pr, pi = pr * pr - pi * pi, 2.0 * pr * pi
                sft *= 2
            o_ref[:, re] = dr
            o_ref[:, im] = di
            last = 0 if dec else tc - 1
            carry[0:1, re] = o_ref[last:last + 1, re]
            carry[0:1, im] = o_ref[last:last + 1, im]
            if with_h:
                hr, hi = h_ref[:, re], h_ref[:, im]
                er = 7 if dec else 0
                br = jnp.where(has_next, hb_ref[er:er + 1, re], 0.0)
                bi = jnp.where(has_next, hb_ref[er:er + 1, im], 0.0)
                if dec:
                    nr = jnp.where(rows == 0, br, pltpu.roll(hr, 1, 0))
                    ni = jnp.where(rows == 0, bi, pltpu.roll(hi, 1, 0))
                else:
                    nr = jnp.where(rows == tc - 1, br, pltpu.roll(hr, tc - 1, 0))
                    ni = jnp.where(rows == tc - 1, bi, pltpu.roll(hi, tc - 1, 0))
                dl_ref[0:1, re] += jnp.sum(dr * nr + di * ni, axis=0, keepdims=True)
                dl_ref[0:1, im] += jnp.sum(di * nr - dr * ni, axis=0, keepdims=True)

    xspec = pl.BlockSpec((tc, 2 * n), lambda bb, j: (chunk(bb, j), 0))
    lspec = pl.BlockSpec((1, n), lambda bb, j: (0, 0))
    in_specs, args = [xspec, lspec, lspec], [x, lam_r, lam_i]
    out_shape, out_specs = [SDS(x.shape, F32)], [xspec]
    if with_h:
        in_specs += [xspec, pl.BlockSpec((8, 2 * n), lambda bb, j: (edge(bb, j), 0))]
        args += [h, h]
        out_shape.append(SDS((8, 2 * n), F32))
        out_specs.append(pl.BlockSpec((8, 2 * n), lambda bb, j: (0, 0)))
    tag = ('c' if ctx_first else 'l') + ('d' if dec else 'u') + ('h' if with_h else '')
    res = pl.pallas_call(
        body, name=f"scan_{tag}_{x.shape[0]}", grid=(b, n_chunks),
        out_shape=out_shape, in_specs=in_specs, out_specs=out_specs,
        scratch_shapes=[pltpu.VMEM((8, 2 * n), F32)],
    )(*args)
    return res


def _make_scan(dims, direction):
    @jax.custom_vjp
    def scan(x, lam_r, lam_i):
        return _scan_call(x, lam_r, lam_i, dims, True, direction == 1)[0]

    def scan_fwd(x, lam_r, lam_i):
        h = _scan_call(x, lam_r, lam_i, dims, True, direction == 1)[0]
        return h, (h, lam_r, lam_i)

    def scan_bwd(res, g):
        h, lam_r, lam_i = res
        gd, dl = _scan_call(g, lam_r, -lam_i, dims, False, direction == 0, h=h)
        return gd, dl[0:1, :SSM_N], dl[0:1, SSM_N:]

    scan.defvjp(scan_fwd, scan_bwd)
    return scan


def _cmul(ar, ai, br, bi):
    return ar * br - ai * bi, ar * bi + ai * br


def _scan_chunk(buf, lr_ref, li_ref, carry, dec, tc):
    sub = lax.broadcasted_iota(jnp.int32, (8, LANES), 0)
    groups = range(tc // 8 - 1, -1, -1) if dec else range(tc // 8)
    for s in range(SSM_N // LANES):
        re, im = slice(s * LANES, (s + 1) * LANES), slice(SSM_N + s * LANES, SSM_N + (s + 1) * LANES)
        pw = [(lr_ref[:, re], li_ref[:, re])]
        for _ in range(7):
            pw.append(_cmul(*pw[-1], *pw[0]))
        p8r, p8i = jnp.zeros((8, LANES), F32), jnp.zeros((8, LANES), F32)
        for k in range(8):
            row = 7 - k if dec else k
            p8r = jnp.where(sub == row, pw[k][0], p8r)
            p8i = jnp.where(sub == row, pw[k][1], p8i)
        steps = []
        for sft in (1, 2, 4):
            keep = sub < 8 - sft if dec else sub >= sft
            steps.append((8 - sft if dec else sft, jnp.where(keep, pw[sft - 1][0], 0.0),
                          jnp.where(keep, pw[sft - 1][1], 0.0)))
        cr, ci = carry[0:1, re], carry[0:1, im]
        for r in groups:
            rows = slice(8 * r, 8 * r + 8)
            xr, xi = buf[rows, re], buf[rows, im]
            for amount, mr, mi in steps:
                sr, si = pltpu.roll(xr, amount, 0), pltpu.roll(xi, amount, 0)
                xr, xi = xr + mr * sr - mi * si, xi + mr * si + mi * sr
            xr, xi = xr + p8r * cr - p8i * ci, xi + p8r * ci + p8i * cr
            buf[rows, re] = xr
            buf[rows, im] = xi
            last = 8 * r if dec else 8 * r + 7
            cr, ci = buf[last:last + 1, re], buf[last:last + 1, im]
        carry[0:1, re] = cr
        carry[0:1, im] = ci


def _ssm_chunks(dims, ctx_first, dec):
    b, l, lc = dims
    tc = _pick(lc, (256, 128, 64, 32, 16, 8))
    nc_c, nc_l = lc // tc, l // tc
    n_chunks = nc_c + nc_l
    ctx_base = b * l // tc
    g8 = tc // 8

    def chunk(bb, j):
        if ctx_first:
            is_ctx = j < nc_c
            jj = jnp.where(is_ctx, j, j - nc_c)
        else:
            is_ctx = j >= nc_l
            jj = jnp.where(is_ctx, j - nc_l, j)
        ic = nc_c - 1 - jj if dec else jj
        il = nc_l - 1 - jj if dec else jj
        return jnp.where(is_ctx, ctx_base + bb * nc_c + ic, bb * nc_l + il)

    def edge(bb, j):
        jn = jnp.minimum(j + 1, n_chunks - 1)
        return chunk(bb, jn) * g8 + (g8 - 1 if dec else 0)

    return tc, n_chunks, chunk, edge


def _ssm_fwd_call(u, bmat, cmat, lam_r, lam_i, dims, dec):
    n2 = 2 * SSM_N
    tc, n_chunks, chunk, _ = _ssm_chunks(dims, True, dec)

    def body(u_ref, b_ref, c_ref, lr_ref, li_ref, h_ref, y_ref, carry):
        @pl.when(pl.program_id(1) == 0)
        def _():
            carry[...] = jnp.zeros_like(carry)

        h_ref[...] = jnp.dot(_mx(u_ref[...]), b_ref[...], preferred_element_type=F32)
        _scan_chunk(h_ref, lr_ref, li_ref, carry, dec, tc)
        y_ref[...] = jnp.dot(_mx(h_ref[...]), c_ref[...], preferred_element_type=F32)

    def full(shape):
        return pl.BlockSpec(shape, lambda bb, j: (0, 0))

    return pl.pallas_call(
        body, name=f"ssm_fwd_{'d' if dec else 'u'}_{u.shape[0]}", grid=(dims[0], n_chunks),
        out_shape=[SDS((u.shape[0], n2), F32), SDS(u.shape, F32)],
        in_specs=[pl.BlockSpec((tc, SSM_WIDTH), lambda bb, j: (chunk(bb, j), 0)), full((SSM_WIDTH, n2)),
                  full((n2, SSM_WIDTH)), full((1, SSM_N)), full((1, SSM_N))],
        out_specs=[pl.BlockSpec((tc, n2), lambda bb, j: (chunk(bb, j), 0)),
                   pl.BlockSpec((tc, SSM_WIDTH), lambda bb, j: (chunk(bb, j), 0))],
        scratch_shapes=[pltpu.VMEM((8, n2), F32)],
    )(u, bmat, cmat, lam_r, lam_i)


def _ssm_bwd_call(gy, h, u, bmat, cmat, lam_r, lam_i, dims, dec):
    n2 = 2 * SSM_N
    tc, n_chunks, chunk, edge = _ssm_chunks(dims, False, dec)

    def body(gy_ref, h_ref, hb_ref, u_ref, b_ref, c_ref, lr_ref, li_ref, du_ref, db_ref, dct_ref, dl_ref, gd, carry):
        bb, j = pl.program_id(0), pl.program_id(1)

        @pl.when(j == 0)
        def _():
            carry[...] = jnp.zeros_like(carry)

        @pl.when(jnp.logical_and(bb == 0, j == 0))
        def _():
            db_ref[...] = jnp.zeros_like(db_ref)
            dct_ref[...] = jnp.zeros_like(dct_ref)
            dl_ref[...] = jnp.zeros_like(dl_ref)

        gyv = _mx(gy_ref[...])
        gd[...] = _dg_nt(gyv, c_ref[...])
        _scan_chunk(gd, lr_ref, li_ref, carry, dec, tc)
        gdv = _mx(gd[...])
        du_ref[...] = _dg_nt(gdv, b_ref[...])
        db_ref[...] += _dg_tn(_mx(u_ref[...]), gdv)
        dct_ref[...] += _dg_tn(gyv, _mx(h_ref[...]))
        rows = lax.broadcasted_iota(jnp.int32, (tc, LANES), 0)
        has_next = j + 1 < n_chunks
        er = 7 if dec else 0
        for s in range(SSM_N // LANES):
            re, im = slice(s * LANES, (s + 1) * LANES), slice(SSM_N + s * LANES, SSM_N + (s + 1) * LANES)
            dr, di, hr, hi = gd[:, re], gd[:, im], h_ref[:, re], h_ref[:, im]
            br = jnp.where(has_next, hb_ref[er:er + 1, re], 0.0)
            bi = jnp.where(has_next, hb_ref[er:er + 1, im], 0.0)
            if dec:
                nr = jnp.where(rows == 0, br, pltpu.roll(hr, 1, 0))
                ni = jnp.where(rows == 0, bi, pltpu.roll(hi, 1, 0))
            else:
                nr = jnp.where(rows == tc - 1, br, pltpu.roll(hr, tc - 1, 0))
                ni = jnp.where(rows == tc - 1, bi, pltpu.roll(hi, tc - 1, 0))
            dl_ref[0:1, re] += jnp.sum(dr * nr + di * ni, axis=0, keepdims=True)
            dl_ref[0:1, im] += jnp.sum(di * nr - dr * ni, axis=0, keepdims=True)

    def full(shape):
        return pl.BlockSpec(shape, lambda bb, j: (0, 0))

    def at_chunk(width):
        return pl.BlockSpec((tc, width), lambda bb, j: (chunk(bb, j), 0))

    return pl.pallas_call(
        body, name=f"ssm_bwd_{'d' if dec else 'u'}_{u.shape[0]}", grid=(dims[0], n_chunks),
        out_shape=[SDS(u.shape, F32), SDS((SSM_WIDTH, n2), F32), SDS((SSM_WIDTH, n2), F32), SDS((8, n2), F32)],
        in_specs=[at_chunk(SSM_WIDTH), at_chunk(n2), pl.BlockSpec((8, n2), lambda bb, j: (edge(bb, j), 0)),
                  at_chunk(SSM_WIDTH), full((SSM_WIDTH, n2)), full((n2, SSM_WIDTH)), full((1, SSM_N)),
                  full((1, SSM_N))],
        out_specs=[at_chunk(SSM_WIDTH), full((SSM_WIDTH, n2)), full((SSM_WIDTH, n2)), full((8, n2))],
        scratch_shapes=[pltpu.VMEM((tc, n2), F32), pltpu.VMEM((8, n2), F32)],
    )(gy, h, h, u, bmat, cmat, lam_r, lam_i)


def _make_ssm(dims, direction):
    @jax.custom_vjp
    def ssm(u, bmat, cmat, lam_r, lam_i):
        return _ssm_fwd_call(u, _mx(bmat), _mx(cmat), lam_r, lam_i, dims, direction == 1)[1]

    def ssm_fwd(u, bmat, cmat, lam_r, lam_i):
        h, y = _ssm_fwd_call(u, _mx(bmat), _mx(cmat), lam_r, lam_i, dims, direction == 1)
        return y, (u, bmat, cmat, lam_r, lam_i, h)

    def ssm_bwd(res, gy):
        u, bmat, cmat, lam_r, lam_i, h = res
        du, db, dct, dl = _ssm_bwd_call(gy, h, u, _mx(bmat), _mx(cmat), lam_r, -lam_i, dims, direction == 0)
        return du, db, dct.T, dl[0:1, :SSM_N], dl[0:1, SSM_N:]

    ssm.defvjp(ssm_fwd, ssm_bwd)
    return ssm


def _block_diag(m):
    g, a, bdim = m.shape[-3:]
    eye = jnp.eye(g, dtype=m.dtype)
    full = m[..., :, :, None, :] * eye[:, None, :, None]
    return full.reshape(m.shape[:-3] + (g * a, g * bdim))


def _ssm_mats(lam_re, lam_im, log_dt, b_re, b_im, c_re, c_im):
    lam = lax.complex(lam_re, lam_im)
    dt = jnp.exp(log_dt)[..., None]
    lam_bar = jnp.exp(lam * dt)
    b_bar = ((lam_bar - 1) / lam)[..., None] * lax.complex(b_re, b_im)
    bt = jnp.swapaxes(b_bar, -1, -2)
    bmat = jnp.concatenate([_block_diag(jnp.real(bt)), _block_diag(jnp.imag(bt))], axis=-1)
    cmat = jnp.concatenate([_block_diag(jnp.swapaxes(c_re, -1, -2)), _block_diag(-jnp.swapaxes(c_im, -1, -2))],
                           axis=-2)
    flat = lam_bar.shape[:2] + (1, SSM_N)
    return bmat, cmat, jnp.real(lam_bar).reshape(flat), jnp.imag(lam_bar).reshape(flat)


def _loss_call(y, target):
    t, d = y.shape
    tr = _pick(t, (ROW_TILE, 128, 64, 32, 16, 8))

    def body(y_ref, t_ref, acc_ref, dy_ref):
        @pl.when(pl.program_id(0) == 0)
        def _():
            acc_ref[...] = jnp.zeros_like(acc_ref)

        diff = y_ref[...] - t_ref[...]
        dy_ref[...] = diff * (1.0 / d)
        acc_ref[0:1, :] += jnp.sum(diff * diff, axis=0, keepdims=True)

    spec = pl.BlockSpec((tr, d), lambda i: (i, 0))
    return pl.pallas_call(
        body, name=f"loss_{t}", grid=(t // tr,),
        out_shape=[SDS((8, d), F32), SDS((t, d), F32)],
        in_specs=[spec, spec], out_specs=[pl.BlockSpec((8, d), lambda i: (0, 0)), spec],
    )(y, target)


def _adamw_call(w, g, m, v):
    r, c = w.shape
    tr = _pick(r, [t for t in (1024, 512, 256, 128, 64, 32, 16, 8) if t * c <= 256 * 1024])

    def body(w_ref, g_ref, m_ref, v_ref, d_ref, mo_ref, vo_ref):
        gv = g_ref[...]
        mn = ADAM_B1 * m_ref[...] + (1.0 - ADAM_B1) * gv
        vn = ADAM_B2 * v_ref[...] + (1.0 - ADAM_B2) * (gv * gv)
        m_hat = mn / (1.0 - ADAM_B1 ** ADAM_STEP)
        v_hat = vn / (1.0 - ADAM_B2 ** ADAM_STEP)
        d_ref[...] = -ADAM_LR * (m_hat / (jnp.sqrt(v_hat) + ADAM_EPS) + ADAM_WD * w_ref[...])
        mo_ref[...] = mn
        vo_ref[...] = vn

    spec = pl.BlockSpec((tr, c), lambda i: (i, 0))
    return pl.pallas_call(
        body, name=f"adamw_{r}x{c}", grid=(r // tr,),
        out_shape=[SDS((r, c), F32)] * 3, in_specs=[spec] * 4, out_specs=[spec] * 3,
    )(w, g, m, v)


def _forward(diff, gathered, dims, depth):
    b, l, lc = dims
    t_lat = b * l
    d_model = diff['x'].shape[-1]
    cos, sin = _rope_tables(l)
    nm_linear, nm_mlp = _make_nm_ops(*_rowwise("norm_mod", _norm_mod_f, (d_model,), l))
    resid_gate = _make_rowwise_op("resid_gate", _resid_gate_f, d_model, 2, 1, l)
    glu = _make_rowwise_op("glu", _glu_f, SSM_WIDTH, 3, 0, l)
    att_conv = _make_att_conv(dims, cos, sin)
    ssms = [_make_ssm(dims, 0), _make_ssm(dims, 1)]
    bmats, cmats, lam_r, lam_i = _ssm_mats(*[diff['ssm_' + n] for n in
                                             ('lam_re', 'lam_im', 'log_dt', 'b_re', 'b_im', 'c_re', 'c_im')])

    c_act = jax.nn.silu(diff['c'])
    c_ctx_act = jax.nn.silu(diff['c_ctx'])
    mod_in = jnp.concatenate([c_act, c_ctx_act[None, :], jnp.zeros((MOD_ROWS - b - 1, d_model), F32)], axis=0)
    h = jnp.concatenate([diff['x'].reshape(t_lat, d_model), diff['ctx'].reshape(b * lc, d_model)], axis=0)

    for layer in range(depth):
        last = layer == depth - 1
        mod = linear_t(mod_in, gathered['w_ada'][layer], diff['w_ada'][layer])[:b + 1] + diff['b_ada'][layer][None, :]
        sh1, sc1, g1, sh2, sc2, g2 = [m.reshape(b + 1, 1, d_model) for m in jnp.split(mod, N_MOD, axis=-1)]
        ng = diff['norm_g'][layer]
        p = nm_linear(h, sh1, sc1, ng[0:1], gathered['w_in'][layer], diff['w_in'][layer])
        ac = att_conv(p, diff['conv_w'][layer], diff['attn_sink'][layer][None, :])
        u = p[:, IN_WIDTH - SSM_WIDTH:]
        ys = [ssms[d](u, bmats[layer, d], cmats[layer, d], lam_r[layer, d], lam_i[layer, d]) for d in range(2)]
        s = glu(ys[0], ys[1], u, diff['ssm_d'][layer][None, :], diff['w_glu'][layer], diff['b_glu'][layer][None, :])
        if last:
            ac, s, h = ac[:t_lat], s[:t_lat], h[:t_lat]
            g1, sh2, sc2, g2 = g1[:b], sh2[:b], sc2[:b], g2[:b]
        m = mix_linear(ac, s, gathered['w_out'][layer], diff['w_out'][layer])
        h = resid_gate(h, m, g1, ng[1:2])
        f = nm_mlp(h, sh2, sc2, ng[2:3], gathered['w_mlp_in'][layer], diff['w_mlp_in'][layer],
                   gathered['w_mlp_out'][layer], diff['w_mlp_out'][layer])
        h = resid_gate(h, f, g2, ng[3:4])
    return h


BIG_TRANSPOSED = {'w_ada': True, 'w_in': True, 'w_out': False, 'w_mlp_in': True, 'w_mlp_out': False}


def _big_rows(weights):
    blocks = []
    depth = weights['w_ada'].shape[0]
    for n in BIG:
        for layer in range(depth):
            w = weights[n][layer]
            blocks.append((layer, n, w.shape[1] if BIG_TRANSPOSED[n] else w.shape[0]))
    return blocks


def _stack_big(arrs, dtype):
    parts = []
    for n in BIG:
        w = jnp.swapaxes(arrs[n], 1, 2) if BIG_TRANSPOSED[n] else arrs[n]
        parts.append(w.reshape(-1, w.shape[-1]).astype(dtype))
    return jnp.concatenate(parts, axis=0)


def _adamw_any(w, g, m, v):
    flat = [a.reshape(-1, w.shape[-1]) for a in (w, g, m, v)]
    return [o.reshape(w.shape) for o in _adamw_call(*flat)]


def kernel(x, c, ctx, c_ctx, w_ada, b_ada, norm_g, w_in, conv_w, attn_sink, ssm_lam_re, ssm_lam_im, ssm_log_dt, ssm_b_re, ssm_b_im, ssm_c_re, ssm_c_im, ssm_d, w_glu, b_glu, w_out, w_mlp_in, w_mlp_out, loss_target, m_c_ctx, m_w_ada, m_b_ada, m_norm_g, m_w_in, m_conv_w, m_attn_sink, m_ssm_lam_re, m_ssm_lam_im, m_ssm_log_dt, m_ssm_b_re, m_ssm_b_im, m_ssm_c_re, m_ssm_c_im, m_ssm_d, m_w_glu, m_b_glu, m_w_out, m_w_mlp_in, m_w_mlp_out, v_c_ctx, v_w_ada, v_b_ada, v_norm_g, v_w_in, v_conv_w, v_attn_sink, v_ssm_lam_re, v_ssm_lam_im, v_ssm_log_dt, v_ssm_b_re, v_ssm_b_im, v_ssm_c_re, v_ssm_c_im, v_ssm_d, v_w_glu, v_b_glu, v_w_out, v_w_mlp_in, v_w_mlp_out):
    given = dict(locals())
    weights = {n: given[n] for n in WEIGHTS}
    moms = {n: given['m_' + n] for n in WEIGHTS}
    vars_ = {n: given['v_' + n] for n in WEIGHTS}
    b, l, d_model = x.shape
    lc = ctx.shape[1]
    dims = (b, l, lc)
    depth = w_ada.shape[0]

    core = lax.axis_index('c')
    me = 4 * lax.axis_index('x') + 2 * lax.axis_index('y') + core

    blocks = _big_rows(weights)
    big_all = _all_gather(_stack_big(weights, MXU_DTYPE))
    gathered = {n: [None] * depth for n in BIG}
    off = 0
    for layer, n, rows in blocks:
        gathered[n][layer] = big_all[:, off:off + rows].reshape(N_DEV * rows, -1)
        off += rows
    small_shapes = [weights[n].shape for n in SMALL]
    small_all = _all_gather(_pack([weights[n] for n in SMALL], F32, PACK_ROWS))
    small_full = {n: _from_pieces(pc, SHARD_AXIS[n])
                  for n, pc in zip(SMALL, _unpack(small_all, small_shapes, (N_DEV,)))}

    diff = {'x': x, 'c': c, 'ctx': ctx}
    for n in REPLICATED:
        diff[n] = weights[n]
    for n in SMALL:
        diff[n] = small_full[n]
    for n in BIG:
        diff[n] = [jnp.zeros(gathered[n][layer].shape, F32) for layer in range(depth)]
    y, vjp = jax.vjp(lambda dd: _forward(dd, gathered, dims, depth), diff)
    sq, dy = _loss_call(y, loss_target.reshape(b * l, d_model))
    loss = lax.psum(0.5 * jnp.sum(sq) / d_model, ('x', 'y', 'c'))
    grads = vjp(dy)[0]
    grad_x = grads['x']

    g8 = jnp.concatenate([grads[n][layer].reshape(N_DEV, rows, -1) for layer, n, rows in blocks], axis=1)
    chip_sums = _rs_add(g8, _rs_sibling(g8), core.reshape(1).astype(jnp.int32))
    g_big = _sum_slots(_rs_chips(chip_sums))
    big_grad, off = {}, 0
    for n in BIG:
        rows = sum(r for _, name, r in blocks if name == n)
        blk = g_big[off:off + rows].reshape(depth, rows // depth, -1)
        big_grad[n] = jnp.swapaxes(blk, 1, 2) if BIG_TRANSPOSED[n] else blk
        off += rows
    rest = SMALL + REPLICATED
    g_rest = _sum_slots(_all_gather(_pack([grads[n] for n in rest], F32, PACK_ROWS)))
    rest_grad = dict(zip(rest, _unpack(g_rest, [grads[n].shape for n in rest])))
    for n in SMALL:
        width = weights[n].shape[SHARD_AXIS[n]]
        rest_grad[n] = lax.dynamic_slice_in_dim(rest_grad[n], me * width, width, axis=SHARD_AXIS[n])

    out = {}
    all_grads = {**big_grad, **rest_grad}
    for n in WEIGHTS:
        out[('grad', n)] = all_grads[n]
        for kind, arr in zip(('delta', 'new_m', 'new_v'), _adamw_any(weights[n], all_grads[n], moms[n], vars_[n])):
            out[(kind, n)] = arr
    return (loss, grad_x, *[out[(kind, n)] for kind in ('grad', 'delta', 'new_m', 'new_v') for n in WEIGHTS])
```

```python
import functools

import jax
import jax.numpy as jnp
from jax import lax
from jax.experimental import pallas as pl
from jax.experimental.pallas import tpu as pltpu

F32 = jnp.float32
MXU_DTYPE = jnp.bfloat16
SDS = jax.ShapeDtypeStruct

N_DEV = 8
HEAD_DIM = 64
N_Q_HEADS = 8
Q_PER_KV = 4
ATTN_WIDTH = 512
KV_WIDTH = 128
WINDOW = 128
Q_BLOCK = 128
GRID_W = 64
ROPE_BASE = 10000.0
ROPE_PAIRS = 16
CONV_WIDTH = 256
SSM_WIDTH = 256
SSM_GROUP = 16
SSM_GROUPS = 16
SSM_STATE = 64
SSM_N = SSM_GROUPS * SSM_STATE
IN_WIDTH = 1792
N_MOD = 6
EPS = 1e-6
NEG_INF = -1e30
ADAM_LR = 0.001
ADAM_B1 = 0.9
ADAM_B2 = 0.999
ADAM_EPS = 1e-08
ADAM_WD = 0.01
ADAM_STEP = 10
MOD_ROWS = 128
ROW_TILE = 256
PACK_ROWS = 256
LANES = 128
MESH = pl.DeviceIdType.MESH
_SCALE = HEAD_DIM ** -0.5
MIX_SPLIT = ATTN_WIDTH + CONV_WIDTH
ATTN_STACK_ROWS = 256
MM_TILE = 1024
MM_VMEM_BYTES = 56 * 1024 * 1024

WEIGHTS = ['c_ctx', 'w_ada', 'b_ada', 'norm_g', 'w_in', 'conv_w', 'attn_sink', 'ssm_lam_re', 'ssm_lam_im',
           'ssm_log_dt', 'ssm_b_re', 'ssm_b_im', 'ssm_c_re', 'ssm_c_im', 'ssm_d', 'w_glu', 'b_glu', 'w_out',
           'w_mlp_in', 'w_mlp_out']
SHARD_AXIS = {'w_ada': 2, 'norm_g': 2, 'w_in': 2, 'conv_w': 2, 'w_glu': 1, 'w_out': 1, 'w_mlp_in': 2, 'w_mlp_out': 1}
BIG = ['w_in', 'w_out', 'w_mlp_in', 'w_mlp_out']
SMALL = ['norm_g', 'conv_w', 'w_glu']
LOCAL = ['w_ada']
REPLICATED = [n for n in WEIGHTS if n not in SHARD_AXIS]


def _pick(n, cands):
    for c in cands:
        if n % c == 0:
            return c
    return n


def _div_tile(n, cap):
    if n <= cap:
        return n
    for c in range(cap, LANES - 1, -LANES):
        if n % c == 0:
            return c
    return n


def _mx(x):
    return x.astype(MXU_DTYPE)


def _dg_nt(a, b):
    return lax.dot_general(a, b, (((1,), (1,)), ((), ())), preferred_element_type=F32)


def _dg_tn(a, b):
    return lax.dot_general(a, b, (((0,), (0,)), ((), ())), preferred_element_type=F32)


def _all_gather(x2d):
    r, cdim = x2d.shape

    def body(x_ref, out_ref, send_sems, recv_sems, local_sem):
        x, y, c = lax.axis_index("x"), lax.axis_index("y"), lax.axis_index("c")
        me, sibling = (x, y, c), (x, y, 1 - c)
        chips = [(1 - x, y), (x, 1 - y), (1 - x, 1 - y)]

        def slot(px, py, pc):
            return out_ref.at[4 * px + 2 * py + pc]

        def copy(k, block, to, src=None):
            return pltpu.make_async_remote_copy(
                src_ref=slot(*block) if src is None else src, dst_ref=slot(*block),
                send_sem=send_sems.at[k], recv_sem=recv_sems.at[k], device_id=to, device_id_type=MESH)

        mine = pltpu.make_async_copy(x_ref, slot(*me), local_sem)
        mine.start()
        first = [copy(0, me, sibling, src=x_ref)]
        first += [copy(1 + j, me, (*chip, c), src=x_ref) for j, chip in enumerate(chips)]
        for cp in first:
            cp.start()
        passed = [copy(4 + j, (*chip, c), sibling) for j, chip in enumerate(chips)]
        for j, chip in enumerate(chips):
            copy(1 + j, (*chip, c), me).wait_recv()
            passed[j].start()
        copy(0, sibling, me).wait_recv()
        for j, chip in enumerate(chips):
            copy(4 + j, (*chip, 1 - c), me).wait_recv()
        for cp in first + passed:
            cp.wait_send()
        mine.wait()

    return pl.pallas_call(
        body, name=f"all_gather_{r}x{cdim}_{jnp.dtype(x2d.dtype).name}",
        out_shape=SDS((N_DEV, r, cdim), x2d.dtype),
        in_specs=[pl.BlockSpec(memory_space=pl.ANY)],
        out_specs=pl.BlockSpec(memory_space=pl.ANY),
        scratch_shapes=[pltpu.SemaphoreType.DMA((7,)), pltpu.SemaphoreType.DMA((7,)), pltpu.SemaphoreType.DMA],
    )(x2d)


def _rs_sibling(g8):
    _, r, cdim = g8.shape

    def body(g_ref, out_ref, send_sems, recv_sems):
        x, y, c = lax.axis_index("x"), lax.axis_index("y"), lax.axis_index("c")
        copies = [pltpu.make_async_remote_copy(
            src_ref=g_ref.at[2 * k + (1 - c)], dst_ref=out_ref.at[k], send_sem=send_sems.at[k],
            recv_sem=recv_sems.at[k], device_id=(x, y, 1 - c), device_id_type=MESH) for k in range(4)]
        for cp in copies:
            cp.start()
        for cp in copies:
            cp.wait()

    return pl.pallas_call(
        body, name=f"rs_sibling_{r}x{cdim}",
        out_shape=SDS((4, r, cdim), g8.dtype),
        in_specs=[pl.BlockSpec(memory_space=pl.ANY)],
        out_specs=pl.BlockSpec(memory_space=pl.ANY),
        scratch_shapes=[pltpu.SemaphoreType.DMA((4,)), pltpu.SemaphoreType.DMA((4,))],
    )(g8)


def _rs_add(g8, sib, c_idx):
    _, r, cdim = g8.shape
    tr = _pick(r, (256, 128, 64, 32, 16))

    def body(c_ref, g_ref, s_ref, o_ref):
        del c_ref
        o_ref[...] = (g_ref[...] + s_ref[...]).astype(o_ref.dtype)

    return pl.pallas_call(
        body, name=f"rs_add_{r}x{cdim}",
        grid_spec=pltpu.PrefetchScalarGridSpec(
            num_scalar_prefetch=1, grid=(4, r // tr),
            in_specs=[pl.BlockSpec((1, tr, cdim), lambda k, i, c: (2 * k + c[0], i, 0)),
                      pl.BlockSpec((1, tr, cdim), lambda k, i, c: (k, i, 0))],
            out_specs=pl.BlockSpec((1, tr, cdim), lambda k, i, c: (k, i, 0))),
        out_shape=SDS((4, r, cdim), MXU_DTYPE),
    )(c_idx, g8, sib)


def _rs_chips(s4):
    _, r, cdim = s4.shape

    def body(s_ref, out_ref, send_sems, recv_sems, local_sem):
        x, y, c = lax.axis_index("x"), lax.axis_index("y"), lax.axis_index("c")
        me = 2 * x + y

        def peer(j):
            px = 1 - x if j & 2 else x
            py = 1 - y if j & 1 else y
            return (px, py, c), 2 * px + py

        def copy(j, landing):
            to, to_chip = peer(j)
            return pltpu.make_async_remote_copy(
                src_ref=s_ref.at[to_chip], dst_ref=out_ref.at[to_chip if landing else me],
                send_sem=send_sems.at[j - 1], recv_sem=recv_sems.at[j - 1], device_id=to, device_id_type=MESH)

        mine = pltpu.make_async_copy(s_ref.at[me], out_ref.at[me], local_sem)
        mine.start()
        sends = [copy(j, False) for j in range(1, 4)]
        for cp in sends:
            cp.start()
        for j in range(1, 4):
            copy(j, True).wait_recv()
        for cp in sends:
            cp.wait_send()
        mine.wait()

    return pl.pallas_call(
        body, name=f"rs_chips_{r}x{cdim}",
        out_shape=SDS(s4.shape, s4.dtype),
        in_specs=[pl.BlockSpec(memory_space=pl.ANY)],
        out_specs=pl.BlockSpec(memory_space=pl.ANY),
        scratch_shapes=[pltpu.SemaphoreType.DMA((3,)), pltpu.SemaphoreType.DMA((3,)), pltpu.SemaphoreType.DMA],
    )(s4)


def _sum_slots(g3):
    n, r, cdim = g3.shape
    tr = _pick(r, (256, 128, 64, 32, 16, 8))

    def body(g_ref, o_ref):
        acc = g_ref[0].astype(F32)
        for s in range(1, n):
            acc = acc + g_ref[s].astype(F32)
        o_ref[...] = acc

    return pl.pallas_call(
        body, name=f"sum_slots_{n}x{r}x{cdim}", grid=(r // tr,),
        out_shape=SDS((r, cdim), F32),
        in_specs=[pl.BlockSpec((n, tr, cdim), lambda i: (0, i, 0))],
        out_specs=pl.BlockSpec((tr, cdim), lambda i: (i, 0)),
    )(g3)


def _rows_of(n):
    return -(-n // (8 * LANES)) * 8


def _pack(arrs, dtype, row_mult):
    parts = []
    for a in arrs:
        flat = a.reshape(-1).astype(dtype)
        parts.append(jnp.pad(flat, (0, _rows_of(flat.size) * LANES - flat.size)).reshape(-1, LANES))
    rows = sum(p.shape[0] for p in parts)
    if rows % row_mult:
        parts.append(jnp.zeros((row_mult - rows % row_mult, LANES), dtype))
    return jnp.concatenate(parts, axis=0)


def _unpack(mat, shapes, lead=()):
    out, off = [], 0
    for s in shapes:
        n = 1
        for d in s:
            n *= d
        rows = _rows_of(n)
        part = mat[..., off:off + rows, :].reshape(lead + (rows * LANES,))
        out.append(part[..., :n].reshape(lead + tuple(s)))
        off += rows
    return out


def _to_pieces(g, axis):
    s = g.shape
    g = g.reshape(s[:axis] + (N_DEV, s[axis] // N_DEV) + s[axis + 1:])
    return jnp.moveaxis(g, axis, 0)


def _from_pieces(p, axis):
    p = jnp.moveaxis(p, 0, axis)
    s = p.shape
    return p.reshape(s[:axis] + (s[axis] * s[axis + 1],) + s[axis + 2:])


def _mm_call(kind, a, b, z=None, sqrelu=False, out_dtype=F32):
    if kind == 'nn':
        (m, k), n = a.shape, b.shape[1]
    elif kind == 'nt':
        (m, k), n = a.shape, b.shape[0]
    else:
        (k, m), n = a.shape, b.shape[1]
    bm = _div_tile(m, MM_TILE)
    bn = _div_tile(n, MM_TILE)
    bk = _div_tile(k, MM_TILE if kind == 'tn' else 2 * MM_TILE)
    nk = k // bk
    if sqrelu:
        out_dtype = MXU_DTYPE
    plain = not sqrelu and z is None and out_dtype == F32
    use_acc = nk > 1 and not plain

    def body(a_ref, b_ref, *rest):
        rest = list(rest)
        z_ref = rest.pop(0) if z is not None else None
        o_ref = rest.pop(0)
        act_ref = rest.pop(0) if sqrelu else None
        acc = rest.pop(0) if use_acc else None
        kk = pl.program_id(2)
        av, bv = _mx(a_ref[...]), _mx(b_ref[...])
        if kind == 'nn':
            prod = jnp.dot(av, bv, preferred_element_type=F32)
        elif kind == 'nt':
            prod = _dg_nt(av, bv)
        else:
            prod = _dg_tn(av, bv)

        def finish(r):
            if z_ref is not None:
                r = r * (2.0 * jnp.maximum(z_ref[...].astype(F32), 0.0))
            o_ref[...] = r.astype(o_ref.dtype)
            if act_ref is not None:
                rr = jnp.maximum(r, 0.0)
                act_ref[...] = (rr * rr).astype(act_ref.dtype)

        if nk == 1:
            finish(prod)
        else:
            tgt = acc if use_acc else o_ref

            @pl.when(kk == 0)
            def _():
                tgt[...] = prod

            @pl.when(kk > 0)
            def _():
                tgt[...] += prod

            if use_acc:
                @pl.when(kk == nk - 1)
                def _():
                    finish(acc[...])

    if kind == 'nn':
        a_spec = pl.BlockSpec((bm, bk), lambda i, j, kk: (i, kk))
        b_spec = pl.BlockSpec((bk, bn), lambda i, j, kk: (kk, j))
    elif kind == 'nt':
        a_spec = pl.BlockSpec((bm, bk), lambda i, j, kk: (i, kk))
        b_spec = pl.BlockSpec((bn, bk), lambda i, j, kk: (j, kk))
    else:
        a_spec = pl.BlockSpec((bk, bm), lambda i, j, kk: (kk, i))
        b_spec = pl.BlockSpec((bk, bn), lambda i, j, kk: (kk, j))
    o_spec = pl.BlockSpec((bm, bn), lambda i, j, kk: (i, j))
    in_specs, args = [a_spec, b_spec], [a, b]
    if z is not None:
        in_specs.append(o_spec)
        args.append(z)
    out_shape, out_specs = [SDS((m, n), out_dtype)], [o_spec]
    if sqrelu:
        out_shape.append(SDS((m, n), MXU_DTYPE))
        out_specs.append(o_spec)
    tag = kind + ('_sq' if sqrelu else '') + ('_z' if z is not None else '')
    res = pl.pallas_call(
        body, name=f"mm_{tag}_{m}x{k}x{n}", grid=(m // bm, n // bn, nk),
        out_shape=out_shape, in_specs=in_specs, out_specs=out_specs,
        scratch_shapes=[pltpu.VMEM((bm, bn), F32)] if use_acc else [],
        compiler_params=pltpu.CompilerParams(dimension_semantics=("parallel", "parallel", "arbitrary"),
                                             vmem_limit_bytes=MM_VMEM_BYTES),
    )(*args)
    return res if sqrelu else res[0]


@jax.custom_vjp
def linear(a, w):
    return _mm_call('nn', a, w)


def _linear_fwd(a, w):
    return _mm_call('nn', a, w), (a, w)


def _linear_bwd(res, g):
    a, w = res
    return _mm_call('nt', g, w), _mm_call('tn', a, g)


linear.defvjp(_linear_fwd, _linear_bwd)


@jax.custom_vjp
def linear_t(a, wt, wtd):
    return _mm_call('nt', a, wt)


def _linear_t_fwd(a, wt, wtd):
    return _mm_call('nt', a, wt), (a, wt)


def _linear_t_bwd(res, g):
    a, wt = res
    return _mm_call('nn', g, wt), jnp.zeros_like(wt), _mm_call('tn', g, a)


linear_t.defvjp(_linear_t_fwd, _linear_t_bwd)


@jax.custom_vjp
def mix_linear(ac, s, w, wd):
    return _mm_call('nn', _mx(jnp.concatenate([ac, s], axis=1)), w)


def _mix_linear_fwd(ac, s, w, wd):
    mix = _mx(jnp.concatenate([ac, s], axis=1))
    return _mm_call('nn', mix, w), (mix, w)


def _mix_linear_bwd(res, g):
    mix, w = res
    dmix = _mm_call('nt', g, w)
    return dmix[:, :MIX_SPLIT], dmix[:, MIX_SPLIT:], jnp.zeros_like(w), _mm_call('tn', mix, g)


mix_linear.defvjp(_mix_linear_fwd, _mix_linear_bwd)


def _make_nm_ops(nm_fwd, nm_bwd):
    @jax.custom_vjp
    def nm_linear(h, sh, sc, g, wt, wtd):
        return _mm_call('nt', nm_fwd((h,), (sh, sc), (g,), MXU_DTYPE)[0], wt)

    def nm_linear_fwd(h, sh, sc, g, wt, wtd):
        a = nm_fwd((h,), (sh, sc), (g,), MXU_DTYPE)[0]
        return _mm_call('nt', a, wt), (h, sh, sc, g, a, wt)

    def nm_linear_bwd(res, gp):
        h, sh, sc, g, a, wt = res
        da = _mm_call('nn', gp, wt)
        dwt = _mm_call('tn', gp, a)
        dh, dsh, dsc, dg = nm_bwd((h,), (sh, sc), (g,), (da,))
        return dh, dsh, dsc, dg, jnp.zeros_like(wt), dwt

    nm_linear.defvjp(nm_linear_fwd, nm_linear_bwd)

    @jax.custom_vjp
    def nm_mlp(h, sh, sc, g, w1t, w1td, w2, w2d):
        a = nm_fwd((h,), (sh, sc), (g,), MXU_DTYPE)[0]
        _, act = _mm_call('nt', a, w1t, sqrelu=True)
        return _mm_call('nn', act, w2)

    def nm_mlp_fwd(h, sh, sc, g, w1t, w1td, w2, w2d):
        a = nm_fwd((h,), (sh, sc), (g,), MXU_DTYPE)[0]
        zb, act = _mm_call('nt', a, w1t, sqrelu=True)
        return _mm_call('nn', act, w2), (h, sh, sc, g, a, w1t, w2, zb, act)

    def nm_mlp_bwd(res, gf):
        h, sh, sc, g, a, w1t, w2, zb, act = res
        dz = _mm_call('nt', gf, w2, z=zb, out_dtype=MXU_DTYPE)
        dw2 = _mm_call('tn', act, gf)
        da = _mm_call('nn', dz, w1t)
        dw1t = _mm_call('tn', dz, a)
        dh, dsh, dsc, dg = nm_bwd((h,), (sh, sc), (g,), (da,))
        return dh, dsh, dsc, dg, jnp.zeros_like(w1t), dw1t, jnp.zeros_like(w2), dw2

    nm_mlp.defvjp(nm_mlp_fwd, nm_mlp_bwd)
    return nm_linear, nm_mlp


def _rowwise(name, f, out_widths, seg_len):
    def specs(rows, segs, globs, tr):
        nseg = segs[0].shape[0] if segs else 1

        def seg_of(i):
            return jnp.minimum((i * tr) // seg_len, nseg - 1)

        row_specs = [pl.BlockSpec((tr, r.shape[1]), lambda i: (i, 0)) for r in rows]
        seg_specs = [pl.BlockSpec((1, 1, s.shape[2]), lambda i: (seg_of(i), 0, 0)) for s in segs]
        glob_specs = [pl.BlockSpec(g.shape, lambda i: (0, 0)) for g in globs]
        return seg_of, row_specs, seg_specs, glob_specs

    def fwd_call(rows, segs, globs, out_dtype=F32):
        t = rows[0].shape[0]
        tr = _pick(t, (ROW_TILE, 128, 64, 32, 16, 8))
        _, row_specs, seg_specs, glob_specs = specs(rows, segs, globs, tr)
        nr, ns = len(rows), len(segs)

        def body(*refs):
            ins, outs = refs[:nr + ns + len(globs)], refs[nr + ns + len(globs):]
            vals = [r[...] for r in ins[:nr]] + [r[0] for r in ins[nr:nr + ns]] + [r[...] for r in ins[nr + ns:]]
            for o_ref, v in zip(outs, f(*vals)):
                o_ref[...] = v.astype(o_ref.dtype)

        return pl.pallas_call(
            body, name=f"{name}_fwd_{t}_{jnp.dtype(out_dtype).name}", grid=(t // tr,),
            out_shape=[SDS((t, w), out_dtype) for w in out_widths],
            in_specs=row_specs + seg_specs + glob_specs,
            out_specs=[pl.BlockSpec((tr, w), lambda i: (i, 0)) for w in out_widths],
        )(*rows, *segs, *globs)

    def bwd_call(rows, segs, globs, douts):
        t = rows[0].shape[0]
        tr = _pick(t, (ROW_TILE, 128, 64, 32, 16, 8))
        seg_of, row_specs, seg_specs, glob_specs = specs(rows, segs, globs, tr)
        nr, ns, ng, no = len(rows), len(segs), len(globs), len(out_widths)

        def body(*refs):
            ins = refs[:nr + ns + ng]
            dos = refs[nr + ns + ng:nr + ns + ng + no]
            outs = refs[nr + ns + ng + no:]
            i = pl.program_id(0)
            first_of_seg = jnp.logical_or(i == 0, seg_of(i) != seg_of(jnp.maximum(i - 1, 0)))
            vals = [r[...] for r in ins[:nr]] + [r[0] for r in ins[nr:nr + ns]] + [r[...] for r in ins[nr + ns:]]
            _, vjp = jax.vjp(f, *vals)
            grads = vjp(tuple(d[...] for d in dos))
            for o_ref, gval in zip(outs[:nr], grads[:nr]):
                o_ref[...] = gval
            for o_ref, gval in zip(outs[nr:nr + ns], grads[nr:nr + ns]):
                @pl.when(first_of_seg)
                def _(o_ref=o_ref):
                    o_ref[...] = jnp.zeros_like(o_ref)
                o_ref[0] += gval
            for o_ref, gval in zip(outs[nr + ns:], grads[nr + ns:]):
                @pl.when(i == 0)
                def _(o_ref=o_ref):
                    o_ref[...] = jnp.zeros_like(o_ref)
                o_ref[...] += gval

        do_specs = [pl.BlockSpec((tr, w), lambda i: (i, 0)) for w in out_widths]
        return pl.pallas_call(
            body, name=f"{name}_bwd_{t}", grid=(t // tr,),
            out_shape=[SDS(r.shape, F32) for r in rows] + [SDS(s.shape, F32) for s in segs]
            + [SDS(g.shape, F32) for g in globs],
            in_specs=row_specs + seg_specs + glob_specs + do_specs,
            out_specs=row_specs + seg_specs + glob_specs,
        )(*rows, *segs, *globs, *douts)

    return fwd_call, bwd_call


def _norm_mod_f(x, shift, scale, g):
    r = lax.rsqrt(jnp.mean(x * x, axis=-1, keepdims=True) + EPS)
    return ((x * r) * g * (1.0 + scale) + shift,)


def _resid_gate_f(h, m, gate, g):
    r = lax.rsqrt(jnp.mean(m * m, axis=-1, keepdims=True) + EPS)
    return (h + gate * ((m * r) * g),)


def _glu_f(y0, y1, u, d, w, b):
    y = y0 + y1 + d * u
    g = 0.5 * y * (1.0 + jnp.tanh(0.7978845608028654 * (y + 0.044715 * (y * y * y))))
    zz = jnp.dot(_mx(g), _mx(w), preferred_element_type=F32) + b
    return (g * (1.0 / (1.0 + jnp.exp(-zz))),)


def _make_rowwise_op(name, f, out_width, n_rows, n_segs, seg_len):
    fwd_call, bwd_call = _rowwise(name, f, (out_width,), seg_len)

    @jax.custom_vjp
    def op(*args):
        return fwd_call(args[:n_rows], args[n_rows:n_rows + n_segs], args[n_rows + n_segs:])[0]

    def op_fwd(*args):
        return fwd_call(args[:n_rows], args[n_rows:n_rows + n_segs], args[n_rows + n_segs:])[0], args

    def op_bwd(args, g):
        return tuple(bwd_call(args[:n_rows], args[n_rows:n_rows + n_segs], args[n_rows + n_segs:], (g,)))

    op.defvjp(op_fwd, op_bwd)
    return op


def _lane():
    return lax.broadcasted_iota(jnp.int32, (1, LANES), 1)


def _hm(off):
    lane = _lane()
    return jnp.logical_and(lane >= off, lane < off + HEAD_DIM)


def _col(tile, hq):
    return jnp.sum(jnp.where(_lane() == hq, tile, 0.0), axis=1, keepdims=True)


def _setcol(tile, hq, col):
    return jnp.where(_lane() == hq, col, tile)


def _head_fwd(qa, groups, sk):
    ss, m = [], None
    for kmat, _, mask in groups:
        s = _dg_nt(qa, kmat) * _SCALE
        if mask is not None:
            s = jnp.where(mask, s, NEG_INF)
        ss.append(s)
        mm = jnp.max(s, axis=1, keepdims=True)
        m = mm if m is None else jnp.maximum(m, mm)
    m = jnp.maximum(m, sk)
    l = jnp.exp(sk - m)
    pv = None
    for s, (_, vmat, _) in zip(ss, groups):
        p = jnp.exp(s - m)
        l = l + jnp.sum(p, axis=1, keepdims=True)
        t = jnp.dot(_mx(p), vmat, preferred_element_type=F32)
        pv = t if pv is None else pv + t
    return pv / l, m + jnp.log(l)


def _head_bwd(qa, doa, groups, sk, lse_h, delta_h):
    dq, outs = None, []
    for kmat, vmat, mask in groups:
        s = _dg_nt(qa, kmat) * _SCALE
        if mask is not None:
            s = jnp.where(mask, s, NEG_INF)
        p = jnp.exp(s - lse_h)
        dp = _dg_nt(doa, vmat)
        ds = p * (dp - delta_h) * _SCALE
        t = jnp.dot(_mx(ds), kmat, preferred_element_type=F32)
        dq = t if dq is None else dq + t
        outs.append((_mx(ds), _mx(p)))
    return dq, outs, jnp.exp(sk - lse_h)


def _kv_group(h):
    return [(g, Q_PER_KV * h + g, (Q_PER_KV * h + g) // 2, (g % 2) != h) for g in range(Q_PER_KV)]


def _to_kv_lanes(pair_tile, roll):
    if not roll:
        return _mx(pair_tile)
    return _mx(pltpu.roll(pair_tile.astype(F32), HEAD_DIM, 1))


def _attn_stack(n):
    return max(1, min(Q_PER_KV, ATTN_STACK_ROWS // n))


def _kv_masked(x, h):
    return _mx(jnp.where(_hm(h * HEAD_DIM), x.astype(F32), 0.0))


def _attn_fwd_block(sink_ref, q_ref, kv, n, o_ref, lse_ref):
    lse_t = jnp.zeros((n, LANES), F32)
    stack = _attn_stack(n)
    for h in range(2):
        groups = [(_kv_masked(k, h), _kv_masked(v, h), mask) for k, v, mask in kv]
        heads = _kv_group(h)
        o_pair = None
        for first in range(0, Q_PER_KV, stack):
            part = heads[first:first + stack]
            q = jnp.concatenate([_to_kv_lanes(q_ref[:, pair * LANES:(pair + 1) * LANES], roll)
                                 for _, _, pair, roll in part], axis=0)
            sk = jnp.concatenate([jnp.full((n, 1), sink_ref[0, hq], F32) for _, hq, _, _ in part], axis=0)
            o, lse = _head_fwd(q, groups, sk)
            for idx, (g, hq, pair, roll) in enumerate(part):
                og = o[idx * n:(idx + 1) * n]
                og = pltpu.roll(og, HEAD_DIM, 1) if roll else og
                lse_t = _setcol(lse_t, hq, lse[idx * n:(idx + 1) * n])
                if g % 2 == 0:
                    o_pair = og
                else:
                    o_ref[:, pair * LANES:(pair + 1) * LANES] = o_pair + og
    lse_ref[...] = lse_t


def _attn_bwd_block(sink_ref, q_ref, do_ref, o_ref, lse_t, kv, want, n, dq_ref):
    delta_t = jnp.zeros((n, LANES), F32)
    dsk = jnp.zeros((1, LANES), F32)
    dkv = [None] * len(kv)
    stack = _attn_stack(n)
    for h in range(2):
        hm = _hm(h * HEAD_DIM)
        groups = [(_kv_masked(k, h), _kv_masked(v, h), mask) for k, v, mask in kv]
        heads = _kv_group(h)
        dq_pair = None
        for first in range(0, Q_PER_KV, stack):
            part = heads[first:first + stack]
            qs, dos, sks, lses, deltas = [], [], [], [], []
            for g, hq, pair, roll in part:
                sl = slice(pair * LANES, (pair + 1) * LANES)
                do_p = do_ref[:, sl]
                delta_h = jnp.sum(jnp.where(_hm((hq % 2) * HEAD_DIM), do_p * o_ref[:, sl], 0.0), axis=1,
                                  keepdims=True)
                delta_t = _setcol(delta_t, hq, delta_h)
                qs.append(_to_kv_lanes(q_ref[:, sl], roll))
                dos.append(_to_kv_lanes(do_p, roll))
                sks.append(jnp.full((n, 1), sink_ref[0, hq], F32))
                lses.append(_col(lse_t, hq))
                deltas.append(delta_h)
            q, do = jnp.concatenate(qs, axis=0), jnp.concatenate(dos, axis=0)
            dq, outs, p_s = _head_bwd(q, do, groups, jnp.concatenate(sks, axis=0), jnp.concatenate(lses, axis=0),
                                      jnp.concatenate(deltas, axis=0))
            for idx, (g, hq, pair, roll) in enumerate(part):
                dg = dq[idx * n:(idx + 1) * n]
                dg = pltpu.roll(dg, HEAD_DIM, 1) if roll else dg
                dsk = dsk + jnp.where(_lane() == hq,
                                      jnp.sum(-p_s[idx * n:(idx + 1) * n] * deltas[idx], axis=0, keepdims=True), 0.0)
                if g % 2 == 0:
                    dq_pair = dg
                else:
                    dq_ref[:, pair * LANES:(pair + 1) * LANES] = dq_pair + dg
            for gi, (ds, pp) in enumerate(outs):
                if want[gi]:
                    dk_h = jnp.where(hm, _dg_tn(ds, q), 0.0)
                    dv_h = jnp.where(hm, _dg_tn(pp, do), 0.0)
                    dkv[gi] = (dk_h, dv_h) if dkv[gi] is None else (dkv[gi][0] + dk_h, dkv[gi][1] + dv_h)
    return delta_t, dsk, dkv


def _partner(t):
    lane = lax.broadcasted_iota(jnp.int32, t.shape, 1)
    return jnp.where((lane & ROPE_PAIRS) == 0, pltpu.roll(t, LANES - ROPE_PAIRS, 1), pltpu.roll(t, ROPE_PAIRS, 1))


def _rope_tables(n_tokens):
    rows = n_tokens // GRID_W
    row = jnp.broadcast_to(jnp.arange(rows)[:, None], (rows, GRID_W)).reshape(-1)
    col = jnp.broadcast_to(jnp.arange(GRID_W)[None, :], (rows, GRID_W)).reshape(-1)
    freqs = ROPE_BASE ** (-jnp.arange(ROPE_PAIRS, dtype=F32) / ROPE_PAIRS)
    ang = jnp.concatenate([row[:, None].astype(F32) * freqs, col[:, None].astype(F32) * freqs], axis=-1)
    c, s = jnp.cos(ang), jnp.sin(ang)
    n = ROPE_PAIRS
    cos64 = jnp.concatenate([c[:, :n], c[:, :n], c[:, n:], c[:, n:]], axis=1)
    sin64 = jnp.concatenate([-s[:, :n], s[:, :n], -s[:, n:], s[:, n:]], axis=1)
    return jnp.tile(cos64, (1, 2)), jnp.tile(sin64, (1, 2))


def _rope_call(q_src, k_src, cos, sin, dims, inverse):
    b, l, _ = dims
    t_lat = b * l
    tr = _pick(l, (ROW_TILE, 128))
    per = l // tr
    out_dtype = F32 if inverse else MXU_DTYPE

    def body(q_ref, k_ref, c_ref, s_ref, qo_ref, ko_ref):
        c, s = c_ref[...], s_ref[...]

        def rot(t):
            if inverse:
                return t * c + _partner(t * s)
            return t * c + _partner(t) * s

        for j in range(ATTN_WIDTH // LANES):
            qo_ref[:, j * LANES:(j + 1) * LANES] = rot(q_ref[:, j * LANES:(j + 1) * LANES]).astype(out_dtype)
        ko_ref[...] = rot(k_ref[...]).astype(out_dtype)

    kcol = 0 if inverse else ATTN_WIDTH // KV_WIDTH
    return pl.pallas_call(
        body, name=f"rope_{'inv' if inverse else 'fwd'}_{t_lat}", grid=(t_lat // tr,),
        out_shape=[SDS((t_lat, ATTN_WIDTH), out_dtype), SDS((t_lat, KV_WIDTH), out_dtype)],
        in_specs=[pl.BlockSpec((tr, ATTN_WIDTH), lambda i: (i, 0)),
                  pl.BlockSpec((tr, KV_WIDTH), lambda i: (i, kcol)),
                  pl.BlockSpec((tr, LANES), lambda i: (i % per, 0)),
                  pl.BlockSpec((tr, LANES), lambda i: (i % per, 0))],
        out_specs=[pl.BlockSpec((tr, ATTN_WIDTH), lambda i: (i, 0)), pl.BlockSpec((tr, KV_WIDTH), lambda i: (i, 0))],
    )(q_src, k_src, cos, sin)


_SMEM_SPEC = pl.BlockSpec(memory_space=pltpu.SMEM)
_KCOL = ATTN_WIDTH // KV_WIDTH
_VCOL = _KCOL + 1


def _win_specs(arr_cols, nb, col):
    del arr_cols
    return [pl.BlockSpec((Q_BLOCK, KV_WIDTH), lambda b, i, d=d: (b * nb + jnp.clip(i + d, 0, nb - 1), col))
            for d in (-1, 0, 1)]


def _win_mask(i, l):
    shape = (_attn_stack(Q_BLOCK) * Q_BLOCK, 3 * Q_BLOCK)
    qpos = i * Q_BLOCK + (lax.broadcasted_iota(jnp.int32, shape, 0) & (Q_BLOCK - 1))
    kpos = (i - 1) * Q_BLOCK + lax.broadcasted_iota(jnp.int32, shape, 1)
    return jnp.logical_and(jnp.abs(qpos - kpos) <= WINDOW, jnp.logical_and(kpos >= 0, kpos < l))


def _attn_lat_fwd(qr, kr, p, sink, dims):
    b, l, lc = dims
    nb = l // Q_BLOCK
    t_lat = b * l
    cbase = t_lat // lc

    def body(sink_ref, q_ref, kp, kc, kn, vp, vc, vn, ck_ref, cv_ref, o_ref, lse_ref):
        kwin = jnp.concatenate([kp[...], kc[...], kn[...]], axis=0)
        vwin = jnp.concatenate([vp[...], vc[...], vn[...]], axis=0)
        kv = [(kwin, vwin, _win_mask(pl.program_id(1), l)), (ck_ref[...], cv_ref[...], None)]
        _attn_fwd_block(sink_ref, q_ref, kv, Q_BLOCK, o_ref, lse_ref)

    return pl.pallas_call(
        body, name=f"attn_lat_fwd_{t_lat}", grid=(b, nb),
        out_shape=[SDS((t_lat, ATTN_WIDTH), F32), SDS((t_lat, LANES), F32)],
        in_specs=[_SMEM_SPEC, pl.BlockSpec((Q_BLOCK, ATTN_WIDTH), lambda bb, i: (bb * nb + i, 0))]
        + _win_specs(None, nb, 0) + _win_specs(None, nb, _VCOL)
        + [pl.BlockSpec((lc, KV_WIDTH), lambda bb, i: (cbase + bb, _KCOL)),
           pl.BlockSpec((lc, KV_WIDTH), lambda bb, i: (cbase + bb, _VCOL))],
        out_specs=[pl.BlockSpec((Q_BLOCK, ATTN_WIDTH), lambda bb, i: (bb * nb + i, 0)),
                   pl.BlockSpec((Q_BLOCK, LANES), lambda bb, i: (bb * nb + i, 0))],
    )(sink, qr, kr, kr, kr, p, p, p, p, p)


def _attn_lat_bwd_dq(qr, kr, p, sink, o, lse, dout, dims):
    b, l, lc = dims
    nb = l // Q_BLOCK
    t_lat = b * l
    cbase = t_lat // lc

    def body(sink_ref, q_ref, kp, kc, kn, vp, vc, vn, ck_ref, cv_ref, o_ref, lse_ref, do_ref,
             dq_ref, delta_ref, dkc_ref, dvc_ref, dsk_ref):
        bb, i = pl.program_id(0), pl.program_id(1)

        @pl.when(i == 0)
        def _():
            dkc_ref[...] = jnp.zeros_like(dkc_ref)
            dvc_ref[...] = jnp.zeros_like(dvc_ref)

        @pl.when(jnp.logical_and(bb == 0, i == 0))
        def _():
            dsk_ref[...] = jnp.zeros_like(dsk_ref)

        kwin = jnp.concatenate([kp[...], kc[...], kn[...]], axis=0)
        vwin = jnp.concatenate([vp[...], vc[...], vn[...]], axis=0)
        kv = [(kwin, vwin, _win_mask(i, l)), (ck_ref[...], cv_ref[...], None)]
        delta_t, dsk, dkv = _attn_bwd_block(sink_ref, q_ref, do_ref, o_ref, lse_ref[...], kv, [False, True], Q_BLOCK,
                                            dq_ref)
        delta_ref[...] = delta_t
        dkc_ref[...] += dkv[1][0]
        dvc_ref[...] += dkv[1][1]
        dsk_ref[0:1, :] += dsk

    qspec = pl.BlockSpec((Q_BLOCK, ATTN_WIDTH), lambda bb, i: (bb * nb + i, 0))
    tspec = pl.BlockSpec((Q_BLOCK, LANES), lambda bb, i: (bb * nb + i, 0))
    cspec = pl.BlockSpec((lc, KV_WIDTH), lambda bb, i: (bb, 0))
    return pl.pallas_call(
        body, name=f"attn_lat_bwd_dq_{t_lat}", grid=(b, nb),
        out_shape=[SDS((t_lat, ATTN_WIDTH), F32), SDS((t_lat, LANES), F32), SDS((b * lc, KV_WIDTH), F32),
                   SDS((b * lc, KV_WIDTH), F32), SDS((8, LANES), F32)],
        in_specs=[_SMEM_SPEC, qspec] + _win_specs(None, nb, 0) + _win_specs(None, nb, _VCOL)
        + [pl.BlockSpec((lc, KV_WIDTH), lambda bb, i: (cbase + bb, _KCOL)),
           pl.BlockSpec((lc, KV_WIDTH), lambda bb, i: (cbase + bb, _VCOL)), qspec, tspec, qspec],
        out_specs=[qspec, tspec, cspec, cspec, pl.BlockSpec((8, LANES), lambda bb, i: (0, 0))],
    )(sink, qr, kr, kr, kr, p, p, p, p, p, o, lse, dout)


def _attn_lat_bwd_dkv(qr, kr, p, lse, delta, dout, dims):
    b, l, _ = dims
    nb = l // Q_BLOCK
    t_lat = b * l

    def body(k_ref, v_ref, *refs):
        j = pl.program_id(1)
        kj, vj = k_ref[...], v_ref[...]
        n_q = 3 * Q_PER_KV * Q_BLOCK
        sub = lax.broadcasted_iota(jnp.int32, (8, LANES), 0)
        col = lax.broadcasted_iota(jnp.int32, (Q_BLOCK, n_q), 1)
        i_of = j + col // (Q_PER_KV * Q_BLOCK) - 1
        qpos = i_of * Q_BLOCK + (col & (Q_BLOCK - 1))
        kpos = j * Q_BLOCK + lax.broadcasted_iota(jnp.int32, (Q_BLOCK, n_q), 0)
        mask = jnp.logical_and(jnp.abs(qpos - kpos) <= WINDOW, jnp.logical_and(i_of >= 0, i_of < nb))
        lse_rows = [refs[4 * n + 2][...].T[0:8, :] for n in range(3)]
        delta_rows = [refs[4 * n + 3][...].T[0:8, :] for n in range(3)]
        dk = jnp.zeros((Q_BLOCK, KV_WIDTH), F32)
        dv = jnp.zeros((Q_BLOCK, KV_WIDTH), F32)
        for h in range(2):
            hm = _hm(h * HEAD_DIM)
            kh, vh = _kv_masked(kj, h), _kv_masked(vj, h)
            qs, dos, lrow, drow = [], [], [], []
            for n in range(3):
                q_ref, do_ref = refs[4 * n], refs[4 * n + 1]
                for _, hq, pair, roll in _kv_group(h):
                    sl = slice(pair * LANES, (pair + 1) * LANES)
                    qs.append(_to_kv_lanes(q_ref[:, sl], roll))
                    dos.append(_to_kv_lanes(do_ref[:, sl], roll))
                    lrow.append(jnp.sum(jnp.where(sub == hq, lse_rows[n], 0.0), axis=0, keepdims=True))
                    drow.append(jnp.sum(jnp.where(sub == hq, delta_rows[n], 0.0), axis=0, keepdims=True))
            q, do = jnp.concatenate(qs, axis=0), jnp.concatenate(dos, axis=0)
            lse_r, delta_r = jnp.concatenate(lrow, axis=1), jnp.concatenate(drow, axis=1)
            s_t = jnp.where(mask, _dg_nt(kh, q) * _SCALE, NEG_INF)
            p_t = jnp.exp(s_t - lse_r)
            ds_t = p_t * (_dg_nt(vh, do) - delta_r) * _SCALE
            dv = dv + jnp.where(hm, jnp.dot(_mx(p_t), do, preferred_element_type=F32), 0.0)
            dk = dk + jnp.where(hm, jnp.dot(_mx(ds_t), q, preferred_element_type=F32), 0.0)
        dk_ref, dv_ref = refs[12], refs[13]
        dk_ref[...] = dk
        dv_ref[...] = dv

    def blk(width, d, col=0):
        return pl.BlockSpec((Q_BLOCK, width), lambda bb, j: (bb * nb + jnp.clip(j + d, 0, nb - 1), col))

    in_specs = [blk(KV_WIDTH, 0), blk(KV_WIDTH, 0, _VCOL)]
    args = [kr, p]
    for d in (-1, 0, 1):
        in_specs += [blk(ATTN_WIDTH, d), blk(ATTN_WIDTH, d), blk(LANES, d), blk(LANES, d)]
        args += [qr, dout, lse, delta]
    return pl.pallas_call(
        body, name=f"attn_lat_bwd_dkv_{t_lat}", grid=(b, nb),
        out_shape=[SDS((t_lat, KV_WIDTH), F32), SDS((t_lat, KV_WIDTH), F32)],
        in_specs=in_specs, out_specs=[blk(KV_WIDTH, 0), blk(KV_WIDTH, 0)],
    )(*args)


def _attn_ctx_fwd(p, sink, dims):
    b, l, lc = dims
    cbase = b * l // lc

    def body(sink_ref, q_ref, k_ref, v_ref, o_ref, lse_ref):
        _attn_fwd_block(sink_ref, q_ref, [(k_ref[...], v_ref[...], None)], lc, o_ref, lse_ref)

    return pl.pallas_call(
        body, name=f"attn_ctx_fwd_{b * lc}", grid=(b,),
        out_shape=[SDS((b * lc, ATTN_WIDTH), F32), SDS((b * lc, LANES), F32)],
        in_specs=[_SMEM_SPEC, pl.BlockSpec((lc, ATTN_WIDTH), lambda bb: (cbase + bb, 0)),
                  pl.BlockSpec((lc, KV_WIDTH), lambda bb: (cbase + bb, _KCOL)),
                  pl.BlockSpec((lc, KV_WIDTH), lambda bb: (cbase + bb, _VCOL))],
        out_specs=[pl.BlockSpec((lc, ATTN_WIDTH), lambda bb: (bb, 0)), pl.BlockSpec((lc, LANES), lambda bb: (bb, 0))],
    )(sink, p, p, p)


def _attn_ctx_bwd(p, sink, o, lse, dout, dims):
    b, l, lc = dims
    cbase = b * l // lc

    def body(sink_ref, q_ref, k_ref, v_ref, o_ref, lse_ref, do_ref, dq_ref, dk_ref, dv_ref, dsk_ref):
        bb = pl.program_id(0)

        @pl.when(bb == 0)
        def _():
            dsk_ref[...] = jnp.zeros_like(dsk_ref)

        _, dsk, dkv = _attn_bwd_block(sink_ref, q_ref, do_ref, o_ref, lse_ref[...], [(k_ref[...], v_ref[...], None)],
                                      [True], lc, dq_ref)
        dk_ref[...] = dkv[0][0]
        dv_ref[...] = dkv[0][1]
        dsk_ref[0:1, :] += dsk

    qs = pl.BlockSpec((lc, ATTN_WIDTH), lambda bb: (bb, 0))
    ks = pl.BlockSpec((lc, KV_WIDTH), lambda bb: (bb, 0))
    return pl.pallas_call(
        body, name=f"attn_ctx_bwd_{b * lc}", grid=(b,),
        out_shape=[SDS((b * lc, ATTN_WIDTH), F32), SDS((b * lc, KV_WIDTH), F32), SDS((b * lc, KV_WIDTH), F32),
                   SDS((8, LANES), F32)],
        in_specs=[_SMEM_SPEC, pl.BlockSpec((lc, ATTN_WIDTH), lambda bb: (cbase + bb, 0)),
                  pl.BlockSpec((lc, KV_WIDTH), lambda bb: (cbase + bb, _KCOL)),
                  pl.BlockSpec((lc, KV_WIDTH), lambda bb: (cbase + bb, _VCOL)),
                  qs, pl.BlockSpec((lc, LANES), lambda bb: (bb, 0)),
                  pl.BlockSpec((lc, ATTN_WIDTH), lambda bb: (cbase + bb, 0))],
        out_specs=[qs, ks, ks, pl.BlockSpec((8, LANES), lambda bb: (0, 0))],
    )(sink, p, p, p, o, lse, dout)


_CONV_COL = (ATTN_WIDTH + 2 * KV_WIDTH) // CONV_WIDTH


def _shift_prev(z, n):
    rows = lax.broadcasted_iota(jnp.int32, z.shape, 0)
    return jnp.where(rows == 0, 0.0, pltpu.roll(z, 1, 0))


def _shift_next(z, n):
    rows = lax.broadcasted_iota(jnp.int32, z.shape, 0)
    return jnp.where(rows == n - 1, 0.0, pltpu.roll(z, n - 1, 0))


def _conv_fwd(p, w, base, n_seq, ls):
    def body(cb_ref, cc_ref, cx_ref, w_ref, o_ref):
        z = cc_ref[...] * cx_ref[...]
        c3 = _shift_prev(z, ls) * w_ref[0:1, :] + z * w_ref[1:2, :] + _shift_next(z, ls) * w_ref[2:3, :]
        o_ref[...] = cb_ref[...] * c3

    return pl.pallas_call(
        body, name=f"conv_fwd_{n_seq}x{ls}", grid=(n_seq,),
        out_shape=SDS((n_seq * ls, CONV_WIDTH), F32),
        in_specs=[pl.BlockSpec((ls, CONV_WIDTH), lambda s, c=c: (base + s, _CONV_COL + c)) for c in range(3)]
        + [pl.BlockSpec(w.shape, lambda s: (0, 0))],
        out_specs=pl.BlockSpec((ls, CONV_WIDTH), lambda s: (s, 0)),
    )(p, p, p, w)


def _conv_bwd(p, w, dout, base, n_seq, ls):
    dcol = ATTN_WIDTH // CONV_WIDTH

    def body(cb_ref, cc_ref, cx_ref, w_ref, do_ref, dcb_ref, dcc_ref, dcx_ref, dw_ref):
        @pl.when(pl.program_id(0) == 0)
        def _():
            dw_ref[...] = jnp.zeros_like(dw_ref)

        cc, cx = cc_ref[...], cx_ref[...]
        z = cc * cx
        zp, zn = _shift_prev(z, ls), _shift_next(z, ls)
        c3 = zp * w_ref[0:1, :] + z * w_ref[1:2, :] + zn * w_ref[2:3, :]
        do = do_ref[...]
        dcb_ref[...] = do * c3
        e = do * cb_ref[...]
        dz = _shift_next(e, ls) * w_ref[0:1, :] + e * w_ref[1:2, :] + _shift_prev(e, ls) * w_ref[2:3, :]
        dcc_ref[...] = dz * cx
        dcx_ref[...] = dz * cc
        dw_ref[0:1, :] += jnp.sum(e * zp, axis=0, keepdims=True)
        dw_ref[1:2, :] += jnp.sum(e * z, axis=0, keepdims=True)
        dw_ref[2:3, :] += jnp.sum(e * zn, axis=0, keepdims=True)

    ospec = pl.BlockSpec((ls, CONV_WIDTH), lambda s: (s, 0))
    return pl.pallas_call(
        body, name=f"conv_bwd_{n_seq}x{ls}", grid=(n_seq,),
        out_shape=[SDS((n_seq * ls, CONV_WIDTH), F32)] * 3 + [SDS((8, CONV_WIDTH), F32)],
        in_specs=[pl.BlockSpec((ls, CONV_WIDTH), lambda s, c=c: (base + s, _CONV_COL + c)) for c in range(3)]
        + [pl.BlockSpec(w.shape, lambda s: (0, 0)), pl.BlockSpec((ls, CONV_WIDTH), lambda s: (base + s, dcol))],
        out_specs=[ospec, ospec, ospec, pl.BlockSpec((8, CONV_WIDTH), lambda s: (0, 0))],
    )(p, p, p, w, dout)


def _make_att_conv(dims, cos, sin):
    b, l, lc = dims
    t_lat = b * l

    def forward(p, conv_w, sink):
        qr, kr = _rope_call(p, p, cos, sin, dims, inverse=False)
        o_lat, lse_lat = _attn_lat_fwd(qr, kr, p, sink, dims)
        o_ctx, lse_ctx = _attn_ctx_fwd(p, sink, dims)
        conv_lat = _conv_fwd(p, conv_w, 0, b, l)
        conv_ctx = _conv_fwd(p, conv_w, t_lat // lc, b, lc)
        out = jnp.concatenate([jnp.concatenate([o_lat, conv_lat], axis=1),
                               jnp.concatenate([o_ctx, conv_ctx], axis=1)], axis=0)
        return out, (p, conv_w, sink, qr, kr, o_lat, lse_lat, o_ctx, lse_ctx)

    @jax.custom_vjp
    def op(p, conv_w, sink):
        return forward(p, conv_w, sink)[0]

    def op_bwd(res, dout):
        p, conv_w, sink, qr, kr, o_lat, lse_lat, o_ctx, lse_ctx = res
        dqr, delta, dkc1, dvc1, dsk1 = _attn_lat_bwd_dq(qr, kr, p, sink, o_lat, lse_lat, dout, dims)
        dkr, dv = _attn_lat_bwd_dkv(qr, kr, p, lse_lat, delta, dout, dims)
        dq, dk = _rope_call(dqr, dkr, cos, sin, dims, inverse=True)
        dqc, dkc2, dvc2, dsk2 = _attn_ctx_bwd(p, sink, o_ctx, lse_ctx, dout, dims)
        dcb_l, dcc_l, dcx_l, dw_l = _conv_bwd(p, conv_w, dout, 0, b, l)
        dcb_c, dcc_c, dcx_c, dw_c = _conv_bwd(p, conv_w, dout, t_lat // lc, b, lc)
        zeros_u = jnp.zeros((p.shape[0], SSM_WIDTH), F32)
        lat = jnp.concatenate([dq, dk, dv, dcb_l, dcc_l, dcx_l], axis=1)
        ctx = jnp.concatenate([dqc, dkc1 + dkc2, dvc1 + dvc2, dcb_c, dcc_c, dcx_c], axis=1)
        dp = jnp.concatenate([jnp.concatenate([lat, ctx], axis=0), zeros_u], axis=1)
        dsink = (dsk1 + dsk2)[0:1, :N_Q_HEADS]
        return dp, (dw_l + dw_c)[:3], dsink

    op.defvjp(forward, op_bwd)
    return op


def _scan_call(x, lam_r, lam_i, dims, ctx_first, dec, h=None):
    b, l, lc = dims
    n = SSM_N
    tc = _pick(lc, (256, 128, 64, 32, 16, 8))
    nc_c, nc_l = lc // tc, l // tc
    n_chunks = nc_c + nc_l
    ctx_base = b * l // tc
    g8 = tc // 8
    with_h = h is not None

    def chunk(bb, j):
        if ctx_first:
            is_ctx = j < nc_c
            jj = jnp.where(is_ctx, j, j - nc_c)
        else:
            is_ctx = j >= nc_l
            jj = jnp.where(is_ctx, j - nc_l, j)
        ic = nc_c - 1 - jj if dec else jj
        il = nc_l - 1 - jj if dec else jj
        return jnp.where(is_ctx, ctx_base + bb * nc_c + ic, bb * nc_l + il)

    def edge(bb, j):
        jn = jnp.minimum(j + 1, n_chunks - 1)
        return chunk(bb, jn) * g8 + (g8 - 1 if dec else 0)

    def body(x_ref, lr_ref, li_ref, *rest):
        rest = list(rest)
        h_ref, hb_ref = (rest.pop(0), rest.pop(0)) if with_h else (None, None)
        o_ref = rest.pop(0)
        dl_ref = rest.pop(0) if with_h else None
        carry = rest.pop(0)
        bb, j = pl.program_id(0), pl.program_id(1)

        @pl.when(j == 0)
        def _():
            carry[...] = jnp.zeros_like(carry)

        if with_h:
            @pl.when(jnp.logical_and(bb == 0, j == 0))
            def _():
                dl_ref[...] = jnp.zeros_like(dl_ref)

        rows = lax.broadcasted_iota(jnp.int32, (tc, LANES), 0)
        first = rows == (tc - 1 if dec else 0)
        has_next = j + 1 < n_chunks
        for s in range(n // LANES):
            re, im = slice(s * LANES, (s + 1) * LANES), slice(n + s * LANES, n + (s + 1) * LANES)
            ar, ai = lr_ref[:, re], li_ref[:, re]
            cr, ci = carry[0:1, re], carry[0:1, im]
            dr = x_ref[:, re] + jnp.where(first, ar * cr - ai * ci, 0.0)
            di = x_ref[:, im] + jnp.where(first, ar * ci + ai * cr, 0.0)
            pr, pi = ar, ai
            sft = 1
            while sft < tc:
                if dec:
                    keep = rows < tc - sft
                    sr = jnp.where(keep, pltpu.roll(dr, tc - sft, 0), 0.0)
                    si = jnp.where(keep, pltpu.roll(di, tc - sft, 0), 0.0)
                else:
                    keep = rows >= sft
                    sr = jnp.where(keep, pltpu.roll(dr, sft, 0), 0.0)
                    si = jnp.where(keep, pltpu.roll(di, sft, 0), 0.0)
                dr, di = dr + pr * sr - pi * si, di + pr * si + pi * sr
                pr, pi = pr * pr - pi * pi, 2.0 * pr * pi
                sft *= 2
            o_ref[:, re] = dr
            o_ref[:, im] = di
            last = 0 if dec else tc - 1
            carry[0:1, re] = o_ref[last:last + 1, re]
            carry[0:1, im] = o_ref[last:last + 1, im]
            if with_h:
                hr, hi = h_ref[:, re], h_ref[:, im]
                er = 7 if dec else 0
                br = jnp.where(has_next, hb_ref[er:er + 1, re], 0.0)
                bi = jnp.where(has_next, hb_ref[er:er + 1, im], 0.0)
                if dec:
                    nr = jnp.where(rows == 0, br, pltpu.roll(hr, 1, 0))
                    ni = jnp.where(rows == 0, bi, pltpu.roll(hi, 1, 0))
                else:
                    nr = jnp.where(rows == tc - 1, br, pltpu.roll(hr, tc - 1, 0))
                    ni = jnp.where(rows == tc - 1, bi, pltpu.roll(hi, tc - 1, 0))
                dl_ref[0:1, re] += jnp.sum(dr * nr + di * ni, axis=0, keepdims=True)
                dl_ref[0:1, im] += jnp.sum(di * nr - dr * ni, axis=0, keepdims=True)

    xspec = pl.BlockSpec((tc, 2 * n), lambda bb, j: (chunk(bb, j), 0))
    lspec = pl.BlockSpec((1, n), lambda bb, j: (0, 0))
    in_specs, args = [xspec, lspec, lspec], [x, lam_r, lam_i]
    out_shape, out_specs = [SDS(x.shape, F32)], [xspec]
    if with_h:
        in_specs += [xspec, pl.BlockSpec((8, 2 * n), lambda bb, j: (edge(bb, j), 0))]
        args += [h, h]
        out_shape.append(SDS((8, 2 * n), F32))
        out_specs.append(pl.BlockSpec((8, 2 * n), lambda bb, j: (0, 0)))
    tag = ('c' if ctx_first else 'l') + ('d' if dec else 'u') + ('h' if with_h else '')
    res = pl.pallas_call(
        body, name=f"scan_{tag}_{x.shape[0]}", grid=(b, n_chunks),
        out_shape=out_shape, in_specs=in_specs, out_specs=out_specs,
        scratch_shapes=[pltpu.VMEM((8, 2 * n), F32)],
    )(*args)
    return res


def _make_scan(dims, direction):
    @jax.custom_vjp
    def scan(x, lam_r, lam_i):
        return _scan_call(x, lam_r, lam_i, dims, True, direction == 1)[0]

    def scan_fwd(x, lam_r, lam_i):
        h = _scan_call(x, lam_r, lam_i, dims, True, direction == 1)[0]
        return h, (h, lam_r, lam_i)

    def scan_bwd(res, g):
        h, lam_r, lam_i = res
        gd, dl = _scan_call(g, lam_r, -lam_i, dims, False, direction == 0, h=h)
        return gd, dl[0:1, :SSM_N], dl[0:1, SSM_N:]

    scan.defvjp(scan_fwd, scan_bwd)
    return scan


def _cmul(ar, ai, br, bi):
    return ar * br - ai * bi, ar * bi + ai * br


def _scan_chunk(buf, lr_ref, li_ref, carry, dec, tc):
    sub = lax.broadcasted_iota(jnp.int32, (8, LANES), 0)
    groups = range(tc // 8 - 1, -1, -1) if dec else range(tc // 8)
    for s in range(SSM_N // LANES):
        re, im = slice(s * LANES, (s + 1) * LANES), slice(SSM_N + s * LANES, SSM_N + (s + 1) * LANES)
        pw = [(lr_ref[:, re], li_ref[:, re])]
        for _ in range(7):
            pw.append(_cmul(*pw[-1], *pw[0]))
        p8r, p8i = jnp.zeros((8, LANES), F32), jnp.zeros((8, LANES), F32)
        for k in range(8):
            row = 7 - k if dec else k
            p8r = jnp.where(sub == row, pw[k][0], p8r)
            p8i = jnp.where(sub == row, pw[k][1], p8i)
        steps = []
        for sft in (1, 2, 4):
            keep = sub < 8 - sft if dec else sub >= sft
            steps.append((8 - sft if dec else sft, jnp.where(keep, pw[sft - 1][0], 0.0),
                          jnp.where(keep, pw[sft - 1][1], 0.0)))
        cr, ci = carry[0:1, re], carry[0:1, im]
        for r in groups:
            rows = slice(8 * r, 8 * r + 8)
            xr, xi = buf[rows, re], buf[rows, im]
            for amount, mr, mi in steps:
                sr, si = pltpu.roll(xr, amount, 0), pltpu.roll(xi, amount, 0)
                xr, xi = xr + mr * sr - mi * si, xi + mr * si + mi * sr
            xr, xi = xr + p8r * cr - p8i * ci, xi + p8r * ci + p8i * cr
            buf[rows, re] = xr
            buf[rows, im] = xi
            last = 8 * r if dec else 8 * r + 7
            cr, ci = buf[last:last + 1, re], buf[last:last + 1, im]
        carry[0:1, re] = cr
        carry[0:1, im] = ci


def _ssm_chunks(dims, ctx_first, dec):
    b, l, lc = dims
    tc = _pick(lc, (256, 128, 64, 32, 16, 8))
    nc_c, nc_l = lc // tc, l // tc
    n_chunks = nc_c + nc_l
    ctx_base = b * l // tc
    g8 = tc // 8

    def chunk(bb, j):
        if ctx_first:
            is_ctx = j < nc_c
            jj = jnp.where(is_ctx, j, j - nc_c)
        else:
            is_ctx = j >= nc_l
            jj = jnp.where(is_ctx, j - nc_l, j)
        ic = nc_c - 1 - jj if dec else jj
        il = nc_l - 1 - jj if dec else jj
        return jnp.where(is_ctx, ctx_base + bb * nc_c + ic, bb * nc_l + il)

    def edge(bb, j):
        jn = jnp.minimum(j + 1, n_chunks - 1)
        return chunk(bb, jn) * g8 + (g8 - 1 if dec else 0)

    return tc, n_chunks, chunk, edge


def _ssm_fwd_call(u, bmat, cmat, lam_r, lam_i, dims, dec):
    n2 = 2 * SSM_N
    tc, n_chunks, chunk, _ = _ssm_chunks(dims, True, dec)

    def body(u_ref, b_ref, c_ref, lr_ref, li_ref, h_ref, y_ref, carry):
        @pl.when(pl.program_id(1) == 0)
        def _():
            carry[...] = jnp.zeros_like(carry)

        h_ref[...] = jnp.dot(_mx(u_ref[...]), b_ref[...], preferred_element_type=F32)
        _scan_chunk(h_ref, lr_ref, li_ref, carry, dec, tc)
        y_ref[...] = jnp.dot(_mx(h_ref[...]), c_ref[...], preferred_element_type=F32)

    def full(shape):
        return pl.BlockSpec(shape, lambda bb, j: (0, 0))

    return pl.pallas_call(
        body, name=f"ssm_fwd_{'d' if dec else 'u'}_{u.shape[0]}", grid=(dims[0], n_chunks),
        out_shape=[SDS((u.shape[0], n2), F32), SDS(u.shape, F32)],
        in_specs=[pl.BlockSpec((tc, SSM_WIDTH), lambda bb, j: (chunk(bb, j), 0)), full((SSM_WIDTH, n2)),
                  full((n2, SSM_WIDTH)), full((1, SSM_N)), full((1, SSM_N))],
        out_specs=[pl.BlockSpec((tc, n2), lambda bb, j: (chunk(bb, j), 0)),
                   pl.BlockSpec((tc, SSM_WIDTH), lambda bb, j: (chunk(bb, j), 0))],
        scratch_shapes=[pltpu.VMEM((8, n2), F32)],
    )(u, bmat, cmat, lam_r, lam_i)


def _ssm_bwd_call(gy, h, u, bmat, cmat, lam_r, lam_i, dims, dec):
    n2 = 2 * SSM_N
    tc, n_chunks, chunk, edge = _ssm_chunks(dims, False, dec)

    def body(gy_ref, h_ref, hb_ref, u_ref, b_ref, c_ref, lr_ref, li_ref, du_ref, db_ref, dct_ref, dl_ref, gd, carry):
        bb, j = pl.program_id(0), pl.program_id(1)

        @pl.when(j == 0)
        def _():
            carry[...] = jnp.zeros_like(carry)

        @pl.when(jnp.logical_and(bb == 0, j == 0))
        def _():
            db_ref[...] = jnp.zeros_like(db_ref)
            dct_ref[...] = jnp.zeros_like(dct_ref)
            dl_ref[...] = jnp.zeros_like(dl_ref)

        gyv = _mx(gy_ref[...])
        gd[...] = _dg_nt(gyv, c_ref[...])
        _scan_chunk(gd, lr_ref, li_ref, carry, dec, tc)
        gdv = _mx(gd[...])
        du_ref[...] = _dg_nt(gdv, b_ref[...])
        db_ref[...] += _dg_tn(_mx(u_ref[...]), gdv)
        dct_ref[...] += _dg_tn(gyv, _mx(h_ref[...]))
        rows = lax.broadcasted_iota(jnp.int32, (tc, LANES), 0)
        has_next = j + 1 < n_chunks
        er = 7 if dec else 0
        for s in range(SSM_N // LANES):
            re, im = slice(s * LANES, (s + 1) * LANES), slice(SSM_N + s * LANES, SSM_N + (s + 1) * LANES)
            dr, di, hr, hi = gd[:, re], gd[:, im], h_ref[:, re], h_ref[:, im]
            br = jnp.where(has_next, hb_ref[er:er + 1, re], 0.0)
            bi = jnp.where(has_next, hb_ref[er:er + 1, im], 0.0)
            if dec:
                nr = jnp.where(rows == 0, br, pltpu.roll(hr, 1, 0))
                ni = jnp.where(rows == 0, bi, pltpu.roll(hi, 1, 0))
            else:
                nr = jnp.where(rows == tc - 1, br, pltpu.roll(hr, tc - 1, 0))
                ni = jnp.where(rows == tc - 1, bi, pltpu.roll(hi, tc - 1, 0))
            dl_ref[0:1, re] += jnp.sum(dr * nr + di * ni, axis=0, keepdims=True)
            dl_ref[0:1, im] += jnp.sum(di * nr - dr * ni, axis=0, keepdims=True)

    def full(shape):
        return pl.BlockSpec(shape, lambda bb, j: (0, 0))

    def at_chunk(width):
        return pl.BlockSpec((tc, width), lambda bb, j: (chunk(bb, j), 0))

    return pl.pallas_call(
        body, name=f"ssm_bwd_{'d' if dec else 'u'}_{u.shape[0]}", grid=(dims[0], n_chunks),
        out_shape=[SDS(u.shape, F32), SDS((SSM_WIDTH, n2), F32), SDS((SSM_WIDTH, n2), F32), SDS((8, n2), F32)],
        in_specs=[at_chunk(SSM_WIDTH), at_chunk(n2), pl.BlockSpec((8, n2), lambda bb, j: (edge(bb, j), 0)),
                  at_chunk(SSM_WIDTH), full((SSM_WIDTH, n2)), full((n2, SSM_WIDTH)), full((1, SSM_N)),
                  full((1, SSM_N))],
        out_specs=[at_chunk(SSM_WIDTH), full((SSM_WIDTH, n2)), full((SSM_WIDTH, n2)), full((8, n2))],
        scratch_shapes=[pltpu.VMEM((tc, n2), F32), pltpu.VMEM((8, n2), F32)],
    )(gy, h, h, u, bmat, cmat, lam_r, lam_i)


def _make_ssm(dims, direction):
    @jax.custom_vjp
    def ssm(u, bmat, cmat, lam_r, lam_i):
        return _ssm_fwd_call(u, _mx(bmat), _mx(cmat), lam_r, lam_i, dims, direction == 1)[1]

    def ssm_fwd(u, bmat, cmat, lam_r, lam_i):
        h, y = _ssm_fwd_call(u, _mx(bmat), _mx(cmat), lam_r, lam_i, dims, direction == 1)
        return y, (u, bmat, cmat, lam_r, lam_i, h)

    def ssm_bwd(res, gy):
        u, bmat, cmat, lam_r, lam_i, h = res
        du, db, dct, dl = _ssm_bwd_call(gy, h, u, _mx(bmat), _mx(cmat), lam_r, -lam_i, dims, direction == 0)
        return du, db, dct.T, dl[0:1, :SSM_N], dl[0:1, SSM_N:]

    ssm.defvjp(ssm_fwd, ssm_bwd)
    return ssm


def _block_diag(m):
    g, a, bdim = m.shape[-3:]
    eye = jnp.eye(g, dtype=m.dtype)
    full = m[..., :, :, None, :] * eye[:, None, :, None]
    return full.reshape(m.shape[:-3] + (g * a, g * bdim))


def _ssm_mats(lam_re, lam_im, log_dt, b_re, b_im, c_re, c_im):
    lam = lax.complex(lam_re, lam_im)
    dt = jnp.exp(log_dt)[..., None]
    lam_bar = jnp.exp(lam * dt)
    b_bar = ((lam_bar - 1) / lam)[..., None] * lax.complex(b_re, b_im)
    bt = jnp.swapaxes(b_bar, -1, -2)
    bmat = jnp.concatenate([_block_diag(jnp.real(bt)), _block_diag(jnp.imag(bt))], axis=-1)
    cmat = jnp.concatenate([_block_diag(jnp.swapaxes(c_re, -1, -2)), _block_diag(-jnp.swapaxes(c_im, -1, -2))],
                           axis=-2)
    flat = lam_bar.shape[:2] + (1, SSM_N)
    return bmat, cmat, jnp.real(lam_bar).reshape(flat), jnp.imag(lam_bar).reshape(flat)


def _loss_call(y, target):
    t, d = y.shape
    tr = _pick(t, (ROW_TILE, 128, 64, 32, 16, 8))

    def body(y_ref, t_ref, acc_ref, dy_ref):
        @pl.when(pl.program_id(0) == 0)
        def _():
            acc_ref[...] = jnp.zeros_like(acc_ref)

        diff = y_ref[...] - t_ref[...]
        dy_ref[...] = diff * (1.0 / d)
        acc_ref[0:1, :] += jnp.sum(diff * diff, axis=0, keepdims=True)

    spec = pl.BlockSpec((tr, d), lambda i: (i, 0))
    return pl.pallas_call(
        body, name=f"loss_{t}", grid=(t // tr,),
        out_shape=[SDS((8, d), F32), SDS((t, d), F32)],
        in_specs=[spec, spec], out_specs=[pl.BlockSpec((8, d), lambda i: (0, 0)), spec],
    )(y, target)


def _adamw_call(w, g, m, v):
    r, c = w.shape
    tr = _pick(r, [t for t in (1024, 512, 256, 128, 64, 32, 16, 8) if t * c <= 256 * 1024])

    def body(w_ref, g_ref, m_ref, v_ref, d_ref, mo_ref, vo_ref):
        gv = g_ref[...]
        mn = ADAM_B1 * m_ref[...] + (1.0 - ADAM_B1) * gv
        vn = ADAM_B2 * v_ref[...] + (1.0 - ADAM_B2) * (gv * gv)
        m_hat = mn / (1.0 - ADAM_B1 ** ADAM_STEP)
        v_hat = vn / (1.0 - ADAM_B2 ** ADAM_STEP)
        d_ref[...] = -ADAM_LR * (m_hat / (jnp.sqrt(v_hat) + ADAM_EPS) + ADAM_WD * w_ref[...])
        mo_ref[...] = mn
        vo_ref[...] = vn

    spec = pl.BlockSpec((tr, c), lambda i: (i, 0))
    return pl.pallas_call(
        body, name=f"adamw_{r}x{c}", grid=(r // tr,),
        out_shape=[SDS((r, c), F32)] * 3, in_specs=[spec] * 4, out_specs=[spec] * 3,
    )(w, g, m, v)


def _make_modulation(b, me):
    n_ex = N_DEV * b

    def a_matrix(c_ctx, c_all):
        rows = [jax.nn.silu(c_all), jax.nn.silu(c_ctx)[None, :],
                jnp.zeros((MOD_ROWS - n_ex - 1, c_all.shape[1]), F32)]
        return jnp.concatenate(rows, axis=0)

    def forward(c_ctx, b_ada, w_loc, c_all):
        depth, _, width = w_loc.shape
        a_mat = a_matrix(c_ctx, c_all)
        cols = jnp.stack([_mm_call('nn', a_mat, w_loc[layer]) for layer in range(depth)])
        gathered = _all_gather(cols.reshape(depth * MOD_ROWS, width))
        full = jnp.moveaxis(gathered.reshape(N_DEV, depth, MOD_ROWS, width), 0, 2).reshape(depth, MOD_ROWS, -1)
        mine = lax.dynamic_slice_in_dim(full, me * b, b, axis=1)
        return jnp.concatenate([mine, full[:, n_ex:n_ex + 1]], axis=1) + b_ada[:, None, :]

    @jax.custom_vjp
    def modulation(c_ctx, b_ada, w_loc, c_all):
        return forward(c_ctx, b_ada, w_loc, c_all)

    def modulation_fwd(c_ctx, b_ada, w_loc, c_all):
        return forward(c_ctx, b_ada, w_loc, c_all), (c_ctx, w_loc, c_all)

    def modulation_bwd(res, g):
        c_ctx, w_loc, c_all = res
        depth, _, width = w_loc.shape
        rows = depth * (b + 1)
        flat = jnp.pad(g.reshape(rows, -1), ((0, -rows % 8), (0, 0)))
        everyone = _all_gather(flat)[:, :rows].reshape(N_DEV, depth, b + 1, -1)
        examples = jnp.moveaxis(everyone[:, :, :b], 0, 1).reshape(depth, n_ex, -1)
        ctx_total = jnp.sum(everyone[:, :, b], axis=0)
        g_mat = jnp.concatenate([examples, ctx_total[:, None, :],
                                 jnp.zeros((depth, MOD_ROWS - n_ex - 1, g.shape[-1]), F32)], axis=1)
        g_cols = lax.dynamic_slice_in_dim(g_mat, me * width, width, axis=2)
        a_mat, a_vjp = jax.vjp(lambda cc: a_matrix(cc, c_all), c_ctx)
        dw = jnp.stack([_mm_call('tn', a_mat, g_cols[layer]) for layer in range(depth)])
        da = sum(_mm_call('nt', g_cols[layer], w_loc[layer]) for layer in range(depth))
        dc_ctx, = a_vjp(da)
        return dc_ctx, jnp.sum(g, axis=1), dw, jnp.zeros_like(c_all)

    modulation.defvjp(modulation_fwd, modulation_bwd)
    return modulation


def _forward(diff, gathered, c_all, me, dims, depth):
    b, l, lc = dims
    t_lat = b * l
    d_model = diff['x'].shape[-1]
    cos, sin = _rope_tables(l)
    nm_linear, nm_mlp = _make_nm_ops(*_rowwise("norm_mod", _norm_mod_f, (d_model,), l))
    resid_gate = _make_rowwise_op("resid_gate", _resid_gate_f, d_model, 2, 1, l)
    glu = _make_rowwise_op("glu", _glu_f, SSM_WIDTH, 3, 0, l)
    att_conv = _make_att_conv(dims, cos, sin)
    ssms = [_make_ssm(dims, 0), _make_ssm(dims, 1)]
    bmats, cmats, lam_r, lam_i = _ssm_mats(*[diff['ssm_' + n] for n in
                                             ('lam_re', 'lam_im', 'log_dt', 'b_re', 'b_im', 'c_re', 'c_im')])

    mods = _make_modulation(b, me)(diff['c_ctx'], diff['b_ada'], diff['w_ada'], c_all)
    h = jnp.concatenate([diff['x'].reshape(t_lat, d_model), diff['ctx'].reshape(b * lc, d_model)], axis=0)

    for layer in range(depth):
        last = layer == depth - 1
        sh1, sc1, g1, sh2, sc2, g2 = [m.reshape(b + 1, 1, d_model) for m in jnp.split(mods[layer], N_MOD, axis=-1)]
        ng = diff['norm_g'][layer]
        p = nm_linear(h, sh1, sc1, ng[0:1], gathered['w_in'][layer], diff['w_in'][layer])
        ac = att_conv(p, diff['conv_w'][layer], diff['attn_sink'][layer][None, :])
        u = p[:, IN_WIDTH - SSM_WIDTH:]
        ys = [ssms[d](u, bmats[layer, d], cmats[layer, d], lam_r[layer, d], lam_i[layer, d]) for d in range(2)]
        s = glu(ys[0], ys[1], u, diff['ssm_d'][layer][None, :], diff['w_glu'][layer], diff['b_glu'][layer][None, :])
        if last:
            ac, s, h = ac[:t_lat], s[:t_lat], h[:t_lat]
            g1, sh2, sc2, g2 = g1[:b], sh2[:b], sc2[:b], g2[:b]
        m = mix_linear(ac, s, gathered['w_out'][layer], diff['w_out'][layer])
        h = resid_gate(h, m, g1, ng[1:2])
        f = nm_mlp(h, sh2, sc2, ng[2:3], gathered['w_mlp_in'][layer], diff['w_mlp_in'][layer],
                   gathered['w_mlp_out'][layer], diff['w_mlp_out'][layer])
        h = resid_gate(h, f, g2, ng[3:4])
    return h


BIG_TRANSPOSED = {'w_in': True, 'w_out': False, 'w_mlp_in': True, 'w_mlp_out': False}


def _big_rows(weights):
    blocks = []
    depth = weights['w_ada'].shape[0]
    for n in BIG:
        for layer in range(depth):
            w = weights[n][layer]
            blocks.append((layer, n, w.shape[1] if BIG_TRANSPOSED[n] else w.shape[0]))
    return blocks


def _stack_big(arrs, dtype):
    parts = []
    for n in BIG:
        w = jnp.swapaxes(arrs[n], 1, 2) if BIG_TRANSPOSED[n] else arrs[n]
        parts.append(w.reshape(-1, w.shape[-1]).astype(dtype))
    return jnp.concatenate(parts, axis=0)


def _adamw_any(w, g, m, v):
    flat = [a.reshape(-1, w.shape[-1]) for a in (w, g, m, v)]
    return [o.reshape(w.shape) for o in _adamw_call(*flat)]


def kernel(x, c, ctx, c_ctx, w_ada, b_ada, norm_g, w_in, conv_w, attn_sink, ssm_lam_re, ssm_lam_im, ssm_log_dt, ssm_b_re, ssm_b_im, ssm_c_re, ssm_c_im, ssm_d, w_glu, b_glu, w_out, w_mlp_in, w_mlp_out, loss_target, m_c_ctx, m_w_ada, m_b_ada, m_norm_g, m_w_in, m_conv_w, m_attn_sink, m_ssm_lam_re, m_ssm_lam_im, m_ssm_log_dt, m_ssm_b_re, m_ssm_b_im, m_ssm_c_re, m_ssm_c_im, m_ssm_d, m_w_glu, m_b_glu, m_w_out, m_w_mlp_in, m_w_mlp_out, v_c_ctx, v_w_ada, v_b_ada, v_norm_g, v_w_in, v_conv_w, v_attn_sink, v_ssm_lam_re, v_ssm_lam_im, v_ssm_log_dt, v_ssm_b_re, v_ssm_b_im, v_ssm_c_re, v_ssm_c_im, v_ssm_d, v_w_glu, v_b_glu, v_w_out, v_w_mlp_in, v_w_mlp_out):
    given = dict(locals())
    weights = {n: given[n] for n in WEIGHTS}
    moms = {n: given['m_' + n] for n in WEIGHTS}
    vars_ = {n: given['v_' + n] for n in WEIGHTS}
    b, l, d_model = x.shape
    lc = ctx.shape[1]
    dims = (b, l, lc)
    depth = w_ada.shape[0]

    core = lax.axis_index('c')
    me = 4 * lax.axis_index('x') + 2 * lax.axis_index('y') + core

    blocks = _big_rows(weights)
    big_all = _all_gather(_stack_big(weights, MXU_DTYPE))
    gathered = {n: [None] * depth for n in BIG}
    off = 0
    for layer, n, rows in blocks:
        gathered[n][layer] = big_all[:, off:off + rows].reshape(N_DEV * rows, -1)
        off += rows
    small_shapes = [weights[n].shape for n in SMALL]
    small_all = _all_gather(_pack([weights[n] for n in SMALL], F32, PACK_ROWS))
    small_full = {n: _from_pieces(pc, SHARD_AXIS[n])
                  for n, pc in zip(SMALL, _unpack(small_all, small_shapes, (N_DEV,)))}

    c_all = _all_gather(jnp.pad(c, ((0, -b % 8), (0, 0))))[:, :b].reshape(N_DEV * b, d_model)
    diff = {'x': x, 'ctx': ctx, 'w_ada': w_ada}
    for n in REPLICATED:
        diff[n] = weights[n]
    for n in SMALL:
        diff[n] = small_full[n]
    for n in BIG:
        diff[n] = [jnp.zeros(gathered[n][layer].shape, F32) for layer in range(depth)]
    y, vjp = jax.vjp(lambda dd: _forward(dd, gathered, c_all, me, dims, depth), diff)
    sq, dy = _loss_call(y, loss_target.reshape(b * l, d_model))
    loss = lax.psum(0.5 * jnp.sum(sq) / d_model, ('x', 'y', 'c'))
    grads = vjp(dy)[0]
    grad_x = grads['x']

    g8 = jnp.concatenate([grads[n][layer].reshape(N_DEV, rows, -1) for layer, n, rows in blocks], axis=1)
    chip_sums = _rs_add(g8, _rs_sibling(g8), core.reshape(1).astype(jnp.int32))
    g_big = _sum_slots(_rs_chips(chip_sums))
    big_grad, off = {}, 0
    for n in BIG:
        rows = sum(r for _, name, r in blocks if name == n)
        blk = g_big[off:off + rows].reshape(depth, rows // depth, -1)
        big_grad[n] = jnp.swapaxes(blk, 1, 2) if BIG_TRANSPOSED[n] else blk
        off += rows
    rest = SMALL + REPLICATED
    g_rest = _sum_slots(_all_gather(_pack([grads[n] for n in rest], F32, PACK_ROWS)))
    rest_grad = dict(zip(rest, _unpack(g_rest, [grads[n].shape for n in rest])))
    for n in SMALL:
        width = weights[n].shape[SHARD_AXIS[n]]
        rest_grad[n] = lax.dynamic_slice_in_dim(rest_grad[n], me * width, width, axis=SHARD_AXIS[n])

    out = {}
    all_grads = {**big_grad, **rest_grad, **{n: grads[n] for n in LOCAL}}
    for n in WEIGHTS:
        out[('grad', n)] = all_grads[n]
        for kind, arr in zip(('delta', 'new_m', 'new_v'), _adamw_any(weights[n], all_grads[n], moms[n], vars_[n])):
            out[(kind, n)] = arr
    return (loss, grad_x, *[out[(kind, n)] for kind in ('grad', 'delta', 'new_m', 'new_v') for n in WEIGHTS])
```

```python
import jax
import jax.numpy as jnp
from jax import lax
from jax.experimental import pallas as pl
from jax.experimental.pallas import tpu as pltpu

F32 = jnp.float32
MXU_DTYPE = jnp.bfloat16
SDS = jax.ShapeDtypeStruct

N_DEV = 8
HEAD_DIM = 64
N_Q_HEADS = 8
Q_PER_KV = 4
ATTN_WIDTH = 512
KV_WIDTH = 128
WINDOW = 128
Q_BLOCK = 128
GRID_W = 64
ROPE_BASE = 10000.0
ROPE_PAIRS = 16
CONV_WIDTH = 256
SSM_WIDTH = 256
SSM_GROUP = 16
SSM_GROUPS = 16
SSM_STATE = 64
SSM_N = SSM_GROUPS * SSM_STATE
IN_WIDTH = 1792
N_MOD = 6
EPS = 1e-6
NEG_INF = -1e30
ADAM_LR = 0.001
ADAM_B1 = 0.9
ADAM_B2 = 0.999
ADAM_EPS = 1e-08
ADAM_WD = 0.01
ADAM_STEP = 10
MOD_ROWS = 128
ROW_TILE = 256
PACK_ROWS = 256
LANES = 128
MESH = pl.DeviceIdType.MESH
_SCALE = HEAD_DIM ** -0.5
MIX_SPLIT = ATTN_WIDTH + CONV_WIDTH
ATTN_STACK_ROWS = 256
MM_TILE = 1024
MM_VMEM_BYTES = 56 * 1024 * 1024

WEIGHTS = ['c_ctx', 'w_ada', 'b_ada', 'norm_g', 'w_in', 'conv_w', 'attn_sink', 'ssm_lam_re', 'ssm_lam_im',
           'ssm_log_dt', 'ssm_b_re', 'ssm_b_im', 'ssm_c_re', 'ssm_c_im', 'ssm_d', 'w_glu', 'b_glu', 'w_out',
           'w_mlp_in', 'w_mlp_out']
SHARD_AXIS = {'w_ada': 2, 'norm_g': 2, 'w_in': 2, 'conv_w': 2, 'w_glu': 1, 'w_out': 1, 'w_mlp_in': 2, 'w_mlp_out': 1}
BIG = ['w_in', 'w_out', 'w_mlp_in', 'w_mlp_out']
SMALL = ['norm_g', 'conv_w', 'w_glu']
LOCAL = ['w_ada']
REPLICATED = [n for n in WEIGHTS if n not in SHARD_AXIS]


def _pick(n, cands):
    for c in cands:
        if n % c == 0:
            return c
    return n


def _div_tile(n, cap):
    if n <= cap:
        return n
    for c in range(cap, LANES - 1, -LANES):
        if n % c == 0:
            return c
    return n


def _mx(x):
    return x.astype(MXU_DTYPE)


def _dg_nt(a, b):
    return lax.dot_general(a, b, (((1,), (1,)), ((), ())), preferred_element_type=F32)


def _dg_tn(a, b):
    return lax.dot_general(a, b, (((0,), (0,)), ((), ())), preferred_element_type=F32)


def _all_gather(x2d):
    r, cdim = x2d.shape

    def body(x_ref, out_ref, send_sems, recv_sems, local_sem):
        x, y, c = lax.axis_index("x"), lax.axis_index("y"), lax.axis_index("c")
        me, sibling = (x, y, c), (x, y, 1 - c)
        chips = [(1 - x, y), (x, 1 - y), (1 - x, 1 - y)]

        def slot(px, py, pc):
            return out_ref.at[4 * px + 2 * py + pc]

        def copy(k, block, to, src=None):
            return pltpu.make_async_remote_copy(
                src_ref=slot(*block) if src is None else src, dst_ref=slot(*block),
                send_sem=send_sems.at[k], recv_sem=recv_sems.at[k], device_id=to, device_id_type=MESH)

        mine = pltpu.make_async_copy(x_ref, slot(*me), local_sem)
        mine.start()
        first = [copy(0, me, sibling, src=x_ref)]
        first += [copy(1 + j, me, (*chip, c), src=x_ref) for j, chip in enumerate(chips)]
        for cp in first:
            cp.start()
        passed = [copy(4 + j, (*chip, c), sibling) for j, chip in enumerate(chips)]
        for j, chip in enumerate(chips):
            copy(1 + j, (*chip, c), me).wait_recv()
            passed[j].start()
        copy(0, sibling, me).wait_recv()
        for j, chip in enumerate(chips):
            copy(4 + j, (*chip, 1 - c), me).wait_recv()
        for cp in first + passed:
            cp.wait_send()
        mine.wait()

    return pl.pallas_call(
        body, name=f"all_gather_{r}x{cdim}_{jnp.dtype(x2d.dtype).name}",
        out_shape=SDS((N_DEV, r, cdim), x2d.dtype),
        in_specs=[pl.BlockSpec(memory_space=pl.ANY)],
        out_specs=pl.BlockSpec(memory_space=pl.ANY),
        scratch_shapes=[pltpu.SemaphoreType.DMA((7,)), pltpu.SemaphoreType.DMA((7,)), pltpu.SemaphoreType.DMA],
    )(x2d)


def _rs_sibling(g8):
    _, r, cdim = g8.shape

    def body(g_ref, out_ref, send_sems, recv_sems):
        x, y, c = lax.axis_index("x"), lax.axis_index("y"), lax.axis_index("c")
        copies = [pltpu.make_async_remote_copy(
            src_ref=g_ref.at[2 * k + (1 - c)], dst_ref=out_ref.at[k], send_sem=send_sems.at[k],
            recv_sem=recv_sems.at[k], device_id=(x, y, 1 - c), device_id_type=MESH) for k in range(4)]
        for cp in copies:
            cp.start()
        for cp in copies:
            cp.wait()

    return pl.pallas_call(
        body, name=f"rs_sibling_{r}x{cdim}",
        out_shape=SDS((4, r, cdim), g8.dtype),
        in_specs=[pl.BlockSpec(memory_space=pl.ANY)],
        out_specs=pl.BlockSpec(memory_space=pl.ANY),
        scratch_shapes=[pltpu.SemaphoreType.DMA((4,)), pltpu.SemaphoreType.DMA((4,))],
    )(g8)


def _rs_add(g8, sib, c_idx):
    _, r, cdim = g8.shape
    tr = _pick(r, (256, 128, 64, 32, 16))

    def body(c_ref, g_ref, s_ref, o_ref):
        del c_ref
        o_ref[...] = (g_ref[...] + s_ref[...]).astype(o_ref.dtype)

    return pl.pallas_call(
        body, name=f"rs_add_{r}x{cdim}",
        grid_spec=pltpu.PrefetchScalarGridSpec(
            num_scalar_prefetch=1, grid=(4, r // tr),
            in_specs=[pl.BlockSpec((1, tr, cdim), lambda k, i, c: (2 * k + c[0], i, 0)),
                      pl.BlockSpec((1, tr, cdim), lambda k, i, c: (k, i, 0))],
            out_specs=pl.BlockSpec((1, tr, cdim), lambda k, i, c: (k, i, 0))),
        out_shape=SDS((4, r, cdim), MXU_DTYPE),
    )(c_idx, g8, sib)


def _rs_chips(s4):
    _, r, cdim = s4.shape

    def body(s_ref, out_ref, send_sems, recv_sems, local_sem):
        x, y, c = lax.axis_index("x"), lax.axis_index("y"), lax.axis_index("c")
        me = 2 * x + y

        def peer(j):
            px = 1 - x if j & 2 else x
            py = 1 - y if j & 1 else y
            return (px, py, c), 2 * px + py

        def copy(j, landing):
            to, to_chip = peer(j)
            return pltpu.make_async_remote_copy(
                src_ref=s_ref.at[to_chip], dst_ref=out_ref.at[to_chip if landing else me],
                send_sem=send_sems.at[j - 1], recv_sem=recv_sems.at[j - 1], device_id=to, device_id_type=MESH)

        mine = pltpu.make_async_copy(s_ref.at[me], out_ref.at[me], local_sem)
        mine.start()
        sends = [copy(j, False) for j in range(1, 4)]
        for cp in sends:
            cp.start()
        for j in range(1, 4):
            copy(j, True).wait_recv()
        for cp in sends:
            cp.wait_send()
        mine.wait()

    return pl.pallas_call(
        body, name=f"rs_chips_{r}x{cdim}",
        out_shape=SDS(s4.shape, s4.dtype),
        in_specs=[pl.BlockSpec(memory_space=pl.ANY)],
        out_specs=pl.BlockSpec(memory_space=pl.ANY),
        scratch_shapes=[pltpu.SemaphoreType.DMA((3,)), pltpu.SemaphoreType.DMA((3,)), pltpu.SemaphoreType.DMA],
    )(s4)


def _sum_slots(g3):
    n, r, cdim = g3.shape
    tr = _pick(r, (256, 128, 64, 32, 16, 8))

    def body(g_ref, o_ref):
        acc = g_ref[0].astype(F32)
        for s in range(1, n):
            acc = acc + g_ref[s].astype(F32)
        o_ref[...] = acc

    return pl.pallas_call(
        body, name=f"sum_slots_{n}x{r}x{cdim}", grid=(r // tr,),
        out_shape=SDS((r, cdim), F32),
        in_specs=[pl.BlockSpec((n, tr, cdim), lambda i: (0, i, 0))],
        out_specs=pl.BlockSpec((tr, cdim), lambda i: (i, 0)),
    )(g3)


def _rows_of(n):
    return -(-n // (8 * LANES)) * 8


def _pack(arrs, dtype, row_mult):
    parts = []
    for a in arrs:
        flat = a.reshape(-1).astype(dtype)
        parts.append(jnp.pad(flat, (0, _rows_of(flat.size) * LANES - flat.size)).reshape(-1, LANES))
    rows = sum(p.shape[0] for p in parts)
    if rows % row_mult:
        parts.append(jnp.zeros((row_mult - rows % row_mult, LANES), dtype))
    return jnp.concatenate(parts, axis=0)


def _unpack(mat, shapes, lead=()):
    out, off = [], 0
    for s in shapes:
        n = 1
        for d in s:
            n *= d
        rows = _rows_of(n)
        part = mat[..., off:off + rows, :].reshape(lead + (rows * LANES,))
        out.append(part[..., :n].reshape(lead + tuple(s)))
        off += rows
    return out


def _from_pieces(p, axis):
    p = jnp.moveaxis(p, 0, axis)
    s = p.shape
    return p.reshape(s[:axis] + (s[axis] * s[axis + 1],) + s[axis + 2:])


def _mm_call(kind, a, b, z=None, sqrelu=False, out_dtype=F32):
    if kind == 'nn':
        (m, k), n = a.shape, b.shape[1]
    elif kind == 'nt':
        (m, k), n = a.shape, b.shape[0]
    else:
        (k, m), n = a.shape, b.shape[1]
    bm = _div_tile(m, MM_TILE)
    bn = _div_tile(n, MM_TILE)
    bk = _div_tile(k, MM_TILE if kind == 'tn' else 2 * MM_TILE)
    nk = k // bk
    if sqrelu:
        out_dtype = MXU_DTYPE
    plain = not sqrelu and z is None and out_dtype == F32
    use_acc = nk > 1 and not plain

    def body(a_ref, b_ref, *rest):
        rest = list(rest)
        z_ref = rest.pop(0) if z is not None else None
        o_ref = rest.pop(0)
        act_ref = rest.pop(0) if sqrelu else None
        acc = rest.pop(0) if use_acc else None
        kk = pl.program_id(2)
        av, bv = _mx(a_ref[...]), _mx(b_ref[...])
        if kind == 'nn':
            prod = jnp.dot(av, bv, preferred_element_type=F32)
        elif kind == 'nt':
            prod = _dg_nt(av, bv)
        else:
            prod = _dg_tn(av, bv)

        def finish(r):
            if z_ref is not None:
                r = r * (2.0 * jnp.maximum(z_ref[...].astype(F32), 0.0))
            o_ref[...] = r.astype(o_ref.dtype)
            if act_ref is not None:
                rr = jnp.maximum(r, 0.0)
                act_ref[...] = (rr * rr).astype(act_ref.dtype)

        if nk == 1:
            finish(prod)
        else:
            tgt = acc if use_acc else o_ref

            @pl.when(kk == 0)
            def _():
                tgt[...] = prod

            @pl.when(kk > 0)
            def _():
                tgt[...] += prod

            if use_acc:
                @pl.when(kk == nk - 1)
                def _():
                    finish(acc[...])

    if kind == 'nn':
        a_spec = pl.BlockSpec((bm, bk), lambda i, j, kk: (i, kk))
        b_spec = pl.BlockSpec((bk, bn), lambda i, j, kk: (kk, j))
    elif kind == 'nt':
        a_spec = pl.BlockSpec((bm, bk), lambda i, j, kk: (i, kk))
        b_spec = pl.BlockSpec((bn, bk), lambda i, j, kk: (j, kk))
    else:
        a_spec = pl.BlockSpec((bk, bm), lambda i, j, kk: (kk, i))
        b_spec = pl.BlockSpec((bk, bn), lambda i, j, kk: (kk, j))
    o_spec = pl.BlockSpec((bm, bn), lambda i, j, kk: (i, j))
    in_specs, args = [a_spec, b_spec], [a, b]
    if z is not None:
        in_specs.append(o_spec)
        args.append(z)
    out_shape, out_specs = [SDS((m, n), out_dtype)], [o_spec]
    if sqrelu:
        out_shape.append(SDS((m, n), MXU_DTYPE))
        out_specs.append(o_spec)
    tag = kind + ('_sq' if sqrelu else '') + ('_z' if z is not None else '')
    res = pl.pallas_call(
        body, name=f"mm_{tag}_{m}x{k}x{n}", grid=(m // bm, n // bn, nk),
        out_shape=out_shape, in_specs=in_specs, out_specs=out_specs,
        scratch_shapes=[pltpu.VMEM((bm, bn), F32)] if use_acc else [],
        compiler_params=pltpu.CompilerParams(dimension_semantics=("parallel", "parallel", "arbitrary"),
                                             vmem_limit_bytes=MM_VMEM_BYTES),
    )(*args)
    return res if sqrelu else res[0]


def _make_block_ops(nm, rg, middle, rows_out):
    nm_fwd, nm_bwd = nm
    rg_fwd, rg_bwd = rg

    def mixer_fwd(h, sh, sc, g_pre, wt, wtd, params, w, wd, gate, g_post):
        a = nm_fwd((h,), (sh, sc), (g_pre,), MXU_DTYPE)[0]
        (ac, s), mid_vjp = jax.vjp(middle, _mm_call('nt', a, wt), params)
        mix = _mx(jnp.concatenate([ac[:rows_out], s[:rows_out]], axis=1))
        m = _mm_call('nn', mix, w)
        out = rg_fwd((h[:rows_out], m), (gate,), (g_post,))[0]
        return out, (h, sh, sc, g_pre, a, wt, mid_vjp, mix, w, m, gate, g_post)

    @jax.custom_vjp
    def mixer_block(h, sh, sc, g_pre, wt, wtd, params, w, wd, gate, g_post):
        return mixer_fwd(h, sh, sc, g_pre, wt, wtd, params, w, wd, gate, g_post)[0]

    def mixer_bwd(res, g):
        h, sh, sc, g_pre, a, wt, mid_vjp, mix, w, m, gate, g_post = res
        dm, dgate, dg_post = rg_bwd((h[:rows_out], m), (gate,), (g_post,), (g,), skip_first_row=True)
        dmix = jnp.pad(_mm_call('nt', dm, w), ((0, h.shape[0] - rows_out), (0, 0)))
        dp, dparams = mid_vjp((dmix[:, :MIX_SPLIT], dmix[:, MIX_SPLIT:]))
        da = _mm_call('nn', dp, wt)
        dh, dsh, dsc, dg_pre = nm_bwd((h,), (sh, sc), (g_pre,), (da,), add=g)
        return (dh, dsh, dsc, dg_pre, jnp.zeros_like(wt), _mm_call('tn', dp, a), dparams, jnp.zeros_like(w),
                _mm_call('tn', mix, dm), dgate, dg_post)

    mixer_block.defvjp(mixer_fwd, mixer_bwd)

    def mlp_fwd(h, sh, sc, g_pre, w1t, w1td, w2, w2d, gate, g_post):
        a = nm_fwd((h,), (sh, sc), (g_pre,), MXU_DTYPE)[0]
        zb, act = _mm_call('nt', a, w1t, sqrelu=True)
        f = _mm_call('nn', act, w2)
        out = rg_fwd((h, f), (gate,), (g_post,))[0]
        return out, (h, sh, sc, g_pre, a, w1t, w2, zb, act, f, gate, g_post)

    @jax.custom_vjp
    def mlp_block(h, sh, sc, g_pre, w1t, w1td, w2, w2d, gate, g_post):
        return mlp_fwd(h, sh, sc, g_pre, w1t, w1td, w2, w2d, gate, g_post)[0]

    def mlp_bwd(res, g):
        h, sh, sc, g_pre, a, w1t, w2, zb, act, f, gate, g_post = res
        df, dgate, dg_post = rg_bwd((h, f), (gate,), (g_post,), (g,), skip_first_row=True)
        dz = _mm_call('nt', df, w2, z=zb, out_dtype=MXU_DTYPE)
        da = _mm_call('nn', dz, w1t)
        dh, dsh, dsc, dg_pre = nm_bwd((h,), (sh, sc), (g_pre,), (da,), add=g)
        return (dh, dsh, dsc, dg_pre, jnp.zeros_like(w1t), _mm_call('tn', dz, a), jnp.zeros_like(w2),
                _mm_call('tn', act, df), dgate, dg_post)

    mlp_block.defvjp(mlp_fwd, mlp_bwd)
    return mixer_block, mlp_block


def _rowwise(name, f, out_widths, seg_len):
    def specs(rows, segs, globs, tr):
        nseg = segs[0].shape[0] if segs else 1

        def seg_of(i):
            return jnp.minimum((i * tr) // seg_len, nseg - 1)

        row_specs = [pl.BlockSpec((tr, r.shape[1]), lambda i: (i, 0)) for r in rows]
        seg_specs = [pl.BlockSpec((1, 1, s.shape[2]), lambda i: (seg_of(i), 0, 0)) for s in segs]
        glob_specs = [pl.BlockSpec(g.shape, lambda i: (0, 0)) for g in globs]
        return seg_of, row_specs, seg_specs, glob_specs

    def fwd_call(rows, segs, globs, out_dtype=F32):
        t = rows[0].shape[0]
        tr = _pick(t, (ROW_TILE, 128, 64, 32, 16, 8))
        _, row_specs, seg_specs, glob_specs = specs(rows, segs, globs, tr)
        nr, ns = len(rows), len(segs)

        def body(*refs):
            ins, outs = refs[:nr + ns + len(globs)], refs[nr + ns + len(globs):]
            vals = [r[...] for r in ins[:nr]] + [r[0] for r in ins[nr:nr + ns]] + [r[...] for r in ins[nr + ns:]]
            for o_ref, v in zip(outs, f(*vals)):
                o_ref[...] = v.astype(o_ref.dtype)

        return pl.pallas_call(
            body, name=f"{name}_fwd_{t}_{jnp.dtype(out_dtype).name}", grid=(t // tr,),
            out_shape=[SDS((t, w), out_dtype) for w in out_widths],
            in_specs=row_specs + seg_specs + glob_specs,
            out_specs=[pl.BlockSpec((tr, w), lambda i: (i, 0)) for w in out_widths],
        )(*rows, *segs, *globs)

    def bwd_call(rows, segs, globs, douts, add=None, skip_first_row=False):
        t = rows[0].shape[0]
        tr = _pick(t, (ROW_TILE, 128, 64, 32, 16, 8))
        seg_of, row_specs, seg_specs, glob_specs = specs(rows, segs, globs, tr)
        nr, ns, ng, no = len(rows), len(segs), len(globs), len(out_widths)
        n_add = 0 if add is None else add.shape[0] // tr
        first_out = 1 if skip_first_row else 0

        def body(*refs):
            ins = refs[:nr + ns + ng]
            dos = refs[nr + ns + ng:nr + ns + ng + no]
            add_ref = refs[nr + ns + ng + no] if n_add else None
            outs = (None,) * first_out + refs[nr + ns + ng + no + (1 if n_add else 0):]
            i = pl.program_id(0)
            first_of_seg = jnp.logical_or(i == 0, seg_of(i) != seg_of(jnp.maximum(i - 1, 0)))
            vals = [r[...] for r in ins[:nr]] + [r[0] for r in ins[nr:nr + ns]] + [r[...] for r in ins[nr + ns:]]
            _, vjp = jax.vjp(f, *vals)
            grads = list(vjp(tuple(d[...] for d in dos)))
            if n_add:
                grads[0] = grads[0] + (add_ref[...] if n_add * tr == t else jnp.where(i < n_add, add_ref[...], 0.0))
            for o_ref, gval in zip(outs[first_out:nr], grads[first_out:nr]):
                o_ref[...] = gval
            for o_ref, gval in zip(outs[nr:nr + ns], grads[nr:nr + ns]):
                @pl.when(first_of_seg)
                def _(o_ref=o_ref):
                    o_ref[...] = jnp.zeros_like(o_ref)
                o_ref[0] += gval
            for o_ref, gval in zip(outs[nr + ns:], grads[nr + ns:]):
                @pl.when(i == 0)
                def _(o_ref=o_ref):
                    o_ref[...] = jnp.zeros_like(o_ref)
                o_ref[...] += gval

        do_specs = [pl.BlockSpec((tr, w), lambda i: (i, 0)) for w in out_widths]
        add_specs, add_args = [], []
        if n_add:
            add_specs = [pl.BlockSpec((tr, add.shape[1]), lambda i: (jnp.minimum(i, n_add - 1), 0))]
            add_args = [add]
        tag = ('_add' if n_add else '') + ('_skip' if skip_first_row else '')
        return pl.pallas_call(
            body, name=f"{name}_bwd{tag}_{t}", grid=(t // tr,),
            out_shape=[SDS(r.shape, F32) for r in rows[first_out:]] + [SDS(s.shape, F32) for s in segs]
            + [SDS(g.shape, F32) for g in globs],
            in_specs=row_specs + seg_specs + glob_specs + do_specs + add_specs,
            out_specs=row_specs[first_out:] + seg_specs + glob_specs,
        )(*rows, *segs, *globs, *douts, *add_args)

    return fwd_call, bwd_call


def _norm_mod_f(x, shift, scale, g):
    r = lax.rsqrt(jnp.mean(x * x, axis=-1, keepdims=True) + EPS)
    return ((x * r) * g * (1.0 + scale) + shift,)


def _resid_gate_f(h, m, gate, g):
    r = lax.rsqrt(jnp.mean(m * m, axis=-1, keepdims=True) + EPS)
    return (h + gate * ((m * r) * g),)


def _glu_f(y0, y1, u, d, w, b):
    y = y0 + y1 + d * u
    g = 0.5 * y * (1.0 + jnp.tanh(0.7978845608028654 * (y + 0.044715 * (y * y * y))))
    zz = jnp.dot(_mx(g), _mx(w), preferred_element_type=F32) + b
    return (g * (1.0 / (1.0 + jnp.exp(-zz))),)


def _make_rowwise_op(name, f, out_width, n_rows, n_segs, seg_len):
    fwd_call, bwd_call = _rowwise(name, f, (out_width,), seg_len)

    @jax.custom_vjp
    def op(*args):
        return fwd_call(args[:n_rows], args[n_rows:n_rows + n_segs], args[n_rows + n_segs:])[0]

    def op_fwd(*args):
        return fwd_call(args[:n_rows], args[n_rows:n_rows + n_segs], args[n_rows + n_segs:])[0], args

    def op_bwd(args, g):
        return tuple(bwd_call(args[:n_rows], args[n_rows:n_rows + n_segs], args[n_rows + n_segs:], (g,)))

    op.defvjp(op_fwd, op_bwd)
    return op


def _lane():
    return lax.broadcasted_iota(jnp.int32, (1, LANES), 1)


def _hm(off):
    lane = _lane()
    return jnp.logical_and(lane >= off, lane < off + HEAD_DIM)


def _col(tile, hq):
    return jnp.sum(jnp.where(_lane() == hq, tile, 0.0), axis=1, keepdims=True)


def _setcol(tile, hq, col):
    return jnp.where(_lane() == hq, col, tile)


def _head_fwd(qa, groups, sk):
    ss, m = [], None
    for kmat, _, mask in groups:
        s = _dg_nt(qa, kmat) * _SCALE
        if mask is not None:
            s = jnp.where(mask, s, NEG_INF)
        ss.append(s)
        mm = jnp.max(s, axis=1, keepdims=True)
        m = mm if m is None else jnp.maximum(m, mm)
    m = jnp.maximum(m, sk)
    l = jnp.exp(sk - m)
    pv = None
    for s, (_, vmat, _) in zip(ss, groups):
        p = jnp.exp(s - m)
        l = l + jnp.sum(p, axis=1, keepdims=True)
        t = jnp.dot(_mx(p), vmat, preferred_element_type=F32)
        pv = t if pv is None else pv + t
    return pv / l, m + jnp.log(l)


def _head_bwd(qa, doa, groups, sk, lse_h, delta_h):
    dq, outs = None, []
    for kmat, vmat, mask in groups:
        s = _dg_nt(qa, kmat) * _SCALE
        if mask is not None:
            s = jnp.where(mask, s, NEG_INF)
        p = jnp.exp(s - lse_h)
        dp = _dg_nt(doa, vmat)
        ds = p * (dp - delta_h) * _SCALE
        t = jnp.dot(_mx(ds), kmat, preferred_element_type=F32)
        dq = t if dq is None else dq + t
        outs.append((_mx(ds), _mx(p)))
    return dq, outs, jnp.exp(sk - lse_h)


def _kv_group(h):
    return [(g, Q_PER_KV * h + g, (Q_PER_KV * h + g) // 2, (g % 2) != h) for g in range(Q_PER_KV)]


def _to_kv_lanes(pair_tile, roll):
    if not roll:
        return _mx(pair_tile)
    return _mx(pltpu.roll(pair_tile.astype(F32), HEAD_DIM, 1))


def _attn_stack(n):
    return max(1, min(Q_PER_KV, ATTN_STACK_ROWS // n))


def _kv_masked(x, h):
    return _mx(jnp.where(_hm(h * HEAD_DIM), x.astype(F32), 0.0))


def _attn_fwd_block(sink_ref, q_ref, kv, n, o_ref, lse_ref):
    lse_t = jnp.zeros((n, LANES), F32)
    stack = _attn_stack(n)
    for h in range(2):
        groups = [(_kv_masked(k, h), _kv_masked(v, h), mask) for k, v, mask in kv]
        heads = _kv_group(h)
        o_pair = None
        for first in range(0, Q_PER_KV, stack):
            part = heads[first:first + stack]
            q = jnp.concatenate([_to_kv_lanes(q_ref[:, pair * LANES:(pair + 1) * LANES], roll)
                                 for _, _, pair, roll in part], axis=0)
            sk = jnp.concatenate([jnp.full((n, 1), sink_ref[0, hq], F32) for _, hq, _, _ in part], axis=0)
            o, lse = _head_fwd(q, groups, sk)
            for idx, (g, hq, pair, roll) in enumerate(part):
                og = o[idx * n:(idx + 1) * n]
                og = pltpu.roll(og, HEAD_DIM, 1) if roll else og
                lse_t = _setcol(lse_t, hq, lse[idx * n:(idx + 1) * n])
                if g % 2 == 0:
                    o_pair = og
                else:
                    o_ref[:, pair * LANES:(pair + 1) * LANES] = o_pair + og
    lse_ref[...] = lse_t


def _attn_bwd_block(sink_ref, q_ref, do_ref, o_ref, lse_t, kv, want, n, dq_ref):
    delta_t = jnp.zeros((n, LANES), F32)
    dsk = jnp.zeros((1, LANES), F32)
    dkv = [None] * len(kv)
    stack = _attn_stack(n)
    for h in range(2):
        hm = _hm(h * HEAD_DIM)
        groups = [(_kv_masked(k, h), _kv_masked(v, h), mask) for k, v, mask in kv]
        heads = _kv_group(h)
        dq_pair = None
        for first in range(0, Q_PER_KV, stack):
            part = heads[first:first + stack]
            qs, dos, sks, lses, deltas = [], [], [], [], []
            for g, hq, pair, roll in part:
                sl = slice(pair * LANES, (pair + 1) * LANES)
                do_p = do_ref[:, sl]
                delta_h = jnp.sum(jnp.where(_hm((hq % 2) * HEAD_DIM), do_p * o_ref[:, sl], 0.0), axis=1,
                                  keepdims=True)
                delta_t = _setcol(delta_t, hq, delta_h)
                qs.append(_to_kv_lanes(q_ref[:, sl], roll))
                dos.append(_to_kv_lanes(do_p, roll))
                sks.append(jnp.full((n, 1), sink_ref[0, hq], F32))
                lses.append(_col(lse_t, hq))
                deltas.append(delta_h)
            q, do = jnp.concatenate(qs, axis=0), jnp.concatenate(dos, axis=0)
            dq, outs, p_s = _head_bwd(q, do, groups, jnp.concatenate(sks, axis=0), jnp.concatenate(lses, axis=0),
                                      jnp.concatenate(deltas, axis=0))
            for idx, (g, hq, pair, roll) in enumerate(part):
                dg = dq[idx * n:(idx + 1) * n]
                dg = pltpu.roll(dg, HEAD_DIM, 1) if roll else dg
                dsk = dsk + jnp.where(_lane() == hq,
                                      jnp.sum(-p_s[idx * n:(idx + 1) * n] * deltas[idx], axis=0, keepdims=True), 0.0)
                if g % 2 == 0:
                    dq_pair = dg
                else:
                    dq_ref[:, pair * LANES:(pair + 1) * LANES] = dq_pair + dg
            for gi, (ds, pp) in enumerate(outs):
                if want[gi]:
                    dk_h = jnp.where(hm, _dg_tn(ds, q), 0.0)
                    dv_h = jnp.where(hm, _dg_tn(pp, do), 0.0)
                    dkv[gi] = (dk_h, dv_h) if dkv[gi] is None else (dkv[gi][0] + dk_h, dkv[gi][1] + dv_h)
    return delta_t, dsk, dkv


def _partner(t):
    lane = lax.broadcasted_iota(jnp.int32, t.shape, 1)
    return jnp.where((lane & ROPE_PAIRS) == 0, pltpu.roll(t, LANES - ROPE_PAIRS, 1), pltpu.roll(t, ROPE_PAIRS, 1))


def _rope_tables(n_tokens):
    rows = n_tokens // GRID_W
    row = jnp.broadcast_to(jnp.arange(rows)[:, None], (rows, GRID_W)).reshape(-1)
    col = jnp.broadcast_to(jnp.arange(GRID_W)[None, :], (rows, GRID_W)).reshape(-1)
    freqs = ROPE_BASE ** (-jnp.arange(ROPE_PAIRS, dtype=F32) / ROPE_PAIRS)
    ang = jnp.concatenate([row[:, None].astype(F32) * freqs, col[:, None].astype(F32) * freqs], axis=-1)
    c, s = jnp.cos(ang), jnp.sin(ang)
    n = ROPE_PAIRS
    cos64 = jnp.concatenate([c[:, :n], c[:, :n], c[:, n:], c[:, n:]], axis=1)
    sin64 = jnp.concatenate([-s[:, :n], s[:, :n], -s[:, n:], s[:, n:]], axis=1)
    return jnp.tile(cos64, (1, 2)), jnp.tile(sin64, (1, 2))


def _rope_call(q_src, k_src, cos, sin, dims, inverse):
    b, l, _ = dims
    t_lat = b * l
    tr = _pick(l, (ROW_TILE, 128))
    per = l // tr
    out_dtype = F32 if inverse else MXU_DTYPE

    def body(q_ref, k_ref, c_ref, s_ref, qo_ref, ko_ref):
        c, s = c_ref[...], s_ref[...]

        def rot(t):
            if inverse:
                return t * c + _partner(t * s)
            return t * c + _partner(t) * s

        for j in range(ATTN_WIDTH // LANES):
            qo_ref[:, j * LANES:(j + 1) * LANES] = rot(q_ref[:, j * LANES:(j + 1) * LANES]).astype(out_dtype)
        ko_ref[...] = rot(k_ref[...]).astype(out_dtype)

    kcol = 0 if inverse else ATTN_WIDTH // KV_WIDTH
    return pl.pallas_call(
        body, name=f"rope_{'inv' if inverse else 'fwd'}_{t_lat}", grid=(t_lat // tr,),
        out_shape=[SDS((t_lat, ATTN_WIDTH), out_dtype), SDS((t_lat, KV_WIDTH), out_dtype)],
        in_specs=[pl.BlockSpec((tr, ATTN_WIDTH), lambda i: (i, 0)),
                  pl.BlockSpec((tr, KV_WIDTH), lambda i: (i, kcol)),
                  pl.BlockSpec((tr, LANES), lambda i: (i % per, 0)),
                  pl.BlockSpec((tr, LANES), lambda i: (i % per, 0))],
        out_specs=[pl.BlockSpec((tr, ATTN_WIDTH), lambda i: (i, 0)), pl.BlockSpec((tr, KV_WIDTH), lambda i: (i, 0))],
    )(q_src, k_src, cos, sin)


_SMEM_SPEC = pl.BlockSpec(memory_space=pltpu.SMEM)
_KCOL = ATTN_WIDTH // KV_WIDTH
_VCOL = _KCOL + 1


def _win_specs(arr_cols, nb, col):
    del arr_cols
    return [pl.BlockSpec((Q_BLOCK, KV_WIDTH), lambda b, i, d=d: (b * nb + jnp.clip(i + d, 0, nb - 1), col))
            for d in (-1, 0, 1)]


def _win_mask(i, l):
    shape = (_attn_stack(Q_BLOCK) * Q_BLOCK, 3 * Q_BLOCK)
    qpos = i * Q_BLOCK + (lax.broadcasted_iota(jnp.int32, shape, 0) & (Q_BLOCK - 1))
    kpos = (i - 1) * Q_BLOCK + lax.broadcasted_iota(jnp.int32, shape, 1)
    return jnp.logical_and(jnp.abs(qpos - kpos) <= WINDOW, jnp.logical_and(kpos >= 0, kpos < l))


def _attn_lat_fwd(qr, kr, p, sink, dims):
    b, l, lc = dims
    nb = l // Q_BLOCK
    t_lat = b * l
    cbase = t_lat // lc

    def body(sink_ref, q_ref, kp, kc, kn, vp, vc, vn, ck_ref, cv_ref, o_ref, lse_ref):
        kwin = jnp.concatenate([kp[...], kc[...], kn[...]], axis=0)
        vwin = jnp.concatenate([vp[...], vc[...], vn[...]], axis=0)
        kv = [(kwin, vwin, _win_mask(pl.program_id(1), l)), (ck_ref[...], cv_ref[...], None)]
        _attn_fwd_block(sink_ref, q_ref, kv, Q_BLOCK, o_ref, lse_ref)

    return pl.pallas_call(
        body, name=f"attn_lat_fwd_{t_lat}", grid=(b, nb),
        out_shape=[SDS((t_lat, ATTN_WIDTH), F32), SDS((t_lat, LANES), F32)],
        in_specs=[_SMEM_SPEC, pl.BlockSpec((Q_BLOCK, ATTN_WIDTH), lambda bb, i: (bb * nb + i, 0))]
        + _win_specs(None, nb, 0) + _win_specs(None, nb, _VCOL)
        + [pl.BlockSpec((lc, KV_WIDTH), lambda bb, i: (cbase + bb, _KCOL)),
           pl.BlockSpec((lc, KV_WIDTH), lambda bb, i: (cbase + bb, _VCOL))],
        out_specs=[pl.BlockSpec((Q_BLOCK, ATTN_WIDTH), lambda bb, i: (bb * nb + i, 0)),
                   pl.BlockSpec((Q_BLOCK, LANES), lambda bb, i: (bb * nb + i, 0))],
    )(sink, qr, kr, kr, kr, p, p, p, p, p)


def _attn_lat_bwd_dq(qr, kr, p, sink, o, lse, dout, dims):
    b, l, lc = dims
    nb = l // Q_BLOCK
    t_lat = b * l
    cbase = t_lat // lc

    def body(sink_ref, q_ref, kp, kc, kn, vp, vc, vn, ck_ref, cv_ref, o_ref, lse_ref, do_ref,
             dq_ref, delta_ref, dkc_ref, dvc_ref, dsk_ref):
        bb, i = pl.program_id(0), pl.program_id(1)

        @pl.when(i == 0)
        def _():
            dkc_ref[...] = jnp.zeros_like(dkc_ref)
            dvc_ref[...] = jnp.zeros_like(dvc_ref)

        @pl.when(jnp.logical_and(bb == 0, i == 0))
        def _():
            dsk_ref[...] = jnp.zeros_like(dsk_ref)

        kwin = jnp.concatenate([kp[...], kc[...], kn[...]], axis=0)
        vwin = jnp.concatenate([vp[...], vc[...], vn[...]], axis=0)
        kv = [(kwin, vwin, _win_mask(i, l)), (ck_ref[...], cv_ref[...], None)]
        delta_t, dsk, dkv = _attn_bwd_block(sink_ref, q_ref, do_ref, o_ref, lse_ref[...], kv, [False, True], Q_BLOCK,
                                            dq_ref)
        delta_ref[...] = delta_t
        dkc_ref[...] += dkv[1][0]
        dvc_ref[...] += dkv[1][1]
        dsk_ref[0:1, :] += dsk

    qspec = pl.BlockSpec((Q_BLOCK, ATTN_WIDTH), lambda bb, i: (bb * nb + i, 0))
    tspec = pl.BlockSpec((Q_BLOCK, LANES), lambda bb, i: (bb * nb + i, 0))
    cspec = pl.BlockSpec((lc, KV_WIDTH), lambda bb, i: (bb, 0))
    return pl.pallas_call(
        body, name=f"attn_lat_bwd_dq_{t_lat}", grid=(b, nb),
        out_shape=[SDS((t_lat, ATTN_WIDTH), F32), SDS((t_lat, LANES), F32), SDS((b * lc, KV_WIDTH), F32),
                   SDS((b * lc, KV_WIDTH), F32), SDS((8, LANES), F32)],
        in_specs=[_SMEM_SPEC, qspec] + _win_specs(None, nb, 0) + _win_specs(None, nb, _VCOL)
        + [pl.BlockSpec((lc, KV_WIDTH), lambda bb, i: (cbase + bb, _KCOL)),
           pl.BlockSpec((lc, KV_WIDTH), lambda bb, i: (cbase + bb, _VCOL)), qspec, tspec, qspec],
        out_specs=[qspec, tspec, cspec, cspec, pl.BlockSpec((8, LANES), lambda bb, i: (0, 0))],
    )(sink, qr, kr, kr, kr, p, p, p, p, p, o, lse, dout)


def _attn_lat_bwd_dkv(qr, kr, p, lse, delta, dout, dims):
    b, l, _ = dims
    nb = l // Q_BLOCK
    t_lat = b * l

    def body(k_ref, v_ref, *refs):
        j = pl.program_id(1)
        kj, vj = k_ref[...], v_ref[...]
        n_q = 3 * Q_PER_KV * Q_BLOCK
        sub = lax.broadcasted_iota(jnp.int32, (8, LANES), 0)
        col = lax.broadcasted_iota(jnp.int32, (Q_BLOCK, n_q), 1)
        i_of = j + col // (Q_PER_KV * Q_BLOCK) - 1
        qpos = i_of * Q_BLOCK + (col & (Q_BLOCK - 1))
        kpos = j * Q_BLOCK + lax.broadcasted_iota(jnp.int32, (Q_BLOCK, n_q), 0)
        mask = jnp.logical_and(jnp.abs(qpos - kpos) <= WINDOW, jnp.logical_and(i_of >= 0, i_of < nb))
        lse_rows = [refs[4 * n + 2][...].T[0:8, :] for n in range(3)]
        delta_rows = [refs[4 * n + 3][...].T[0:8, :] for n in range(3)]
        dk = jnp.zeros((Q_BLOCK, KV_WIDTH), F32)
        dv = jnp.zeros((Q_BLOCK, KV_WIDTH), F32)
        for h in range(2):
            hm = _hm(h * HEAD_DIM)
            kh, vh = _kv_masked(kj, h), _kv_masked(vj, h)
            qs, dos, lrow, drow = [], [], [], []
            for n in range(3):
                q_ref, do_ref = refs[4 * n], refs[4 * n + 1]
                for _, hq, pair, roll in _kv_group(h):
                    sl = slice(pair * LANES, (pair + 1) * LANES)
                    qs.append(_to_kv_lanes(q_ref[:, sl], roll))
                    dos.append(_to_kv_lanes(do_ref[:, sl], roll))
                    lrow.append(jnp.sum(jnp.where(sub == hq, lse_rows[n], 0.0), axis=0, keepdims=True))
                    drow.append(jnp.sum(jnp.where(sub == hq, delta_rows[n], 0.0), axis=0, keepdims=True))
            q, do = jnp.concatenate(qs, axis=0), jnp.concatenate(dos, axis=0)
            lse_r, delta_r = jnp.concatenate(lrow, axis=1), jnp.concatenate(drow, axis=1)
            s_t = jnp.where(mask, _dg_nt(kh, q) * _SCALE, NEG_INF)
            p_t = jnp.exp(s_t - lse_r)
            ds_t = p_t * (_dg_nt(vh, do) - delta_r) * _SCALE
            dv = dv + jnp.where(hm, jnp.dot(_mx(p_t), do, preferred_element_type=F32), 0.0)
            dk = dk + jnp.where(hm, jnp.dot(_mx(ds_t), q, preferred_element_type=F32), 0.0)
        dk_ref, dv_ref = refs[12], refs[13]
        dk_ref[...] = dk
        dv_ref[...] = dv

    def blk(width, d, col=0):
        return pl.BlockSpec((Q_BLOCK, width), lambda bb, j: (bb * nb + jnp.clip(j + d, 0, nb - 1), col))

    in_specs = [blk(KV_WIDTH, 0), blk(KV_WIDTH, 0, _VCOL)]
    args = [kr, p]
    for d in (-1, 0, 1):
        in_specs += [blk(ATTN_WIDTH, d), blk(ATTN_WIDTH, d), blk(LANES, d), blk(LANES, d)]
        args += [qr, dout, lse, delta]
    return pl.pallas_call(
        body, name=f"attn_lat_bwd_dkv_{t_lat}", grid=(b, nb),
        out_shape=[SDS((t_lat, KV_WIDTH), F32), SDS((t_lat, KV_WIDTH), F32)],
        in_specs=in_specs, out_specs=[blk(KV_WIDTH, 0), blk(KV_WIDTH, 0)],
    )(*args)


def _attn_ctx_fwd(p, sink, dims):
    b, l, lc = dims
    cbase = b * l // lc

    def body(sink_ref, q_ref, k_ref, v_ref, o_ref, lse_ref):
        _attn_fwd_block(sink_ref, q_ref, [(k_ref[...], v_ref[...], None)], lc, o_ref, lse_ref)

    return pl.pallas_call(
        body, name=f"attn_ctx_fwd_{b * lc}", grid=(b,),
        out_shape=[SDS((b * lc, ATTN_WIDTH), F32), SDS((b * lc, LANES), F32)],
        in_specs=[_SMEM_SPEC, pl.BlockSpec((lc, ATTN_WIDTH), lambda bb: (cbase + bb, 0)),
                  pl.BlockSpec((lc, KV_WIDTH), lambda bb: (cbase + bb, _KCOL)),
                  pl.BlockSpec((lc, KV_WIDTH), lambda bb: (cbase + bb, _VCOL))],
        out_specs=[pl.BlockSpec((lc, ATTN_WIDTH), lambda bb: (bb, 0)), pl.BlockSpec((lc, LANES), lambda bb: (bb, 0))],
    )(sink, p, p, p)


def _attn_ctx_bwd(p, sink, o, lse, dout, dims):
    b, l, lc = dims
    cbase = b * l // lc

    def body(sink_ref, q_ref, k_ref, v_ref, o_ref, lse_ref, do_ref, dq_ref, dk_ref, dv_ref, dsk_ref):
        bb = pl.program_id(0)

        @pl.when(bb == 0)
        def _():
            dsk_ref[...] = jnp.zeros_like(dsk_ref)

        _, dsk, dkv = _attn_bwd_block(sink_ref, q_ref, do_ref, o_ref, lse_ref[...], [(k_ref[...], v_ref[...], None)],
                                      [True], lc, dq_ref)
        dk_ref[...] = dkv[0][0]
        dv_ref[...] = dkv[0][1]
        dsk_ref[0:1, :] += dsk

    qs = pl.BlockSpec((lc, ATTN_WIDTH), lambda bb: (bb, 0))
    ks = pl.BlockSpec((lc, KV_WIDTH), lambda bb: (bb, 0))
    return pl.pallas_call(
        body, name=f"attn_ctx_bwd_{b * lc}", grid=(b,),
        out_shape=[SDS((b * lc, ATTN_WIDTH), F32), SDS((b * lc, KV_WIDTH), F32), SDS((b * lc, KV_WIDTH), F32),
                   SDS((8, LANES), F32)],
        in_specs=[_SMEM_SPEC, pl.BlockSpec((lc, ATTN_WIDTH), lambda bb: (cbase + bb, 0)),
                  pl.BlockSpec((lc, KV_WIDTH), lambda bb: (cbase + bb, _KCOL)),
                  pl.BlockSpec((lc, KV_WIDTH), lambda bb: (cbase + bb, _VCOL)),
                  qs, pl.BlockSpec((lc, LANES), lambda bb: (bb, 0)),
                  pl.BlockSpec((lc, ATTN_WIDTH), lambda bb: (cbase + bb, 0))],
        out_specs=[qs, ks, ks, pl.BlockSpec((8, LANES), lambda bb: (0, 0))],
    )(sink, p, p, p, o, lse, dout)


_CONV_COL = (ATTN_WIDTH + 2 * KV_WIDTH) // CONV_WIDTH


def _shift_prev(z, n):
    rows = lax.broadcasted_iota(jnp.int32, z.shape, 0)
    return jnp.where(rows == 0, 0.0, pltpu.roll(z, 1, 0))


def _shift_next(z, n):
    rows = lax.broadcasted_iota(jnp.int32, z.shape, 0)
    return jnp.where(rows == n - 1, 0.0, pltpu.roll(z, n - 1, 0))


def _conv_fwd(p, w, base, n_seq, ls):
    def body(cb_ref, cc_ref, cx_ref, w_ref, o_ref):
        z = cc_ref[...] * cx_ref[...]
        c3 = _shift_prev(z, ls) * w_ref[0:1, :] + z * w_ref[1:2, :] + _shift_next(z, ls) * w_ref[2:3, :]
        o_ref[...] = cb_ref[...] * c3

    return pl.pallas_call(
        body, name=f"conv_fwd_{n_seq}x{ls}", grid=(n_seq,),
        out_shape=SDS((n_seq * ls, CONV_WIDTH), F32),
        in_specs=[pl.BlockSpec((ls, CONV_WIDTH), lambda s, c=c: (base + s, _CONV_COL + c)) for c in range(3)]
        + [pl.BlockSpec(w.shape, lambda s: (0, 0))],
        out_specs=pl.BlockSpec((ls, CONV_WIDTH), lambda s: (s, 0)),
    )(p, p, p, w)


def _conv_bwd(p, w, dout, base, n_seq, ls):
    dcol = ATTN_WIDTH // CONV_WIDTH

    def body(cb_ref, cc_ref, cx_ref, w_ref, do_ref, dcb_ref, dcc_ref, dcx_ref, dw_ref):
        @pl.when(pl.program_id(0) == 0)
        def _():
            dw_ref[...] = jnp.zeros_like(dw_ref)

        cc, cx = cc_ref[...], cx_ref[...]
        z = cc * cx
        zp, zn = _shift_prev(z, ls), _shift_next(z, ls)
        c3 = zp * w_ref[0:1, :] + z * w_ref[1:2, :] + zn * w_ref[2:3, :]
        do = do_ref[...]
        dcb_ref[...] = do * c3
        e = do * cb_ref[...]
        dz = _shift_next(e, ls) * w_ref[0:1, :] + e * w_ref[1:2, :] + _shift_prev(e, ls) * w_ref[2:3, :]
        dcc_ref[...] = dz * cx
        dcx_ref[...] = dz * cc
        dw_ref[0:1, :] += jnp.sum(e * zp, axis=0, keepdims=True)
        dw_ref[1:2, :] += jnp.sum(e * z, axis=0, keepdims=True)
        dw_ref[2:3, :] += jnp.sum(e * zn, axis=0, keepdims=True)

    ospec = pl.BlockSpec((ls, CONV_WIDTH), lambda s: (s, 0))
    return pl.pallas_call(
        body, name=f"conv_bwd_{n_seq}x{ls}", grid=(n_seq,),
        out_shape=[SDS((n_seq * ls, CONV_WIDTH), F32)] * 3 + [SDS((8, CONV_WIDTH), F32)],
        in_specs=[pl.BlockSpec((ls, CONV_WIDTH), lambda s, c=c: (base + s, _CONV_COL + c)) for c in range(3)]
        + [pl.BlockSpec(w.shape, lambda s: (0, 0)), pl.BlockSpec((ls, CONV_WIDTH), lambda s: (base + s, dcol))],
        out_specs=[ospec, ospec, ospec, pl.BlockSpec((8, CONV_WIDTH), lambda s: (0, 0))],
    )(p, p, p, w, dout)


def _make_att_conv(dims, cos, sin):
    b, l, lc = dims
    t_lat = b * l

    def forward(p, conv_w, sink):
        qr, kr = _rope_call(p, p, cos, sin, dims, inverse=False)
        o_lat, lse_lat = _attn_lat_fwd(qr, kr, p, sink, dims)
        o_ctx, lse_ctx = _attn_ctx_fwd(p, sink, dims)
        conv_lat = _conv_fwd(p, conv_w, 0, b, l)
        conv_ctx = _conv_fwd(p, conv_w, t_lat // lc, b, lc)
        out = jnp.concatenate([jnp.concatenate([o_lat, conv_lat], axis=1),
                               jnp.concatenate([o_ctx, conv_ctx], axis=1)], axis=0)
        return out, (p, conv_w, sink, qr, kr, o_lat, lse_lat, o_ctx, lse_ctx)

    @jax.custom_vjp
    def op(p, conv_w, sink):
        return forward(p, conv_w, sink)[0]

    def op_bwd(res, dout):
        p, conv_w, sink, qr, kr, o_lat, lse_lat, o_ctx, lse_ctx = res
        dqr, delta, dkc1, dvc1, dsk1 = _attn_lat_bwd_dq(qr, kr, p, sink, o_lat, lse_lat, dout, dims)
        dkr, dv = _attn_lat_bwd_dkv(qr, kr, p, lse_lat, delta, dout, dims)
        dq, dk = _rope_call(dqr, dkr, cos, sin, dims, inverse=True)
        dqc, dkc2, dvc2, dsk2 = _attn_ctx_bwd(p, sink, o_ctx, lse_ctx, dout, dims)
        dcb_l, dcc_l, dcx_l, dw_l = _conv_bwd(p, conv_w, dout, 0, b, l)
        dcb_c, dcc_c, dcx_c, dw_c = _conv_bwd(p, conv_w, dout, t_lat // lc, b, lc)
        zeros_u = jnp.zeros((p.shape[0], SSM_WIDTH), F32)
        lat = jnp.concatenate([dq, dk, dv, dcb_l, dcc_l, dcx_l], axis=1)
        ctx = jnp.concatenate([dqc, dkc1 + dkc2, dvc1 + dvc2, dcb_c, dcc_c, dcx_c], axis=1)
        dp = jnp.concatenate([jnp.concatenate([lat, ctx], axis=0), zeros_u], axis=1)
        dsink = (dsk1 + dsk2)[0:1, :N_Q_HEADS]
        return dp, (dw_l + dw_c)[:3], dsink

    op.defvjp(forward, op_bwd)
    return op


def _cmul(ar, ai, br, bi):
    return ar * br - ai * bi, ar * bi + ai * br


def _scan_chunk(buf, lr_ref, li_ref, carry, dec, tc):
    sub = lax.broadcasted_iota(jnp.int32, (8, LANES), 0)
    groups = range(tc // 8 - 1, -1, -1) if dec else range(tc // 8)
    for s in range(SSM_N // LANES):
        re, im = slice(s * LANES, (s + 1) * LANES), slice(SSM_N + s * LANES, SSM_N + (s + 1) * LANES)
        pw = [(lr_ref[:, re], li_ref[:, re])]
        for _ in range(7):
            pw.append(_cmul(*pw[-1], *pw[0]))
        p8r, p8i = jnp.zeros((8, LANES), F32), jnp.zeros((8, LANES), F32)
        for k in range(8):
            row = 7 - k if dec else k
            p8r = jnp.where(sub == row, pw[k][0], p8r)
            p8i = jnp.where(sub == row, pw[k][1], p8i)
        steps = []
        for sft in (1, 2, 4):
            keep = sub < 8 - sft if dec else sub >= sft
            steps.append((8 - sft if dec else sft, jnp.where(keep, pw[sft - 1][0], 0.0),
                          jnp.where(keep, pw[sft - 1][1], 0.0)))
        cr, ci = carry[0:1, re], carry[0:1, im]
        for r in groups:
            rows = slice(8 * r, 8 * r + 8)
            xr, xi = buf[rows, re], buf[rows, im]
            for amount, mr, mi in steps:
                sr, si = pltpu.roll(xr, amount, 0), pltpu.roll(xi, amount, 0)
                xr, xi = xr + mr * sr - mi * si, xi + mr * si + mi * sr
            xr, xi = xr + p8r * cr - p8i * ci, xi + p8r * ci + p8i * cr
            buf[rows, re] = xr
            buf[rows, im] = xi
            last = 8 * r if dec else 8 * r + 7
            cr, ci = buf[last:last + 1, re], buf[last:last + 1, im]
        carry[0:1, re] = cr
        carry[0:1, im] = ci


def _ssm_chunks(dims, ctx_first, dec):
    b, l, lc = dims
    tc = _pick(lc, (256, 128, 64, 32, 16, 8))
    nc_c, nc_l = lc // tc, l // tc
    n_chunks = nc_c + nc_l
    ctx_base = b * l // tc
    g8 = tc // 8

    def chunk(bb, j):
        if ctx_first:
            is_ctx = j < nc_c
            jj = jnp.where(is_ctx, j, j - nc_c)
        else:
            is_ctx = j >= nc_l
            jj = jnp.where(is_ctx, j - nc_l, j)
        ic = nc_c - 1 - jj if dec else jj
        il = nc_l - 1 - jj if dec else jj
        return jnp.where(is_ctx, ctx_base + bb * nc_c + ic, bb * nc_l + il)

    def edge(bb, j):
        jn = jnp.minimum(j + 1, n_chunks - 1)
        return chunk(bb, jn) * g8 + (g8 - 1 if dec else 0)

    return tc, n_chunks, chunk, edge


def _ssm_fwd_call(u, bmat, cmat, lam_r, lam_i, dims, dec):
    n2 = 2 * SSM_N
    tc, n_chunks, chunk, _ = _ssm_chunks(dims, True, dec)

    def body(u_ref, b_ref, c_ref, lr_ref, li_ref, h_ref, y_ref, carry):
        @pl.when(pl.program_id(1) == 0)
        def _():
            carry[...] = jnp.zeros_like(carry)

        h_ref[...] = jnp.dot(_mx(u_ref[...]), b_ref[...], preferred_element_type=F32)
        _scan_chunk(h_ref, lr_ref, li_ref, carry, dec, tc)
        y_ref[...] = jnp.dot(_mx(h_ref[...]), c_ref[...], preferred_element_type=F32)

    def full(shape):
        return pl.BlockSpec(shape, lambda bb, j: (0, 0))

    return pl.pallas_call(
        body, name=f"ssm_fwd_{'d' if dec else 'u'}_{u.shape[0]}", grid=(dims[0], n_chunks),
        out_shape=[SDS((u.shape[0], n2), F32), SDS(u.shape, F32)],
        in_specs=[pl.BlockSpec((tc, SSM_WIDTH), lambda bb, j: (chunk(bb, j), 0)), full((SSM_WIDTH, n2)),
                  full((n2, SSM_WIDTH)), full((1, SSM_N)), full((1, SSM_N))],
        out_specs=[pl.BlockSpec((tc, n2), lambda bb, j: (chunk(bb, j), 0)),
                   pl.BlockSpec((tc, SSM_WIDTH), lambda bb, j: (chunk(bb, j), 0))],
        scratch_shapes=[pltpu.VMEM((8, n2), F32)],
    )(u, bmat, cmat, lam_r, lam_i)


def _ssm_bwd_call(gy, h, u, bmat, cmat, lam_r, lam_i, dims, dec):
    n2 = 2 * SSM_N
    tc, n_chunks, chunk, edge = _ssm_chunks(dims, False, dec)

    def body(gy_ref, h_ref, hb_ref, u_ref, b_ref, c_ref, lr_ref, li_ref, du_ref, db_ref, dct_ref, dl_ref, gd, carry):
        bb, j = pl.program_id(0), pl.program_id(1)

        @pl.when(j == 0)
        def _():
            carry[...] = jnp.zeros_like(carry)

        @pl.when(jnp.logical_and(bb == 0, j == 0))
        def _():
            db_ref[...] = jnp.zeros_like(db_ref)
            dct_ref[...] = jnp.zeros_like(dct_ref)
            dl_ref[...] = jnp.zeros_like(dl_ref)

        gyv = _mx(gy_ref[...])
        gd[...] = _dg_nt(gyv, c_ref[...])
        _scan_chunk(gd, lr_ref, li_ref, carry, dec, tc)
        gdv = _mx(gd[...])
        du_ref[...] = _dg_nt(gdv, b_ref[...])
        db_ref[...] += _dg_tn(_mx(u_ref[...]), gdv)
        dct_ref[...] += _dg_tn(gyv, _mx(h_ref[...]))
        rows = lax.broadcasted_iota(jnp.int32, (tc, LANES), 0)
        has_next = j + 1 < n_chunks
        er = 7 if dec else 0
        for s in range(SSM_N // LANES):
            re, im = slice(s * LANES, (s + 1) * LANES), slice(SSM_N + s * LANES, SSM_N + (s + 1) * LANES)
            dr, di, hr, hi = gd[:, re], gd[:, im], h_ref[:, re], h_ref[:, im]
            br = jnp.where(has_next, hb_ref[er:er + 1, re], 0.0)
            bi = jnp.where(has_next, hb_ref[er:er + 1, im], 0.0)
            if dec:
                nr = jnp.where(rows == 0, br, pltpu.roll(hr, 1, 0))
                ni = jnp.where(rows == 0, bi, pltpu.roll(hi, 1, 0))
            else:
                nr = jnp.where(rows == tc - 1, br, pltpu.roll(hr, tc - 1, 0))
                ni = jnp.where(rows == tc - 1, bi, pltpu.roll(hi, tc - 1, 0))
            dl_ref[0:1, re] += jnp.sum(dr * nr + di * ni, axis=0, keepdims=True)
            dl_ref[0:1, im] += jnp.sum(di * nr - dr * ni, axis=0, keepdims=True)

    def full(shape):
        return pl.BlockSpec(shape, lambda bb, j: (0, 0))

    def at_chunk(width):
        return pl.BlockSpec((tc, width), lambda bb, j: (chunk(bb, j), 0))

    return pl.pallas_call(
        body, name=f"ssm_bwd_{'d' if dec else 'u'}_{u.shape[0]}", grid=(dims[0], n_chunks),
        out_shape=[SDS(u.shape, F32), SDS((SSM_WIDTH, n2), F32), SDS((SSM_WIDTH, n2), F32), SDS((8, n2), F32)],
        in_specs=[at_chunk(SSM_WIDTH), at_chunk(n2), pl.BlockSpec((8, n2), lambda bb, j: (edge(bb, j), 0)),
                  at_chunk(SSM_WIDTH), full((SSM_WIDTH, n2)), full((n2, SSM_WIDTH)), full((1, SSM_N)),
                  full((1, SSM_N))],
        out_specs=[at_chunk(SSM_WIDTH), full((SSM_WIDTH, n2)), full((SSM_WIDTH, n2)), full((8, n2))],
        scratch_shapes=[pltpu.VMEM((tc, n2), F32), pltpu.VMEM((8, n2), F32)],
    )(gy, h, h, u, bmat, cmat, lam_r, lam_i)


def _make_ssm(dims, direction):
    @jax.custom_vjp
    def ssm(u, bmat, cmat, lam_r, lam_i):
        return _ssm_fwd_call(u, _mx(bmat), _mx(cmat), lam_r, lam_i, dims, direction == 1)[1]

    def ssm_fwd(u, bmat, cmat, lam_r, lam_i):
        h, y = _ssm_fwd_call(u, _mx(bmat), _mx(cmat), lam_r, lam_i, dims, direction == 1)
        return y, (u, bmat, cmat, lam_r, lam_i, h)

    def ssm_bwd(res, gy):
        u, bmat, cmat, lam_r, lam_i, h = res
        du, db, dct, dl = _ssm_bwd_call(gy, h, u, _mx(bmat), _mx(cmat), lam_r, -lam_i, dims, direction == 0)
        return du, db, dct.T, dl[0:1, :SSM_N], dl[0:1, SSM_N:]

    ssm.defvjp(ssm_fwd, ssm_bwd)
    return ssm


def _block_diag(m):
    g, a, bdim = m.shape[-3:]
    eye = jnp.eye(g, dtype=m.dtype)
    full = m[..., :, :, None, :] * eye[:, None, :, None]
    return full.reshape(m.shape[:-3] + (g * a, g * bdim))


def _ssm_mats(lam_re, lam_im, log_dt, b_re, b_im, c_re, c_im):
    lam = lax.complex(lam_re, lam_im)
    dt = jnp.exp(log_dt)[..., None]
    lam_bar = jnp.exp(lam * dt)
    b_bar = ((lam_bar - 1) / lam)[..., None] * lax.complex(b_re, b_im)
    bt = jnp.swapaxes(b_bar, -1, -2)
    bmat = jnp.concatenate([_block_diag(jnp.real(bt)), _block_diag(jnp.imag(bt))], axis=-1)
    cmat = jnp.concatenate([_block_diag(jnp.swapaxes(c_re, -1, -2)), _block_diag(-jnp.swapaxes(c_im, -1, -2))],
                           axis=-2)
    flat = lam_bar.shape[:2] + (1, SSM_N)
    return bmat, cmat, jnp.real(lam_bar).reshape(flat), jnp.imag(lam_bar).reshape(flat)


def _loss_call(y, target):
    t, d = y.shape
    tr = _pick(t, (ROW_TILE, 128, 64, 32, 16, 8))

    def body(y_ref, t_ref, acc_ref, dy_ref):
        @pl.when(pl.program_id(0) == 0)
        def _():
            acc_ref[...] = jnp.zeros_like(acc_ref)

        diff = y_ref[...] - t_ref[...]
        dy_ref[...] = diff * (1.0 / d)
        acc_ref[0:1, :] += jnp.sum(diff * diff, axis=0, keepdims=True)

    spec = pl.BlockSpec((tr, d), lambda i: (i, 0))
    return pl.pallas_call(
        body, name=f"loss_{t}", grid=(t // tr,),
        out_shape=[SDS((8, d), F32), SDS((t, d), F32)],
        in_specs=[spec, spec], out_specs=[pl.BlockSpec((8, d), lambda i: (0, 0)), spec],
    )(y, target)


def _adamw_call(w, g, m, v):
    r, c = w.shape
    tr = _pick(r, [t for t in (1024, 512, 256, 128, 64, 32, 16, 8) if t * c <= 256 * 1024])

    def body(w_ref, g_ref, m_ref, v_ref, d_ref, mo_ref, vo_ref):
        gv = g_ref[...]
        mn = ADAM_B1 * m_ref[...] + (1.0 - ADAM_B1) * gv
        vn = ADAM_B2 * v_ref[...] + (1.0 - ADAM_B2) * (gv * gv)
        m_hat = mn / (1.0 - ADAM_B1 ** ADAM_STEP)
        v_hat = vn / (1.0 - ADAM_B2 ** ADAM_STEP)
        d_ref[...] = -ADAM_LR * (m_hat / (jnp.sqrt(v_hat) + ADAM_EPS) + ADAM_WD * w_ref[...])
        mo_ref[...] = mn
        vo_ref[...] = vn

    spec = pl.BlockSpec((tr, c), lambda i: (i, 0))
    return pl.pallas_call(
        body, name=f"adamw_{r}x{c}", grid=(r // tr,),
        out_shape=[SDS((r, c), F32)] * 3, in_specs=[spec] * 4, out_specs=[spec] * 3,
    )(w, g, m, v)


def _make_modulation(b, me):
    n_ex = N_DEV * b

    def a_matrix(c_ctx, c_all):
        rows = [jax.nn.silu(c_all), jax.nn.silu(c_ctx)[None, :],
                jnp.zeros((MOD_ROWS - n_ex - 1, c_all.shape[1]), F32)]
        return jnp.concatenate(rows, axis=0)

    def forward(c_ctx, b_ada, w_loc, c_all):
        depth, _, width = w_loc.shape
        a_mat = a_matrix(c_ctx, c_all)
        cols = jnp.stack([_mm_call('nn', a_mat, w_loc[layer]) for layer in range(depth)])
        gathered = _all_gather(cols.reshape(depth * MOD_ROWS, width))
        full = jnp.moveaxis(gathered.reshape(N_DEV, depth, MOD_ROWS, width), 0, 2).reshape(depth, MOD_ROWS, -1)
        mine = lax.dynamic_slice_in_dim(full, me * b, b, axis=1)
        return jnp.concatenate([mine, full[:, n_ex:n_ex + 1]], axis=1) + b_ada[:, None, :]

    @jax.custom_vjp
    def modulation(c_ctx, b_ada, w_loc, c_all):
        return forward(c_ctx, b_ada, w_loc, c_all)

    def modulation_fwd(c_ctx, b_ada, w_loc, c_all):
        return forward(c_ctx, b_ada, w_loc, c_all), (c_ctx, w_loc, c_all)

    def modulation_bwd(res, g):
        c_ctx, w_loc, c_all = res
        depth, _, width = w_loc.shape
        rows = depth * (b + 1)
        flat = jnp.pad(g.reshape(rows, -1), ((0, -rows % 8), (0, 0)))
        everyone = _all_gather(flat)[:, :rows].reshape(N_DEV, depth, b + 1, -1)
        examples = jnp.moveaxis(everyone[:, :, :b], 0, 1).reshape(depth, n_ex, -1)
        ctx_total = jnp.sum(everyone[:, :, b], axis=0)
        g_mat = jnp.concatenate([examples, ctx_total[:, None, :],
                                 jnp.zeros((depth, MOD_ROWS - n_ex - 1, g.shape[-1]), F32)], axis=1)
        g_cols = lax.dynamic_slice_in_dim(g_mat, me * width, width, axis=2)
        a_mat, a_vjp = jax.vjp(lambda cc: a_matrix(cc, c_all), c_ctx)
        dw = jnp.stack([_mm_call('tn', a_mat, g_cols[layer]) for layer in range(depth)])
        da = sum(_mm_call('nt', g_cols[layer], w_loc[layer]) for layer in range(depth))
        dc_ctx, = a_vjp(da)
        return dc_ctx, jnp.sum(g, axis=1), dw, jnp.zeros_like(c_all)

    modulation.defvjp(modulation_fwd, modulation_bwd)
    return modulation


def _forward(diff, gathered, c_all, me, dims, depth):
    b, l, lc = dims
    t_lat = b * l
    d_model = diff['x'].shape[-1]
    cos, sin = _rope_tables(l)
    glu = _make_rowwise_op("glu", _glu_f, SSM_WIDTH, 3, 0, l)
    att_conv = _make_att_conv(dims, cos, sin)
    ssms = [_make_ssm(dims, 0), _make_ssm(dims, 1)]
    bmats, cmats, lam_r, lam_i = _ssm_mats(*[diff['ssm_' + n] for n in
                                             ('lam_re', 'lam_im', 'log_dt', 'b_re', 'b_im', 'c_re', 'c_im')])

    def middle(p, prm):
        ac = att_conv(p, prm['conv_w'], prm['sink'])
        u = p[:, IN_WIDTH - SSM_WIDTH:]
        ys = [ssms[d](u, prm['bmat'][d], prm['cmat'][d], prm['lam_r'][d], prm['lam_i'][d]) for d in range(2)]
        return ac, glu(ys[0], ys[1], u, prm['ssm_d'], prm['w_glu'], prm['b_glu'])

    nm = _rowwise("norm_mod", _norm_mod_f, (d_model,), l)
    rg = _rowwise("resid_gate", _resid_gate_f, (d_model,), l)
    mixer_block, mlp_block = _make_block_ops(nm, rg, middle, t_lat + b * lc)
    mixer_block_last, _ = _make_block_ops(nm, rg, middle, t_lat)

    mods = _make_modulation(b, me)(diff['c_ctx'], diff['b_ada'], diff['w_ada'], c_all)
    h = jnp.concatenate([diff['x'].reshape(t_lat, d_model), diff['ctx'].reshape(b * lc, d_model)], axis=0)

    for layer in range(depth):
        last = layer == depth - 1
        sh1, sc1, g1, sh2, sc2, g2 = [m.reshape(b + 1, 1, d_model) for m in jnp.split(mods[layer], N_MOD, axis=-1)]
        if last:
            g1, sh2, sc2, g2 = g1[:b], sh2[:b], sc2[:b], g2[:b]
        ng = diff['norm_g'][layer]
        prm = dict(conv_w=diff['conv_w'][layer], sink=diff['attn_sink'][layer][None, :], bmat=bmats[layer],
                   cmat=cmats[layer], lam_r=lam_r[layer], lam_i=lam_i[layer], ssm_d=diff['ssm_d'][layer][None, :],
                   w_glu=diff['w_glu'][layer], b_glu=diff['b_glu'][layer][None, :])
        h = (mixer_block_last if last else mixer_block)(
            h, sh1, sc1, ng[0:1], gathered['w_in'][layer], diff['w_in'][layer], prm,
            gathered['w_out'][layer], diff['w_out'][layer], g1, ng[1:2])
        h = mlp_block(h, sh2, sc2, ng[2:3], gathered['w_mlp_in'][layer], diff['w_mlp_in'][layer],
                      gathered['w_mlp_out'][layer], diff['w_mlp_out'][layer], g2, ng[3:4])
    return h


BIG_TRANSPOSED = {'w_in': True, 'w_out': False, 'w_mlp_in': True, 'w_mlp_out': False}


def _big_rows(weights):
    blocks = []
    depth = weights['w_ada'].shape[0]
    for n in BIG:
        for layer in range(depth):
            w = weights[n][layer]
            blocks.append((layer, n, w.shape[1] if BIG_TRANSPOSED[n] else w.shape[0]))
    return blocks


def _stack_big(arrs, dtype):
    parts = []
    for n in BIG:
        w = jnp.swapaxes(arrs[n], 1, 2) if BIG_TRANSPOSED[n] else arrs[n]
        parts.append(w.reshape(-1, w.shape[-1]).astype(dtype))
    return jnp.concatenate(parts, axis=0)


def _adamw_any(w, g, m, v):
    flat = [a.reshape(-1, w.shape[-1]) for a in (w, g, m, v)]
    return [o.reshape(w.shape) for o in _adamw_call(*flat)]


def kernel(x, c, ctx, c_ctx, w_ada, b_ada, norm_g, w_in, conv_w, attn_sink, ssm_lam_re, ssm_lam_im, ssm_log_dt, ssm_b_re, ssm_b_im, ssm_c_re, ssm_c_im, ssm_d, w_glu, b_glu, w_out, w_mlp_in, w_mlp_out, loss_target, m_c_ctx, m_w_ada, m_b_ada, m_norm_g, m_w_in, m_conv_w, m_attn_sink, m_ssm_lam_re, m_ssm_lam_im, m_ssm_log_dt, m_ssm_b_re, m_ssm_b_im, m_ssm_c_re, m_ssm_c_im, m_ssm_d, m_w_glu, m_b_glu, m_w_out, m_w_mlp_in, m_w_mlp_out, v_c_ctx, v_w_ada, v_b_ada, v_norm_g, v_w_in, v_conv_w, v_attn_sink, v_ssm_lam_re, v_ssm_lam_im, v_ssm_log_dt, v_ssm_b_re, v_ssm_b_im, v_ssm_c_re, v_ssm_c_im, v_ssm_d, v_w_glu, v_b_glu, v_w_out, v_w_mlp_in, v_w_mlp_out):
    given = dict(locals())
    weights = {n: given[n] for n in WEIGHTS}
    moms = {n: given['m_' + n] for n in WEIGHTS}
    vars_ = {n: given['v_' + n] for n in WEIGHTS}
    b, l, d_model = x.shape
    lc = ctx.shape[1]
    dims = (b, l, lc)
    depth = w_ada.shape[0]

    core = lax.axis_index('c')
    me = 4 * lax.axis_index('x') + 2 * lax.axis_index('y') + core

    blocks = _big_rows(weights)
    big_all = _all_gather(_stack_big(weights, MXU_DTYPE))
    gathered = {n: [None] * depth for n in BIG}
    off = 0
    for layer, n, rows in blocks:
        gathered[n][layer] = big_all[:, off:off + rows].reshape(N_DEV * rows, -1)
        off += rows
    small_shapes = [weights[n].shape for n in SMALL]
    small_all = _all_gather(_pack([weights[n] for n in SMALL], F32, PACK_ROWS))
    small_full = {n: _from_pieces(pc, SHARD_AXIS[n])
                  for n, pc in zip(SMALL, _unpack(small_all, small_shapes, (N_DEV,)))}

    c_all = _all_gather(jnp.pad(c, ((0, -b % 8), (0, 0))))[:, :b].reshape(N_DEV * b, d_model)
    diff = {'x': x, 'ctx': ctx, 'w_ada': w_ada}
    for n in REPLICATED:
        diff[n] = weights[n]
    for n in SMALL:
        diff[n] = small_full[n]
    for n in BIG:
        diff[n] = [jnp.zeros(gathered[n][layer].shape, F32) for layer in range(depth)]
    y, vjp = jax.vjp(lambda dd: _forward(dd, gathered, c_all, me, dims, depth), diff)
    sq, dy = _loss_call(y, loss_target.reshape(b * l, d_model))
    loss = lax.psum(0.5 * jnp.sum(sq) / d_model, ('x', 'y', 'c'))
    grads = vjp(dy)[0]
    grad_x = grads['x']

    g8 = jnp.concatenate([grads[n][layer].reshape(N_DEV, rows, -1) for layer, n, rows in blocks], axis=1)
    chip_sums = _rs_add(g8, _rs_sibling(g8), core.reshape(1).astype(jnp.int32))
    g_big = _sum_slots(_rs_chips(chip_sums))
    big_grad, off = {}, 0
    for n in BIG:
        rows = sum(r for _, name, r in blocks if name == n)
        blk = g_big[off:off + rows].reshape(depth, rows // depth, -1)
        big_grad[n] = jnp.swapaxes(blk, 1, 2) if BIG_TRANSPOSED[n] else blk
        off += rows
    rest = SMALL + REPLICATED
    g_rest = _sum_slots(_all_gather(_pack([grads[n] for n in rest], F32, PACK_ROWS)))
    rest_grad = dict(zip(rest, _unpack(g_rest, [grads[n].shape for n in rest])))
    for n in SMALL:
        width = weights[n].shape[SHARD_AXIS[n]]
        rest_grad[n] = lax.dynamic_slice_in_dim(rest_grad[n], me * width, width, axis=SHARD_AXIS[n])

    out = {}
    all_grads = {**big_grad, **rest_grad, **{n: grads[n] for n in LOCAL}}
    for n in WEIGHTS:
        out[('grad', n)] = all_grads[n]
        for kind, arr in zip(('delta', 'new_m', 'new_v'), _adamw_any(weights[n], all_grads[n], moms[n], vars_[n])):
            out[(kind, n)] = arr
    return (loss, grad_x, *[out[(kind, n)] for kind in ('grad', 'delta', 'new_m', 'new_v') for n in WEIGHTS])
```

```python
import jax
import jax.numpy as jnp
from jax import lax
from jax.experimental import pallas as pl
from jax.experimental.pallas import tpu as pltpu

F32 = jnp.float32
MXU_DTYPE = jnp.bfloat16
SDS = jax.ShapeDtypeStruct

N_DEV = 8
HEAD_DIM = 64
N_Q_HEADS = 8
Q_PER_KV = 4
ATTN_WIDTH = 512
KV_WIDTH = 128
WINDOW = 128
Q_BLOCK = 128
GRID_W = 64
ROPE_BASE = 10000.0
ROPE_PAIRS = 16
CONV_WIDTH = 256
SSM_WIDTH = 256
SSM_GROUP = 16
SSM_GROUPS = 16
SSM_STATE = 64
SSM_N = SSM_GROUPS * SSM_STATE
IN_WIDTH = 1792
N_MOD = 6
EPS = 1e-6
NEG_INF = -1e30
ADAM_LR = 0.001
ADAM_B1 = 0.9
ADAM_B2 = 0.999
ADAM_EPS = 1e-08
ADAM_WD = 0.01
ADAM_STEP = 10
MOD_ROWS = 128
ROW_TILE = 512
ROW_VMEM_BYTES = 48 * 1024 * 1024
BIG_ROW_TILE = 512
PACK_ROWS = 256
LANES = 128
MESH = pl.DeviceIdType.MESH
_SCALE = HEAD_DIM ** -0.5
MIX_SPLIT = ATTN_WIDTH + CONV_WIDTH
ATTN_STACK_ROWS = 256
MM_TILE = 1024
MM_VMEM_BYTES = 56 * 1024 * 1024

WEIGHTS = ['c_ctx', 'w_ada', 'b_ada', 'norm_g', 'w_in', 'conv_w', 'attn_sink', 'ssm_lam_re', 'ssm_lam_im',
           'ssm_log_dt', 'ssm_b_re', 'ssm_b_im', 'ssm_c_re', 'ssm_c_im', 'ssm_d', 'w_glu', 'b_glu', 'w_out',
           'w_mlp_in', 'w_mlp_out']
SHARD_AXIS = {'w_ada': 2, 'norm_g': 2, 'w_in': 2, 'conv_w': 2, 'w_glu': 1, 'w_out': 1, 'w_mlp_in': 2, 'w_mlp_out': 1}
BIG = ['w_in', 'w_out', 'w_mlp_in', 'w_mlp_out']
SMALL = ['norm_g', 'conv_w', 'w_glu']
LOCAL = ['w_ada']
REPLICATED = [n for n in WEIGHTS if n not in SHARD_AXIS]


def _pick(n, cands):
    for c in cands:
        if n % c == 0:
            return c
    return n


def _div_tile(n, cap):
    if n <= cap:
        return n
    for c in range(cap, LANES - 1, -LANES):
        if n % c == 0:
            return c
    return n


def _mx(x):
    return x.astype(MXU_DTYPE)


def _dg_nt(a, b):
    return lax.dot_general(a, b, (((1,), (1,)), ((), ())), preferred_element_type=F32)


def _dg_tn(a, b):
    return lax.dot_general(a, b, (((0,), (0,)), ((), ())), preferred_element_type=F32)


def _all_gather(x2d):
    r, cdim = x2d.shape

    def body(x_ref, out_ref, send_sems, recv_sems, local_sem):
        x, y, c = lax.axis_index("x"), lax.axis_index("y"), lax.axis_index("c")
        me, sibling = (x, y, c), (x, y, 1 - c)
        chips = [(1 - x, y), (x, 1 - y), (1 - x, 1 - y)]

        def slot(px, py, pc):
            return out_ref.at[4 * px + 2 * py + pc]

        def copy(k, block, to, src=None):
            return pltpu.make_async_remote_copy(
                src_ref=slot(*block) if src is None else src, dst_ref=slot(*block),
                send_sem=send_sems.at[k], recv_sem=recv_sems.at[k], device_id=to, device_id_type=MESH)

        mine = pltpu.make_async_copy(x_ref, slot(*me), local_sem)
        mine.start()
        first = [copy(0, me, sibling, src=x_ref)]
        first += [copy(1 + j, me, (*chip, c), src=x_ref) for j, chip in enumerate(chips)]
        for cp in first:
            cp.start()
        passed = [copy(4 + j, (*chip, c), sibling) for j, chip in enumerate(chips)]
        for j, chip in enumerate(chips):
            copy(1 + j, (*chip, c), me).wait_recv()
            passed[j].start()
        copy(0, sibling, me).wait_recv()
        for j, chip in enumerate(chips):
            copy(4 + j, (*chip, 1 - c), me).wait_recv()
        for cp in first + passed:
            cp.wait_send()
        mine.wait()

    return pl.pallas_call(
        body, name=f"all_gather_{r}x{cdim}_{jnp.dtype(x2d.dtype).name}",
        out_shape=SDS((N_DEV, r, cdim), x2d.dtype),
        in_specs=[pl.BlockSpec(memory_space=pl.ANY)],
        out_specs=pl.BlockSpec(memory_space=pl.ANY),
        scratch_shapes=[pltpu.SemaphoreType.DMA((7,)), pltpu.SemaphoreType.DMA((7,)), pltpu.SemaphoreType.DMA],
    )(x2d)


def _rs_sibling(g8):
    _, r, cdim = g8.shape

    def body(g_ref, out_ref, send_sems, recv_sems):
        x, y, c = lax.axis_index("x"), lax.axis_index("y"), lax.axis_index("c")
        copies = [pltpu.make_async_remote_copy(
            src_ref=g_ref.at[2 * k + (1 - c)], dst_ref=out_ref.at[k], send_sem=send_sems.at[k],
            recv_sem=recv_sems.at[k], device_id=(x, y, 1 - c), device_id_type=MESH) for k in range(4)]
        for cp in copies:
            cp.start()
        for cp in copies:
            cp.wait()

    return pl.pallas_call(
        body, name=f"rs_sibling_{r}x{cdim}",
        out_shape=SDS((4, r, cdim), g8.dtype),
        in_specs=[pl.BlockSpec(memory_space=pl.ANY)],
        out_specs=pl.BlockSpec(memory_space=pl.ANY),
        scratch_shapes=[pltpu.SemaphoreType.DMA((4,)), pltpu.SemaphoreType.DMA((4,))],
    )(g8)


def _rs_add(g8, sib, c_idx):
    _, r, cdim = g8.shape
    tr = _pick(r, (BIG_ROW_TILE, 256, 128, 64, 32, 16))

    def body(c_ref, g_ref, s_ref, o_ref):
        del c_ref
        o_ref[...] = (g_ref[...] + s_ref[...]).astype(o_ref.dtype)

    return pl.pallas_call(
        body, name=f"rs_add_{r}x{cdim}",
        grid_spec=pltpu.PrefetchScalarGridSpec(
            num_scalar_prefetch=1, grid=(4, r // tr),
            in_specs=[pl.BlockSpec((1, tr, cdim), lambda k, i, c: (2 * k + c[0], i, 0)),
                      pl.BlockSpec((1, tr, cdim), lambda k, i, c: (k, i, 0))],
            out_specs=pl.BlockSpec((1, tr, cdim), lambda k, i, c: (k, i, 0))),
        out_shape=SDS((4, r, cdim), MXU_DTYPE),
    )(c_idx, g8, sib)


def _rs_chips(s4):
    _, r, cdim = s4.shape

    def body(s_ref, out_ref, send_sems, recv_sems, local_sem):
        x, y, c = lax.axis_index("x"), lax.axis_index("y"), lax.axis_index("c")
        me = 2 * x + y

        def peer(j):
            px = 1 - x if j & 2 else x
            py = 1 - y if j & 1 else y
            return (px, py, c), 2 * px + py

        def copy(j, landing):
            to, to_chip = peer(j)
            return pltpu.make_async_remote_copy(
                src_ref=s_ref.at[to_chip], dst_ref=out_ref.at[to_chip if landing else me],
                send_sem=send_sems.at[j - 1], recv_sem=recv_sems.at[j - 1], device_id=to, device_id_type=MESH)

        mine = pltpu.make_async_copy(s_ref.at[me], out_ref.at[me], local_sem)
        mine.start()
        sends = [copy(j, False) for j in range(1, 4)]
        for cp in sends:
            cp.start()
        for j in range(1, 4):
            copy(j, True).wait_recv()
        for cp in sends:
            cp.wait_send()
        mine.wait()

    return pl.pallas_call(
        body, name=f"rs_chips_{r}x{cdim}",
        out_shape=SDS(s4.shape, s4.dtype),
        in_specs=[pl.BlockSpec(memory_space=pl.ANY)],
        out_specs=pl.BlockSpec(memory_space=pl.ANY),
        scratch_shapes=[pltpu.SemaphoreType.DMA((3,)), pltpu.SemaphoreType.DMA((3,)), pltpu.SemaphoreType.DMA],
    )(s4)


def _sum_slots(g3):
    n, r, cdim = g3.shape
    tr = _pick(r, (BIG_ROW_TILE, 256, 128, 64, 32, 16, 8))

    def body(g_ref, o_ref):
        acc = g_ref[0].astype(F32)
        for s in range(1, n):
            acc = acc + g_ref[s].astype(F32)
        o_ref[...] = acc

    return pl.pallas_call(
        body, name=f"sum_slots_{n}x{r}x{cdim}", grid=(r // tr,),
        out_shape=SDS((r, cdim), F32),
        in_specs=[pl.BlockSpec((n, tr, cdim), lambda i: (0, i, 0))],
        out_specs=pl.BlockSpec((tr, cdim), lambda i: (i, 0)),
    )(g3)


def _rows_of(n):
    return -(-n // (8 * LANES)) * 8


def _pack(arrs, dtype, row_mult):
    parts = []
    for a in arrs:
        flat = a.reshape(-1).astype(dtype)
        parts.append(jnp.pad(flat, (0, _rows_of(flat.size) * LANES - flat.size)).reshape(-1, LANES))
    rows = sum(p.shape[0] for p in parts)
    if rows % row_mult:
        parts.append(jnp.zeros((row_mult - rows % row_mult, LANES), dtype))
    return jnp.concatenate(parts, axis=0)


def _unpack(mat, shapes, lead=()):
    out, off = [], 0
    for s in shapes:
        n = 1
        for d in s:
            n *= d
        rows = _rows_of(n)
        part = mat[..., off:off + rows, :].reshape(lead + (rows * LANES,))
        out.append(part[..., :n].reshape(lead + tuple(s)))
        off += rows
    return out


def _from_pieces(p, axis):
    p = jnp.moveaxis(p, 0, axis)
    s = p.shape
    return p.reshape(s[:axis] + (s[axis] * s[axis + 1],) + s[axis + 2:])


def _mm_call(kind, a, b, z=None, sqrelu=False, out_dtype=F32):
    if kind == 'nn':
        (m, k), n = a.shape, b.shape[1]
    elif kind == 'nt':
        (m, k), n = a.shape, b.shape[0]
    else:
        (k, m), n = a.shape, b.shape[1]
    bm = _div_tile(m, MM_TILE)
    bn = _div_tile(n, MM_TILE)
    bk = _div_tile(k, MM_TILE if kind == 'tn' else 2 * MM_TILE)
    nk = k // bk
    if sqrelu:
        out_dtype = MXU_DTYPE
    plain = not sqrelu and z is None and out_dtype == F32
    use_acc = nk > 1 and not plain

    def body(a_ref, b_ref, *rest):
        rest = list(rest)
        z_ref = rest.pop(0) if z is not None else None
        o_ref = rest.pop(0)
        act_ref = rest.pop(0) if sqrelu else None
        acc = rest.pop(0) if use_acc else None
        kk = pl.program_id(2)
        av, bv = _mx(a_ref[...]), _mx(b_ref[...])
        if kind == 'nn':
            prod = jnp.dot(av, bv, preferred_element_type=F32)
        elif kind == 'nt':
            prod = _dg_nt(av, bv)
        else:
            prod = _dg_tn(av, bv)

        def finish(r):
            if z_ref is not None:
                r = r * (2.0 * jnp.maximum(z_ref[...].astype(F32), 0.0))
            o_ref[...] = r.astype(o_ref.dtype)
            if act_ref is not None:
                rr = jnp.maximum(r, 0.0)
                act_ref[...] = (rr * rr).astype(act_ref.dtype)

        if nk == 1:
            finish(prod)
        else:
            tgt = acc if use_acc else o_ref

            @pl.when(kk == 0)
            def _():
                tgt[...] = prod

            @pl.when(kk > 0)
            def _():
                tgt[...] += prod

            if use_acc:
                @pl.when(kk == nk - 1)
                def _():
                    finish(acc[...])

    if kind == 'nn':
        a_spec = pl.BlockSpec((bm, bk), lambda i, j, kk: (i, kk))
        b_spec = pl.BlockSpec((bk, bn), lambda i, j, kk: (kk, j))
    elif kind == 'nt':
        a_spec = pl.BlockSpec((bm, bk), lambda i, j, kk: (i, kk))
        b_spec = pl.BlockSpec((bn, bk), lambda i, j, kk: (j, kk))
    else:
        a_spec = pl.BlockSpec((bk, bm), lambda i, j, kk: (kk, i))
        b_spec = pl.BlockSpec((bk, bn), lambda i, j, kk: (kk, j))
    o_spec = pl.BlockSpec((bm, bn), lambda i, j, kk: (i, j))
    in_specs, args = [a_spec, b_spec], [a, b]
    if z is not None:
        in_specs.append(o_spec)
        args.append(z)
    out_shape, out_specs = [SDS((m, n), out_dtype)], [o_spec]
    if sqrelu:
        out_shape.append(SDS((m, n), MXU_DTYPE))
        out_specs.append(o_spec)
    tag = kind + ('_sq' if sqrelu else '') + ('_z' if z is not None else '')
    res = pl.pallas_call(
        body, name=f"mm_{tag}_{m}x{k}x{n}", grid=(m // bm, n // bn, nk),
        out_shape=out_shape, in_specs=in_specs, out_specs=out_specs,
        scratch_shapes=[pltpu.VMEM((bm, bn), F32)] if use_acc else [],
        compiler_params=pltpu.CompilerParams(dimension_semantics=("parallel", "parallel", "arbitrary"),
                                             vmem_limit_bytes=MM_VMEM_BYTES),
    )(*args)
    return res if sqrelu else res[0]


def _make_block_ops(nm, rg, middle, rows_out):
    nm_fwd, nm_bwd = nm
    rg_fwd, rg_bwd = rg

    def mixer_fwd(h, sh, sc, g_pre, wt, wtd, params, w, wd, gate, g_post):
        a = nm_fwd((h,), (sh, sc), (g_pre,), MXU_DTYPE)[0]
        (ac, s), mid_vjp = jax.vjp(middle, _mm_call('nt', a, wt), params)
        mix = _mx(jnp.concatenate([ac[:rows_out], s[:rows_out]], axis=1))
        m = _mm_call('nn', mix, w)
        out = rg_fwd((h[:rows_out], m), (gate,), (g_post,))[0]
        return out, (h, sh, sc, g_pre, a, wt, mid_vjp, mix, w, m, gate, g_post)

    @jax.custom_vjp
    def mixer_block(h, sh, sc, g_pre, wt, wtd, params, w, wd, gate, g_post):
        return mixer_fwd(h, sh, sc, g_pre, wt, wtd, params, w, wd, gate, g_post)[0]

    def mixer_bwd(res, g):
        h, sh, sc, g_pre, a, wt, mid_vjp, mix, w, m, gate, g_post = res
        dm, dgate, dg_post = rg_bwd((h[:rows_out], m), (gate,), (g_post,), (g,), skip_first_row=True)
        dmix = jnp.pad(_mm_call('nt', dm, w), ((0, h.shape[0] - rows_out), (0, 0)))
        dp, dparams = mid_vjp((dmix[:, :MIX_SPLIT], dmix[:, MIX_SPLIT:]))
        da = _mm_call('nn', dp, wt)
        dh, dsh, dsc, dg_pre = nm_bwd((h,), (sh, sc), (g_pre,), (da,), add=g)
        return (dh, dsh, dsc, dg_pre, jnp.zeros_like(wt), _mm_call('tn', dp, a), dparams, jnp.zeros_like(w),
                _mm_call('tn', mix, dm), dgate, dg_post)

    mixer_block.defvjp(mixer_fwd, mixer_bwd)

    def mlp_fwd(h, sh, sc, g_pre, w1t, w1td, w2, w2d, gate, g_post):
        a = nm_fwd((h,), (sh, sc), (g_pre,), MXU_DTYPE)[0]
        zb, act = _mm_call('nt', a, w1t, sqrelu=True)
        f = _mm_call('nn', act, w2)
        out = rg_fwd((h, f), (gate,), (g_post,))[0]
        return out, (h, sh, sc, g_pre, a, w1t, w2, zb, act, f, gate, g_post)

    @jax.custom_vjp
    def mlp_block(h, sh, sc, g_pre, w1t, w1td, w2, w2d, gate, g_post):
        return mlp_fwd(h, sh, sc, g_pre, w1t, w1td, w2, w2d, gate, g_post)[0]

    def mlp_bwd(res, g):
        h, sh, sc, g_pre, a, w1t, w2, zb, act, f, gate, g_post = res
        df, dgate, dg_post = rg_bwd((h, f), (gate,), (g_post,), (g,), skip_first_row=True)
        dz = _mm_call('nt', df, w2, z=zb, out_dtype=MXU_DTYPE)
        da = _mm_call('nn', dz, w1t)
        dh, dsh, dsc, dg_pre = nm_bwd((h,), (sh, sc), (g_pre,), (da,), add=g)
        return (dh, dsh, dsc, dg_pre, jnp.zeros_like(w1t), _mm_call('tn', dz, a), jnp.zeros_like(w2),
                _mm_call('tn', act, df), dgate, dg_post)

    mlp_block.defvjp(mlp_fwd, mlp_bwd)
    return mixer_block, mlp_block


def _rowwise(name, f, out_widths, seg_len):
    def specs(rows, segs, globs, tr):
        nseg = segs[0].shape[0] if segs else 1

        def seg_of(i):
            return jnp.minimum((i * tr) // seg_len, nseg - 1)

        row_specs = [pl.BlockSpec((tr, r.shape[1]), lambda i: (i, 0)) for r in rows]
        seg_specs = [pl.BlockSpec((1, 1, s.shape[2]), lambda i: (seg_of(i), 0, 0)) for s in segs]
        glob_specs = [pl.BlockSpec(g.shape, lambda i: (0, 0)) for g in globs]
        return seg_of, row_specs, seg_specs, glob_specs

    def fwd_call(rows, segs, globs, out_dtype=F32):
        t = rows[0].shape[0]
        tr = _pick(t, (ROW_TILE, 128, 64, 32, 16, 8))
        _, row_specs, seg_specs, glob_specs = specs(rows, segs, globs, tr)
        nr, ns = len(rows), len(segs)

        def body(*refs):
            ins, outs = refs[:nr + ns + len(globs)], refs[nr + ns + len(globs):]
            vals = [r[...] for r in ins[:nr]] + [r[0] for r in ins[nr:nr + ns]] + [r[...] for r in ins[nr + ns:]]
            for o_ref, v in zip(outs, f(*vals)):
                o_ref[...] = v.astype(o_ref.dtype)

        return pl.pallas_call(
            body, name=f"{name}_fwd_{t}_{jnp.dtype(out_dtype).name}", grid=(t // tr,),
            out_shape=[SDS((t, w), out_dtype) for w in out_widths],
            in_specs=row_specs + seg_specs + glob_specs,
            out_specs=[pl.BlockSpec((tr, w), lambda i: (i, 0)) for w in out_widths],
            compiler_params=pltpu.CompilerParams(vmem_limit_bytes=ROW_VMEM_BYTES),
        )(*rows, *segs, *globs)

    def bwd_call(rows, segs, globs, douts, add=None, skip_first_row=False):
        t = rows[0].shape[0]
        tr = _pick(t, (ROW_TILE, 128, 64, 32, 16, 8))
        seg_of, row_specs, seg_specs, glob_specs = specs(rows, segs, globs, tr)
        nr, ns, ng, no = len(rows), len(segs), len(globs), len(out_widths)
        n_add = 0 if add is None else add.shape[0] // tr
        first_out = 1 if skip_first_row else 0

        def body(*refs):
            ins = refs[:nr + ns + ng]
            dos = refs[nr + ns + ng:nr + ns + ng + no]
            add_ref = refs[nr + ns + ng + no] if n_add else None
            outs = (None,) * first_out + refs[nr + ns + ng + no + (1 if n_add else 0):]
            i = pl.program_id(0)
            first_of_seg = jnp.logical_or(i == 0, seg_of(i) != seg_of(jnp.maximum(i - 1, 0)))
            vals = [r[...] for r in ins[:nr]] + [r[0] for r in ins[nr:nr + ns]] + [r[...] for r in ins[nr + ns:]]
            _, vjp = jax.vjp(f, *vals)
            grads = list(vjp(tuple(d[...] for d in dos)))
            if n_add:
                grads[0] = grads[0] + (add_ref[...] if n_add * tr == t else jnp.where(i < n_add, add_ref[...], 0.0))
            for o_ref, gval in zip(outs[first_out:nr], grads[first_out:nr]):
                o_ref[...] = gval
            for o_ref, gval in zip(outs[nr:nr + ns], grads[nr:nr + ns]):
                @pl.when(first_of_seg)
                def _(o_ref=o_ref):
                    o_ref[...] = jnp.zeros_like(o_ref)
                o_ref[0] += gval
            for o_ref, gval in zip(outs[nr + ns:], grads[nr + ns:]):
                @pl.when(i == 0)
                def _(o_ref=o_ref):
                    o_ref[...] = jnp.zeros_like(o_ref)
                o_ref[...] += gval

        do_specs = [pl.BlockSpec((tr, w), lambda i: (i, 0)) for w in out_widths]
        add_specs, add_args = [], []
        if n_add:
            add_specs = [pl.BlockSpec((tr, add.shape[1]), lambda i: (jnp.minimum(i, n_add - 1), 0))]
            add_args = [add]
        tag = ('_add' if n_add else '') + ('_skip' if skip_first_row else '')
        return pl.pallas_call(
            body, name=f"{name}_bwd{tag}_{t}", grid=(t // tr,),
            out_shape=[SDS(r.shape, F32) for r in rows[first_out:]] + [SDS(s.shape, F32) for s in segs]
            + [SDS(g.shape, F32) for g in globs],
            in_specs=row_specs + seg_specs + glob_specs + do_specs + add_specs,
            out_specs=row_specs[first_out:] + seg_specs + glob_specs,
            compiler_params=pltpu.CompilerParams(vmem_limit_bytes=ROW_VMEM_BYTES),
        )(*rows, *segs, *globs, *douts, *add_args)

    return fwd_call, bwd_call


def _norm_mod_f(x, shift, scale, g):
    r = lax.rsqrt(jnp.mean(x * x, axis=-1, keepdims=True) + EPS)
    return ((x * r) * g * (1.0 + scale) + shift,)


def _resid_gate_f(h, m, gate, g):
    r = lax.rsqrt(jnp.mean(m * m, axis=-1, keepdims=True) + EPS)
    return (h + gate * ((m * r) * g),)


def _glu_f(y0, y1, u, d, w, b):
    y = y0 + y1 + d * u
    g = 0.5 * y * (1.0 + jnp.tanh(0.7978845608028654 * (y + 0.044715 * (y * y * y))))
    zz = jnp.dot(_mx(g), _mx(w), preferred_element_type=F32) + b
    return (g * (1.0 / (1.0 + jnp.exp(-zz))),)


def _make_rowwise_op(name, f, out_width, n_rows, n_segs, seg_len):
    fwd_call, bwd_call = _rowwise(name, f, (out_width,), seg_len)

    @jax.custom_vjp
    def op(*args):
        return fwd_call(args[:n_rows], args[n_rows:n_rows + n_segs], args[n_rows + n_segs:])[0]

    def op_fwd(*args):
        return fwd_call(args[:n_rows], args[n_rows:n_rows + n_segs], args[n_rows + n_segs:])[0], args

    def op_bwd(args, g):
        return tuple(bwd_call(args[:n_rows], args[n_rows:n_rows + n_segs], args[n_rows + n_segs:], (g,)))

    op.defvjp(op_fwd, op_bwd)
    return op


def _lane():
    return lax.broadcasted_iota(jnp.int32, (1, LANES), 1)


def _hm(off):
    lane = _lane()
    return jnp.logical_and(lane >= off, lane < off + HEAD_DIM)


def _col(tile, hq):
    return jnp.sum(jnp.where(_lane() == hq, tile, 0.0), axis=1, keepdims=True)


def _setcol(tile, hq, col):
    return jnp.where(_lane() == hq, col, tile)


def _head_fwd(qa, groups, sk):
    ss, m = [], None
    for kmat, _, mask in groups:
        s = _dg_nt(qa, kmat) * _SCALE
        if mask is not None:
            s = jnp.where(mask, s, NEG_INF)
        ss.append(s)
        mm = jnp.max(s, axis=1, keepdims=True)
        m = mm if m is None else jnp.maximum(m, mm)
    m = jnp.maximum(m, sk)
    l = jnp.exp(sk - m)
    pv = None
    for s, (_, vmat, _) in zip(ss, groups):
        p = jnp.exp(s - m)
        l = l + jnp.sum(p, axis=1, keepdims=True)
        t = jnp.dot(_mx(p), vmat, preferred_element_type=F32)
        pv = t if pv is None else pv + t
    return pv / l, m + jnp.log(l)


def _head_bwd(qa, doa, groups, sk, lse_h, delta_h):
    dq, outs = None, []
    for kmat, vmat, mask in groups:
        s = _dg_nt(qa, kmat) * _SCALE
        if mask is not None:
            s = jnp.where(mask, s, NEG_INF)
        p = jnp.exp(s - lse_h)
        dp = _dg_nt(doa, vmat)
        ds = p * (dp - delta_h) * _SCALE
        t = jnp.dot(_mx(ds), kmat, preferred_element_type=F32)
        dq = t if dq is None else dq + t
        outs.append((_mx(ds), _mx(p)))
    return dq, outs, jnp.exp(sk - lse_h)


def _kv_group(h):
    return [(g, Q_PER_KV * h + g, (Q_PER_KV * h + g) // 2, (g % 2) != h) for g in range(Q_PER_KV)]


def _to_kv_lanes(pair_tile, roll):
    if not roll:
        return _mx(pair_tile)
    return _mx(pltpu.roll(pair_tile.astype(F32), HEAD_DIM, 1))


def _attn_stack(n):
    return max(1, min(Q_PER_KV, ATTN_STACK_ROWS // n))


def _kv_masked(x, h):
    return _mx(jnp.where(_hm(h * HEAD_DIM), x.astype(F32), 0.0))


def _attn_fwd_block(sink_ref, q_ref, kv, n, o_ref, lse_ref):
    lse_t = jnp.zeros((n, LANES), F32)
    stack = _attn_stack(n)
    for h in range(2):
        groups = [(_kv_masked(k, h), _kv_masked(v, h), mask) for k, v, mask in kv]
        heads = _kv_group(h)
        o_pair = None
        for first in range(0, Q_PER_KV, stack):
            part = heads[first:first + stack]
            q = jnp.concatenate([_to_kv_lanes(q_ref[:, pair * LANES:(pair + 1) * LANES], roll)
                                 for _, _, pair, roll in part], axis=0)
            sk = jnp.concatenate([jnp.full((n, 1), sink_ref[0, hq], F32) for _, hq, _, _ in part], axis=0)
            o, lse = _head_fwd(q, groups, sk)
            for idx, (g, hq, pair, roll) in enumerate(part):
                og = o[idx * n:(idx + 1) * n]
                og = pltpu.roll(og, HEAD_DIM, 1) if roll else og
                lse_t = _setcol(lse_t, hq, lse[idx * n:(idx + 1) * n])
                if g % 2 == 0:
                    o_pair = og
                else:
                    o_ref[:, pair * LANES:(pair + 1) * LANES] = o_pair + og
    lse_ref[...] = lse_t


def _attn_bwd_block(sink_ref, q_ref, do_ref, o_ref, lse_t, kv, want, n, dq_ref):
    delta_t = jnp.zeros((n, LANES), F32)
    dsk = jnp.zeros((1, LANES), F32)
    dkv = [None] * len(kv)
    stack = _attn_stack(n)
    for h in range(2):
        hm = _hm(h * HEAD_DIM)
        groups = [(_kv_masked(k, h), _kv_masked(v, h), mask) for k, v, mask in kv]
        heads = _kv_group(h)
        dq_pair = None
        for first in range(0, Q_PER_KV, stack):
            part = heads[first:first + stack]
            qs, dos, sks, lses, deltas = [], [], [], [], []
            for g, hq, pair, roll in part:
                sl = slice(pair * LANES, (pair + 1) * LANES)
                do_p = do_ref[:, sl]
                delta_h = jnp.sum(jnp.where(_hm((hq % 2) * HEAD_DIM), do_p * o_ref[:, sl], 0.0), axis=1,
                                  keepdims=True)
                delta_t = _setcol(delta_t, hq, delta_h)
                qs.append(_to_kv_lanes(q_ref[:, sl], roll))
                dos.append(_to_kv_lanes(do_p, roll))
                sks.append(jnp.full((n, 1), sink_ref[0, hq], F32))
                lses.append(_col(lse_t, hq))
                deltas.append(delta_h)
            q, do = jnp.concatenate(qs, axis=0), jnp.concatenate(dos, axis=0)
            dq, outs, p_s = _head_bwd(q, do, groups, jnp.concatenate(sks, axis=0), jnp.concatenate(lses, axis=0),
                                      jnp.concatenate(deltas, axis=0))
            for idx, (g, hq, pair, roll) in enumerate(part):
                dg = dq[idx * n:(idx + 1) * n]
                dg = pltpu.roll(dg, HEAD_DIM, 1) if roll else dg
                dsk = dsk + jnp.where(_lane() == hq,
                                      jnp.sum(-p_s[idx * n:(idx + 1) * n] * deltas[idx], axis=0, keepdims=True), 0.0)
                if g % 2 == 0:
                    dq_pair = dg
                else:
                    dq_ref[:, pair * LANES:(pair + 1) * LANES] = dq_pair + dg
            for gi, (ds, pp) in enumerate(outs):
                if want[gi]:
                    dk_h = jnp.where(hm, _dg_tn(ds, q), 0.0)
                    dv_h = jnp.where(hm, _dg_tn(pp, do), 0.0)
                    dkv[gi] = (dk_h, dv_h) if dkv[gi] is None else (dkv[gi][0] + dk_h, dkv[gi][1] + dv_h)
    return delta_t, dsk, dkv


def _partner(t):
    lane = lax.broadcasted_iota(jnp.int32, t.shape, 1)
    return jnp.where((lane & ROPE_PAIRS) == 0, pltpu.roll(t, LANES - ROPE_PAIRS, 1), pltpu.roll(t, ROPE_PAIRS, 1))


def _rope_tables(n_tokens):
    rows = n_tokens // GRID_W
    row = jnp.broadcast_to(jnp.arange(rows)[:, None], (rows, GRID_W)).reshape(-1)
    col = jnp.broadcast_to(jnp.arange(GRID_W)[None, :], (rows, GRID_W)).reshape(-1)
    freqs = ROPE_BASE ** (-jnp.arange(ROPE_PAIRS, dtype=F32) / ROPE_PAIRS)
    ang = jnp.concatenate([row[:, None].astype(F32) * freqs, col[:, None].astype(F32) * freqs], axis=-1)
    c, s = jnp.cos(ang), jnp.sin(ang)
    n = ROPE_PAIRS
    cos64 = jnp.concatenate([c[:, :n], c[:, :n], c[:, n:], c[:, n:]], axis=1)
    sin64 = jnp.concatenate([-s[:, :n], s[:, :n], -s[:, n:], s[:, n:]], axis=1)
    return jnp.tile(cos64, (1, 2)), jnp.tile(sin64, (1, 2))


def _rope_call(q_src, k_src, cos, sin, dims, inverse):
    b, l, _ = dims
    t_lat = b * l
    tr = _pick(l, (ROW_TILE, 128))
    per = l // tr
    out_dtype = F32 if inverse else MXU_DTYPE

    def body(q_ref, k_ref, c_ref, s_ref, qo_ref, ko_ref):
        c, s = c_ref[...], s_ref[...]

        def rot(t):
            if inverse:
                return t * c + _partner(t * s)
            return t * c + _partner(t) * s

        for j in range(ATTN_WIDTH // LANES):
            qo_ref[:, j * LANES:(j + 1) * LANES] = rot(q_ref[:, j * LANES:(j + 1) * LANES]).astype(out_dtype)
        ko_ref[...] = rot(k_ref[...]).astype(out_dtype)

    kcol = 0 if inverse else ATTN_WIDTH // KV_WIDTH
    return pl.pallas_call(
        body, name=f"rope_{'inv' if inverse else 'fwd'}_{t_lat}", grid=(t_lat // tr,),
        out_shape=[SDS((t_lat, ATTN_WIDTH), out_dtype), SDS((t_lat, KV_WIDTH), out_dtype)],
        in_specs=[pl.BlockSpec((tr, ATTN_WIDTH), lambda i: (i, 0)),
                  pl.BlockSpec((tr, KV_WIDTH), lambda i: (i, kcol)),
                  pl.BlockSpec((tr, LANES), lambda i: (i % per, 0)),
                  pl.BlockSpec((tr, LANES), lambda i: (i % per, 0))],
        out_specs=[pl.BlockSpec((tr, ATTN_WIDTH), lambda i: (i, 0)), pl.BlockSpec((tr, KV_WIDTH), lambda i: (i, 0))],
    )(q_src, k_src, cos, sin)


_SMEM_SPEC = pl.BlockSpec(memory_space=pltpu.SMEM)
_KCOL = ATTN_WIDTH // KV_WIDTH
_VCOL = _KCOL + 1


def _win_specs(arr_cols, nb, col):
    del arr_cols
    return [pl.BlockSpec((Q_BLOCK, KV_WIDTH), lambda b, i, d=d: (b * nb + jnp.clip(i + d, 0, nb - 1), col))
            for d in (-1, 0, 1)]


def _win_mask(i, l):
    shape = (_attn_stack(Q_BLOCK) * Q_BLOCK, 3 * Q_BLOCK)
    qpos = i * Q_BLOCK + (lax.broadcasted_iota(jnp.int32, shape, 0) & (Q_BLOCK - 1))
    kpos = (i - 1) * Q_BLOCK + lax.broadcasted_iota(jnp.int32, shape, 1)
    return jnp.logical_and(jnp.abs(qpos - kpos) <= WINDOW, jnp.logical_and(kpos >= 0, kpos < l))


def _attn_lat_fwd(qr, kr, p, sink, dims):
    b, l, lc = dims
    nb = l // Q_BLOCK
    t_lat = b * l
    cbase = t_lat // lc

    def body(sink_ref, q_ref, kp, kc, kn, vp, vc, vn, ck_ref, cv_ref, o_ref, lse_ref):
        kwin = jnp.concatenate([kp[...], kc[...], kn[...]], axis=0)
        vwin = jnp.concatenate([vp[...], vc[...], vn[...]], axis=0)
        kv = [(kwin, vwin, _win_mask(pl.program_id(1), l)), (ck_ref[...], cv_ref[...], None)]
        _attn_fwd_block(sink_ref, q_ref, kv, Q_BLOCK, o_ref, lse_ref)

    return pl.pallas_call(
        body, name=f"attn_lat_fwd_{t_lat}", grid=(b, nb),
        out_shape=[SDS((t_lat, ATTN_WIDTH), F32), SDS((t_lat, LANES), F32)],
        in_specs=[_SMEM_SPEC, pl.BlockSpec((Q_BLOCK, ATTN_WIDTH), lambda bb, i: (bb * nb + i, 0))]
        + _win_specs(None, nb, 0) + _win_specs(None, nb, _VCOL)
        + [pl.BlockSpec((lc, KV_WIDTH), lambda bb, i: (cbase + bb, _KCOL)),
           pl.BlockSpec((lc, KV_WIDTH), lambda bb, i: (cbase + bb, _VCOL))],
        out_specs=[pl.BlockSpec((Q_BLOCK, ATTN_WIDTH), lambda bb, i: (bb * nb + i, 0)),
                   pl.BlockSpec((Q_BLOCK, LANES), lambda bb, i: (bb * nb + i, 0))],
    )(sink, qr, kr, kr, kr, p, p, p, p, p)


def _attn_lat_bwd_dq(qr, kr, p, sink, o, lse, dout, dims):
    b, l, lc = dims
    nb = l // Q_BLOCK
    t_lat = b * l
    cbase = t_lat // lc

    def body(sink_ref, q_ref, kp, kc, kn, vp, vc, vn, ck_ref, cv_ref, o_ref, lse_ref, do_ref,
             dq_ref, delta_ref, dkc_ref, dvc_ref, dsk_ref):
        bb, i = pl.program_id(0), pl.program_id(1)

        @pl.when(i == 0)
        def _():
            dkc_ref[...] = jnp.zeros_like(dkc_ref)
            dvc_ref[...] = jnp.zeros_like(dvc_ref)

        @pl.when(jnp.logical_and(bb == 0, i == 0))
        def _():
            dsk_ref[...] = jnp.zeros_like(dsk_ref)

        kwin = jnp.concatenate([kp[...], kc[...], kn[...]], axis=0)
        vwin = jnp.concatenate([vp[...], vc[...], vn[...]], axis=0)
        kv = [(kwin, vwin, _win_mask(i, l)), (ck_ref[...], cv_ref[...], None)]
        delta_t, dsk, dkv = _attn_bwd_block(sink_ref, q_ref, do_ref, o_ref, lse_ref[...], kv, [False, True], Q_BLOCK,
                                            dq_ref)
        delta_ref[...] = delta_t
        dkc_ref[...] += dkv[1][0]
        dvc_ref[...] += dkv[1][1]
        dsk_ref[0:1, :] += dsk

    qspec = pl.BlockSpec((Q_BLOCK, ATTN_WIDTH), lambda bb, i: (bb * nb + i, 0))
    tspec = pl.BlockSpec((Q_BLOCK, LANES), lambda bb, i: (bb * nb + i, 0))
    cspec = pl.BlockSpec((lc, KV_WIDTH), lambda bb, i: (bb, 0))
    return pl.pallas_call(
        body, name=f"attn_lat_bwd_dq_{t_lat}", grid=(b, nb),
        out_shape=[SDS((t_lat, ATTN_WIDTH), F32), SDS((t_lat, LANES), F32), SDS((b * lc, KV_WIDTH), F32),
                   SDS((b * lc, KV_WIDTH), F32), SDS((8, LANES), F32)],
        in_specs=[_SMEM_SPEC, qspec] + _win_specs(None, nb, 0) + _win_specs(None, nb, _VCOL)
        + [pl.BlockSpec((lc, KV_WIDTH), lambda bb, i: (cbase + bb, _KCOL)),
           pl.BlockSpec((lc, KV_WIDTH), lambda bb, i: (cbase + bb, _VCOL)), qspec, tspec, qspec],
        out_specs=[qspec, tspec, cspec, cspec, pl.BlockSpec((8, LANES), lambda bb, i: (0, 0))],
    )(sink, qr, kr, kr, kr, p, p, p, p, p, o, lse, dout)


def _attn_lat_bwd_dkv(qr, kr, p, lse, delta, dout, dims):
    b, l, _ = dims
    nb = l // Q_BLOCK
    t_lat = b * l

    def body(k_ref, v_ref, *refs):
        j = pl.program_id(1)
        kj, vj = k_ref[...], v_ref[...]
        n_q = 3 * Q_PER_KV * Q_BLOCK
        sub = lax.broadcasted_iota(jnp.int32, (8, LANES), 0)
        col = lax.broadcasted_iota(jnp.int32, (Q_BLOCK, n_q), 1)
        i_of = j + col // (Q_PER_KV * Q_BLOCK) - 1
        qpos = i_of * Q_BLOCK + (col & (Q_BLOCK - 1))
        kpos = j * Q_BLOCK + lax.broadcasted_iota(jnp.int32, (Q_BLOCK, n_q), 0)
        mask = jnp.logical_and(jnp.abs(qpos - kpos) <= WINDOW, jnp.logical_and(i_of >= 0, i_of < nb))
        lse_rows = [refs[4 * n + 2][...].T[0:8, :] for n in range(3)]
        delta_rows = [refs[4 * n + 3][...].T[0:8, :] for n in range(3)]
        dk = jnp.zeros((Q_BLOCK, KV_WIDTH), F32)
        dv = jnp.zeros((Q_BLOCK, KV_WIDTH), F32)
        for h in range(2):
            hm = _hm(h * HEAD_DIM)
            kh, vh = _kv_masked(kj, h), _kv_masked(vj, h)
            qs, dos, lrow, drow = [], [], [], []
            for n in range(3):
                q_ref, do_ref = refs[4 * n], refs[4 * n + 1]
                for _, hq, pair, roll in _kv_group(h):
                    sl = slice(pair * LANES, (pair + 1) * LANES)
                    qs.append(_to_kv_lanes(q_ref[:, sl], roll))
                    dos.append(_to_kv_lanes(do_ref[:, sl], roll))
                    lrow.append(jnp.sum(jnp.where(sub == hq, lse_rows[n], 0.0), axis=0, keepdims=True))
                    drow.append(jnp.sum(jnp.where(sub == hq, delta_rows[n], 0.0), axis=0, keepdims=True))
            q, do = jnp.concatenate(qs, axis=0), jnp.concatenate(dos, axis=0)
            lse_r, delta_r = jnp.concatenate(lrow, axis=1), jnp.concatenate(drow, axis=1)
            s_t = jnp.where(mask, _dg_nt(kh, q) * _SCALE, NEG_INF)
            p_t = jnp.exp(s_t - lse_r)
            ds_t = p_t * (_dg_nt(vh, do) - delta_r) * _SCALE
            dv = dv + jnp.where(hm, jnp.dot(_mx(p_t), do, preferred_element_type=F32), 0.0)
            dk = dk + jnp.where(hm, jnp.dot(_mx(ds_t), q, preferred_element_type=F32), 0.0)
        dk_ref, dv_ref = refs[12], refs[13]
        dk_ref[...] = dk
        dv_ref[...] = dv

    def blk(width, d, col=0):
        return pl.BlockSpec((Q_BLOCK, width), lambda bb, j: (bb * nb + jnp.clip(j + d, 0, nb - 1), col))

    in_specs = [blk(KV_WIDTH, 0), blk(KV_WIDTH, 0, _VCOL)]
    args = [kr, p]
    for d in (-1, 0, 1):
        in_specs += [blk(ATTN_WIDTH, d), blk(ATTN_WIDTH, d), blk(LANES, d), blk(LANES, d)]
        args += [qr, dout, lse, delta]
    return pl.pallas_call(
        body, name=f"attn_lat_bwd_dkv_{t_lat}", grid=(b, nb),
        out_shape=[SDS((t_lat, KV_WIDTH), F32), SDS((t_lat, KV_WIDTH), F32)],
        in_specs=in_specs, out_specs=[blk(KV_WIDTH, 0), blk(KV_WIDTH, 0)],
    )(*args)


def _attn_ctx_fwd(p, sink, dims):
    b, l, lc = dims
    cbase = b * l // lc

    def body(sink_ref, q_ref, k_ref, v_ref, o_ref, lse_ref):
        _attn_fwd_block(sink_ref, q_ref, [(k_ref[...], v_ref[...], None)], lc, o_ref, lse_ref)

    return pl.pallas_call(
        body, name=f"attn_ctx_fwd_{b * lc}", grid=(b,),
        out_shape=[SDS((b * lc, ATTN_WIDTH), F32), SDS((b * lc, LANES), F32)],
        in_specs=[_SMEM_SPEC, pl.BlockSpec((lc, ATTN_WIDTH), lambda bb: (cbase + bb, 0)),
                  pl.BlockSpec((lc, KV_WIDTH), lambda bb: (cbase + bb, _KCOL)),
                  pl.BlockSpec((lc, KV_WIDTH), lambda bb: (cbase + bb, _VCOL))],
        out_specs=[pl.BlockSpec((lc, ATTN_WIDTH), lambda bb: (bb, 0)), pl.BlockSpec((lc, LANES), lambda bb: (bb, 0))],
    )(sink, p, p, p)


def _attn_ctx_bwd(p, sink, o, lse, dout, dims):
    b, l, lc = dims
    cbase = b * l // lc

    def body(sink_ref, q_ref, k_ref, v_ref, o_ref, lse_ref, do_ref, dq_ref, dk_ref, dv_ref, dsk_ref):
        bb = pl.program_id(0)

        @pl.when(bb == 0)
        def _():
            dsk_ref[...] = jnp.zeros_like(dsk_ref)

        _, dsk, dkv = _attn_bwd_block(sink_ref, q_ref, do_ref, o_ref, lse_ref[...], [(k_ref[...], v_ref[...], None)],
                                      [True], lc, dq_ref)
        dk_ref[...] = dkv[0][0]
        dv_ref[...] = dkv[0][1]
        dsk_ref[0:1, :] += dsk

    qs = pl.BlockSpec((lc, ATTN_WIDTH), lambda bb: (bb, 0))
    ks = pl.BlockSpec((lc, KV_WIDTH), lambda bb: (bb, 0))
    return pl.pallas_call(
        body, name=f"attn_ctx_bwd_{b * lc}", grid=(b,),
        out_shape=[SDS((b * lc, ATTN_WIDTH), F32), SDS((b * lc, KV_WIDTH), F32), SDS((b * lc, KV_WIDTH), F32),
                   SDS((8, LANES), F32)],
        in_specs=[_SMEM_SPEC, pl.BlockSpec((lc, ATTN_WIDTH), lambda bb: (cbase + bb, 0)),
                  pl.BlockSpec((lc, KV_WIDTH), lambda bb: (cbase + bb, _KCOL)),
                  pl.BlockSpec((lc, KV_WIDTH), lambda bb: (cbase + bb, _VCOL)),
                  qs, pl.BlockSpec((lc, LANES), lambda bb: (bb, 0)),
                  pl.BlockSpec((lc, ATTN_WIDTH), lambda bb: (cbase + bb, 0))],
        out_specs=[qs, ks, ks, pl.BlockSpec((8, LANES), lambda bb: (0, 0))],
    )(sink, p, p, p, o, lse, dout)


_CONV_COL = (ATTN_WIDTH + 2 * KV_WIDTH) // CONV_WIDTH


def _shift_prev(z, n):
    rows = lax.broadcasted_iota(jnp.int32, z.shape, 0)
    return jnp.where(rows == 0, 0.0, pltpu.roll(z, 1, 0))


def _shift_next(z, n):
    rows = lax.broadcasted_iota(jnp.int32, z.shape, 0)
    return jnp.where(rows == n - 1, 0.0, pltpu.roll(z, n - 1, 0))


def _conv_fwd(p, w, base, n_seq, ls):
    def body(cb_ref, cc_ref, cx_ref, w_ref, o_ref):
        z = cc_ref[...] * cx_ref[...]
        c3 = _shift_prev(z, ls) * w_ref[0:1, :] + z * w_ref[1:2, :] + _shift_next(z, ls) * w_ref[2:3, :]
        o_ref[...] = cb_ref[...] * c3

    return pl.pallas_call(
        body, name=f"conv_fwd_{n_seq}x{ls}", grid=(n_seq,),
        out_shape=SDS((n_seq * ls, CONV_WIDTH), F32),
        in_specs=[pl.BlockSpec((ls, CONV_WIDTH), lambda s, c=c: (base + s, _CONV_COL + c)) for c in range(3)]
        + [pl.BlockSpec(w.shape, lambda s: (0, 0))],
        out_specs=pl.BlockSpec((ls, CONV_WIDTH), lambda s: (s, 0)),
    )(p, p, p, w)


def _conv_bwd(p, w, dout, base, n_seq, ls):
    dcol = ATTN_WIDTH // CONV_WIDTH

    def body(cb_ref, cc_ref, cx_ref, w_ref, do_ref, dcb_ref, dcc_ref, dcx_ref, dw_ref):
        @pl.when(pl.program_id(0) == 0)
        def _():
            dw_ref[...] = jnp.zeros_like(dw_ref)

        cc, cx = cc_ref[...], cx_ref[...]
        z = cc * cx
        zp, zn = _shift_prev(z, ls), _shift_next(z, ls)
        c3 = zp * w_ref[0:1, :] + z * w_ref[1:2, :] + zn * w_ref[2:3, :]
        do = do_ref[...]
        dcb_ref[...] = do * c3
        e = do * cb_ref[...]
        dz = _shift_next(e, ls) * w_ref[0:1, :] + e * w_ref[1:2, :] + _shift_prev(e, ls) * w_ref[2:3, :]
        dcc_ref[...] = dz * cx
        dcx_ref[...] = dz * cc
        dw_ref[0:1, :] += jnp.sum(e * zp, axis=0, keepdims=True)
        dw_ref[1:2, :] += jnp.sum(e * z, axis=0, keepdims=True)
        dw_ref[2:3, :] += jnp.sum(e * zn, axis=0, keepdims=True)

    ospec = pl.BlockSpec((ls, CONV_WIDTH), lambda s: (s, 0))
    return pl.pallas_call(
        body, name=f"conv_bwd_{n_seq}x{ls}", grid=(n_seq,),
        out_shape=[SDS((n_seq * ls, CONV_WIDTH), F32)] * 3 + [SDS((8, CONV_WIDTH), F32)],
        in_specs=[pl.BlockSpec((ls, CONV_WIDTH), lambda s, c=c: (base + s, _CONV_COL + c)) for c in range(3)]
        + [pl.BlockSpec(w.shape, lambda s: (0, 0)), pl.BlockSpec((ls, CONV_WIDTH), lambda s: (base + s, dcol))],
        out_specs=[ospec, ospec, ospec, pl.BlockSpec((8, CONV_WIDTH), lambda s: (0, 0))],
    )(p, p, p, w, dout)


def _make_att_conv(dims, cos, sin):
    b, l, lc = dims
    t_lat = b * l

    def forward(p, conv_w, sink):
        qr, kr = _rope_call(p, p, cos, sin, dims, inverse=False)
        o_lat, lse_lat = _attn_lat_fwd(qr, kr, p, sink, dims)
        o_ctx, lse_ctx = _attn_ctx_fwd(p, sink, dims)
        conv_lat = _conv_fwd(p, conv_w, 0, b, l)
        conv_ctx = _conv_fwd(p, conv_w, t_lat // lc, b, lc)
        out = jnp.concatenate([jnp.concatenate([o_lat, conv_lat], axis=1),
                               jnp.concatenate([o_ctx, conv_ctx], axis=1)], axis=0)
        return out, (p, conv_w, sink, qr, kr, o_lat, lse_lat, o_ctx, lse_ctx)

    @jax.custom_vjp
    def op(p, conv_w, sink):
        return forward(p, conv_w, sink)[0]

    def op_bwd(res, dout):
        p, conv_w, sink, qr, kr, o_lat, lse_lat, o_ctx, lse_ctx = res
        dqr, delta, dkc1, dvc1, dsk1 = _attn_lat_bwd_dq(qr, kr, p, sink, o_lat, lse_lat, dout, dims)
        dkr, dv = _attn_lat_bwd_dkv(qr, kr, p, lse_lat, delta, dout, dims)
        dq, dk = _rope_call(dqr, dkr, cos, sin, dims, inverse=True)
        dqc, dkc2, dvc2, dsk2 = _attn_ctx_bwd(p, sink, o_ctx, lse_ctx, dout, dims)
        dcb_l, dcc_l, dcx_l, dw_l = _conv_bwd(p, conv_w, dout, 0, b, l)
        dcb_c, dcc_c, dcx_c, dw_c = _conv_bwd(p, conv_w, dout, t_lat // lc, b, lc)
        zeros_u = jnp.zeros((p.shape[0], SSM_WIDTH), F32)
        lat = jnp.concatenate([dq, dk, dv, dcb_l, dcc_l, dcx_l], axis=1)
        ctx = jnp.concatenate([dqc, dkc1 + dkc2, dvc1 + dvc2, dcb_c, dcc_c, dcx_c], axis=1)
        dp = jnp.concatenate([jnp.concatenate([lat, ctx], axis=0), zeros_u], axis=1)
        dsink = (dsk1 + dsk2)[0:1, :N_Q_HEADS]
        return dp, (dw_l + dw_c)[:3], dsink

    op.defvjp(forward, op_bwd)
    return op


def _cmul(ar, ai, br, bi):
    return ar * br - ai * bi, ar * bi + ai * br


def _scan_chunk(buf, lr_ref, li_ref, carry, dec, tc):
    sub = lax.broadcasted_iota(jnp.int32, (8, LANES), 0)
    groups = range(tc // 8 - 1, -1, -1) if dec else range(tc // 8)
    for s in range(SSM_N // LANES):
        re, im = slice(s * LANES, (s + 1) * LANES), slice(SSM_N + s * LANES, SSM_N + (s + 1) * LANES)
        pw = [(lr_ref[:, re], li_ref[:, re])]
        for _ in range(7):
            pw.append(_cmul(*pw[-1], *pw[0]))
        p8r, p8i = jnp.zeros((8, LANES), F32), jnp.zeros((8, LANES), F32)
        for k in range(8):
            row = 7 - k if dec else k
            p8r = jnp.where(sub == row, pw[k][0], p8r)
            p8i = jnp.where(sub == row, pw[k][1], p8i)
        steps = []
        for sft in (1, 2, 4):
            keep = sub < 8 - sft if dec else sub >= sft
            steps.append((8 - sft if dec else sft, jnp.where(keep, pw[sft - 1][0], 0.0),
                          jnp.where(keep, pw[sft - 1][1], 0.0)))
        cr, ci = carry[0:1, re], carry[0:1, im]
        for r in groups:
            rows = slice(8 * r, 8 * r + 8)
            xr, xi = buf[rows, re], buf[rows, im]
            for amount, mr, mi in steps:
                sr, si = pltpu.roll(xr, amount, 0), pltpu.roll(xi, amount, 0)
                xr, xi = xr + mr * sr - mi * si, xi + mr * si + mi * sr
            xr, xi = xr + p8r * cr - p8i * ci, xi + p8r * ci + p8i * cr
            buf[rows, re] = xr
            buf[rows, im] = xi
            last = 8 * r if dec else 8 * r + 7
            cr, ci = buf[last:last + 1, re], buf[last:last + 1, im]
        carry[0:1, re] = cr
        carry[0:1, im] = ci


def _ssm_chunks(dims, ctx_first, dec):
    b, l, lc = dims
    tc = _pick(lc, (256, 128, 64, 32, 16, 8))
    nc_c, nc_l = lc // tc, l // tc
    n_chunks = nc_c + nc_l
    ctx_base = b * l // tc
    g8 = tc // 8

    def chunk(bb, j):
        if ctx_first:
            is_ctx = j < nc_c
            jj = jnp.where(is_ctx, j, j - nc_c)
        else:
            is_ctx = j >= nc_l
            jj = jnp.where(is_ctx, j - nc_l, j)
        ic = nc_c - 1 - jj if dec else jj
        il = nc_l - 1 - jj if dec else jj
        return jnp.where(is_ctx, ctx_base + bb * nc_c + ic, bb * nc_l + il)

    def edge(bb, j):
        jn = jnp.minimum(j + 1, n_chunks - 1)
        return chunk(bb, jn) * g8 + (g8 - 1 if dec else 0)

    return tc, n_chunks, chunk, edge


def _ssm_fwd_call(u, bmat, cmat, lam_r, lam_i, dims, dec):
    n2 = 2 * SSM_N
    tc, n_chunks, chunk, _ = _ssm_chunks(dims, True, dec)

    def body(u_ref, b_ref, c_ref, lr_ref, li_ref, h_ref, y_ref, carry):
        @pl.when(pl.program_id(1) == 0)
        def _():
            carry[...] = jnp.zeros_like(carry)

        h_ref[...] = jnp.dot(_mx(u_ref[...]), b_ref[...], preferred_element_type=F32)
        _scan_chunk(h_ref, lr_ref, li_ref, carry, dec, tc)
        y_ref[...] = jnp.dot(_mx(h_ref[...]), c_ref[...], preferred_element_type=F32)

    def full(shape):
        return pl.BlockSpec(shape, lambda bb, j: (0, 0))

    return pl.pallas_call(
        body, name=f"ssm_fwd_{'d' if dec else 'u'}_{u.shape[0]}", grid=(dims[0], n_chunks),
        out_shape=[SDS((u.shape[0], n2), F32), SDS(u.shape, F32)],
        in_specs=[pl.BlockSpec((tc, SSM_WIDTH), lambda bb, j: (chunk(bb, j), 0)), full((SSM_WIDTH, n2)),
                  full((n2, SSM_WIDTH)), full((1, SSM_N)), full((1, SSM_N))],
        out_specs=[pl.BlockSpec((tc, n2), lambda bb, j: (chunk(bb, j), 0)),
                   pl.BlockSpec((tc, SSM_WIDTH), lambda bb, j: (chunk(bb, j), 0))],
        scratch_shapes=[pltpu.VMEM((8, n2), F32)],
    )(u, bmat, cmat, lam_r, lam_i)


def _ssm_bwd_call(gy, h, u, bmat, cmat, lam_r, lam_i, dims, dec):
    n2 = 2 * SSM_N
    tc, n_chunks, chunk, edge = _ssm_chunks(dims, False, dec)

    def body(gy_ref, h_ref, hb_ref, u_ref, b_ref, c_ref, lr_ref, li_ref, du_ref, db_ref, dct_ref, dl_ref, gd, carry):
        bb, j = pl.program_id(0), pl.program_id(1)

        @pl.when(j == 0)
        def _():
            carry[...] = jnp.zeros_like(carry)

        @pl.when(jnp.logical_and(bb == 0, j == 0))
        def _():
            db_ref[...] = jnp.zeros_like(db_ref)
            dct_ref[...] = jnp.zeros_like(dct_ref)
            dl_ref[...] = jnp.zeros_like(dl_ref)

        gyv = _mx(gy_ref[...])
        gd[...] = _dg_nt(gyv, c_ref[...])
        _scan_chunk(gd, lr_ref, li_ref, carry, dec, tc)
        gdv = _mx(gd[...])
        du_ref[...] = _dg_nt(gdv, b_ref[...])
        db_ref[...] += _dg_tn(_mx(u_ref[...]), gdv)
        dct_ref[...] += _dg_tn(gyv, _mx(h_ref[...]))
        rows = lax.broadcasted_iota(jnp.int32, (tc, LANES), 0)
        has_next = j + 1 < n_chunks
        er = 7 if dec else 0
        for s in range(SSM_N // LANES):
            re, im = slice(s * LANES, (s + 1) * LANES), slice(SSM_N + s * LANES, SSM_N + (s + 1) * LANES)
            dr, di, hr, hi = gd[:, re], gd[:, im], h_ref[:, re], h_ref[:, im]
            br = jnp.where(has_next, hb_ref[er:er + 1, re], 0.0)
            bi = jnp.where(has_next, hb_ref[er:er + 1, im], 0.0)
            if dec:
                nr = jnp.where(rows == 0, br, pltpu.roll(hr, 1, 0))
                ni = jnp.where(rows == 0, bi, pltpu.roll(hi, 1, 0))
            else:
                nr = jnp.where(rows == tc - 1, br, pltpu.roll(hr, tc - 1, 0))
                ni = jnp.where(rows == tc - 1, bi, pltpu.roll(hi, tc - 1, 0))
            dl_ref[0:1, re] += jnp.sum(dr * nr + di * ni, axis=0, keepdims=True)
            dl_ref[0:1, im] += jnp.sum(di * nr - dr * ni, axis=0, keepdims=True)

    def full(shape):
        return pl.BlockSpec(shape, lambda bb, j: (0, 0))

    def at_chunk(width):
        return pl.BlockSpec((tc, width), lambda bb, j: (chunk(bb, j), 0))

    return pl.pallas_call(
        body, name=f"ssm_bwd_{'d' if dec else 'u'}_{u.shape[0]}", grid=(dims[0], n_chunks),
        out_shape=[SDS(u.shape, F32), SDS((SSM_WIDTH, n2), F32), SDS((SSM_WIDTH, n2), F32), SDS((8, n2), F32)],
        in_specs=[at_chunk(SSM_WIDTH), at_chunk(n2), pl.BlockSpec((8, n2), lambda bb, j: (edge(bb, j), 0)),
                  at_chunk(SSM_WIDTH), full((SSM_WIDTH, n2)), full((n2, SSM_WIDTH)), full((1, SSM_N)),
                  full((1, SSM_N))],
        out_specs=[at_chunk(SSM_WIDTH), full((SSM_WIDTH, n2)), full((SSM_WIDTH, n2)), full((8, n2))],
        scratch_shapes=[pltpu.VMEM((tc, n2), F32), pltpu.VMEM((8, n2), F32)],
    )(gy, h, h, u, bmat, cmat, lam_r, lam_i)


def _make_ssm(dims, direction):
    @jax.custom_vjp
    def ssm(u, bmat, cmat, lam_r, lam_i):
        return _ssm_fwd_call(u, _mx(bmat), _mx(cmat), lam_r, lam_i, dims, direction == 1)[1]

    def ssm_fwd(u, bmat, cmat, lam_r, lam_i):
        h, y = _ssm_fwd_call(u, _mx(bmat), _mx(cmat), lam_r, lam_i, dims, direction == 1)
        return y, (u, bmat, cmat, lam_r, lam_i, h)

    def ssm_bwd(res, gy):
        u, bmat, cmat, lam_r, lam_i, h = res
        du, db, dct, dl = _ssm_bwd_call(gy, h, u, _mx(bmat), _mx(cmat), lam_r, -lam_i, dims, direction == 0)
        return du, db, dct.T, dl[0:1, :SSM_N], dl[0:1, SSM_N:]

    ssm.defvjp(ssm_fwd, ssm_bwd)
    return ssm


def _block_diag(m):
    g, a, bdim = m.shape[-3:]
    eye = jnp.eye(g, dtype=m.dtype)
    full = m[..., :, :, None, :] * eye[:, None, :, None]
    return full.reshape(m.shape[:-3] + (g * a, g * bdim))


def _ssm_mats(lam_re, lam_im, log_dt, b_re, b_im, c_re, c_im):
    lam = lax.complex(lam_re, lam_im)
    dt = jnp.exp(log_dt)[..., None]
    lam_bar = jnp.exp(lam * dt)
    b_bar = ((lam_bar - 1) / lam)[..., None] * lax.complex(b_re, b_im)
    bt = jnp.swapaxes(b_bar, -1, -2)
    bmat = jnp.concatenate([_block_diag(jnp.real(bt)), _block_diag(jnp.imag(bt))], axis=-1)
    cmat = jnp.concatenate([_block_diag(jnp.swapaxes(c_re, -1, -2)), _block_diag(-jnp.swapaxes(c_im, -1, -2))],
                           axis=-2)
    flat = lam_bar.shape[:2] + (1, SSM_N)
    return bmat, cmat, jnp.real(lam_bar).reshape(flat), jnp.imag(lam_bar).reshape(flat)


def _loss_call(y, target):
    t, d = y.shape
    tr = _pick(t, (ROW_TILE, 128, 64, 32, 16, 8))

    def body(y_ref, t_ref, acc_ref, dy_ref):
        @pl.when(pl.program_id(0) == 0)
        def _():
            acc_ref[...] = jnp.zeros_like(acc_ref)

        diff = y_ref[...] - t_ref[...]
        dy_ref[...] = diff * (1.0 / d)
        acc_ref[0:1, :] += jnp.sum(diff * diff, axis=0, keepdims=True)

    spec = pl.BlockSpec((tr, d), lambda i: (i, 0))
    return pl.pallas_call(
        body, name=f"loss_{t}", grid=(t // tr,),
        out_shape=[SDS((8, d), F32), SDS((t, d), F32)],
        in_specs=[spec, spec], out_specs=[pl.BlockSpec((8, d), lambda i: (0, 0)), spec],
    )(y, target)


def _adamw_call(w, g, m, v):
    r, c = w.shape
    tr = _pick(r, [t for t in (1024, 512, 256, 128, 64, 32, 16, 8) if t * c <= 256 * 1024])

    def body(w_ref, g_ref, m_ref, v_ref, d_ref, mo_ref, vo_ref):
        gv = g_ref[...]
        mn = ADAM_B1 * m_ref[...] + (1.0 - ADAM_B1) * gv
        vn = ADAM_B2 * v_ref[...] + (1.0 - ADAM_B2) * (gv * gv)
        m_hat = mn / (1.0 - ADAM_B1 ** ADAM_STEP)
        v_hat = vn / (1.0 - ADAM_B2 ** ADAM_STEP)
        d_ref[...] = -ADAM_LR * (m_hat / (jnp.sqrt(v_hat) + ADAM_EPS) + ADAM_WD * w_ref[...])
        mo_ref[...] = mn
        vo_ref[...] = vn

    spec = pl.BlockSpec((tr, c), lambda i: (i, 0))
    return pl.pallas_call(
        body, name=f"adamw_{r}x{c}", grid=(r // tr,),
        out_shape=[SDS((r, c), F32)] * 3, in_specs=[spec] * 4, out_specs=[spec] * 3,
    )(w, g, m, v)


def _make_modulation(b, me):
    n_ex = N_DEV * b

    def a_matrix(c_ctx, c_all):
        rows = [jax.nn.silu(c_all), jax.nn.silu(c_ctx)[None, :],
                jnp.zeros((MOD_ROWS - n_ex - 1, c_all.shape[1]), F32)]
        return jnp.concatenate(rows, axis=0)

    def forward(c_ctx, b_ada, w_loc, c_all):
        depth, _, width = w_loc.shape
        a_mat = a_matrix(c_ctx, c_all)
        cols = jnp.stack([_mm_call('nn', a_mat, w_loc[layer]) for layer in range(depth)])
        gathered = _all_gather(cols.reshape(depth * MOD_ROWS, width))
        full = jnp.moveaxis(gathered.reshape(N_DEV, depth, MOD_ROWS, width), 0, 2).reshape(depth, MOD_ROWS, -1)
        mine = lax.dynamic_slice_in_dim(full, me * b, b, axis=1)
        return jnp.concatenate([mine, full[:, n_ex:n_ex + 1]], axis=1) + b_ada[:, None, :]

    @jax.custom_vjp
    def modulation(c_ctx, b_ada, w_loc, c_all):
        return forward(c_ctx, b_ada, w_loc, c_all)

    def modulation_fwd(c_ctx, b_ada, w_loc, c_all):
        return forward(c_ctx, b_ada, w_loc, c_all), (c_ctx, w_loc, c_all)

    def modulation_bwd(res, g):
        c_ctx, w_loc, c_all = res
        depth, _, width = w_loc.shape
        rows = depth * (b + 1)
        flat = jnp.pad(g.reshape(rows, -1), ((0, -rows % 8), (0, 0)))
        everyone = _all_gather(flat)[:, :rows].reshape(N_DEV, depth, b + 1, -1)
        examples = jnp.moveaxis(everyone[:, :, :b], 0, 1).reshape(depth, n_ex, -1)
        ctx_total = jnp.sum(everyone[:, :, b], axis=0)
        g_mat = jnp.concatenate([examples, ctx_total[:, None, :],
                                 jnp.zeros((depth, MOD_ROWS - n_ex - 1, g.shape[-1]), F32)], axis=1)
        g_cols = lax.dynamic_slice_in_dim(g_mat, me * width, width, axis=2)
        a_mat, a_vjp = jax.vjp(lambda cc: a_matrix(cc, c_all), c_ctx)
        dw = jnp.stack([_mm_call('tn', a_mat, g_cols[layer]) for layer in range(depth)])
        da = sum(_mm_call('nt', g_cols[layer], w_loc[layer]) for layer in range(depth))
        dc_ctx, = a_vjp(da)
        return dc_ctx, jnp.sum(g, axis=1), dw, jnp.zeros_like(c_all)

    modulation.defvjp(modulation_fwd, modulation_bwd)
    return modulation


def _forward(diff, gathered, c_all, me, dims, depth):
    b, l, lc = dims
    t_lat = b * l
    d_model = diff['x'].shape[-1]
    cos, sin = _rope_tables(l)
    glu = _make_rowwise_op("glu", _glu_f, SSM_WIDTH, 3, 0, l)
    att_conv = _make_att_conv(dims, cos, sin)
    ssms = [_make_ssm(dims, 0), _make_ssm(dims, 1)]
    bmats, cmats, lam_r, lam_i = _ssm_mats(*[diff['ssm_' + n] for n in
                                             ('lam_re', 'lam_im', 'log_dt', 'b_re', 'b_im', 'c_re', 'c_im')])

    def middle(p, prm):
        ac = att_conv(p, prm['conv_w'], prm['sink'])
        u = p[:, IN_WIDTH - SSM_WIDTH:]
        ys = [ssms[d](u, prm['bmat'][d], prm['cmat'][d], prm['lam_r'][d], prm['lam_i'][d]) for d in range(2)]
        return ac, glu(ys[0], ys[1], u, prm['ssm_d'], prm['w_glu'], prm['b_glu'])

    nm = _rowwise("norm_mod", _norm_mod_f, (d_model,), l)
    rg = _rowwise("resid_gate", _resid_gate_f, (d_model,), l)
    mixer_block, mlp_block = _make_block_ops(nm, rg, middle, t_lat + b * lc)
    mixer_block_last, _ = _make_block_ops(nm, rg, middle, t_lat)

    mods = _make_modulation(b, me)(diff['c_ctx'], diff['b_ada'], diff['w_ada'], c_all)
    h = jnp.concatenate([diff['x'].reshape(t_lat, d_model), diff['ctx'].reshape(b * lc, d_model)], axis=0)

    for layer in range(depth):
        last = layer == depth - 1
        sh1, sc1, g1, sh2, sc2, g2 = [m.reshape(b + 1, 1, d_model) for m in jnp.split(mods[layer], N_MOD, axis=-1)]
        if last:
            g1, sh2, sc2, g2 = g1[:b], sh2[:b], sc2[:b], g2[:b]
        ng = diff['norm_g'][layer]
        prm = dict(conv_w=diff['conv_w'][layer], sink=diff['attn_sink'][layer][None, :], bmat=bmats[layer],
                   cmat=cmats[layer], lam_r=lam_r[layer], lam_i=lam_i[layer], ssm_d=diff['ssm_d'][layer][None, :],
                   w_glu=diff['w_glu'][layer], b_glu=diff['b_glu'][layer][None, :])
        h = (mixer_block_last if last else mixer_block)(
            h, sh1, sc1, ng[0:1], gathered['w_in'][layer], diff['w_in'][layer], prm,
            gathered['w_out'][layer], diff['w_out'][layer], g1, ng[1:2])
        h = mlp_block(h, sh2, sc2, ng[2:3], gathered['w_mlp_in'][layer], diff['w_mlp_in'][layer],
                      gathered['w_mlp_out'][layer], diff['w_mlp_out'][layer], g2, ng[3:4])
    return h


BIG_TRANSPOSED = {'w_in': True, 'w_out': False, 'w_mlp_in': True, 'w_mlp_out': False}


def _big_rows(weights):
    blocks = []
    depth = weights['w_ada'].shape[0]
    for n in BIG:
        for layer in range(depth):
            w = weights[n][layer]
            blocks.append((layer, n, w.shape[1] if BIG_TRANSPOSED[n] else w.shape[0]))
    return blocks


def _stack_big(arrs, dtype):
    parts = []
    for n in BIG:
        w = jnp.swapaxes(arrs[n], 1, 2) if BIG_TRANSPOSED[n] else arrs[n]
        parts.append(w.reshape(-1, w.shape[-1]).astype(dtype))
    rows = sum(p.shape[0] for p in parts)
    parts.append(jnp.zeros((-rows % BIG_ROW_TILE, parts[0].shape[1]), dtype))
    return jnp.concatenate(parts, axis=0)


def _adamw_any(w, g, m, v):
    flat = [a.reshape(-1, w.shape[-1]) for a in (w, g, m, v)]
    return [o.reshape(w.shape) for o in _adamw_call(*flat)]


def kernel(x, c, ctx, c_ctx, w_ada, b_ada, norm_g, w_in, conv_w, attn_sink, ssm_lam_re, ssm_lam_im, ssm_log_dt, ssm_b_re, ssm_b_im, ssm_c_re, ssm_c_im, ssm_d, w_glu, b_glu, w_out, w_mlp_in, w_mlp_out, loss_target, m_c_ctx, m_w_ada, m_b_ada, m_norm_g, m_w_in, m_conv_w, m_attn_sink, m_ssm_lam_re, m_ssm_lam_im, m_ssm_log_dt, m_ssm_b_re, m_ssm_b_im, m_ssm_c_re, m_ssm_c_im, m_ssm_d, m_w_glu, m_b_glu, m_w_out, m_w_mlp_in, m_w_mlp_out, v_c_ctx, v_w_ada, v_b_ada, v_norm_g, v_w_in, v_conv_w, v_attn_sink, v_ssm_lam_re, v_ssm_lam_im, v_ssm_log_dt, v_ssm_b_re, v_ssm_b_im, v_ssm_c_re, v_ssm_c_im, v_ssm_d, v_w_glu, v_b_glu, v_w_out, v_w_mlp_in, v_w_mlp_out):
    given = dict(locals())
    weights = {n: given[n] for n in WEIGHTS}
    moms = {n: given['m_' + n] for n in WEIGHTS}
    vars_ = {n: given['v_' + n] for n in WEIGHTS}
    b, l, d_model = x.shape
    lc = ctx.shape[1]
    dims = (b, l, lc)
    depth = w_ada.shape[0]

    core = lax.axis_index('c')
    me = 4 * lax.axis_index('x') + 2 * lax.axis_index('y') + core

    blocks = _big_rows(weights)
    big_all = _all_gather(_stack_big(weights, MXU_DTYPE))
    gathered = {n: [None] * depth for n in BIG}
    off = 0
    for layer, n, rows in blocks:
        gathered[n][layer] = big_all[:, off:off + rows].reshape(N_DEV * rows, -1)
        off += rows
    small_shapes = [weights[n].shape for n in SMALL]
    small_all = _all_gather(_pack([weights[n] for n in SMALL], F32, PACK_ROWS))
    small_full = {n: _from_pieces(pc, SHARD_AXIS[n])
                  for n, pc in zip(SMALL, _unpack(small_all, small_shapes, (N_DEV,)))}

    c_all = _all_gather(jnp.pad(c, ((0, -b % 8), (0, 0))))[:, :b].reshape(N_DEV * b, d_model)
    diff = {'x': x, 'ctx': ctx, 'w_ada': w_ada}
    for n in REPLICATED:
        diff[n] = weights[n]
    for n in SMALL:
        diff[n] = small_full[n]
    for n in BIG:
        diff[n] = [jnp.zeros(gathered[n][layer].shape, F32) for layer in range(depth)]
    y, vjp = jax.vjp(lambda dd: _forward(dd, gathered, c_all, me, dims, depth), diff)
    sq, dy = _loss_call(y, loss_target.reshape(b * l, d_model))
    loss = lax.psum(0.5 * jnp.sum(sq) / d_model, ('x', 'y', 'c'))
    grads = vjp(dy)[0]
    grad_x = grads['x']

    pieces = [grads[n][layer].reshape(N_DEV, rows, -1) for layer, n, rows in blocks]
    pieces.append(jnp.zeros((N_DEV, -sum(r for _, _, r in blocks) % BIG_ROW_TILE, d_model), F32))
    g8 = jnp.concatenate(pieces, axis=1)
    chip_sums = _rs_add(g8, _rs_sibling(g8), core.reshape(1).astype(jnp.int32))
    g_big = _sum_slots(_rs_chips(chip_sums))
    big_grad, off = {}, 0
    for n in BIG:
        rows = sum(r for _, name, r in blocks if name == n)
        blk = g_big[off:off + rows].reshape(depth, rows // depth, -1)
        big_grad[n] = jnp.swapaxes(blk, 1, 2) if BIG_TRANSPOSED[n] else blk
        off += rows
    rest = SMALL + REPLICATED
    g_rest = _sum_slots(_all_gather(_pack([grads[n] for n in rest], F32, PACK_ROWS)))
    rest_grad = dict(zip(rest, _unpack(g_rest, [grads[n].shape for n in rest])))
    for n in SMALL:
        width = weights[n].shape[SHARD_AXIS[n]]
        rest_grad[n] = lax.dynamic_slice_in_dim(rest_grad[n], me * width, width, axis=SHARD_AXIS[n])

    out = {}
    all_grads = {**big_grad, **rest_grad, **{n: grads[n] for n in LOCAL}}
    for n in WEIGHTS:
        out[('grad', n)] = all_grads[n]
        for kind, arr in zip(('delta', 'new_m', 'new_v'), _adamw_any(weights[n], all_grads[n], moms[n], vars_[n])):
            out[(kind, n)] = arr
    return (loss, grad_x, *[out[(kind, n)] for kind in ('grad', 'delta', 'new_m', 'new_v') for n in WEIGHTS])
```

```python
import jax
import jax.numpy as jnp
from jax import lax
from jax.experimental import pallas as pl
from jax.experimental.pallas import tpu as pltpu

F32 = jnp.float32
MXU_DTYPE = jnp.bfloat16
SDS = jax.ShapeDtypeStruct

N_DEV = 8
HEAD_DIM = 64
N_Q_HEADS = 8
Q_PER_KV = 4
ATTN_WIDTH = 512
KV_WIDTH = 128
WINDOW = 128
Q_BLOCK = 128
GRID_W = 64
ROPE_BASE = 10000.0
ROPE_PAIRS = 16
CONV_WIDTH = 256
SSM_WIDTH = 256
SSM_GROUP = 16
SSM_GROUPS = 16
SSM_STATE = 64
SSM_N = SSM_GROUPS * SSM_STATE
IN_WIDTH = 1792
N_MOD = 6
EPS = 1e-6
NEG_INF = -1e30
ADAM_LR = 0.001
ADAM_B1 = 0.9
ADAM_B2 = 0.999
ADAM_EPS = 1e-08
ADAM_WD = 0.01
ADAM_STEP = 10
MOD_ROWS = 128
ROW_TILE = 512
ROW_VMEM_BYTES = 48 * 1024 * 1024
BIG_ROW_TILE = 512
PACK_ROWS = 256
LANES = 128
MESH = pl.DeviceIdType.MESH
_SCALE = HEAD_DIM ** -0.5
MIX_SPLIT = ATTN_WIDTH + CONV_WIDTH
ATTN_STACK_ROWS = 256
MM_TILE = 1024
MM_VMEM_BYTES = 56 * 1024 * 1024

WEIGHTS = ['c_ctx', 'w_ada', 'b_ada', 'norm_g', 'w_in', 'conv_w', 'attn_sink', 'ssm_lam_re', 'ssm_lam_im',
           'ssm_log_dt', 'ssm_b_re', 'ssm_b_im', 'ssm_c_re', 'ssm_c_im', 'ssm_d', 'w_glu', 'b_glu', 'w_out',
           'w_mlp_in', 'w_mlp_out']
SHARD_AXIS = {'w_ada': 2, 'norm_g': 2, 'w_in': 2, 'conv_w': 2, 'w_glu': 1, 'w_out': 1, 'w_mlp_in': 2, 'w_mlp_out': 1}
BIG = ['w_in', 'w_out', 'w_mlp_in', 'w_mlp_out']
SMALL = ['norm_g', 'conv_w', 'w_glu']
LOCAL = ['w_ada']
REPLICATED = [n for n in WEIGHTS if n not in SHARD_AXIS]


def _pick(n, cands):
    for c in cands:
        if n % c == 0:
            return c
    return n


def _div_tile(n, cap):
    if n <= cap:
        return n
    for c in range(cap, LANES - 1, -LANES):
        if n % c == 0:
            return c
    return n


def _mx(x):
    return x.astype(MXU_DTYPE)


def _dg_nt(a, b):
    return lax.dot_general(a, b, (((1,), (1,)), ((), ())), preferred_element_type=F32)


def _dg_tn(a, b):
    return lax.dot_general(a, b, (((0,), (0,)), ((), ())), preferred_element_type=F32)


def _all_gather(x2d):
    r, cdim = x2d.shape

    def body(x_ref, out_ref, send_sems, recv_sems, local_sem):
        x, y, c = lax.axis_index("x"), lax.axis_index("y"), lax.axis_index("c")
        me, sibling = (x, y, c), (x, y, 1 - c)
        chips = [(1 - x, y), (x, 1 - y), (1 - x, 1 - y)]

        def slot(px, py, pc):
            return out_ref.at[4 * px + 2 * py + pc]

        def copy(k, block, to, src=None):
            return pltpu.make_async_remote_copy(
                src_ref=slot(*block) if src is None else src, dst_ref=slot(*block),
                send_sem=send_sems.at[k], recv_sem=recv_sems.at[k], device_id=to, device_id_type=MESH)

        mine = pltpu.make_async_copy(x_ref, slot(*me), local_sem)
        mine.start()
        first = [copy(0, me, sibling, src=x_ref)]
        first += [copy(1 + j, me, (*chip, c), src=x_ref) for j, chip in enumerate(chips)]
        for cp in first:
            cp.start()
        passed = [copy(4 + j, (*chip, c), sibling) for j, chip in enumerate(chips)]
        for j, chip in enumerate(chips):
            copy(1 + j, (*chip, c), me).wait_recv()
            passed[j].start()
        copy(0, sibling, me).wait_recv()
        for j, chip in enumerate(chips):
            copy(4 + j, (*chip, 1 - c), me).wait_recv()
        for cp in first + passed:
            cp.wait_send()
        mine.wait()

    return pl.pallas_call(
        body, name=f"all_gather_{r}x{cdim}_{jnp.dtype(x2d.dtype).name}",
        out_shape=SDS((N_DEV, r, cdim), x2d.dtype),
        in_specs=[pl.BlockSpec(memory_space=pl.ANY)],
        out_specs=pl.BlockSpec(memory_space=pl.ANY),
        scratch_shapes=[pltpu.SemaphoreType.DMA((7,)), pltpu.SemaphoreType.DMA((7,)), pltpu.SemaphoreType.DMA],
    )(x2d)


def _rs_sibling(g8):
    _, r, cdim = g8.shape

    def body(g_ref, out_ref, send_sems, recv_sems):
        x, y, c = lax.axis_index("x"), lax.axis_index("y"), lax.axis_index("c")
        copies = [pltpu.make_async_remote_copy(
            src_ref=g_ref.at[2 * k + (1 - c)], dst_ref=out_ref.at[k], send_sem=send_sems.at[k],
            recv_sem=recv_sems.at[k], device_id=(x, y, 1 - c), device_id_type=MESH) for k in range(4)]
        for cp in copies:
            cp.start()
        for cp in copies:
            cp.wait()

    return pl.pallas_call(
        body, name=f"rs_sibling_{r}x{cdim}",
        out_shape=SDS((4, r, cdim), g8.dtype),
        in_specs=[pl.BlockSpec(memory_space=pl.ANY)],
        out_specs=pl.BlockSpec(memory_space=pl.ANY),
        scratch_shapes=[pltpu.SemaphoreType.DMA((4,)), pltpu.SemaphoreType.DMA((4,))],
    )(g8)


def _rs_add(g8, sib, c_idx):
    _, r, cdim = g8.shape
    tr = _pick(r, (BIG_ROW_TILE, 256, 128, 64, 32, 16))

    def body(c_ref, g_ref, s_ref, o_ref):
        del c_ref
        o_ref[...] = (g_ref[...] + s_ref[...]).astype(o_ref.dtype)

    return pl.pallas_call(
        body, name=f"rs_add_{r}x{cdim}",
        grid_spec=pltpu.PrefetchScalarGridSpec(
            num_scalar_prefetch=1, grid=(4, r // tr),
            in_specs=[pl.BlockSpec((1, tr, cdim), lambda k, i, c: (2 * k + c[0], i, 0)),
                      pl.BlockSpec((1, tr, cdim), lambda k, i, c: (k, i, 0))],
            out_specs=pl.BlockSpec((1, tr, cdim), lambda k, i, c: (k, i, 0))),
        out_shape=SDS((4, r, cdim), MXU_DTYPE),
    )(c_idx, g8, sib)


def _rs_chips(s4):
    _, r, cdim = s4.shape

    def body(s_ref, out_ref, send_sems, recv_sems, local_sem):
        x, y, c = lax.axis_index("x"), lax.axis_index("y"), lax.axis_index("c")
        me = 2 * x + y

        def peer(j):
            px = 1 - x if j & 2 else x
            py = 1 - y if j & 1 else y
            return (px, py, c), 2 * px + py

        def copy(j, landing):
            to, to_chip = peer(j)
            return pltpu.make_async_remote_copy(
                src_ref=s_ref.at[to_chip], dst_ref=out_ref.at[to_chip if landing else me],
                send_sem=send_sems.at[j - 1], recv_sem=recv_sems.at[j - 1], device_id=to, device_id_type=MESH)

        mine = pltpu.make_async_copy(s_ref.at[me], out_ref.at[me], local_sem)
        mine.start()
        sends = [copy(j, False) for j in range(1, 4)]
        for cp in sends:
            cp.start()
        for j in range(1, 4):
            copy(j, True).wait_recv()
        for cp in sends:
            cp.wait_send()
        mine.wait()

    return pl.pallas_call(
        body, name=f"rs_chips_{r}x{cdim}",
        out_shape=SDS(s4.shape, s4.dtype),
        in_specs=[pl.BlockSpec(memory_space=pl.ANY)],
        out_specs=pl.BlockSpec(memory_space=pl.ANY),
        scratch_shapes=[pltpu.SemaphoreType.DMA((3,)), pltpu.SemaphoreType.DMA((3,)), pltpu.SemaphoreType.DMA],
    )(s4)


def _sum_slots(g3):
    n, r, cdim = g3.shape
    tr = _pick(r, (BIG_ROW_TILE, 256, 128, 64, 32, 16, 8))

    def body(g_ref, o_ref):
        acc = g_ref[0].astype(F32)
        for s in range(1, n):
            acc = acc + g_ref[s].astype(F32)
        o_ref[...] = acc

    return pl.pallas_call(
        body, name=f"sum_slots_{n}x{r}x{cdim}", grid=(r // tr,),
        out_shape=SDS((r, cdim), F32),
        in_specs=[pl.BlockSpec((n, tr, cdim), lambda i: (0, i, 0))],
        out_specs=pl.BlockSpec((tr, cdim), lambda i: (i, 0)),
    )(g3)


def _rows_of(n):
    return -(-n // (8 * LANES)) * 8


def _pack(arrs, dtype, row_mult):
    parts = []
    for a in arrs:
        flat = a.reshape(-1).astype(dtype)
        parts.append(jnp.pad(flat, (0, _rows_of(flat.size) * LANES - flat.size)).reshape(-1, LANES))
    rows = sum(p.shape[0] for p in parts)
    if rows % row_mult:
        parts.append(jnp.zeros((row_mult - rows % row_mult, LANES), dtype))
    return jnp.concatenate(parts, axis=0)


def _unpack(mat, shapes, lead=()):
    out, off = [], 0
    for s in shapes:
        n = 1
        for d in s:
            n *= d
        rows = _rows_of(n)
        part = mat[..., off:off + rows, :].reshape(lead + (rows * LANES,))
        out.append(part[..., :n].reshape(lead + tuple(s)))
        off += rows
    return out


def _from_pieces(p, axis):
    p = jnp.moveaxis(p, 0, axis)
    s = p.shape
    return p.reshape(s[:axis] + (s[axis] * s[axis + 1],) + s[axis + 2:])


def _mm_call(kind, a, b, z=None, sqrelu=False, out_dtype=F32):
    if kind == 'nn':
        (m, k), n = a.shape, b.shape[1]
    elif kind == 'nt':
        (m, k), n = a.shape, b.shape[0]
    else:
        (k, m), n = a.shape, b.shape[1]
    bm = _div_tile(m, MM_TILE)
    bn = _div_tile(n, MM_TILE)
    bk = _div_tile(k, MM_TILE if kind == 'tn' else 2 * MM_TILE)
    nk = k // bk
    if sqrelu:
        out_dtype = MXU_DTYPE
    plain = not sqrelu and z is None and out_dtype == F32
    use_acc = nk > 1 and not plain

    def body(a_ref, b_ref, *rest):
        rest = list(rest)
        z_ref = rest.pop(0) if z is not None else None
        o_ref = rest.pop(0)
        act_ref = rest.pop(0) if sqrelu else None
        acc = rest.pop(0) if use_acc else None
        kk = pl.program_id(2)
        av, bv = _mx(a_ref[...]), _mx(b_ref[...])
        if kind == 'nn':
            prod = jnp.dot(av, bv, preferred_element_type=F32)
        elif kind == 'nt':
            prod = _dg_nt(av, bv)
        else:
            prod = _dg_tn(av, bv)

        def finish(r):
            if z_ref is not None:
                r = r * (2.0 * jnp.maximum(z_ref[...].astype(F32), 0.0))
            o_ref[...] = r.astype(o_ref.dtype)
            if act_ref is not None:
                rr = jnp.maximum(r, 0.0)
                act_ref[...] = (rr * rr).astype(act_ref.dtype)

        if nk == 1:
            finish(prod)
        else:
            tgt = acc if use_acc else o_ref

            @pl.when(kk == 0)
            def _():
                tgt[...] = prod

            @pl.when(kk > 0)
            def _():
                tgt[...] += prod

            if use_acc:
                @pl.when(kk == nk - 1)
                def _():
                    finish(acc[...])

    if kind == 'nn':
        a_spec = pl.BlockSpec((bm, bk), lambda i, j, kk: (i, kk))
        b_spec = pl.BlockSpec((bk, bn), lambda i, j, kk: (kk, j))
    elif kind == 'nt':
        a_spec = pl.BlockSpec((bm, bk), lambda i, j, kk: (i, kk))
        b_spec = pl.BlockSpec((bn, bk), lambda i, j, kk: (j, kk))
    else:
        a_spec = pl.BlockSpec((bk, bm), lambda i, j, kk: (kk, i))
        b_spec = pl.BlockSpec((bk, bn), lambda i, j, kk: (kk, j))
    o_spec = pl.BlockSpec((bm, bn), lambda i, j, kk: (i, j))
    in_specs, args = [a_spec, b_spec], [a, b]
    if z is not None:
        in_specs.append(o_spec)
        args.append(z)
    out_shape, out_specs = [SDS((m, n), out_dtype)], [o_spec]
    if sqrelu:
        out_shape.append(SDS((m, n), MXU_DTYPE))
        out_specs.append(o_spec)
    tag = kind + ('_sq' if sqrelu else '') + ('_z' if z is not None else '')
    res = pl.pallas_call(
        body, name=f"mm_{tag}_{m}x{k}x{n}", grid=(m // bm, n // bn, nk),
        out_shape=out_shape, in_specs=in_specs, out_specs=out_specs,
        scratch_shapes=[pltpu.VMEM((bm, bn), F32)] if use_acc else [],
        compiler_params=pltpu.CompilerParams(dimension_semantics=("parallel", "parallel", "arbitrary"),
                                             vmem_limit_bytes=MM_VMEM_BYTES),
    )(*args)
    return res if sqrelu else res[0]


def _make_block_ops(nm, rg, middle, rows_out):
    nm_fwd, nm_bwd = nm
    rg_fwd, rg_bwd = rg

    def mixer_fwd(h, sh, sc, g_pre, wt, wtd, params, w, wd, gate, g_post):
        a = nm_fwd((h,), (sh, sc), (g_pre,), MXU_DTYPE)[0]
        (ac, s), mid_vjp = jax.vjp(middle, _mm_call('nt', a, wt), params)
        mix = _mx(jnp.concatenate([ac[:rows_out], s[:rows_out]], axis=1))
        m = _mm_call('nn', mix, w)
        out = rg_fwd((h[:rows_out], m), (gate,), (g_post,))[0]
        return out, (h, sh, sc, g_pre, a, wt, mid_vjp, mix, w, m, gate, g_post)

    @jax.custom_vjp
    def mixer_block(h, sh, sc, g_pre, wt, wtd, params, w, wd, gate, g_post):
        return mixer_fwd(h, sh, sc, g_pre, wt, wtd, params, w, wd, gate, g_post)[0]

    def mixer_bwd(res, g):
        h, sh, sc, g_pre, a, wt, mid_vjp, mix, w, m, gate, g_post = res
        dm, dgate, dg_post = rg_bwd((h[:rows_out], m), (gate,), (g_post,), (g,), skip_first_row=True,
                                    row_grad_dtype=MXU_DTYPE)
        dmix = jnp.pad(_mm_call('nt', dm, w), ((0, h.shape[0] - rows_out), (0, 0)))
        dp, dparams = mid_vjp((dmix[:, :MIX_SPLIT], dmix[:, MIX_SPLIT:]))
        da = _mm_call('nn', dp, wt)
        dh, dsh, dsc, dg_pre = nm_bwd((h,), (sh, sc), (g_pre,), (da,), add=g)
        return (dh, dsh, dsc, dg_pre, jnp.zeros_like(wt), _mm_call('tn', dp, a), dparams, jnp.zeros_like(w),
                _mm_call('tn', mix, dm), dgate, dg_post)

    mixer_block.defvjp(mixer_fwd, mixer_bwd)

    def mlp_fwd(h, sh, sc, g_pre, w1t, w1td, w2, w2d, gate, g_post):
        a = nm_fwd((h,), (sh, sc), (g_pre,), MXU_DTYPE)[0]
        zb, act = _mm_call('nt', a, w1t, sqrelu=True)
        f = _mm_call('nn', act, w2)
        out = rg_fwd((h, f), (gate,), (g_post,))[0]
        return out, (h, sh, sc, g_pre, a, w1t, w2, zb, act, f, gate, g_post)

    @jax.custom_vjp
    def mlp_block(h, sh, sc, g_pre, w1t, w1td, w2, w2d, gate, g_post):
        return mlp_fwd(h, sh, sc, g_pre, w1t, w1td, w2, w2d, gate, g_post)[0]

    def mlp_bwd(res, g):
        h, sh, sc, g_pre, a, w1t, w2, zb, act, f, gate, g_post = res
        df, dgate, dg_post = rg_bwd((h, f), (gate,), (g_post,), (g,), skip_first_row=True, row_grad_dtype=MXU_DTYPE)
        dz = _mm_call('nt', df, w2, z=zb, out_dtype=MXU_DTYPE)
        da = _mm_call('nn', dz, w1t)
        dh, dsh, dsc, dg_pre = nm_bwd((h,), (sh, sc), (g_pre,), (da,), add=g)
        return (dh, dsh, dsc, dg_pre, jnp.zeros_like(w1t), _mm_call('tn', dz, a), jnp.zeros_like(w2),
                _mm_call('tn', act, df), dgate, dg_post)

    mlp_block.defvjp(mlp_fwd, mlp_bwd)
    return mixer_block, mlp_block


def _rowwise(name, f, out_widths, seg_len):
    def specs(rows, segs, globs, tr):
        nseg = segs[0].shape[0] if segs else 1

        def seg_of(i):
            return jnp.minimum((i * tr) // seg_len, nseg - 1)

        row_specs = [pl.BlockSpec((tr, r.shape[1]), lambda i: (i, 0)) for r in rows]
        seg_specs = [pl.BlockSpec((1, 1, s.shape[2]), lambda i: (seg_of(i), 0, 0)) for s in segs]
        glob_specs = [pl.BlockSpec(g.shape, lambda i: (0, 0)) for g in globs]
        return seg_of, row_specs, seg_specs, glob_specs

    def fwd_call(rows, segs, globs, out_dtype=F32):
        t = rows[0].shape[0]
        tr = _pick(t, (ROW_TILE, 128, 64, 32, 16, 8))
        _, row_specs, seg_specs, glob_specs = specs(rows, segs, globs, tr)
        nr, ns = len(rows), len(segs)

        def body(*refs):
            ins, outs = refs[:nr + ns + len(globs)], refs[nr + ns + len(globs):]
            vals = [r[...] for r in ins[:nr]] + [r[0] for r in ins[nr:nr + ns]] + [r[...] for r in ins[nr + ns:]]
            for o_ref, v in zip(outs, f(*vals)):
                o_ref[...] = v.astype(o_ref.dtype)

        return pl.pallas_call(
            body, name=f"{name}_fwd_{t}_{jnp.dtype(out_dtype).name}", grid=(t // tr,),
            out_shape=[SDS((t, w), out_dtype) for w in out_widths],
            in_specs=row_specs + seg_specs + glob_specs,
            out_specs=[pl.BlockSpec((tr, w), lambda i: (i, 0)) for w in out_widths],
            compiler_params=pltpu.CompilerParams(vmem_limit_bytes=ROW_VMEM_BYTES),
        )(*rows, *segs, *globs)

    def bwd_call(rows, segs, globs, douts, add=None, skip_first_row=False, row_grad_dtype=F32):
        t = rows[0].shape[0]
        tr = _pick(t, (ROW_TILE, 128, 64, 32, 16, 8))
        seg_of, row_specs, seg_specs, glob_specs = specs(rows, segs, globs, tr)
        nr, ns, ng, no = len(rows), len(segs), len(globs), len(out_widths)
        n_add = 0 if add is None else add.shape[0] // tr
        first_out = 1 if skip_first_row else 0

        def body(*refs):
            ins = refs[:nr + ns + ng]
            dos = refs[nr + ns + ng:nr + ns + ng + no]
            add_ref = refs[nr + ns + ng + no] if n_add else None
            outs = (None,) * first_out + refs[nr + ns + ng + no + (1 if n_add else 0):]
            i = pl.program_id(0)
            first_of_seg = jnp.logical_or(i == 0, seg_of(i) != seg_of(jnp.maximum(i - 1, 0)))
            vals = [r[...] for r in ins[:nr]] + [r[0] for r in ins[nr:nr + ns]] + [r[...] for r in ins[nr + ns:]]
            _, vjp = jax.vjp(f, *vals)
            grads = list(vjp(tuple(d[...] for d in dos)))
            if n_add:
                grads[0] = grads[0] + (add_ref[...] if n_add * tr == t else jnp.where(i < n_add, add_ref[...], 0.0))
            for o_ref, gval in zip(outs[first_out:nr], grads[first_out:nr]):
                o_ref[...] = gval.astype(o_ref.dtype)
            for o_ref, gval in zip(outs[nr:nr + ns], grads[nr:nr + ns]):
                @pl.when(first_of_seg)
                def _(o_ref=o_ref):
                    o_ref[...] = jnp.zeros_like(o_ref)
                o_ref[0] += gval
            for o_ref, gval in zip(outs[nr + ns:], grads[nr + ns:]):
                @pl.when(i == 0)
                def _(o_ref=o_ref):
                    o_ref[...] = jnp.zeros_like(o_ref)
                o_ref[...] += gval

        do_specs = [pl.BlockSpec((tr, w), lambda i: (i, 0)) for w in out_widths]
        add_specs, add_args = [], []
        if n_add:
            add_specs = [pl.BlockSpec((tr, add.shape[1]), lambda i: (jnp.minimum(i, n_add - 1), 0))]
            add_args = [add]
        tag = ('_add' if n_add else '') + ('_skip' if skip_first_row else '') + '_' + jnp.dtype(row_grad_dtype).name
        return pl.pallas_call(
            body, name=f"{name}_bwd{tag}_{t}", grid=(t // tr,),
            out_shape=[SDS(r.shape, row_grad_dtype) for r in rows[first_out:]] + [SDS(s.shape, F32) for s in segs]
            + [SDS(g.shape, F32) for g in globs],
            in_specs=row_specs + seg_specs + glob_specs + do_specs + add_specs,
            out_specs=row_specs[first_out:] + seg_specs + glob_specs,
            compiler_params=pltpu.CompilerParams(vmem_limit_bytes=ROW_VMEM_BYTES),
        )(*rows, *segs, *globs, *douts, *add_args)

    return fwd_call, bwd_call


def _norm_mod_f(x, shift, scale, g):
    r = lax.rsqrt(jnp.mean(x * x, axis=-1, keepdims=True) + EPS)
    return ((x * r) * g * (1.0 + scale) + shift,)


def _resid_gate_f(h, m, gate, g):
    r = lax.rsqrt(jnp.mean(m * m, axis=-1, keepdims=True) + EPS)
    return (h + gate * ((m * r) * g),)


def _glu_f(y0, y1, u, d, w, b):
    y = y0 + y1 + d * u
    g = 0.5 * y * (1.0 + jnp.tanh(0.7978845608028654 * (y + 0.044715 * (y * y * y))))
    zz = jnp.dot(_mx(g), _mx(w), preferred_element_type=F32) + b
    return (g * (1.0 / (1.0 + jnp.exp(-zz))),)


def _make_rowwise_op(name, f, out_width, n_rows, n_segs, seg_len):
    fwd_call, bwd_call = _rowwise(name, f, (out_width,), seg_len)

    @jax.custom_vjp
    def op(*args):
        return fwd_call(args[:n_rows], args[n_rows:n_rows + n_segs], args[n_rows + n_segs:])[0]

    def op_fwd(*args):
        return fwd_call(args[:n_rows], args[n_rows:n_rows + n_segs], args[n_rows + n_segs:])[0], args

    def op_bwd(args, g):
        return tuple(bwd_call(args[:n_rows], args[n_rows:n_rows + n_segs], args[n_rows + n_segs:], (g,)))

    op.defvjp(op_fwd, op_bwd)
    return op


def _lane():
    return lax.broadcasted_iota(jnp.int32, (1, LANES), 1)


def _hm(off):
    lane = _lane()
    return jnp.logical_and(lane >= off, lane < off + HEAD_DIM)


def _col(tile, hq):
    return jnp.sum(jnp.where(_lane() == hq, tile, 0.0), axis=1, keepdims=True)


def _setcol(tile, hq, col):
    return jnp.where(_lane() == hq, col, tile)


def _head_fwd(qa, groups, sk):
    ss, m = [], None
    for kmat, _, mask in groups:
        s = _dg_nt(qa, kmat) * _SCALE
        if mask is not None:
            s = jnp.where(mask, s, NEG_INF)
        ss.append(s)
        mm = jnp.max(s, axis=1, keepdims=True)
        m = mm if m is None else jnp.maximum(m, mm)
    m = jnp.maximum(m, sk)
    l = jnp.exp(sk - m)
    pv = None
    for s, (_, vmat, _) in zip(ss, groups):
        p = jnp.exp(s - m)
        l = l + jnp.sum(p, axis=1, keepdims=True)
        t = jnp.dot(_mx(p), vmat, preferred_element_type=F32)
        pv = t if pv is None else pv + t
    return pv / l, m + jnp.log(l)


def _head_bwd(qa, doa, groups, sk, lse_h, delta_h):
    dq, outs = None, []
    for kmat, vmat, mask in groups:
        s = _dg_nt(qa, kmat) * _SCALE
        if mask is not None:
            s = jnp.where(mask, s, NEG_INF)
        p = jnp.exp(s - lse_h)
        dp = _dg_nt(doa, vmat)
        ds = p * (dp - delta_h) * _SCALE
        t = jnp.dot(_mx(ds), kmat, preferred_element_type=F32)
        dq = t if dq is None else dq + t
        outs.append((_mx(ds), _mx(p)))
    return dq, outs, jnp.exp(sk - lse_h)


def _kv_group(h):
    return [(g, Q_PER_KV * h + g, (Q_PER_KV * h + g) // 2, (g % 2) != h) for g in range(Q_PER_KV)]


def _to_kv_lanes(pair_tile, roll):
    if not roll:
        return _mx(pair_tile)
    return _mx(pltpu.roll(pair_tile.astype(F32), HEAD_DIM, 1))


def _attn_stack(n):
    return max(1, min(Q_PER_KV, ATTN_STACK_ROWS // n))


def _kv_masked(x, h):
    return _mx(jnp.where(_hm(h * HEAD_DIM), x.astype(F32), 0.0))


def _attn_fwd_block(sink_ref, q_ref, kv, n, o_ref, lse_ref):
    lse_t = jnp.zeros((n, LANES), F32)
    stack = _attn_stack(n)
    for h in range(2):
        groups = [(_kv_masked(k, h), _kv_masked(v, h), mask) for k, v, mask in kv]
        heads = _kv_group(h)
        o_pair = None
        for first in range(0, Q_PER_KV, stack):
            part = heads[first:first + stack]
            q = jnp.concatenate([_to_kv_lanes(q_ref[:, pair * LANES:(pair + 1) * LANES], roll)
                                 for _, _, pair, roll in part], axis=0)
            sk = jnp.concatenate([jnp.full((n, 1), sink_ref[0, hq], F32) for _, hq, _, _ in part], axis=0)
            o, lse = _head_fwd(q, groups, sk)
            for idx, (g, hq, pair, roll) in enumerate(part):
                og = o[idx * n:(idx + 1) * n]
                og = pltpu.roll(og, HEAD_DIM, 1) if roll else og
                lse_t = _setcol(lse_t, hq, lse[idx * n:(idx + 1) * n])
                if g % 2 == 0:
                    o_pair = og
                else:
                    o_ref[:, pair * LANES:(pair + 1) * LANES] = o_pair + og
    lse_ref[...] = lse_t


def _attn_bwd_block(sink_ref, q_ref, do_ref, o_ref, lse_t, kv, want, n, dq_ref):
    delta_t = jnp.zeros((n, LANES), F32)
    dsk = jnp.zeros((1, LANES), F32)
    dkv = [None] * len(kv)
    stack = _attn_stack(n)
    for h in range(2):
        hm = _hm(h * HEAD_DIM)
        groups = [(_kv_masked(k, h), _kv_masked(v, h), mask) for k, v, mask in kv]
        heads = _kv_group(h)
        dq_pair = None
        for first in range(0, Q_PER_KV, stack):
            part = heads[first:first + stack]
            qs, dos, sks, lses, deltas = [], [], [], [], []
            for g, hq, pair, roll in part:
                sl = slice(pair * LANES, (pair + 1) * LANES)
                do_p = do_ref[:, sl]
                delta_h = jnp.sum(jnp.where(_hm((hq % 2) * HEAD_DIM), do_p * o_ref[:, sl], 0.0), axis=1,
                                  keepdims=True)
                delta_t = _setcol(delta_t, hq, delta_h)
                qs.append(_to_kv_lanes(q_ref[:, sl], roll))
                dos.append(_to_kv_lanes(do_p, roll))
                sks.append(jnp.full((n, 1), sink_ref[0, hq], F32))
                lses.append(_col(lse_t, hq))
                deltas.append(delta_h)
            q, do = jnp.concatenate(qs, axis=0), jnp.concatenate(dos, axis=0)
            dq, outs, p_s = _head_bwd(q, do, groups, jnp.concatenate(sks, axis=0), jnp.concatenate(lses, axis=0),
                                      jnp.concatenate(deltas, axis=0))
            for idx, (g, hq, pair, roll) in enumerate(part):
                dg = dq[idx * n:(idx + 1) * n]
                dg = pltpu.roll(dg, HEAD_DIM, 1) if roll else dg
                dsk = dsk + jnp.where(_lane() == hq,
                                      jnp.sum(-p_s[idx * n:(idx + 1) * n] * deltas[idx], axis=0, keepdims=True), 0.0)
                if g % 2 == 0:
                    dq_pair = dg
                else:
                    dq_ref[:, pair * LANES:(pair + 1) * LANES] = dq_pair + dg
            for gi, (ds, pp) in enumerate(outs):
                if want[gi]:
                    dk_h = jnp.where(hm, _dg_tn(ds, q), 0.0)
                    dv_h = jnp.where(hm, _dg_tn(pp, do), 0.0)
                    dkv[gi] = (dk_h, dv_h) if dkv[gi] is None else (dkv[gi][0] + dk_h, dkv[gi][1] + dv_h)
    return delta_t, dsk, dkv


def _partner(t):
    lane = lax.broadcasted_iota(jnp.int32, t.shape, 1)
    return jnp.where((lane & ROPE_PAIRS) == 0, pltpu.roll(t, LANES - ROPE_PAIRS, 1), pltpu.roll(t, ROPE_PAIRS, 1))


def _rope_tables(n_tokens):
    rows = n_tokens // GRID_W
    row = jnp.broadcast_to(jnp.arange(rows)[:, None], (rows, GRID_W)).reshape(-1)
    col = jnp.broadcast_to(jnp.arange(GRID_W)[None, :], (rows, GRID_W)).reshape(-1)
    freqs = ROPE_BASE ** (-jnp.arange(ROPE_PAIRS, dtype=F32) / ROPE_PAIRS)
    ang = jnp.concatenate([row[:, None].astype(F32) * freqs, col[:, None].astype(F32) * freqs], axis=-1)
    c, s = jnp.cos(ang), jnp.sin(ang)
    n = ROPE_PAIRS
    cos64 = jnp.concatenate([c[:, :n], c[:, :n], c[:, n:], c[:, n:]], axis=1)
    sin64 = jnp.concatenate([-s[:, :n], s[:, :n], -s[:, n:], s[:, n:]], axis=1)
    return jnp.tile(cos64, (1, 2)), jnp.tile(sin64, (1, 2))


def _rope_call(q_src, k_src, cos, sin, dims, inverse):
    b, l, _ = dims
    t_lat = b * l
    tr = _pick(l, (ROW_TILE, 128))
    per = l // tr
    out_dtype = F32 if inverse else MXU_DTYPE

    def body(q_ref, k_ref, c_ref, s_ref, qo_ref, ko_ref):
        c, s = c_ref[...], s_ref[...]

        def rot(t):
            if inverse:
                return t * c + _partner(t * s)
            return t * c + _partner(t) * s

        for j in range(ATTN_WIDTH // LANES):
            qo_ref[:, j * LANES:(j + 1) * LANES] = rot(q_ref[:, j * LANES:(j + 1) * LANES]).astype(out_dtype)
        ko_ref[...] = rot(k_ref[...]).astype(out_dtype)

    kcol = 0 if inverse else ATTN_WIDTH // KV_WIDTH
    return pl.pallas_call(
        body, name=f"rope_{'inv' if inverse else 'fwd'}_{t_lat}", grid=(t_lat // tr,),
        out_shape=[SDS((t_lat, ATTN_WIDTH), out_dtype), SDS((t_lat, KV_WIDTH), out_dtype)],
        in_specs=[pl.BlockSpec((tr, ATTN_WIDTH), lambda i: (i, 0)),
                  pl.BlockSpec((tr, KV_WIDTH), lambda i: (i, kcol)),
                  pl.BlockSpec((tr, LANES), lambda i: (i % per, 0)),
                  pl.BlockSpec((tr, LANES), lambda i: (i % per, 0))],
        out_specs=[pl.BlockSpec((tr, ATTN_WIDTH), lambda i: (i, 0)), pl.BlockSpec((tr, KV_WIDTH), lambda i: (i, 0))],
    )(q_src, k_src, cos, sin)


_SMEM_SPEC = pl.BlockSpec(memory_space=pltpu.SMEM)
_KCOL = ATTN_WIDTH // KV_WIDTH
_VCOL = _KCOL + 1


def _win_specs(arr_cols, nb, col):
    del arr_cols
    return [pl.BlockSpec((Q_BLOCK, KV_WIDTH), lambda b, i, d=d: (b * nb + jnp.clip(i + d, 0, nb - 1), col))
            for d in (-1, 0, 1)]


def _win_mask(i, l):
    shape = (_attn_stack(Q_BLOCK) * Q_BLOCK, 3 * Q_BLOCK)
    qpos = i * Q_BLOCK + (lax.broadcasted_iota(jnp.int32, shape, 0) & (Q_BLOCK - 1))
    kpos = (i - 1) * Q_BLOCK + lax.broadcasted_iota(jnp.int32, shape, 1)
    return jnp.logical_and(jnp.abs(qpos - kpos) <= WINDOW, jnp.logical_and(kpos >= 0, kpos < l))


def _attn_lat_fwd(qr, kr, p, sink, dims):
    b, l, lc = dims
    nb = l // Q_BLOCK
    t_lat = b * l
    cbase = t_lat // lc

    def body(sink_ref, q_ref, kp, kc, kn, vp, vc, vn, ck_ref, cv_ref, o_ref, lse_ref):
        kwin = jnp.concatenate([kp[...], kc[...], kn[...]], axis=0)
        vwin = jnp.concatenate([vp[...], vc[...], vn[...]], axis=0)
        kv = [(kwin, vwin, _win_mask(pl.program_id(1), l)), (ck_ref[...], cv_ref[...], None)]
        _attn_fwd_block(sink_ref, q_ref, kv, Q_BLOCK, o_ref, lse_ref)

    return pl.pallas_call(
        body, name=f"attn_lat_fwd_{t_lat}", grid=(b, nb),
        out_shape=[SDS((t_lat, ATTN_WIDTH), F32), SDS((t_lat, LANES), F32)],
        in_specs=[_SMEM_SPEC, pl.BlockSpec((Q_BLOCK, ATTN_WIDTH), lambda bb, i: (bb * nb + i, 0))]
        + _win_specs(None, nb, 0) + _win_specs(None, nb, _VCOL)
        + [pl.BlockSpec((lc, KV_WIDTH), lambda bb, i: (cbase + bb, _KCOL)),
           pl.BlockSpec((lc, KV_WIDTH), lambda bb, i: (cbase + bb, _VCOL))],
        out_specs=[pl.BlockSpec((Q_BLOCK, ATTN_WIDTH), lambda bb, i: (bb * nb + i, 0)),
                   pl.BlockSpec((Q_BLOCK, LANES), lambda bb, i: (bb * nb + i, 0))],
    )(sink, qr, kr, kr, kr, p, p, p, p, p)


def _attn_lat_bwd_dq(qr, kr, p, sink, o, lse, dout, dims):
    b, l, lc = dims
    nb = l // Q_BLOCK
    t_lat = b * l
    cbase = t_lat // lc

    def body(sink_ref, q_ref, kp, kc, kn, vp, vc, vn, ck_ref, cv_ref, o_ref, lse_ref, do_ref,
             dq_ref, delta_ref, dkc_ref, dvc_ref, dsk_ref):
        bb, i = pl.program_id(0), pl.program_id(1)

        @pl.when(i == 0)
        def _():
            dkc_ref[...] = jnp.zeros_like(dkc_ref)
            dvc_ref[...] = jnp.zeros_like(dvc_ref)

        @pl.when(jnp.logical_and(bb == 0, i == 0))
        def _():
            dsk_ref[...] = jnp.zeros_like(dsk_ref)

        kwin = jnp.concatenate([kp[...], kc[...], kn[...]], axis=0)
        vwin = jnp.concatenate([vp[...], vc[...], vn[...]], axis=0)
        kv = [(kwin, vwin, _win_mask(i, l)), (ck_ref[...], cv_ref[...], None)]
        delta_t, dsk, dkv = _attn_bwd_block(sink_ref, q_ref, do_ref, o_ref, lse_ref[...], kv, [False, True], Q_BLOCK,
                                            dq_ref)
        delta_ref[...] = delta_t
        dkc_ref[...] += dkv[1][0]
        dvc_ref[...] += dkv[1][1]
        dsk_ref[0:1, :] += dsk

    qspec = pl.BlockSpec((Q_BLOCK, ATTN_WIDTH), lambda bb, i: (bb * nb + i, 0))
    tspec = pl.BlockSpec((Q_BLOCK, LANES), lambda bb, i: (bb * nb + i, 0))
    cspec = pl.BlockSpec((lc, KV_WIDTH), lambda bb, i: (bb, 0))
    return pl.pallas_call(
        body, name=f"attn_lat_bwd_dq_{t_lat}", grid=(b, nb),
        out_shape=[SDS((t_lat, ATTN_WIDTH), F32), SDS((t_lat, LANES), F32), SDS((b * lc, KV_WIDTH), F32),
                   SDS((b * lc, KV_WIDTH), F32), SDS((8, LANES), F32)],
        in_specs=[_SMEM_SPEC, qspec] + _win_specs(None, nb, 0) + _win_specs(None, nb, _VCOL)
        + [pl.BlockSpec((lc, KV_WIDTH), lambda bb, i: (cbase + bb, _KCOL)),
           pl.BlockSpec((lc, KV_WIDTH), lambda bb, i: (cbase + bb, _VCOL)), qspec, tspec, qspec],
        out_specs=[qspec, tspec, cspec, cspec, pl.BlockSpec((8, LANES), lambda bb, i: (0, 0))],
    )(sink, qr, kr, kr, kr, p, p, p, p, p, o, lse, dout)


def _attn_lat_bwd_dkv(qr, kr, p, lse, delta, dout, dims):
    b, l, _ = dims
    nb = l // Q_BLOCK
    t_lat = b * l

    def body(k_ref, v_ref, *refs):
        j = pl.program_id(1)
        kj, vj = k_ref[...], v_ref[...]
        n_q = 3 * Q_PER_KV * Q_BLOCK
        sub = lax.broadcasted_iota(jnp.int32, (8, LANES), 0)
        col = lax.broadcasted_iota(jnp.int32, (Q_BLOCK, n_q), 1)
        i_of = j + col // (Q_PER_KV * Q_BLOCK) - 1
        qpos = i_of * Q_BLOCK + (col & (Q_BLOCK - 1))
        kpos = j * Q_BLOCK + lax.broadcasted_iota(jnp.int32, (Q_BLOCK, n_q), 0)
        mask = jnp.logical_and(jnp.abs(qpos - kpos) <= WINDOW, jnp.logical_and(i_of >= 0, i_of < nb))
        lse_rows = [refs[4 * n + 2][...].T[0:8, :] for n in range(3)]
        delta_rows = [refs[4 * n + 3][...].T[0:8, :] for n in range(3)]
        dk = jnp.zeros((Q_BLOCK, KV_WIDTH), F32)
        dv = jnp.zeros((Q_BLOCK, KV_WIDTH), F32)
        for h in range(2):
            hm = _hm(h * HEAD_DIM)
            kh, vh = _kv_masked(kj, h), _kv_masked(vj, h)
            qs, dos, lrow, drow = [], [], [], []
            for n in range(3):
                q_ref, do_ref = refs[4 * n], refs[4 * n + 1]
                for _, hq, pair, roll in _kv_group(h):
                    sl = slice(pair * LANES, (pair + 1) * LANES)
                    qs.append(_to_kv_lanes(q_ref[:, sl], roll))
                    dos.append(_to_kv_lanes(do_ref[:, sl], roll))
                    lrow.append(jnp.sum(jnp.where(sub == hq, lse_rows[n], 0.0), axis=0, keepdims=True))
                    drow.append(jnp.sum(jnp.where(sub == hq, delta_rows[n], 0.0), axis=0, keepdims=True))
            q, do = jnp.concatenate(qs, axis=0), jnp.concatenate(dos, axis=0)
            lse_r, delta_r = jnp.concatenate(lrow, axis=1), jnp.concatenate(drow, axis=1)
            s_t = jnp.where(mask, _dg_nt(kh, q) * _SCALE, NEG_INF)
            p_t = jnp.exp(s_t - lse_r)
            ds_t = p_t * (_dg_nt(vh, do) - delta_r) * _SCALE
            dv = dv + jnp.where(hm, jnp.dot(_mx(p_t), do, preferred_element_type=F32), 0.0)
            dk = dk + jnp.where(hm, jnp.dot(_mx(ds_t), q, preferred_element_type=F32), 0.0)
        dk_ref, dv_ref = refs[12], refs[13]
        dk_ref[...] = dk
        dv_ref[...] = dv

    def blk(width, d, col=0):
        return pl.BlockSpec((Q_BLOCK, width), lambda bb, j: (bb * nb + jnp.clip(j + d, 0, nb - 1), col))

    in_specs = [blk(KV_WIDTH, 0), blk(KV_WIDTH, 0, _VCOL)]
    args = [kr, p]
    for d in (-1, 0, 1):
        in_specs += [blk(ATTN_WIDTH, d), blk(ATTN_WIDTH, d), blk(LANES, d), blk(LANES, d)]
        args += [qr, dout, lse, delta]
    return pl.pallas_call(
        body, name=f"attn_lat_bwd_dkv_{t_lat}", grid=(b, nb),
        out_shape=[SDS((t_lat, KV_WIDTH), F32), SDS((t_lat, KV_WIDTH), F32)],
        in_specs=in_specs, out_specs=[blk(KV_WIDTH, 0), blk(KV_WIDTH, 0)],
    )(*args)


def _attn_ctx_fwd(p, sink, dims):
    b, l, lc = dims
    cbase = b * l // lc

    def body(sink_ref, q_ref, k_ref, v_ref, o_ref, lse_ref):
        _attn_fwd_block(sink_ref, q_ref, [(k_ref[...], v_ref[...], None)], lc, o_ref, lse_ref)

    return pl.pallas_call(
        body, name=f"attn_ctx_fwd_{b * lc}", grid=(b,),
        out_shape=[SDS((b * lc, ATTN_WIDTH), F32), SDS((b * lc, LANES), F32)],
        in_specs=[_SMEM_SPEC, pl.BlockSpec((lc, ATTN_WIDTH), lambda bb: (cbase + bb, 0)),
                  pl.BlockSpec((lc, KV_WIDTH), lambda bb: (cbase + bb, _KCOL)),
                  pl.BlockSpec((lc, KV_WIDTH), lambda bb: (cbase + bb, _VCOL))],
        out_specs=[pl.BlockSpec((lc, ATTN_WIDTH), lambda bb: (bb, 0)), pl.BlockSpec((lc, LANES), lambda bb: (bb, 0))],
    )(sink, p, p, p)


def _attn_ctx_bwd(p, sink, o, lse, dout, dims):
    b, l, lc = dims
    cbase = b * l // lc

    def body(sink_ref, q_ref, k_ref, v_ref, o_ref, lse_ref, do_ref, dq_ref, dk_ref, dv_ref, dsk_ref):
        bb = pl.program_id(0)

        @pl.when(bb == 0)
        def _():
            dsk_ref[...] = jnp.zeros_like(dsk_ref)

        _, dsk, dkv = _attn_bwd_block(sink_ref, q_ref, do_ref, o_ref, lse_ref[...], [(k_ref[...], v_ref[...], None)],
                                      [True], lc, dq_ref)
        dk_ref[...] = dkv[0][0]
        dv_ref[...] = dkv[0][1]
        dsk_ref[0:1, :] += dsk

    qs = pl.BlockSpec((lc, ATTN_WIDTH), lambda bb: (bb, 0))
    ks = pl.BlockSpec((lc, KV_WIDTH), lambda bb: (bb, 0))
    return pl.pallas_call(
        body, name=f"attn_ctx_bwd_{b * lc}", grid=(b,),
        out_shape=[SDS((b * lc, ATTN_WIDTH), F32), SDS((b * lc, KV_WIDTH), F32), SDS((b * lc, KV_WIDTH), F32),
                   SDS((8, LANES), F32)],
        in_specs=[_SMEM_SPEC, pl.BlockSpec((lc, ATTN_WIDTH), lambda bb: (cbase + bb, 0)),
                  pl.BlockSpec((lc, KV_WIDTH), lambda bb: (cbase + bb, _KCOL)),
                  pl.BlockSpec((lc, KV_WIDTH), lambda bb: (cbase + bb, _VCOL)),
                  qs, pl.BlockSpec((lc, LANES), lambda bb: (bb, 0)),
                  pl.BlockSpec((lc, ATTN_WIDTH), lambda bb: (cbase + bb, 0))],
        out_specs=[qs, ks, ks, pl.BlockSpec((8, LANES), lambda bb: (0, 0))],
    )(sink, p, p, p, o, lse, dout)


_CONV_COL = (ATTN_WIDTH + 2 * KV_WIDTH) // CONV_WIDTH


def _shift_prev(z, n):
    rows = lax.broadcasted_iota(jnp.int32, z.shape, 0)
    return jnp.where(rows == 0, 0.0, pltpu.roll(z, 1, 0))


def _shift_next(z, n):
    rows = lax.broadcasted_iota(jnp.int32, z.shape, 0)
    return jnp.where(rows == n - 1, 0.0, pltpu.roll(z, n - 1, 0))


def _conv_fwd(p, w, base, n_seq, ls):
    def body(cb_ref, cc_ref, cx_ref, w_ref, o_ref):
        z = cc_ref[...] * cx_ref[...]
        c3 = _shift_prev(z, ls) * w_ref[0:1, :] + z * w_ref[1:2, :] + _shift_next(z, ls) * w_ref[2:3, :]
        o_ref[...] = cb_ref[...] * c3

    return pl.pallas_call(
        body, name=f"conv_fwd_{n_seq}x{ls}", grid=(n_seq,),
        out_shape=SDS((n_seq * ls, CONV_WIDTH), F32),
        in_specs=[pl.BlockSpec((ls, CONV_WIDTH), lambda s, c=c: (base + s, _CONV_COL + c)) for c in range(3)]
        + [pl.BlockSpec(w.shape, lambda s: (0, 0))],
        out_specs=pl.BlockSpec((ls, CONV_WIDTH), lambda s: (s, 0)),
    )(p, p, p, w)


def _conv_bwd(p, w, dout, base, n_seq, ls):
    dcol = ATTN_WIDTH // CONV_WIDTH

    def body(cb_ref, cc_ref, cx_ref, w_ref, do_ref, dcb_ref, dcc_ref, dcx_ref, dw_ref):
        @pl.when(pl.program_id(0) == 0)
        def _():
            dw_ref[...] = jnp.zeros_like(dw_ref)

        cc, cx = cc_ref[...], cx_ref[...]
        z = cc * cx
        zp, zn = _shift_prev(z, ls), _shift_next(z, ls)
        c3 = zp * w_ref[0:1, :] + z * w_ref[1:2, :] + zn * w_ref[2:3, :]
        do = do_ref[...]
        dcb_ref[...] = do * c3
        e = do * cb_ref[...]
        dz = _shift_next(e, ls) * w_ref[0:1, :] + e * w_ref[1:2, :] + _shift_prev(e, ls) * w_ref[2:3, :]
        dcc_ref[...] = dz * cx
        dcx_ref[...] = dz * cc
        dw_ref[0:1, :] += jnp.sum(e * zp, axis=0, keepdims=True)
        dw_ref[1:2, :] += jnp.sum(e * z, axis=0, keepdims=True)
        dw_ref[2:3, :] += jnp.sum(e * zn, axis=0, keepdims=True)

    ospec = pl.BlockSpec((ls, CONV_WIDTH), lambda s: (s, 0))
    return pl.pallas_call(
        body, name=f"conv_bwd_{n_seq}x{ls}", grid=(n_seq,),
        out_shape=[SDS((n_seq * ls, CONV_WIDTH), F32)] * 3 + [SDS((8, CONV_WIDTH), F32)],
        in_specs=[pl.BlockSpec((ls, CONV_WIDTH), lambda s, c=c: (base + s, _CONV_COL + c)) for c in range(3)]
        + [pl.BlockSpec(w.shape, lambda s: (0, 0)), pl.BlockSpec((ls, CONV_WIDTH), lambda s: (base + s, dcol))],
        out_specs=[ospec, ospec, ospec, pl.BlockSpec((8, CONV_WIDTH), lambda s: (0, 0))],
    )(p, p, p, w, dout)


def _make_att_conv(dims, cos, sin):
    b, l, lc = dims
    t_lat = b * l

    def forward(p, conv_w, sink):
        qr, kr = _rope_call(p, p, cos, sin, dims, inverse=False)
        o_lat, lse_lat = _attn_lat_fwd(qr, kr, p, sink, dims)
        o_ctx, lse_ctx = _attn_ctx_fwd(p, sink, dims)
        conv_lat = _conv_fwd(p, conv_w, 0, b, l)
        conv_ctx = _conv_fwd(p, conv_w, t_lat // lc, b, lc)
        out = jnp.concatenate([jnp.concatenate([o_lat, conv_lat], axis=1),
                               jnp.concatenate([o_ctx, conv_ctx], axis=1)], axis=0)
        return out, (p, conv_w, sink, qr, kr, o_lat, lse_lat, o_ctx, lse_ctx)

    @jax.custom_vjp
    def op(p, conv_w, sink):
        return forward(p, conv_w, sink)[0]

    def op_bwd(res, dout):
        p, conv_w, sink, qr, kr, o_lat, lse_lat, o_ctx, lse_ctx = res
        dqr, delta, dkc1, dvc1, dsk1 = _attn_lat_bwd_dq(qr, kr, p, sink, o_lat, lse_lat, dout, dims)
        dkr, dv = _attn_lat_bwd_dkv(qr, kr, p, lse_lat, delta, dout, dims)
        dq, dk = _rope_call(dqr, dkr, cos, sin, dims, inverse=True)
        dqc, dkc2, dvc2, dsk2 = _attn_ctx_bwd(p, sink, o_ctx, lse_ctx, dout, dims)
        dcb_l, dcc_l, dcx_l, dw_l = _conv_bwd(p, conv_w, dout, 0, b, l)
        dcb_c, dcc_c, dcx_c, dw_c = _conv_bwd(p, conv_w, dout, t_lat // lc, b, lc)
        zeros_u = jnp.zeros((p.shape[0], SSM_WIDTH), F32)
        lat = jnp.concatenate([dq, dk, dv, dcb_l, dcc_l, dcx_l], axis=1)
        ctx = jnp.concatenate([dqc, dkc1 + dkc2, dvc1 + dvc2, dcb_c, dcc_c, dcx_c], axis=1)
        dp = jnp.concatenate([jnp.concatenate([lat, ctx], axis=0), zeros_u], axis=1)
        dsink = (dsk1 + dsk2)[0:1, :N_Q_HEADS]
        return dp, (dw_l + dw_c)[:3], dsink

    op.defvjp(forward, op_bwd)
    return op


def _cmul(ar, ai, br, bi):
    return ar * br - ai * bi, ar * bi + ai * br


def _scan_chunk(buf, lr_ref, li_ref, carry, dec, tc):
    sub = lax.broadcasted_iota(jnp.int32, (8, LANES), 0)
    groups = range(tc // 8 - 1, -1, -1) if dec else range(tc // 8)
    for s in range(SSM_N // LANES):
        re, im = slice(s * LANES, (s + 1) * LANES), slice(SSM_N + s * LANES, SSM_N + (s + 1) * LANES)
        pw = [(lr_ref[:, re], li_ref[:, re])]
        for _ in range(7):
            pw.append(_cmul(*pw[-1], *pw[0]))
        p8r, p8i = jnp.zeros((8, LANES), F32), jnp.zeros((8, LANES), F32)
        for k in range(8):
            row = 7 - k if dec else k
            p8r = jnp.where(sub == row, pw[k][0], p8r)
            p8i = jnp.where(sub == row, pw[k][1], p8i)
        steps = []
        for sft in (1, 2, 4):
            keep = sub < 8 - sft if dec else sub >= sft
            steps.append((8 - sft if dec else sft, jnp.where(keep, pw[sft - 1][0], 0.0),
                          jnp.where(keep, pw[sft - 1][1], 0.0)))
        cr, ci = carry[0:1, re], carry[0:1, im]
        for r in groups:
            rows = slice(8 * r, 8 * r + 8)
            xr, xi = buf[rows, re], buf[rows, im]
            for amount, mr, mi in steps:
                sr, si = pltpu.roll(xr, amount, 0), pltpu.roll(xi, amount, 0)
                xr, xi = xr + mr * sr - mi * si, xi + mr * si + mi * sr
            xr, xi = xr + p8r * cr - p8i * ci, xi + p8r * ci + p8i * cr
            buf[rows, re] = xr
            buf[rows, im] = xi
            last = 8 * r if dec else 8 * r + 7
            cr, ci = buf[last:last + 1, re], buf[last:last + 1, im]
        carry[0:1, re] = cr
        carry[0:1, im] = ci


def _ssm_chunks(dims, ctx_first, dec):
    b, l, lc = dims
    tc = _pick(lc, (256, 128, 64, 32, 16, 8))
    nc_c, nc_l = lc // tc, l // tc
    n_chunks = nc_c + nc_l
    ctx_base = b * l // tc
    g8 = tc // 8

    def chunk(bb, j):
        if ctx_first:
            is_ctx = j < nc_c
            jj = jnp.where(is_ctx, j, j - nc_c)
        else:
            is_ctx = j >= nc_l
            jj = jnp.where(is_ctx, j - nc_l, j)
        ic = nc_c - 1 - jj if dec else jj
        il = nc_l - 1 - jj if dec else jj
        return jnp.where(is_ctx, ctx_base + bb * nc_c + ic, bb * nc_l + il)

    def edge(bb, j):
        jn = jnp.minimum(j + 1, n_chunks - 1)
        return chunk(bb, jn) * g8 + (g8 - 1 if dec else 0)

    return tc, n_chunks, chunk, edge


def _ssm_fwd_call(u, bmat, cmat, lam_r, lam_i, dims, dec):
    n2 = 2 * SSM_N
    tc, n_chunks, chunk, _ = _ssm_chunks(dims, True, dec)

    def body(u_ref, b_ref, c_ref, lr_ref, li_ref, h_ref, y_ref, carry):
        @pl.when(pl.program_id(1) == 0)
        def _():
            carry[...] = jnp.zeros_like(carry)

        h_ref[...] = jnp.dot(_mx(u_ref[...]), b_ref[...], preferred_element_type=F32)
        _scan_chunk(h_ref, lr_ref, li_ref, carry, dec, tc)
        y_ref[...] = jnp.dot(_mx(h_ref[...]), c_ref[...], preferred_element_type=F32)

    def full(shape):
        return pl.BlockSpec(shape, lambda bb, j: (0, 0))

    return pl.pallas_call(
        body, name=f"ssm_fwd_{'d' if dec else 'u'}_{u.shape[0]}", grid=(dims[0], n_chunks),
        out_shape=[SDS((u.shape[0], n2), F32), SDS(u.shape, F32)],
        in_specs=[pl.BlockSpec((tc, SSM_WIDTH), lambda bb, j: (chunk(bb, j), 0)), full((SSM_WIDTH, n2)),
                  full((n2, SSM_WIDTH)), full((1, SSM_N)), full((1, SSM_N))],
        out_specs=[pl.BlockSpec((tc, n2), lambda bb, j: (chunk(bb, j), 0)),
                   pl.BlockSpec((tc, SSM_WIDTH), lambda bb, j: (chunk(bb, j), 0))],
        scratch_shapes=[pltpu.VMEM((8, n2), F32)],
    )(u, bmat, cmat, lam_r, lam_i)


def _ssm_bwd_call(gy, h, u, bmat, cmat, lam_r, lam_i, dims, dec):
    n2 = 2 * SSM_N
    tc, n_chunks, chunk, edge = _ssm_chunks(dims, False, dec)

    def body(gy_ref, h_ref, hb_ref, u_ref, b_ref, c_ref, lr_ref, li_ref, du_ref, db_ref, dct_ref, dl_ref, gd, carry):
        bb, j = pl.program_id(0), pl.program_id(1)

        @pl.when(j == 0)
        def _():
            carry[...] = jnp.zeros_like(carry)

        @pl.when(jnp.logical_and(bb == 0, j == 0))
        def _():
            db_ref[...] = jnp.zeros_like(db_ref)
            dct_ref[...] = jnp.zeros_like(dct_ref)
            dl_ref[...] = jnp.zeros_like(dl_ref)

        gyv = _mx(gy_ref[...])
        gd[...] = _dg_nt(gyv, c_ref[...])
        _scan_chunk(gd, lr_ref, li_ref, carry, dec, tc)
        gdv = _mx(gd[...])
        du_ref[...] = _dg_nt(gdv, b_ref[...])
        db_ref[...] += _dg_tn(_mx(u_ref[...]), gdv)
        dct_ref[...] += _dg_tn(gyv, _mx(h_ref[...]))
        rows = lax.broadcasted_iota(jnp.int32, (tc, LANES), 0)
        has_next = j + 1 < n_chunks
        er = 7 if dec else 0
        for s in range(SSM_N // LANES):
            re, im = slice(s * LANES, (s + 1) * LANES), slice(SSM_N + s * LANES, SSM_N + (s + 1) * LANES)
            dr, di, hr, hi = gd[:, re], gd[:, im], h_ref[:, re], h_ref[:, im]
            br = jnp.where(has_next, hb_ref[er:er + 1, re], 0.0)
            bi = jnp.where(has_next, hb_ref[er:er + 1, im], 0.0)
            if dec:
                nr = jnp.where(rows == 0, br, pltpu.roll(hr, 1, 0))
                ni = jnp.where(rows == 0, bi, pltpu.roll(hi, 1, 0))
            else:
                nr = jnp.where(rows == tc - 1, br, pltpu.roll(hr, tc - 1, 0))
                ni = jnp.where(rows == tc - 1, bi, pltpu.roll(hi, tc - 1, 0))
            dl_ref[0:1, re] += jnp.sum(dr * nr + di * ni, axis=0, keepdims=True)
            dl_ref[0:1, im] += jnp.sum(di * nr - dr * ni, axis=0, keepdims=True)

    def full(shape):
        return pl.BlockSpec(shape, lambda bb, j: (0, 0))

    def at_chunk(width):
        return pl.BlockSpec((tc, width), lambda bb, j: (chunk(bb, j), 0))

    return pl.pallas_call(
        body, name=f"ssm_bwd_{'d' if dec else 'u'}_{u.shape[0]}", grid=(dims[0], n_chunks),
        out_shape=[SDS(u.shape, F32), SDS((SSM_WIDTH, n2), F32), SDS((SSM_WIDTH, n2), F32), SDS((8, n2), F32)],
        in_specs=[at_chunk(SSM_WIDTH), at_chunk(n2), pl.BlockSpec((8, n2), lambda bb, j: (edge(bb, j), 0)),
                  at_chunk(SSM_WIDTH), full((SSM_WIDTH, n2)), full((n2, SSM_WIDTH)), full((1, SSM_N)),
                  full((1, SSM_N))],
        out_specs=[at_chunk(SSM_WIDTH), full((SSM_WIDTH, n2)), full((SSM_WIDTH, n2)), full((8, n2))],
        scratch_shapes=[pltpu.VMEM((tc, n2), F32), pltpu.VMEM((8, n2), F32)],
    )(gy, h, h, u, bmat, cmat, lam_r, lam_i)


def _make_ssm(dims, direction):
    @jax.custom_vjp
    def ssm(u, bmat, cmat, lam_r, lam_i):
        return _ssm_fwd_call(u, _mx(bmat), _mx(cmat), lam_r, lam_i, dims, direction == 1)[1]

    def ssm_fwd(u, bmat, cmat, lam_r, lam_i):
        h, y = _ssm_fwd_call(u, _mx(bmat), _mx(cmat), lam_r, lam_i, dims, direction == 1)
        return y, (u, bmat, cmat, lam_r, lam_i, h)

    def ssm_bwd(res, gy):
        u, bmat, cmat, lam_r, lam_i, h = res
        du, db, dct, dl = _ssm_bwd_call(gy, h, u, _mx(bmat), _mx(cmat), lam_r, -lam_i, dims, direction == 0)
        return du, db, dct.T, dl[0:1, :SSM_N], dl[0:1, SSM_N:]

    ssm.defvjp(ssm_fwd, ssm_bwd)
    return ssm


def _block_diag(m):
    g, a, bdim = m.shape[-3:]
    eye = jnp.eye(g, dtype=m.dtype)
    full = m[..., :, :, None, :] * eye[:, None, :, None]
    return full.reshape(m.shape[:-3] + (g * a, g * bdim))


def _ssm_mats(lam_re, lam_im, log_dt, b_re, b_im, c_re, c_im):
    lam = lax.complex(lam_re, lam_im)
    dt = jnp.exp(log_dt)[..., None]
    lam_bar = jnp.exp(lam * dt)
    b_bar = ((lam_bar - 1) / lam)[..., None] * lax.complex(b_re, b_im)
    bt = jnp.swapaxes(b_bar, -1, -2)
    bmat = jnp.concatenate([_block_diag(jnp.real(bt)), _block_diag(jnp.imag(bt))], axis=-1)
    cmat = jnp.concatenate([_block_diag(jnp.swapaxes(c_re, -1, -2)), _block_diag(-jnp.swapaxes(c_im, -1, -2))],
                           axis=-2)
    flat = lam_bar.shape[:2] + (1, SSM_N)
    return bmat, cmat, jnp.real(lam_bar).reshape(flat), jnp.imag(lam_bar).reshape(flat)


def _loss_call(y, target):
    t, d = y.shape
    tr = _pick(t, (ROW_TILE, 128, 64, 32, 16, 8))

    def body(y_ref, t_ref, acc_ref, dy_ref):
        @pl.when(pl.program_id(0) == 0)
        def _():
            acc_ref[...] = jnp.zeros_like(acc_ref)

        diff = y_ref[...] - t_ref[...]
        dy_ref[...] = diff * (1.0 / d)
        acc_ref[0:1, :] += jnp.sum(diff * diff, axis=0, keepdims=True)

    spec = pl.BlockSpec((tr, d), lambda i: (i, 0))
    return pl.pallas_call(
        body, name=f"loss_{t}", grid=(t // tr,),
        out_shape=[SDS((8, d), F32), SDS((t, d), F32)],
        in_specs=[spec, spec], out_specs=[pl.BlockSpec((8, d), lambda i: (0, 0)), spec],
    )(y, target)


def _adamw_call(w, g, m, v):
    r, c = w.shape
    tr = _pick(r, [t for t in (1024, 512, 256, 128, 64, 32, 16, 8) if t * c <= 256 * 1024])

    def body(w_ref, g_ref, m_ref, v_ref, d_ref, mo_ref, vo_ref):
        gv = g_ref[...]
        mn = ADAM_B1 * m_ref[...] + (1.0 - ADAM_B1) * gv
        vn = ADAM_B2 * v_ref[...] + (1.0 - ADAM_B2) * (gv * gv)
        m_hat = mn / (1.0 - ADAM_B1 ** ADAM_STEP)
        v_hat = vn / (1.0 - ADAM_B2 ** ADAM_STEP)
        d_ref[...] = -ADAM_LR * (m_hat / (jnp.sqrt(v_hat) + ADAM_EPS) + ADAM_WD * w_ref[...])
        mo_ref[...] = mn
        vo_ref[...] = vn

    spec = pl.BlockSpec((tr, c), lambda i: (i, 0))
    return pl.pallas_call(
        body, name=f"adamw_{r}x{c}", grid=(r // tr,),
        out_shape=[SDS((r, c), F32)] * 3, in_specs=[spec] * 4, out_specs=[spec] * 3,
    )(w, g, m, v)


def _make_modulation(b, me):
    n_ex = N_DEV * b

    def a_matrix(c_ctx, c_all):
        rows = [jax.nn.silu(c_all), jax.nn.silu(c_ctx)[None, :],
                jnp.zeros((MOD_ROWS - n_ex - 1, c_all.shape[1]), F32)]
        return jnp.concatenate(rows, axis=0)

    def forward(c_ctx, b_ada, w_loc, c_all):
        depth, _, width = w_loc.shape
        a_mat = a_matrix(c_ctx, c_all)
        cols = jnp.stack([_mm_call('nn', a_mat, w_loc[layer]) for layer in range(depth)])
        gathered = _all_gather(cols.reshape(depth * MOD_ROWS, width))
        full = jnp.moveaxis(gathered.reshape(N_DEV, depth, MOD_ROWS, width), 0, 2).reshape(depth, MOD_ROWS, -1)
        mine = lax.dynamic_slice_in_dim(full, me * b, b, axis=1)
        return jnp.concatenate([mine, full[:, n_ex:n_ex + 1]], axis=1) + b_ada[:, None, :]

    @jax.custom_vjp
    def modulation(c_ctx, b_ada, w_loc, c_all):
        return forward(c_ctx, b_ada, w_loc, c_all)

    def modulation_fwd(c_ctx, b_ada, w_loc, c_all):
        return forward(c_ctx, b_ada, w_loc, c_all), (c_ctx, w_loc, c_all)

    def modulation_bwd(res, g):
        c_ctx, w_loc, c_all = res
        depth, _, width = w_loc.shape
        rows = depth * (b + 1)
        flat = jnp.pad(g.reshape(rows, -1), ((0, -rows % 8), (0, 0)))
        everyone = _all_gather(flat)[:, :rows].reshape(N_DEV, depth, b + 1, -1)
        examples = jnp.moveaxis(everyone[:, :, :b], 0, 1).reshape(depth, n_ex, -1)
        ctx_total = jnp.sum(everyone[:, :, b], axis=0)
        g_mat = jnp.concatenate([examples, ctx_total[:, None, :],
                                 jnp.zeros((depth, MOD_ROWS - n_ex - 1, g.shape[-1]), F32)], axis=1)
        g_cols = lax.dynamic_slice_in_dim(g_mat, me * width, width, axis=2)
        a_mat, a_vjp = jax.vjp(lambda cc: a_matrix(cc, c_all), c_ctx)
        dw = jnp.stack([_mm_call('tn', a_mat, g_cols[layer]) for layer in range(depth)])
        da = sum(_mm_call('nt', g_cols[layer], w_loc[layer]) for layer in range(depth))
        dc_ctx, = a_vjp(da)
        return dc_ctx, jnp.sum(g, axis=1), dw, jnp.zeros_like(c_all)

    modulation.defvjp(modulation_fwd, modulation_bwd)
    return modulation


def _forward(diff, gathered, c_all, me, dims, depth):
    b, l, lc = dims
    t_lat = b * l
    d_model = diff['x'].shape[-1]
    cos, sin = _rope_tables(l)
    glu = _make_rowwise_op("glu", _glu_f, SSM_WIDTH, 3, 0, l)
    att_conv = _make_att_conv(dims, cos, sin)
    ssms = [_make_ssm(dims, 0), _make_ssm(dims, 1)]
    bmats, cmats, lam_r, lam_i = _ssm_mats(*[diff['ssm_' + n] for n in
                                             ('lam_re', 'lam_im', 'log_dt', 'b_re', 'b_im', 'c_re', 'c_im')])

    def middle(p, prm):
        ac = att_conv(p, prm['conv_w'], prm['sink'])
        u = p[:, IN_WIDTH - SSM_WIDTH:]
        ys = [ssms[d](u, prm['bmat'][d], prm['cmat'][d], prm['lam_r'][d], prm['lam_i'][d]) for d in range(2)]
        return ac, glu(ys[0], ys[1], u, prm['ssm_d'], prm['w_glu'], prm['b_glu'])

    nm = _rowwise("norm_mod", _norm_mod_f, (d_model,), l)
    rg = _rowwise("resid_gate", _resid_gate_f, (d_model,), l)
    mixer_block, mlp_block = _make_block_ops(nm, rg, middle, t_lat + b * lc)
    mixer_block_last, _ = _make_block_ops(nm, rg, middle, t_lat)

    mods = _make_modulation(b, me)(diff['c_ctx'], diff['b_ada'], diff['w_ada'], c_all)
    h = jnp.concatenate([diff['x'].reshape(t_lat, d_model), diff['ctx'].reshape(b * lc, d_model)], axis=0)

    for layer in range(depth):
        last = layer == depth - 1
        sh1, sc1, g1, sh2, sc2, g2 = [m.reshape(b + 1, 1, d_model) for m in jnp.split(mods[layer], N_MOD, axis=-1)]
        if last:
            g1, sh2, sc2, g2 = g1[:b], sh2[:b], sc2[:b], g2[:b]
        ng = diff['norm_g'][layer]
        prm = dict(conv_w=diff['conv_w'][layer], sink=diff['attn_sink'][layer][None, :], bmat=bmats[layer],
                   cmat=cmats[layer], lam_r=lam_r[layer], lam_i=lam_i[layer], ssm_d=diff['ssm_d'][layer][None, :],
                   w_glu=diff['w_glu'][layer], b_glu=diff['b_glu'][layer][None, :])
        h = (mixer_block_last if last else mixer_block)(
            h, sh1, sc1, ng[0:1], gathered['w_in'][layer], diff['w_in'][layer], prm,
            gathered['w_out'][layer], diff['w_out'][layer], g1, ng[1:2])
        h = mlp_block(h, sh2, sc2, ng[2:3], gathered['w_mlp_in'][layer], diff['w_mlp_in'][layer],
                      gathered['w_mlp_out'][layer], diff['w_mlp_out'][layer], g2, ng[3:4])
    return h


BIG_TRANSPOSED = {'w_in': True, 'w_out': False, 'w_mlp_in': True, 'w_mlp_out': False}


def _big_rows(weights):
    blocks = []
    depth = weights['w_ada'].shape[0]
    for n in BIG:
        for layer in range(depth):
            w = weights[n][layer]
            blocks.append((layer, n, w.shape[1] if BIG_TRANSPOSED[n] else w.shape[0]))
    return blocks


def _stack_big(arrs, dtype):
    parts = []
    for n in BIG:
        w = jnp.swapaxes(arrs[n], 1, 2) if BIG_TRANSPOSED[n] else arrs[n]
        parts.append(w.reshape(-1, w.shape[-1]).astype(dtype))
    rows = sum(p.shape[0] for p in parts)
    parts.append(jnp.zeros((-rows % BIG_ROW_TILE, parts[0].shape[1]), dtype))
    return jnp.concatenate(parts, axis=0)


def _adamw_any(w, g, m, v):
    flat = [a.reshape(-1, w.shape[-1]) for a in (w, g, m, v)]
    return [o.reshape(w.shape) for o in _adamw_call(*flat)]


def kernel(x, c, ctx, c_ctx, w_ada, b_ada, norm_g, w_in, conv_w, attn_sink, ssm_lam_re, ssm_lam_im, ssm_log_dt, ssm_b_re, ssm_b_im, ssm_c_re, ssm_c_im, ssm_d, w_glu, b_glu, w_out, w_mlp_in, w_mlp_out, loss_target, m_c_ctx, m_w_ada, m_b_ada, m_norm_g, m_w_in, m_conv_w, m_attn_sink, m_ssm_lam_re, m_ssm_lam_im, m_ssm_log_dt, m_ssm_b_re, m_ssm_b_im, m_ssm_c_re, m_ssm_c_im, m_ssm_d, m_w_glu, m_b_glu, m_w_out, m_w_mlp_in, m_w_mlp_out, v_c_ctx, v_w_ada, v_b_ada, v_norm_g, v_w_in, v_conv_w, v_attn_sink, v_ssm_lam_re, v_ssm_lam_im, v_ssm_log_dt, v_ssm_b_re, v_ssm_b_im, v_ssm_c_re, v_ssm_c_im, v_ssm_d, v_w_glu, v_b_glu, v_w_out, v_w_mlp_in, v_w_mlp_out):
    given = dict(locals())
    weights = {n: given[n] for n in WEIGHTS}
    moms = {n: given['m_' + n] for n in WEIGHTS}
    vars_ = {n: given['v_' + n] for n in WEIGHTS}
    b, l, d_model = x.shape
    lc = ctx.shape[1]
    dims = (b, l, lc)
    depth = w_ada.shape[0]

    core = lax.axis_index('c')
    me = 4 * lax.axis_index('x') + 2 * lax.axis_index('y') + core

    blocks = _big_rows(weights)
    big_all = _all_gather(_stack_big(weights, MXU_DTYPE))
    gathered = {n: [None] * depth for n in BIG}
    off = 0
    for layer, n, rows in blocks:
        gathered[n][layer] = big_all[:, off:off + rows].reshape(N_DEV * rows, -1)
        off += rows
    small_shapes = [weights[n].shape for n in SMALL]
    small_all = _all_gather(_pack([weights[n] for n in SMALL], F32, PACK_ROWS))
    small_full = {n: _from_pieces(pc, SHARD_AXIS[n])
                  for n, pc in zip(SMALL, _unpack(small_all, small_shapes, (N_DEV,)))}

    c_all = _all_gather(jnp.pad(c, ((0, -b % 8), (0, 0))))[:, :b].reshape(N_DEV * b, d_model)
    diff = {'x': x, 'ctx': ctx, 'w_ada': w_ada}
    for n in REPLICATED:
        diff[n] = weights[n]
    for n in SMALL:
        diff[n] = small_full[n]
    for n in BIG:
        diff[n] = [jnp.zeros(gathered[n][layer].shape, F32) for layer in range(depth)]
    y, vjp = jax.vjp(lambda dd: _forward(dd, gathered, c_all, me, dims, depth), diff)
    sq, dy = _loss_call(y, loss_target.reshape(b * l, d_model))
    loss = lax.psum(0.5 * jnp.sum(sq) / d_model, ('x', 'y', 'c'))
    grads = vjp(dy)[0]
    grad_x = grads['x']

    pieces = [grads[n][layer].reshape(N_DEV, rows, -1) for layer, n, rows in blocks]
    pieces.append(jnp.zeros((N_DEV, -sum(r for _, _, r in blocks) % BIG_ROW_TILE, d_model), F32))
    g8 = jnp.concatenate(pieces, axis=1)
    chip_sums = _rs_add(g8, _rs_sibling(g8), core.reshape(1).astype(jnp.int32))
    g_big = _sum_slots(_rs_chips(chip_sums))
    big_grad, off = {}, 0
    for n in BIG:
        rows = sum(r for _, name, r in blocks if name == n)
        blk = g_big[off:off + rows].reshape(depth, rows // depth, -1)
        big_grad[n] = jnp.swapaxes(blk, 1, 2) if BIG_TRANSPOSED[n] else blk
        off += rows
    rest = SMALL + REPLICATED
    g_rest = _sum_slots(_all_gather(_pack([grads[n] for n in rest], F32, PACK_ROWS)))
    rest_grad = dict(zip(rest, _unpack(g_rest, [grads[n].shape for n in rest])))
    for n in SMALL:
        width = weights[n].shape[SHARD_AXIS[n]]
        rest_grad[n] = lax.dynamic_slice_in_dim(rest_grad[n], me * width, width, axis=SHARD_AXIS[n])

    out = {}
    all_grads = {**big_grad, **rest_grad, **{n: grads[n] for n in LOCAL}}
    for n in WEIGHTS:
        out[('grad', n)] = all_grads[n]
        for kind, arr in zip(('delta', 'new_m', 'new_v'), _adamw_any(weights[n], all_grads[n], moms[n], vars_[n])):
            out[(kind, n)] = arr
    return (loss, grad_x, *[out[(kind, n)] for kind in ('grad', 'delta', 'new_m', 'new_v') for n in WEIGHTS])
```

```python
import jax
import jax.numpy as jnp
from jax import lax
from jax.experimental import pallas as pl
from jax.experimental.pallas import tpu as pltpu

F32 = jnp.float32
MXU_DTYPE = jnp.bfloat16
SDS = jax.ShapeDtypeStruct

N_DEV = 8
HEAD_DIM = 64
N_Q_HEADS = 8
Q_PER_KV = 4
ATTN_WIDTH = 512
KV_WIDTH = 128
WINDOW = 128
Q_BLOCK = 128
GRID_W = 64
ROPE_BASE = 10000.0
ROPE_PAIRS = 16
CONV_WIDTH = 256
SSM_WIDTH = 256
SSM_GROUP = 16
SSM_GROUPS = 16
SSM_STATE = 64
SSM_N = SSM_GROUPS * SSM_STATE
IN_WIDTH = 1792
N_MOD = 6
EPS = 1e-6
NEG_INF = -1e30
ADAM_LR = 0.001
ADAM_B1 = 0.9
ADAM_B2 = 0.999
ADAM_EPS = 1e-08
ADAM_WD = 0.01
ADAM_STEP = 10
MOD_ROWS = 128
ROW_TILE = 512
ROW_VMEM_BYTES = 48 * 1024 * 1024
BIG_ROW_TILE = 512
PACK_ROWS = 256
LANES = 128
MESH = pl.DeviceIdType.MESH
_SCALE = HEAD_DIM ** -0.5
MIX_SPLIT = ATTN_WIDTH + CONV_WIDTH
ATTN_STACK_ROWS = 256
MM_TILE = 1024
MM_VMEM_BYTES = 56 * 1024 * 1024

WEIGHTS = ['c_ctx', 'w_ada', 'b_ada', 'norm_g', 'w_in', 'conv_w', 'attn_sink', 'ssm_lam_re', 'ssm_lam_im',
           'ssm_log_dt', 'ssm_b_re', 'ssm_b_im', 'ssm_c_re', 'ssm_c_im', 'ssm_d', 'w_glu', 'b_glu', 'w_out',
           'w_mlp_in', 'w_mlp_out']
SHARD_AXIS = {'w_ada': 2, 'norm_g': 2, 'w_in': 2, 'conv_w': 2, 'w_glu': 1, 'w_out': 1, 'w_mlp_in': 2, 'w_mlp_out': 1}
BIG = ['w_in', 'w_out', 'w_mlp_in', 'w_mlp_out']
SMALL = ['norm_g', 'conv_w', 'w_glu']
LOCAL = ['w_ada']
REPLICATED = [n for n in WEIGHTS if n not in SHARD_AXIS]


def _pick(n, cands):
    for c in cands:
        if n % c == 0:
            return c
    return n


def _div_tile(n, cap):
    if n <= cap:
        return n
    for c in range(cap, LANES - 1, -LANES):
        if n % c == 0:
            return c
    return n


def _mx(x):
    return x.astype(MXU_DTYPE)


def _dg_nt(a, b):
    return lax.dot_general(a, b, (((1,), (1,)), ((), ())), preferred_element_type=F32)


def _dg_tn(a, b):
    return lax.dot_general(a, b, (((0,), (0,)), ((), ())), preferred_element_type=F32)


def _all_gather(x2d):
    r, cdim = x2d.shape

    def body(x_ref, out_ref, send_sems, recv_sems, local_sem):
        x, y, c = lax.axis_index("x"), lax.axis_index("y"), lax.axis_index("c")
        me, sibling = (x, y, c), (x, y, 1 - c)
        chips = [(1 - x, y), (x, 1 - y), (1 - x, 1 - y)]

        def slot(px, py, pc):
            return out_ref.at[4 * px + 2 * py + pc]

        def copy(k, block, to, src=None):
            return pltpu.make_async_remote_copy(
                src_ref=slot(*block) if src is None else src, dst_ref=slot(*block),
                send_sem=send_sems.at[k], recv_sem=recv_sems.at[k], device_id=to, device_id_type=MESH)

        mine = pltpu.make_async_copy(x_ref, slot(*me), local_sem)
        mine.start()
        first = [copy(0, me, sibling, src=x_ref)]
        first += [copy(1 + j, me, (*chip, c), src=x_ref) for j, chip in enumerate(chips)]
        for cp in first:
            cp.start()
        passed = [copy(4 + j, (*chip, c), sibling) for j, chip in enumerate(chips)]
        for j, chip in enumerate(chips):
            copy(1 + j, (*chip, c), me).wait_recv()
            passed[j].start()
        copy(0, sibling, me).wait_recv()
        for j, chip in enumerate(chips):
            copy(4 + j, (*chip, 1 - c), me).wait_recv()
        for cp in first + passed:
            cp.wait_send()
        mine.wait()

    return pl.pallas_call(
        body, name=f"all_gather_{r}x{cdim}_{jnp.dtype(x2d.dtype).name}",
        out_shape=SDS((N_DEV, r, cdim), x2d.dtype),
        in_specs=[pl.BlockSpec(memory_space=pl.ANY)],
        out_specs=pl.BlockSpec(memory_space=pl.ANY),
        scratch_shapes=[pltpu.SemaphoreType.DMA((7,)), pltpu.SemaphoreType.DMA((7,)), pltpu.SemaphoreType.DMA],
    )(x2d)


def _rs_sibling(g8):
    _, r, cdim = g8.shape

    def body(g_ref, out_ref, send_sems, recv_sems):
        x, y, c = lax.axis_index("x"), lax.axis_index("y"), lax.axis_index("c")
        copies = [pltpu.make_async_remote_copy(
            src_ref=g_ref.at[2 * k + (1 - c)], dst_ref=out_ref.at[k], send_sem=send_sems.at[k],
            recv_sem=recv_sems.at[k], device_id=(x, y, 1 - c), device_id_type=MESH) for k in range(4)]
        for cp in copies:
            cp.start()
        for cp in copies:
            cp.wait()

    return pl.pallas_call(
        body, name=f"rs_sibling_{r}x{cdim}",
        out_shape=SDS((4, r, cdim), g8.dtype),
        in_specs=[pl.BlockSpec(memory_space=pl.ANY)],
        out_specs=pl.BlockSpec(memory_space=pl.ANY),
        scratch_shapes=[pltpu.SemaphoreType.DMA((4,)), pltpu.SemaphoreType.DMA((4,))],
    )(g8)


def _rs_add(g8, sib, c_idx):
    _, r, cdim = g8.shape
    tr = _pick(r, (BIG_ROW_TILE, 256, 128, 64, 32, 16))

    def body(c_ref, g_ref, s_ref, o_ref):
        del c_ref
        o_ref[...] = (g_ref[...] + s_ref[...]).astype(o_ref.dtype)

    return pl.pallas_call(
        body, name=f"rs_add_{r}x{cdim}",
        grid_spec=pltpu.PrefetchScalarGridSpec(
            num_scalar_prefetch=1, grid=(4, r // tr),
            in_specs=[pl.BlockSpec((1, tr, cdim), lambda k, i, c: (2 * k + c[0], i, 0)),
                      pl.BlockSpec((1, tr, cdim), lambda k, i, c: (k, i, 0))],
            out_specs=pl.BlockSpec((1, tr, cdim), lambda k, i, c: (k, i, 0))),
        out_shape=SDS((4, r, cdim), MXU_DTYPE),
    )(c_idx, g8, sib)


def _rs_chips(s4):
    _, r, cdim = s4.shape

    def body(s_ref, out_ref, send_sems, recv_sems, local_sem):
        x, y, c = lax.axis_index("x"), lax.axis_index("y"), lax.axis_index("c")
        me = 2 * x + y

        def peer(j):
            px = 1 - x if j & 2 else x
            py = 1 - y if j & 1 else y
            return (px, py, c), 2 * px + py

        def copy(j, landing):
            to, to_chip = peer(j)
            return pltpu.make_async_remote_copy(
                src_ref=s_ref.at[to_chip], dst_ref=out_ref.at[to_chip if landing else me],
                send_sem=send_sems.at[j - 1], recv_sem=recv_sems.at[j - 1], device_id=to, device_id_type=MESH)

        mine = pltpu.make_async_copy(s_ref.at[me], out_ref.at[me], local_sem)
        mine.start()
        sends = [copy(j, False) for j in range(1, 4)]
        for cp in sends:
            cp.start()
        for j in range(1, 4):
            copy(j, True).wait_recv()
        for cp in sends:
            cp.wait_send()
        mine.wait()

    return pl.pallas_call(
        body, name=f"rs_chips_{r}x{cdim}",
        out_shape=SDS(s4.shape, s4.dtype),
        in_specs=[pl.BlockSpec(memory_space=pl.ANY)],
        out_specs=pl.BlockSpec(memory_space=pl.ANY),
        scratch_shapes=[pltpu.SemaphoreType.DMA((3,)), pltpu.SemaphoreType.DMA((3,)), pltpu.SemaphoreType.DMA],
    )(s4)


def _sum_slots(g3):
    n, r, cdim = g3.shape
    tr = _pick(r, (BIG_ROW_TILE, 256, 128, 64, 32, 16, 8))

    def body(g_ref, o_ref):
        acc = g_ref[0].astype(F32)
        for s in range(1, n):
            acc = acc + g_ref[s].astype(F32)
        o_ref[...] = acc

    return pl.pallas_call(
        body, name=f"sum_slots_{n}x{r}x{cdim}", grid=(r // tr,),
        out_shape=SDS((r, cdim), F32),
        in_specs=[pl.BlockSpec((n, tr, cdim), lambda i: (0, i, 0))],
        out_specs=pl.BlockSpec((tr, cdim), lambda i: (i, 0)),
    )(g3)


def _rows_of(n):
    return -(-n // (8 * LANES)) * 8


def _pack(arrs, dtype, row_mult):
    parts = []
    for a in arrs:
        flat = a.reshape(-1).astype(dtype)
        parts.append(jnp.pad(flat, (0, _rows_of(flat.size) * LANES - flat.size)).reshape(-1, LANES))
    rows = sum(p.shape[0] for p in parts)
    if rows % row_mult:
        parts.append(jnp.zeros((row_mult - rows % row_mult, LANES), dtype))
    return jnp.concatenate(parts, axis=0)


def _unpack(mat, shapes, lead=()):
    out, off = [], 0
    for s in shapes:
        n = 1
        for d in s:
            n *= d
        rows = _rows_of(n)
        part = mat[..., off:off + rows, :].reshape(lead + (rows * LANES,))
        out.append(part[..., :n].reshape(lead + tuple(s)))
        off += rows
    return out


def _from_pieces(p, axis):
    p = jnp.moveaxis(p, 0, axis)
    s = p.shape
    return p.reshape(s[:axis] + (s[axis] * s[axis + 1],) + s[axis + 2:])


def _mm_call(kind, a, b, z=None, sqrelu=False, out_dtype=F32):
    if kind == 'nn':
        (m, k), n = a.shape, b.shape[1]
    elif kind == 'nt':
        (m, k), n = a.shape, b.shape[0]
    else:
        (k, m), n = a.shape, b.shape[1]
    bm = _div_tile(m, 2 * MM_TILE if kind == 'tn' else MM_TILE)
    bn = _div_tile(n, MM_TILE)
    bk = _div_tile(k, MM_TILE if kind == 'tn' else 2 * MM_TILE)
    nk = k // bk
    if sqrelu:
        out_dtype = MXU_DTYPE
    plain = not sqrelu and z is None and out_dtype == F32
    use_acc = nk > 1 and not plain

    def body(a_ref, b_ref, *rest):
        rest = list(rest)
        z_ref = rest.pop(0) if z is not None else None
        o_ref = rest.pop(0)
        act_ref = rest.pop(0) if sqrelu else None
        acc = rest.pop(0) if use_acc else None
        kk = pl.program_id(2)
        av, bv = _mx(a_ref[...]), _mx(b_ref[...])
        if kind == 'nn':
            prod = jnp.dot(av, bv, preferred_element_type=F32)
        elif kind == 'nt':
            prod = _dg_nt(av, bv)
        else:
            prod = _dg_tn(av, bv)

        def finish(r):
            if z_ref is not None:
                r = r * (2.0 * jnp.maximum(z_ref[...].astype(F32), 0.0))
            o_ref[...] = r.astype(o_ref.dtype)
            if act_ref is not None:
                rr = jnp.maximum(r, 0.0)
                act_ref[...] = (rr * rr).astype(act_ref.dtype)

        if nk == 1:
            finish(prod)
        else:
            tgt = acc if use_acc else o_ref

            @pl.when(kk == 0)
            def _():
                tgt[...] = prod

            @pl.when(kk > 0)
            def _():
                tgt[...] += prod

            if use_acc:
                @pl.when(kk == nk - 1)
                def _():
                    finish(acc[...])

    if kind == 'nn':
        a_spec = pl.BlockSpec((bm, bk), lambda i, j, kk: (i, kk))
        b_spec = pl.BlockSpec((bk, bn), lambda i, j, kk: (kk, j))
    elif kind == 'nt':
        a_spec = pl.BlockSpec((bm, bk), lambda i, j, kk: (i, kk))
        b_spec = pl.BlockSpec((bn, bk), lambda i, j, kk: (j, kk))
    else:
        a_spec = pl.BlockSpec((bk, bm), lambda i, j, kk: (kk, i))
        b_spec = pl.BlockSpec((bk, bn), lambda i, j, kk: (kk, j))
    o_spec = pl.BlockSpec((bm, bn), lambda i, j, kk: (i, j))
    in_specs, args = [a_spec, b_spec], [a, b]
    if z is not None:
        in_specs.append(o_spec)
        args.append(z)
    out_shape, out_specs = [SDS((m, n), out_dtype)], [o_spec]
    if sqrelu:
        out_shape.append(SDS((m, n), MXU_DTYPE))
        out_specs.append(o_spec)
    tag = kind + ('_sq' if sqrelu else '') + ('_z' if z is not None else '')
    res = pl.pallas_call(
        body, name=f"mm_{tag}_{m}x{k}x{n}", grid=(m // bm, n // bn, nk),
        out_shape=out_shape, in_specs=in_specs, out_specs=out_specs,
        scratch_shapes=[pltpu.VMEM((bm, bn), F32)] if use_acc else [],
        compiler_params=pltpu.CompilerParams(dimension_semantics=("parallel", "parallel", "arbitrary"),
                                             vmem_limit_bytes=MM_VMEM_BYTES),
    )(*args)
    return res if sqrelu else res[0]


def _make_block_ops(nm, rg, middle, rows_out):
    nm_fwd, nm_bwd = nm
    rg_fwd, rg_bwd = rg

    def mixer_fwd(h, sh, sc, g_pre, wt, wtd, params, w, wd, gate, g_post):
        a = nm_fwd((h,), (sh, sc), (g_pre,), MXU_DTYPE)[0]
        (ac, s), mid_vjp = jax.vjp(middle, _mm_call('nt', a, wt), params)
        mix = _mx(jnp.concatenate([ac[:rows_out], s[:rows_out]], axis=1))
        m = _mm_call('nn', mix, w)
        out = rg_fwd((h[:rows_out], m), (gate,), (g_post,))[0]
        return out, (h, sh, sc, g_pre, a, wt, mid_vjp, mix, w, m, gate, g_post)

    @jax.custom_vjp
    def mixer_block(h, sh, sc, g_pre, wt, wtd, params, w, wd, gate, g_post):
        return mixer_fwd(h, sh, sc, g_pre, wt, wtd, params, w, wd, gate, g_post)[0]

    def mixer_bwd(res, g):
        h, sh, sc, g_pre, a, wt, mid_vjp, mix, w, m, gate, g_post = res
        dm, dgate, dg_post = rg_bwd((h[:rows_out], m), (gate,), (g_post,), (g,), skip_first_row=True,
                                    row_grad_dtype=MXU_DTYPE)
        dmix = jnp.pad(_mm_call('nt', dm, w), ((0, h.shape[0] - rows_out), (0, 0)))
        dp, dparams = mid_vjp((dmix[:, :MIX_SPLIT], dmix[:, MIX_SPLIT:]))
        da = _mm_call('nn', dp, wt)
        dh, dsh, dsc, dg_pre = nm_bwd((h,), (sh, sc), (g_pre,), (da,), add=g)
        return (dh, dsh, dsc, dg_pre, jnp.zeros_like(wt), _mm_call('tn', dp, a), dparams, jnp.zeros_like(w),
                _mm_call('tn', mix, dm), dgate, dg_post)

    mixer_block.defvjp(mixer_fwd, mixer_bwd)

    def mlp_fwd(h, sh, sc, g_pre, w1t, w1td, w2, w2d, gate, g_post):
        a = nm_fwd((h,), (sh, sc), (g_pre,), MXU_DTYPE)[0]
        zb, act = _mm_call('nt', a, w1t, sqrelu=True)
        f = _mm_call('nn', act, w2)
        out = rg_fwd((h, f), (gate,), (g_post,))[0]
        return out, (h, sh, sc, g_pre, a, w1t, w2, zb, act, f, gate, g_post)

    @jax.custom_vjp
    def mlp_block(h, sh, sc, g_pre, w1t, w1td, w2, w2d, gate, g_post):
        return mlp_fwd(h, sh, sc, g_pre, w1t, w1td, w2, w2d, gate, g_post)[0]

    def mlp_bwd(res, g):
        h, sh, sc, g_pre, a, w1t, w2, zb, act, f, gate, g_post = res
        df, dgate, dg_post = rg_bwd((h, f), (gate,), (g_post,), (g,), skip_first_row=True, row_grad_dtype=MXU_DTYPE)
        dz = _mm_call('nt', df, w2, z=zb, out_dtype=MXU_DTYPE)
        da = _mm_call('nn', dz, w1t)
        dh, dsh, dsc, dg_pre = nm_bwd((h,), (sh, sc), (g_pre,), (da,), add=g)
        return (dh, dsh, dsc, dg_pre, jnp.zeros_like(w1t), _mm_call('tn', dz, a), jnp.zeros_like(w2),
                _mm_call('tn', act, df), dgate, dg_post)

    mlp_block.defvjp(mlp_fwd, mlp_bwd)
    return mixer_block, mlp_block


def _rowwise(name, f, out_widths, seg_len):
    def specs(rows, segs, globs, tr):
        nseg = segs[0].shape[0] if segs else 1

        def seg_of(i):
            return jnp.minimum((i * tr) // seg_len, nseg - 1)

        row_specs = [pl.BlockSpec((tr, r.shape[1]), lambda i: (i, 0)) for r in rows]
        seg_specs = [pl.BlockSpec((1, 1, s.shape[2]), lambda i: (seg_of(i), 0, 0)) for s in segs]
        glob_specs = [pl.BlockSpec(g.shape, lambda i: (0, 0)) for g in globs]
        return seg_of, row_specs, seg_specs, glob_specs

    def fwd_call(rows, segs, globs, out_dtype=F32):
        t = rows[0].shape[0]
        tr = _pick(t, (ROW_TILE, 128, 64, 32, 16, 8))
        _, row_specs, seg_specs, glob_specs = specs(rows, segs, globs, tr)
        nr, ns = len(rows), len(segs)

        def body(*refs):
            ins, outs = refs[:nr + ns + len(globs)], refs[nr + ns + len(globs):]
            vals = [r[...] for r in ins[:nr]] + [r[0] for r in ins[nr:nr + ns]] + [r[...] for r in ins[nr + ns:]]
            for o_ref, v in zip(outs, f(*vals)):
                o_ref[...] = v.astype(o_ref.dtype)

        return pl.pallas_call(
            body, name=f"{name}_fwd_{t}_{jnp.dtype(out_dtype).name}", grid=(t // tr,),
            out_shape=[SDS((t, w), out_dtype) for w in out_widths],
            in_specs=row_specs + seg_specs + glob_specs,
            out_specs=[pl.BlockSpec((tr, w), lambda i: (i, 0)) for w in out_widths],
            compiler_params=pltpu.CompilerParams(vmem_limit_bytes=ROW_VMEM_BYTES),
        )(*rows, *segs, *globs)

    def bwd_call(rows, segs, globs, douts, add=None, skip_first_row=False, row_grad_dtype=F32):
        t = rows[0].shape[0]
        tr = _pick(t, (ROW_TILE, 128, 64, 32, 16, 8))
        seg_of, row_specs, seg_specs, glob_specs = specs(rows, segs, globs, tr)
        nr, ns, ng, no = len(rows), len(segs), len(globs), len(out_widths)
        n_add = 0 if add is None else add.shape[0] // tr
        first_out = 1 if skip_first_row else 0

        def body(*refs):
            ins = refs[:nr + ns + ng]
            dos = refs[nr + ns + ng:nr + ns + ng + no]
            add_ref = refs[nr + ns + ng + no] if n_add else None
            outs = (None,) * first_out + refs[nr + ns + ng + no + (1 if n_add else 0):]
            i = pl.program_id(0)
            first_of_seg = jnp.logical_or(i == 0, seg_of(i) != seg_of(jnp.maximum(i - 1, 0)))
            vals = [r[...] for r in ins[:nr]] + [r[0] for r in ins[nr:nr + ns]] + [r[...] for r in ins[nr + ns:]]
            _, vjp = jax.vjp(f, *vals)
            grads = list(vjp(tuple(d[...] for d in dos)))
            if n_add:
                grads[0] = grads[0] + (add_ref[...] if n_add * tr == t else jnp.where(i < n_add, add_ref[...], 0.0))
            for o_ref, gval in zip(outs[first_out:nr], grads[first_out:nr]):
                o_ref[...] = gval.astype(o_ref.dtype)
            for o_ref, gval in zip(outs[nr:nr + ns], grads[nr:nr + ns]):
                @pl.when(first_of_seg)
                def _(o_ref=o_ref):
                    o_ref[...] = jnp.zeros_like(o_ref)
                o_ref[0] += gval
            for o_ref, gval in zip(outs[nr + ns:], grads[nr + ns:]):
                @pl.when(i == 0)
                def _(o_ref=o_ref):
                    o_ref[...] = jnp.zeros_like(o_ref)
                o_ref[...] += gval

        do_specs = [pl.BlockSpec((tr, w), lambda i: (i, 0)) for w in out_widths]
        add_specs, add_args = [], []
        if n_add:
            add_specs = [pl.BlockSpec((tr, add.shape[1]), lambda i: (jnp.minimum(i, n_add - 1), 0))]
            add_args = [add]
        tag = ('_add' if n_add else '') + ('_skip' if skip_first_row else '') + '_' + jnp.dtype(row_grad_dtype).name
        return pl.pallas_call(
            body, name=f"{name}_bwd{tag}_{t}", grid=(t // tr,),
            out_shape=[SDS(r.shape, row_grad_dtype) for r in rows[first_out:]] + [SDS(s.shape, F32) for s in segs]
            + [SDS(g.shape, F32) for g in globs],
            in_specs=row_specs + seg_specs + glob_specs + do_specs + add_specs,
            out_specs=row_specs[first_out:] + seg_specs + glob_specs,
            compiler_params=pltpu.CompilerParams(vmem_limit_bytes=ROW_VMEM_BYTES),
        )(*rows, *segs, *globs, *douts, *add_args)

    return fwd_call, bwd_call


def _norm_mod_f(x, shift, scale, g):
    r = lax.rsqrt(jnp.mean(x * x, axis=-1, keepdims=True) + EPS)
    return ((x * r) * g * (1.0 + scale) + shift,)


def _resid_gate_f(h, m, gate, g):
    r = lax.rsqrt(jnp.mean(m * m, axis=-1, keepdims=True) + EPS)
    return (h + gate * ((m * r) * g),)


def _glu_f(y0, y1, u, d, w, b):
    y = y0 + y1 + d * u
    g = 0.5 * y * (1.0 + jnp.tanh(0.7978845608028654 * (y + 0.044715 * (y * y * y))))
    zz = jnp.dot(_mx(g), _mx(w), preferred_element_type=F32) + b
    return (g * (1.0 / (1.0 + jnp.exp(-zz))),)


def _make_rowwise_op(name, f, out_width, n_rows, n_segs, seg_len):
    fwd_call, bwd_call = _rowwise(name, f, (out_width,), seg_len)

    @jax.custom_vjp
    def op(*args):
        return fwd_call(args[:n_rows], args[n_rows:n_rows + n_segs], args[n_rows + n_segs:])[0]

    def op_fwd(*args):
        return fwd_call(args[:n_rows], args[n_rows:n_rows + n_segs], args[n_rows + n_segs:])[0], args

    def op_bwd(args, g):
        return tuple(bwd_call(args[:n_rows], args[n_rows:n_rows + n_segs], args[n_rows + n_segs:], (g,)))

    op.defvjp(op_fwd, op_bwd)
    return op


def _lane():
    return lax.broadcasted_iota(jnp.int32, (1, LANES), 1)


def _hm(off):
    lane = _lane()
    return jnp.logical_and(lane >= off, lane < off + HEAD_DIM)


def _col(tile, hq):
    return jnp.sum(jnp.where(_lane() == hq, tile, 0.0), axis=1, keepdims=True)


def _setcol(tile, hq, col):
    return jnp.where(_lane() == hq, col, tile)


def _head_fwd(qa, groups, sk):
    ss, m = [], None
    for kmat, _, mask in groups:
        s = _dg_nt(qa, kmat) * _SCALE
        if mask is not None:
            s = jnp.where(mask, s, NEG_INF)
        ss.append(s)
        mm = jnp.max(s, axis=1, keepdims=True)
        m = mm if m is None else jnp.maximum(m, mm)
    m = jnp.maximum(m, sk)
    l = jnp.exp(sk - m)
    pv = None
    for s, (_, vmat, _) in zip(ss, groups):
        p = jnp.exp(s - m)
        l = l + jnp.sum(p, axis=1, keepdims=True)
        t = jnp.dot(_mx(p), vmat, preferred_element_type=F32)
        pv = t if pv is None else pv + t
    return pv / l, m + jnp.log(l)


def _head_bwd(qa, doa, groups, sk, lse_h, delta_h):
    dq, outs = None, []
    for kmat, vmat, mask in groups:
        s = _dg_nt(qa, kmat) * _SCALE
        if mask is not None:
            s = jnp.where(mask, s, NEG_INF)
        p = jnp.exp(s - lse_h)
        dp = _dg_nt(doa, vmat)
        ds = p * (dp - delta_h) * _SCALE
        t = jnp.dot(_mx(ds), kmat, preferred_element_type=F32)
        dq = t if dq is None else dq + t
        outs.append((_mx(ds), _mx(p)))
    return dq, outs, jnp.exp(sk - lse_h)


def _kv_group(h):
    return [(g, Q_PER_KV * h + g, (Q_PER_KV * h + g) // 2, (g % 2) != h) for g in range(Q_PER_KV)]


def _to_kv_lanes(pair_tile, roll):
    if not roll:
        return _mx(pair_tile)
    return _mx(pltpu.roll(pair_tile.astype(F32), HEAD_DIM, 1))


def _attn_stack(n):
    return max(1, min(Q_PER_KV, ATTN_STACK_ROWS // n))


def _kv_masked(x, h):
    return _mx(jnp.where(_hm(h * HEAD_DIM), x.astype(F32), 0.0))


def _attn_fwd_block(sink_ref, q_ref, kv, n, o_ref, lse_ref):
    lse_t = jnp.zeros((n, LANES), F32)
    stack = _attn_stack(n)
    for h in range(2):
        groups = [(_kv_masked(k, h), _kv_masked(v, h), mask) for k, v, mask in kv]
        heads = _kv_group(h)
        o_pair = None
        for first in range(0, Q_PER_KV, stack):
            part = heads[first:first + stack]
            q = jnp.concatenate([_to_kv_lanes(q_ref[:, pair * LANES:(pair + 1) * LANES], roll)
                                 for _, _, pair, roll in part], axis=0)
            sk = jnp.concatenate([jnp.full((n, 1), sink_ref[0, hq], F32) for _, hq, _, _ in part], axis=0)
            o, lse = _head_fwd(q, groups, sk)
            for idx, (g, hq, pair, roll) in enumerate(part):
                og = o[idx * n:(idx + 1) * n]
                og = pltpu.roll(og, HEAD_DIM, 1) if roll else og
                lse_t = _setcol(lse_t, hq, lse[idx * n:(idx + 1) * n])
                if g % 2 == 0:
                    o_pair = og
                else:
                    o_ref[:, pair * LANES:(pair + 1) * LANES] = o_pair + og
    lse_ref[...] = lse_t


def _attn_bwd_block(sink_ref, q_ref, do_ref, o_ref, lse_t, kv, want, n, dq_ref):
    delta_t = jnp.zeros((n, LANES), F32)
    dsk = jnp.zeros((1, LANES), F32)
    dkv = [None] * len(kv)
    stack = _attn_stack(n)
    for h in range(2):
        hm = _hm(h * HEAD_DIM)
        groups = [(_kv_masked(k, h), _kv_masked(v, h), mask) for k, v, mask in kv]
        heads = _kv_group(h)
        dq_pair = None
        for first in range(0, Q_PER_KV, stack):
            part = heads[first:first + stack]
            qs, dos, sks, lses, deltas = [], [], [], [], []
            for g, hq, pair, roll in part:
                sl = slice(pair * LANES, (pair + 1) * LANES)
                do_p = do_ref[:, sl]
                delta_h = jnp.sum(jnp.where(_hm((hq % 2) * HEAD_DIM), do_p * o_ref[:, sl], 0.0), axis=1,
                                  keepdims=True)
                delta_t = _setcol(delta_t, hq, delta_h)
                qs.append(_to_kv_lanes(q_ref[:, sl], roll))
                dos.append(_to_kv_lanes(do_p, roll))
                sks.append(jnp.full((n, 1), sink_ref[0, hq], F32))
                lses.append(_col(lse_t, hq))
                deltas.append(delta_h)
            q, do = jnp.concatenate(qs, axis=0), jnp.concatenate(dos, axis=0)
            dq, outs, p_s = _head_bwd(q, do, groups, jnp.concatenate(sks, axis=0), jnp.concatenate(lses, axis=0),
                                      jnp.concatenate(deltas, axis=0))
            for idx, (g, hq, pair, roll) in enumerate(part):
                dg = dq[idx * n:(idx + 1) * n]
                dg = pltpu.roll(dg, HEAD_DIM, 1) if roll else dg
                dsk = dsk + jnp.where(_lane() == hq,
                                      jnp.sum(-p_s[idx * n:(idx + 1) * n] * deltas[idx], axis=0, keepdims=True), 0.0)
                if g % 2 == 0:
                    dq_pair = dg
                else:
                    dq_ref[:, pair * LANES:(pair + 1) * LANES] = dq_pair + dg
            for gi, (ds, pp) in enumerate(outs):
                if want[gi]:
                    dk_h = jnp.where(hm, _dg_tn(ds, q), 0.0)
                    dv_h = jnp.where(hm, _dg_tn(pp, do), 0.0)
                    dkv[gi] = (dk_h, dv_h) if dkv[gi] is None else (dkv[gi][0] + dk_h, dkv[gi][1] + dv_h)
    return delta_t, dsk, dkv


def _partner(t):
    lane = lax.broadcasted_iota(jnp.int32, t.shape, 1)
    return jnp.where((lane & ROPE_PAIRS) == 0, pltpu.roll(t, LANES - ROPE_PAIRS, 1), pltpu.roll(t, ROPE_PAIRS, 1))


def _rope_tables(n_tokens):
    rows = n_tokens // GRID_W
    row = jnp.broadcast_to(jnp.arange(rows)[:, None], (rows, GRID_W)).reshape(-1)
    col = jnp.broadcast_to(jnp.arange(GRID_W)[None, :], (rows, GRID_W)).reshape(-1)
    freqs = ROPE_BASE ** (-jnp.arange(ROPE_PAIRS, dtype=F32) / ROPE_PAIRS)
    ang = jnp.concatenate([row[:, None].astype(F32) * freqs, col[:, None].astype(F32) * freqs], axis=-1)
    c, s = jnp.cos(ang), jnp.sin(ang)
    n = ROPE_PAIRS
    cos64 = jnp.concatenate([c[:, :n], c[:, :n], c[:, n:], c[:, n:]], axis=1)
    sin64 = jnp.concatenate([-s[:, :n], s[:, :n], -s[:, n:], s[:, n:]], axis=1)
    return jnp.tile(cos64, (1, 2)), jnp.tile(sin64, (1, 2))


def _rope_call(q_src, k_src, cos, sin, dims, inverse):
    b, l, _ = dims
    t_lat = b * l
    tr = _pick(l, (ROW_TILE, 128))
    per = l // tr
    out_dtype = F32 if inverse else MXU_DTYPE

    def body(q_ref, k_ref, c_ref, s_ref, qo_ref, ko_ref):
        c, s = c_ref[...], s_ref[...]

        def rot(t):
            if inverse:
                return t * c + _partner(t * s)
            return t * c + _partner(t) * s

        for j in range(ATTN_WIDTH // LANES):
            qo_ref[:, j * LANES:(j + 1) * LANES] = rot(q_ref[:, j * LANES:(j + 1) * LANES]).astype(out_dtype)
        ko_ref[...] = rot(k_ref[...]).astype(out_dtype)

    kcol = 0 if inverse else ATTN_WIDTH // KV_WIDTH
    return pl.pallas_call(
        body, name=f"rope_{'inv' if inverse else 'fwd'}_{t_lat}", grid=(t_lat // tr,),
        out_shape=[SDS((t_lat, ATTN_WIDTH), out_dtype), SDS((t_lat, KV_WIDTH), out_dtype)],
        in_specs=[pl.BlockSpec((tr, ATTN_WIDTH), lambda i: (i, 0)),
                  pl.BlockSpec((tr, KV_WIDTH), lambda i: (i, kcol)),
                  pl.BlockSpec((tr, LANES), lambda i: (i % per, 0)),
                  pl.BlockSpec((tr, LANES), lambda i: (i % per, 0))],
        out_specs=[pl.BlockSpec((tr, ATTN_WIDTH), lambda i: (i, 0)), pl.BlockSpec((tr, KV_WIDTH), lambda i: (i, 0))],
    )(q_src, k_src, cos, sin)


_SMEM_SPEC = pl.BlockSpec(memory_space=pltpu.SMEM)
_KCOL = ATTN_WIDTH // KV_WIDTH
_VCOL = _KCOL + 1


def _win_specs(arr_cols, nb, col):
    del arr_cols
    return [pl.BlockSpec((Q_BLOCK, KV_WIDTH), lambda b, i, d=d: (b * nb + jnp.clip(i + d, 0, nb - 1), col))
            for d in (-1, 0, 1)]


def _win_mask(i, l):
    shape = (_attn_stack(Q_BLOCK) * Q_BLOCK, 3 * Q_BLOCK)
    qpos = i * Q_BLOCK + (lax.broadcasted_iota(jnp.int32, shape, 0) & (Q_BLOCK - 1))
    kpos = (i - 1) * Q_BLOCK + lax.broadcasted_iota(jnp.int32, shape, 1)
    return jnp.logical_and(jnp.abs(qpos - kpos) <= WINDOW, jnp.logical_and(kpos >= 0, kpos < l))


def _attn_lat_fwd(qr, kr, p, sink, dims):
    b, l, lc = dims
    nb = l // Q_BLOCK
    t_lat = b * l
    cbase = t_lat // lc

    def body(sink_ref, q_ref, kp, kc, kn, vp, vc, vn, ck_ref, cv_ref, o_ref, lse_ref):
        kwin = jnp.concatenate([kp[...], kc[...], kn[...]], axis=0)
        vwin = jnp.concatenate([vp[...], vc[...], vn[...]], axis=0)
        kv = [(kwin, vwin, _win_mask(pl.program_id(1), l)), (ck_ref[...], cv_ref[...], None)]
        _attn_fwd_block(sink_ref, q_ref, kv, Q_BLOCK, o_ref, lse_ref)

    return pl.pallas_call(
        body, name=f"attn_lat_fwd_{t_lat}", grid=(b, nb),
        out_shape=[SDS((t_lat, ATTN_WIDTH), F32), SDS((t_lat, LANES), F32)],
        in_specs=[_SMEM_SPEC, pl.BlockSpec((Q_BLOCK, ATTN_WIDTH), lambda bb, i: (bb * nb + i, 0))]
        + _win_specs(None, nb, 0) + _win_specs(None, nb, _VCOL)
        + [pl.BlockSpec((lc, KV_WIDTH), lambda bb, i: (cbase + bb, _KCOL)),
           pl.BlockSpec((lc, KV_WIDTH), lambda bb, i: (cbase + bb, _VCOL))],
        out_specs=[pl.BlockSpec((Q_BLOCK, ATTN_WIDTH), lambda bb, i: (bb * nb + i, 0)),
                   pl.BlockSpec((Q_BLOCK, LANES), lambda bb, i: (bb * nb + i, 0))],
    )(sink, qr, kr, kr, kr, p, p, p, p, p)


def _attn_lat_bwd_dq(qr, kr, p, sink, o, lse, dout, dims):
    b, l, lc = dims
    nb = l // Q_BLOCK
    t_lat = b * l
    cbase = t_lat // lc

    def body(sink_ref, q_ref, kp, kc, kn, vp, vc, vn, ck_ref, cv_ref, o_ref, lse_ref, do_ref,
             dq_ref, delta_ref, dkc_ref, dvc_ref, dsk_ref):
        bb, i = pl.program_id(0), pl.program_id(1)

        @pl.when(i == 0)
        def _():
            dkc_ref[...] = jnp.zeros_like(dkc_ref)
            dvc_ref[...] = jnp.zeros_like(dvc_ref)

        @pl.when(jnp.logical_and(bb == 0, i == 0))
        def _():
            dsk_ref[...] = jnp.zeros_like(dsk_ref)

        kwin = jnp.concatenate([kp[...], kc[...], kn[...]], axis=0)
        vwin = jnp.concatenate([vp[...], vc[...], vn[...]], axis=0)
        kv = [(kwin, vwin, _win_mask(i, l)), (ck_ref[...], cv_ref[...], None)]
        delta_t, dsk, dkv = _attn_bwd_block(sink_ref, q_ref, do_ref, o_ref, lse_ref[...], kv, [False, True], Q_BLOCK,
                                            dq_ref)
        delta_ref[...] = delta_t
        dkc_ref[...] += dkv[1][0]
        dvc_ref[...] += dkv[1][1]
        dsk_ref[0:1, :] += dsk

    qspec = pl.BlockSpec((Q_BLOCK, ATTN_WIDTH), lambda bb, i: (bb * nb + i, 0))
    tspec = pl.BlockSpec((Q_BLOCK, LANES), lambda bb, i: (bb * nb + i, 0))
    cspec = pl.BlockSpec((lc, KV_WIDTH), lambda bb, i: (bb, 0))
    return pl.pallas_call(
        body, name=f"attn_lat_bwd_dq_{t_lat}", grid=(b, nb),
        out_shape=[SDS((t_lat, ATTN_WIDTH), F32), SDS((t_lat, LANES), F32), SDS((b * lc, KV_WIDTH), F32),
                   SDS((b * lc, KV_WIDTH), F32), SDS((8, LANES), F32)],
        in_specs=[_SMEM_SPEC, qspec] + _win_specs(None, nb, 0) + _win_specs(None, nb, _VCOL)
        + [pl.BlockSpec((lc, KV_WIDTH), lambda bb, i: (cbase + bb, _KCOL)),
           pl.BlockSpec((lc, KV_WIDTH), lambda bb, i: (cbase + bb, _VCOL)), qspec, tspec, qspec],
        out_specs=[qspec, tspec, cspec, cspec, pl.BlockSpec((8, LANES), lambda bb, i: (0, 0))],
    )(sink, qr, kr, kr, kr, p, p, p, p, p, o, lse, dout)


def _attn_lat_bwd_dkv(qr, kr, p, lse, delta, dout, dims):
    b, l, _ = dims
    nb = l // Q_BLOCK
    t_lat = b * l

    def body(k_ref, v_ref, *refs):
        j = pl.program_id(1)
        kj, vj = k_ref[...], v_ref[...]
        n_q = 3 * Q_PER_KV * Q_BLOCK
        sub = lax.broadcasted_iota(jnp.int32, (8, LANES), 0)
        col = lax.broadcasted_iota(jnp.int32, (Q_BLOCK, n_q), 1)
        i_of = j + col // (Q_PER_KV * Q_BLOCK) - 1
        qpos = i_of * Q_BLOCK + (col & (Q_BLOCK - 1))
        kpos = j * Q_BLOCK + lax.broadcasted_iota(jnp.int32, (Q_BLOCK, n_q), 0)
        mask = jnp.logical_and(jnp.abs(qpos - kpos) <= WINDOW, jnp.logical_and(i_of >= 0, i_of < nb))
        lse_rows = [refs[4 * n + 2][...].T[0:8, :] for n in range(3)]
        delta_rows = [refs[4 * n + 3][...].T[0:8, :] for n in range(3)]
        dk = jnp.zeros((Q_BLOCK, KV_WIDTH), F32)
        dv = jnp.zeros((Q_BLOCK, KV_WIDTH), F32)
        for h in range(2):
            hm = _hm(h * HEAD_DIM)
            kh, vh = _kv_masked(kj, h), _kv_masked(vj, h)
            qs, dos, lrow, drow = [], [], [], []
            for n in range(3):
                q_ref, do_ref = refs[4 * n], refs[4 * n + 1]
                for _, hq, pair, roll in _kv_group(h):
                    sl = slice(pair * LANES, (pair + 1) * LANES)
                    qs.append(_to_kv_lanes(q_ref[:, sl], roll))
                    dos.append(_to_kv_lanes(do_ref[:, sl], roll))
                    lrow.append(jnp.sum(jnp.where(sub == hq, lse_rows[n], 0.0), axis=0, keepdims=True))
                    drow.append(jnp.sum(jnp.where(sub == hq, delta_rows[n], 0.0), axis=0, keepdims=True))
            q, do = jnp.concatenate(qs, axis=0), jnp.concatenate(dos, axis=0)
            lse_r, delta_r = jnp.concatenate(lrow, axis=1), jnp.concatenate(drow, axis=1)
            s_t = jnp.where(mask, _dg_nt(kh, q) * _SCALE, NEG_INF)
            p_t = jnp.exp(s_t - lse_r)
            ds_t = p_t * (_dg_nt(vh, do) - delta_r) * _SCALE
            dv = dv + jnp.where(hm, jnp.dot(_mx(p_t), do, preferred_element_type=F32), 0.0)
            dk = dk + jnp.where(hm, jnp.dot(_mx(ds_t), q, preferred_element_type=F32), 0.0)
        dk_ref, dv_ref = refs[12], refs[13]
        dk_ref[...] = dk
        dv_ref[...] = dv

    def blk(width, d, col=0):
        return pl.BlockSpec((Q_BLOCK, width), lambda bb, j: (bb * nb + jnp.clip(j + d, 0, nb - 1), col))

    in_specs = [blk(KV_WIDTH, 0), blk(KV_WIDTH, 0, _VCOL)]
    args = [kr, p]
    for d in (-1, 0, 1):
        in_specs += [blk(ATTN_WIDTH, d), blk(ATTN_WIDTH, d), blk(LANES, d), blk(LANES, d)]
        args += [qr, dout, lse, delta]
    return pl.pallas_call(
        body, name=f"attn_lat_bwd_dkv_{t_lat}", grid=(b, nb),
        out_shape=[SDS((t_lat, KV_WIDTH), F32), SDS((t_lat, KV_WIDTH), F32)],
        in_specs=in_specs, out_specs=[blk(KV_WIDTH, 0), blk(KV_WIDTH, 0)],
    )(*args)


def _attn_ctx_fwd(p, sink, dims):
    b, l, lc = dims
    cbase = b * l // lc

    def body(sink_ref, q_ref, k_ref, v_ref, o_ref, lse_ref):
        _attn_fwd_block(sink_ref, q_ref, [(k_ref[...], v_ref[...], None)], lc, o_ref, lse_ref)

    return pl.pallas_call(
        body, name=f"attn_ctx_fwd_{b * lc}", grid=(b,),
        out_shape=[SDS((b * lc, ATTN_WIDTH), F32), SDS((b * lc, LANES), F32)],
        in_specs=[_SMEM_SPEC, pl.BlockSpec((lc, ATTN_WIDTH), lambda bb: (cbase + bb, 0)),
                  pl.BlockSpec((lc, KV_WIDTH), lambda bb: (cbase + bb, _KCOL)),
                  pl.BlockSpec((lc, KV_WIDTH), lambda bb: (cbase + bb, _VCOL))],
        out_specs=[pl.BlockSpec((lc, ATTN_WIDTH), lambda bb: (bb, 0)), pl.BlockSpec((lc, LANES), lambda bb: (bb, 0))],
    )(sink, p, p, p)


def _attn_ctx_bwd(p, sink, o, lse, dout, dims):
    b, l, lc = dims
    cbase = b * l // lc

    def body(sink_ref, q_ref, k_ref, v_ref, o_ref, lse_ref, do_ref, dq_ref, dk_ref, dv_ref, dsk_ref):
        bb = pl.program_id(0)

        @pl.when(bb == 0)
        def _():
            dsk_ref[...] = jnp.zeros_like(dsk_ref)

        _, dsk, dkv = _attn_bwd_block(sink_ref, q_ref, do_ref, o_ref, lse_ref[...], [(k_ref[...], v_ref[...], None)],
                                      [True], lc, dq_ref)
        dk_ref[...] = dkv[0][0]
        dv_ref[...] = dkv[0][1]
        dsk_ref[0:1, :] += dsk

    qs = pl.BlockSpec((lc, ATTN_WIDTH), lambda bb: (bb, 0))
    ks = pl.BlockSpec((lc, KV_WIDTH), lambda bb: (bb, 0))
    return pl.pallas_call(
        body, name=f"attn_ctx_bwd_{b * lc}", grid=(b,),
        out_shape=[SDS((b * lc, ATTN_WIDTH), F32), SDS((b * lc, KV_WIDTH), F32), SDS((b * lc, KV_WIDTH), F32),
                   SDS((8, LANES), F32)],
        in_specs=[_SMEM_SPEC, pl.BlockSpec((lc, ATTN_WIDTH), lambda bb: (cbase + bb, 0)),
                  pl.BlockSpec((lc, KV_WIDTH), lambda bb: (cbase + bb, _KCOL)),
                  pl.BlockSpec((lc, KV_WIDTH), lambda bb: (cbase + bb, _VCOL)),
                  qs, pl.BlockSpec((lc, LANES), lambda bb: (bb, 0)),
                  pl.BlockSpec((lc, ATTN_WIDTH), lambda bb: (cbase + bb, 0))],
        out_specs=[qs, ks, ks, pl.BlockSpec((8, LANES), lambda bb: (0, 0))],
    )(sink, p, p, p, o, lse, dout)


_CONV_COL = (ATTN_WIDTH + 2 * KV_WIDTH) // CONV_WIDTH


def _shift_prev(z, n):
    rows = lax.broadcasted_iota(jnp.int32, z.shape, 0)
    return jnp.where(rows == 0, 0.0, pltpu.roll(z, 1, 0))


def _shift_next(z, n):
    rows = lax.broadcasted_iota(jnp.int32, z.shape, 0)
    return jnp.where(rows == n - 1, 0.0, pltpu.roll(z, n - 1, 0))


def _conv_fwd(p, w, base, n_seq, ls):
    def body(cb_ref, cc_ref, cx_ref, w_ref, o_ref):
        z = cc_ref[...] * cx_ref[...]
        c3 = _shift_prev(z, ls) * w_ref[0:1, :] + z * w_ref[1:2, :] + _shift_next(z, ls) * w_ref[2:3, :]
        o_ref[...] = cb_ref[...] * c3

    return pl.pallas_call(
        body, name=f"conv_fwd_{n_seq}x{ls}", grid=(n_seq,),
        out_shape=SDS((n_seq * ls, CONV_WIDTH), F32),
        in_specs=[pl.BlockSpec((ls, CONV_WIDTH), lambda s, c=c: (base + s, _CONV_COL + c)) for c in range(3)]
        + [pl.BlockSpec(w.shape, lambda s: (0, 0))],
        out_specs=pl.BlockSpec((ls, CONV_WIDTH), lambda s: (s, 0)),
    )(p, p, p, w)


def _conv_bwd(p, w, dout, base, n_seq, ls):
    dcol = ATTN_WIDTH // CONV_WIDTH

    def body(cb_ref, cc_ref, cx_ref, w_ref, do_ref, dcb_ref, dcc_ref, dcx_ref, dw_ref):
        @pl.when(pl.program_id(0) == 0)
        def _():
            dw_ref[...] = jnp.zeros_like(dw_ref)

        cc, cx = cc_ref[...], cx_ref[...]
        z = cc * cx
        zp, zn = _shift_prev(z, ls), _shift_next(z, ls)
        c3 = zp * w_ref[0:1, :] + z * w_ref[1:2, :] + zn * w_ref[2:3, :]
        do = do_ref[...]
        dcb_ref[...] = do * c3
        e = do * cb_ref[...]
        dz = _shift_next(e, ls) * w_ref[0:1, :] + e * w_ref[1:2, :] + _shift_prev(e, ls) * w_ref[2:3, :]
        dcc_ref[...] = dz * cx
        dcx_ref[...] = dz * cc
        dw_ref[0:1, :] += jnp.sum(e * zp, axis=0, keepdims=True)
        dw_ref[1:2, :] += jnp.sum(e * z, axis=0, keepdims=True)
        dw_ref[2:3, :] += jnp.sum(e * zn, axis=0, keepdims=True)

    ospec = pl.BlockSpec((ls, CONV_WIDTH), lambda s: (s, 0))
    return pl.pallas_call(
        body, name=f"conv_bwd_{n_seq}x{ls}", grid=(n_seq,),
        out_shape=[SDS((n_seq * ls, CONV_WIDTH), F32)] * 3 + [SDS((8, CONV_WIDTH), F32)],
        in_specs=[pl.BlockSpec((ls, CONV_WIDTH), lambda s, c=c: (base + s, _CONV_COL + c)) for c in range(3)]
        + [pl.BlockSpec(w.shape, lambda s: (0, 0)), pl.BlockSpec((ls, CONV_WIDTH), lambda s: (base + s, dcol))],
        out_specs=[ospec, ospec, ospec, pl.BlockSpec((8, CONV_WIDTH), lambda s: (0, 0))],
    )(p, p, p, w, dout)


def _make_att_conv(dims, cos, sin):
    b, l, lc = dims
    t_lat = b * l

    def forward(p, conv_w, sink):
        qr, kr = _rope_call(p, p, cos, sin, dims, inverse=False)
        o_lat, lse_lat = _attn_lat_fwd(qr, kr, p, sink, dims)
        o_ctx, lse_ctx = _attn_ctx_fwd(p, sink, dims)
        conv_lat = _conv_fwd(p, conv_w, 0, b, l)
        conv_ctx = _conv_fwd(p, conv_w, t_lat // lc, b, lc)
        out = jnp.concatenate([jnp.concatenate([o_lat, conv_lat], axis=1),
                               jnp.concatenate([o_ctx, conv_ctx], axis=1)], axis=0)
        return out, (p, conv_w, sink, qr, kr, o_lat, lse_lat, o_ctx, lse_ctx)

    @jax.custom_vjp
    def op(p, conv_w, sink):
        return forward(p, conv_w, sink)[0]

    def op_bwd(res, dout):
        p, conv_w, sink, qr, kr, o_lat, lse_lat, o_ctx, lse_ctx = res
        dqr, delta, dkc1, dvc1, dsk1 = _attn_lat_bwd_dq(qr, kr, p, sink, o_lat, lse_lat, dout, dims)
        dkr, dv = _attn_lat_bwd_dkv(qr, kr, p, lse_lat, delta, dout, dims)
        dq, dk = _rope_call(dqr, dkr, cos, sin, dims, inverse=True)
        dqc, dkc2, dvc2, dsk2 = _attn_ctx_bwd(p, sink, o_ctx, lse_ctx, dout, dims)
        dcb_l, dcc_l, dcx_l, dw_l = _conv_bwd(p, conv_w, dout, 0, b, l)
        dcb_c, dcc_c, dcx_c, dw_c = _conv_bwd(p, conv_w, dout, t_lat // lc, b, lc)
        zeros_u = jnp.zeros((p.shape[0], SSM_WIDTH), F32)
        lat = jnp.concatenate([dq, dk, dv, dcb_l, dcc_l, dcx_l], axis=1)
        ctx = jnp.concatenate([dqc, dkc1 + dkc2, dvc1 + dvc2, dcb_c, dcc_c, dcx_c], axis=1)
        dp = jnp.concatenate([jnp.concatenate([lat, ctx], axis=0), zeros_u], axis=1)
        dsink = (dsk1 + dsk2)[0:1, :N_Q_HEADS]
        return dp, (dw_l + dw_c)[:3], dsink

    op.defvjp(forward, op_bwd)
    return op


def _cmul(ar, ai, br, bi):
    return ar * br - ai * bi, ar * bi + ai * br


def _scan_chunk(buf, lr_ref, li_ref, carry, dec, tc):
    sub = lax.broadcasted_iota(jnp.int32, (8, LANES), 0)
    groups = range(tc // 8 - 1, -1, -1) if dec else range(tc // 8)
    for s in range(SSM_N // LANES):
        re, im = slice(s * LANES, (s + 1) * LANES), slice(SSM_N + s * LANES, SSM_N + (s + 1) * LANES)
        pw = [(lr_ref[:, re], li_ref[:, re])]
        for _ in range(7):
            pw.append(_cmul(*pw[-1], *pw[0]))
        p8r, p8i = jnp.zeros((8, LANES), F32), jnp.zeros((8, LANES), F32)
        for k in range(8):
            row = 7 - k if dec else k
            p8r = jnp.where(sub == row, pw[k][0], p8r)
            p8i = jnp.where(sub == row, pw[k][1], p8i)
        steps = []
        for sft in (1, 2, 4):
            keep = sub < 8 - sft if dec else sub >= sft
            steps.append((8 - sft if dec else sft, jnp.where(keep, pw[sft - 1][0], 0.0),
                          jnp.where(keep, pw[sft - 1][1], 0.0)))
        cr, ci = carry[0:1, re], carry[0:1, im]
        for r in groups:
            rows = slice(8 * r, 8 * r + 8)
            xr, xi = buf[rows, re], buf[rows, im]
            for amount, mr, mi in steps:
                sr, si = pltpu.roll(xr, amount, 0), pltpu.roll(xi, amount, 0)
                xr, xi = xr + mr * sr - mi * si, xi + mr * si + mi * sr
            xr, xi = xr + p8r * cr - p8i * ci, xi + p8r * ci + p8i * cr
            buf[rows, re] = xr
            buf[rows, im] = xi
            last = 8 * r if dec else 8 * r + 7
            cr, ci = buf[last:last + 1, re], buf[last:last + 1, im]
        carry[0:1, re] = cr
        carry[0:1, im] = ci


def _ssm_chunks(dims, ctx_first, dec):
    b, l, lc = dims
    tc = _pick(lc, (256, 128, 64, 32, 16, 8))
    nc_c, nc_l = lc // tc, l // tc
    n_chunks = nc_c + nc_l
    ctx_base = b * l // tc
    g8 = tc // 8

    def chunk(bb, j):
        if ctx_first:
            is_ctx = j < nc_c
            jj = jnp.where(is_ctx, j, j - nc_c)
        else:
            is_ctx = j >= nc_l
            jj = jnp.where(is_ctx, j - nc_l, j)
        ic = nc_c - 1 - jj if dec else jj
        il = nc_l - 1 - jj if dec else jj
        return jnp.where(is_ctx, ctx_base + bb * nc_c + ic, bb * nc_l + il)

    def edge(bb, j):
        jn = jnp.minimum(j + 1, n_chunks - 1)
        return chunk(bb, jn) * g8 + (g8 - 1 if dec else 0)

    return tc, n_chunks, chunk, edge


def _ssm_fwd_call(u, bmat, cmat, lam_r, lam_i, dims, dec):
    n2 = 2 * SSM_N
    tc, n_chunks, chunk, _ = _ssm_chunks(dims, True, dec)

    def body(u_ref, b_ref, c_ref, lr_ref, li_ref, h_ref, y_ref, carry):
        @pl.when(pl.program_id(1) == 0)
        def _():
            carry[...] = jnp.zeros_like(carry)

        h_ref[...] = jnp.dot(_mx(u_ref[...]), b_ref[...], preferred_element_type=F32)
        _scan_chunk(h_ref, lr_ref, li_ref, carry, dec, tc)
        y_ref[...] = jnp.dot(_mx(h_ref[...]), c_ref[...], preferred_element_type=F32)

    def full(shape):
        return pl.BlockSpec(shape, lambda bb, j: (0, 0))

    return pl.pallas_call(
        body, name=f"ssm_fwd_{'d' if dec else 'u'}_{u.shape[0]}", grid=(dims[0], n_chunks),
        out_shape=[SDS((u.shape[0], n2), F32), SDS(u.shape, F32)],
        in_specs=[pl.BlockSpec((tc, SSM_WIDTH), lambda bb, j: (chunk(bb, j), 0)), full((SSM_WIDTH, n2)),
                  full((n2, SSM_WIDTH)), full((1, SSM_N)), full((1, SSM_N))],
        out_specs=[pl.BlockSpec((tc, n2), lambda bb, j: (chunk(bb, j), 0)),
                   pl.BlockSpec((tc, SSM_WIDTH), lambda bb, j: (chunk(bb, j), 0))],
        scratch_shapes=[pltpu.VMEM((8, n2), F32)],
    )(u, bmat, cmat, lam_r, lam_i)


def _ssm_bwd_call(gy, h, u, bmat, cmat, lam_r, lam_i, dims, dec):
    n2 = 2 * SSM_N
    tc, n_chunks, chunk, edge = _ssm_chunks(dims, False, dec)

    def body(gy_ref, h_ref, hb_ref, u_ref, b_ref, c_ref, lr_ref, li_ref, du_ref, db_ref, dct_ref, dl_ref, gd, carry):
        bb, j = pl.program_id(0), pl.program_id(1)

        @pl.when(j == 0)
        def _():
            carry[...] = jnp.zeros_like(carry)

        @pl.when(jnp.logical_and(bb == 0, j == 0))
        def _():
            db_ref[...] = jnp.zeros_like(db_ref)
            dct_ref[...] = jnp.zeros_like(dct_ref)
            dl_ref[...] = jnp.zeros_like(dl_ref)

        gyv = _mx(gy_ref[...])
        gd[...] = _dg_nt(gyv, c_ref[...])
        _scan_chunk(gd, lr_ref, li_ref, carry, dec, tc)
        gdv = _mx(gd[...])
        du_ref[...] = _dg_nt(gdv, b_ref[...])
        db_ref[...] += _dg_tn(_mx(u_ref[...]), gdv)
        dct_ref[...] += _dg_tn(gyv, _mx(h_ref[...]))
        rows = lax.broadcasted_iota(jnp.int32, (tc, LANES), 0)
        has_next = j + 1 < n_chunks
        er = 7 if dec else 0
        for s in range(SSM_N // LANES):
            re, im = slice(s * LANES, (s + 1) * LANES), slice(SSM_N + s * LANES, SSM_N + (s + 1) * LANES)
            dr, di, hr, hi = gd[:, re], gd[:, im], h_ref[:, re], h_ref[:, im]
            br = jnp.where(has_next, hb_ref[er:er + 1, re], 0.0)
            bi = jnp.where(has_next, hb_ref[er:er + 1, im], 0.0)
            if dec:
                nr = jnp.where(rows == 0, br, pltpu.roll(hr, 1, 0))
                ni = jnp.where(rows == 0, bi, pltpu.roll(hi, 1, 0))
            else:
                nr = jnp.where(rows == tc - 1, br, pltpu.roll(hr, tc - 1, 0))
                ni = jnp.where(rows == tc - 1, bi, pltpu.roll(hi, tc - 1, 0))
            dl_ref[0:1, re] += jnp.sum(dr * nr + di * ni, axis=0, keepdims=True)
            dl_ref[0:1, im] += jnp.sum(di * nr - dr * ni, axis=0, keepdims=True)

    def full(shape):
        return pl.BlockSpec(shape, lambda bb, j: (0, 0))

    def at_chunk(width):
        return pl.BlockSpec((tc, width), lambda bb, j: (chunk(bb, j), 0))

    return pl.pallas_call(
        body, name=f"ssm_bwd_{'d' if dec else 'u'}_{u.shape[0]}", grid=(dims[0], n_chunks),
        out_shape=[SDS(u.shape, F32), SDS((SSM_WIDTH, n2), F32), SDS((SSM_WIDTH, n2), F32), SDS((8, n2), F32)],
        in_specs=[at_chunk(SSM_WIDTH), at_chunk(n2), pl.BlockSpec((8, n2), lambda bb, j: (edge(bb, j), 0)),
                  at_chunk(SSM_WIDTH), full((SSM_WIDTH, n2)), full((n2, SSM_WIDTH)), full((1, SSM_N)),
                  full((1, SSM_N))],
        out_specs=[at_chunk(SSM_WIDTH), full((SSM_WIDTH, n2)), full((SSM_WIDTH, n2)), full((8, n2))],
        scratch_shapes=[pltpu.VMEM((tc, n2), F32), pltpu.VMEM((8, n2), F32)],
    )(gy, h, h, u, bmat, cmat, lam_r, lam_i)


def _make_ssm(dims, direction):
    @jax.custom_vjp
    def ssm(u, bmat, cmat, lam_r, lam_i):
        return _ssm_fwd_call(u, _mx(bmat), _mx(cmat), lam_r, lam_i, dims, direction == 1)[1]

    def ssm_fwd(u, bmat, cmat, lam_r, lam_i):
        h, y = _ssm_fwd_call(u, _mx(bmat), _mx(cmat), lam_r, lam_i, dims, direction == 1)
        return y, (u, bmat, cmat, lam_r, lam_i, h)

    def ssm_bwd(res, gy):
        u, bmat, cmat, lam_r, lam_i, h = res
        du, db, dct, dl = _ssm_bwd_call(gy, h, u, _mx(bmat), _mx(cmat), lam_r, -lam_i, dims, direction == 0)
        return du, db, dct.T, dl[0:1, :SSM_N], dl[0:1, SSM_N:]

    ssm.defvjp(ssm_fwd, ssm_bwd)
    return ssm


def _block_diag(m):
    g, a, bdim = m.shape[-3:]
    eye = jnp.eye(g, dtype=m.dtype)
    full = m[..., :, :, None, :] * eye[:, None, :, None]
    return full.reshape(m.shape[:-3] + (g * a, g * bdim))


def _ssm_mats(lam_re, lam_im, log_dt, b_re, b_im, c_re, c_im):
    lam = lax.complex(lam_re, lam_im)
    dt = jnp.exp(log_dt)[..., None]
    lam_bar = jnp.exp(lam * dt)
    b_bar = ((lam_bar - 1) / lam)[..., None] * lax.complex(b_re, b_im)
    bt = jnp.swapaxes(b_bar, -1, -2)
    bmat = jnp.concatenate([_block_diag(jnp.real(bt)), _block_diag(jnp.imag(bt))], axis=-1)
    cmat = jnp.concatenate([_block_diag(jnp.swapaxes(c_re, -1, -2)), _block_diag(-jnp.swapaxes(c_im, -1, -2))],
                           axis=-2)
    flat = lam_bar.shape[:2] + (1, SSM_N)
    return bmat, cmat, jnp.real(lam_bar).reshape(flat), jnp.imag(lam_bar).reshape(flat)


def _loss_call(y, target):
    t, d = y.shape
    tr = _pick(t, (ROW_TILE, 128, 64, 32, 16, 8))

    def body(y_ref, t_ref, acc_ref, dy_ref):
        @pl.when(pl.program_id(0) == 0)
        def _():
            acc_ref[...] = jnp.zeros_like(acc_ref)

        diff = y_ref[...] - t_ref[...]
        dy_ref[...] = diff * (1.0 / d)
        acc_ref[0:1, :] += jnp.sum(diff * diff, axis=0, keepdims=True)

    spec = pl.BlockSpec((tr, d), lambda i: (i, 0))
    return pl.pallas_call(
        body, name=f"loss_{t}", grid=(t // tr,),
        out_shape=[SDS((8, d), F32), SDS((t, d), F32)],
        in_specs=[spec, spec], out_specs=[pl.BlockSpec((8, d), lambda i: (0, 0)), spec],
    )(y, target)


def _adamw_call(w, g, m, v):
    r, c = w.shape
    tr = _pick(r, [t for t in (1024, 512, 256, 128, 64, 32, 16, 8) if t * c <= 256 * 1024])

    def body(w_ref, g_ref, m_ref, v_ref, d_ref, mo_ref, vo_ref):
        gv = g_ref[...]
        mn = ADAM_B1 * m_ref[...] + (1.0 - ADAM_B1) * gv
        vn = ADAM_B2 * v_ref[...] + (1.0 - ADAM_B2) * (gv * gv)
        m_hat = mn / (1.0 - ADAM_B1 ** ADAM_STEP)
        v_hat = vn / (1.0 - ADAM_B2 ** ADAM_STEP)
        d_ref[...] = -ADAM_LR * (m_hat / (jnp.sqrt(v_hat) + ADAM_EPS) + ADAM_WD * w_ref[...])
        mo_ref[...] = mn
        vo_ref[...] = vn

    spec = pl.BlockSpec((tr, c), lambda i: (i, 0))
    return pl.pallas_call(
        body, name=f"adamw_{r}x{c}", grid=(r // tr,),
        out_shape=[SDS((r, c), F32)] * 3, in_specs=[spec] * 4, out_specs=[spec] * 3,
    )(w, g, m, v)


def _make_modulation(b, me):
    n_ex = N_DEV * b

    def a_matrix(c_ctx, c_all):
        rows = [jax.nn.silu(c_all), jax.nn.silu(c_ctx)[None, :],
                jnp.zeros((MOD_ROWS - n_ex - 1, c_all.shape[1]), F32)]
        return jnp.concatenate(rows, axis=0)

    def forward(c_ctx, b_ada, w_loc, c_all):
        depth, _, width = w_loc.shape
        a_mat = a_matrix(c_ctx, c_all)
        cols = jnp.stack([_mm_call('nn', a_mat, w_loc[layer]) for layer in range(depth)])
        gathered = _all_gather(cols.reshape(depth * MOD_ROWS, width))
        full = jnp.moveaxis(gathered.reshape(N_DEV, depth, MOD_ROWS, width), 0, 2).reshape(depth, MOD_ROWS, -1)
        mine = lax.dynamic_slice_in_dim(full, me * b, b, axis=1)
        return jnp.concatenate([mine, full[:, n_ex:n_ex + 1]], axis=1) + b_ada[:, None, :]

    @jax.custom_vjp
    def modulation(c_ctx, b_ada, w_loc, c_all):
        return forward(c_ctx, b_ada, w_loc, c_all)

    def modulation_fwd(c_ctx, b_ada, w_loc, c_all):
        return forward(c_ctx, b_ada, w_loc, c_all), (c_ctx, w_loc, c_all)

    def modulation_bwd(res, g):
        c_ctx, w_loc, c_all = res
        depth, _, width = w_loc.shape
        rows = depth * (b + 1)
        flat = jnp.pad(g.reshape(rows, -1), ((0, -rows % 8), (0, 0)))
        everyone = _all_gather(flat)[:, :rows].reshape(N_DEV, depth, b + 1, -1)
        examples = jnp.moveaxis(everyone[:, :, :b], 0, 1).reshape(depth, n_ex, -1)
        ctx_total = jnp.sum(everyone[:, :, b], axis=0)
        g_mat = jnp.concatenate([examples, ctx_total[:, None, :],
                                 jnp.zeros((depth, MOD_ROWS - n_ex - 1, g.shape[-1]), F32)], axis=1)
        g_cols = lax.dynamic_slice_in_dim(g_mat, me * width, width, axis=2)
        a_mat, a_vjp = jax.vjp(lambda cc: a_matrix(cc, c_all), c_ctx)
        dw = jnp.stack([_mm_call('tn', a_mat, g_cols[layer]) for layer in range(depth)])
        da = sum(_mm_call('nt', g_cols[layer], w_loc[layer]) for layer in range(depth))
        dc_ctx, = a_vjp(da)
        return dc_ctx, jnp.sum(g, axis=1), dw, jnp.zeros_like(c_all)

    modulation.defvjp(modulation_fwd, modulation_bwd)
    return modulation


def _forward(diff, gathered, c_all, me, dims, depth):
    b, l, lc = dims
    t_lat = b * l
    d_model = diff['x'].shape[-1]
    cos, sin = _rope_tables(l)
    glu = _make_rowwise_op("glu", _glu_f, SSM_WIDTH, 3, 0, l)
    att_conv = _make_att_conv(dims, cos, sin)
    ssms = [_make_ssm(dims, 0), _make_ssm(dims, 1)]
    bmats, cmats, lam_r, lam_i = _ssm_mats(*[diff['ssm_' + n] for n in
                                             ('lam_re', 'lam_im', 'log_dt', 'b_re', 'b_im', 'c_re', 'c_im')])

    def middle(p, prm):
        ac = att_conv(p, prm['conv_w'], prm['sink'])
        u = p[:, IN_WIDTH - SSM_WIDTH:]
        ys = [ssms[d](u, prm['bmat'][d], prm['cmat'][d], prm['lam_r'][d], prm['lam_i'][d]) for d in range(2)]
        return ac, glu(ys[0], ys[1], u, prm['ssm_d'], prm['w_glu'], prm['b_glu'])

    nm = _rowwise("norm_mod", _norm_mod_f, (d_model,), l)
    rg = _rowwise("resid_gate", _resid_gate_f, (d_model,), l)
    mixer_block, mlp_block = _make_block_ops(nm, rg, middle, t_lat + b * lc)
    mixer_block_last, _ = _make_block_ops(nm, rg, middle, t_lat)

    mods = _make_modulation(b, me)(diff['c_ctx'], diff['b_ada'], diff['w_ada'], c_all)
    h = jnp.concatenate([diff['x'].reshape(t_lat, d_model), diff['ctx'].reshape(b * lc, d_model)], axis=0)

    for layer in range(depth):
        last = layer == depth - 1
        sh1, sc1, g1, sh2, sc2, g2 = [m.reshape(b + 1, 1, d_model) for m in jnp.split(mods[layer], N_MOD, axis=-1)]
        if last:
            g1, sh2, sc2, g2 = g1[:b], sh2[:b], sc2[:b], g2[:b]
        ng = diff['norm_g'][layer]
        prm = dict(conv_w=diff['conv_w'][layer], sink=diff['attn_sink'][layer][None, :], bmat=bmats[layer],
                   cmat=cmats[layer], lam_r=lam_r[layer], lam_i=lam_i[layer], ssm_d=diff['ssm_d'][layer][None, :],
                   w_glu=diff['w_glu'][layer], b_glu=diff['b_glu'][layer][None, :])
        h = (mixer_block_last if last else mixer_block)(
            h, sh1, sc1, ng[0:1], gathered['w_in'][layer], diff['w_in'][layer], prm,
            gathered['w_out'][layer], diff['w_out'][layer], g1, ng[1:2])
        h = mlp_block(h, sh2, sc2, ng[2:3], gathered['w_mlp_in'][layer], diff['w_mlp_in'][layer],
                      gathered['w_mlp_out'][layer], diff['w_mlp_out'][layer], g2, ng[3:4])
    return h


BIG_TRANSPOSED = {'w_in': True, 'w_out': False, 'w_mlp_in': True, 'w_mlp_out': False}


def _big_rows(weights):
    blocks = []
    depth = weights['w_ada'].shape[0]
    for n in BIG:
        for layer in range(depth):
            w = weights[n][layer]
            blocks.append((layer, n, w.shape[1] if BIG_TRANSPOSED[n] else w.shape[0]))
    return blocks


def _stack_big(arrs, dtype):
    parts = []
    for n in BIG:
        w = jnp.swapaxes(arrs[n], 1, 2) if BIG_TRANSPOSED[n] else arrs[n]
        parts.append(w.reshape(-1, w.shape[-1]).astype(dtype))
    rows = sum(p.shape[0] for p in parts)
    parts.append(jnp.zeros((-rows % BIG_ROW_TILE, parts[0].shape[1]), dtype))
    return jnp.concatenate(parts, axis=0)


def _adamw_any(w, g, m, v):
    flat = [a.reshape(-1, w.shape[-1]) for a in (w, g, m, v)]
    return [o.reshape(w.shape) for o in _adamw_call(*flat)]


def kernel(x, c, ctx, c_ctx, w_ada, b_ada, norm_g, w_in, conv_w, attn_sink, ssm_lam_re, ssm_lam_im, ssm_log_dt, ssm_b_re, ssm_b_im, ssm_c_re, ssm_c_im, ssm_d, w_glu, b_glu, w_out, w_mlp_in, w_mlp_out, loss_target, m_c_ctx, m_w_ada, m_b_ada, m_norm_g, m_w_in, m_conv_w, m_attn_sink, m_ssm_lam_re, m_ssm_lam_im, m_ssm_log_dt, m_ssm_b_re, m_ssm_b_im, m_ssm_c_re, m_ssm_c_im, m_ssm_d, m_w_glu, m_b_glu, m_w_out, m_w_mlp_in, m_w_mlp_out, v_c_ctx, v_w_ada, v_b_ada, v_norm_g, v_w_in, v_conv_w, v_attn_sink, v_ssm_lam_re, v_ssm_lam_im, v_ssm_log_dt, v_ssm_b_re, v_ssm_b_im, v_ssm_c_re, v_ssm_c_im, v_ssm_d, v_w_glu, v_b_glu, v_w_out, v_w_mlp_in, v_w_mlp_out):
    given = dict(locals())
    weights = {n: given[n] for n in WEIGHTS}
    moms = {n: given['m_' + n] for n in WEIGHTS}
    vars_ = {n: given['v_' + n] for n in WEIGHTS}
    b, l, d_model = x.shape
    lc = ctx.shape[1]
    dims = (b, l, lc)
    depth = w_ada.shape[0]

    core = lax.axis_index('c')
    me = 4 * lax.axis_index('x') + 2 * lax.axis_index('y') + core

    blocks = _big_rows(weights)
    big_all = _all_gather(_stack_big(weights, MXU_DTYPE))
    gathered = {n: [None] * depth for n in BIG}
    off = 0
    for layer, n, rows in blocks:
        gathered[n][layer] = big_all[:, off:off + rows].reshape(N_DEV * rows, -1)
        off += rows
    small_shapes = [weights[n].shape for n in SMALL]
    small_all = _all_gather(_pack([weights[n] for n in SMALL], F32, PACK_ROWS))
    small_full = {n: _from_pieces(pc, SHARD_AXIS[n])
                  for n, pc in zip(SMALL, _unpack(small_all, small_shapes, (N_DEV,)))}

    c_all = _all_gather(jnp.pad(c, ((0, -b % 8), (0, 0))))[:, :b].reshape(N_DEV * b, d_model)
    diff = {'x': x, 'ctx': ctx, 'w_ada': w_ada}
    for n in REPLICATED:
        diff[n] = weights[n]
    for n in SMALL:
        diff[n] = small_full[n]
    for n in BIG:
        diff[n] = [jnp.zeros(gathered[n][layer].shape, F32) for layer in range(depth)]
    y, vjp = jax.vjp(lambda dd: _forward(dd, gathered, c_all, me, dims, depth), diff)
    sq, dy = _loss_call(y, loss_target.reshape(b * l, d_model))
    loss = lax.psum(0.5 * jnp.sum(sq) / d_model, ('x', 'y', 'c'))
    grads = vjp(dy)[0]
    grad_x = grads['x']

    pieces = [grads[n][layer].reshape(N_DEV, rows, -1) for layer, n, rows in blocks]
    pieces.append(jnp.zeros((N_DEV, -sum(r for _, _, r in blocks) % BIG_ROW_TILE, d_model), F32))
    g8 = jnp.concatenate(pieces, axis=1)
    chip_sums = _rs_add(g8, _rs_sibling(g8), core.reshape(1).astype(jnp.int32))
    g_big = _sum_slots(_rs_chips(chip_sums))
    big_grad, off = {}, 0
    for n in BIG:
        rows = sum(r for _, name, r in blocks if name == n)
        blk = g_big[off:off + rows].reshape(depth, rows // depth, -1)
        big_grad[n] = jnp.swapaxes(blk, 1, 2) if BIG_TRANSPOSED[n] else blk
        off += rows
    rest = SMALL + REPLICATED
    g_rest = _sum_slots(_all_gather(_pack([grads[n] for n in rest], F32, PACK_ROWS)))
    rest_grad = dict(zip(rest, _unpack(g_rest, [grads[n].shape for n in rest])))
    for n in SMALL:
        width = weights[n].shape[SHARD_AXIS[n]]
        rest_grad[n] = lax.dynamic_slice_in_dim(rest_grad[n], me * width, width, axis=SHARD_AXIS[n])

    out = {}
    all_grads = {**big_grad, **rest_grad, **{n: grads[n] for n in LOCAL}}
    for n in WEIGHTS:
        out[('grad', n)] = all_grads[n]
        for kind, arr in zip(('delta', 'new_m', 'new_v'), _adamw_any(weights[n], all_grads[n], moms[n], vars_[n])):
            out[(kind, n)] = arr
    return (loss, grad_x, *[out[(kind, n)] for kind in ('grad', 'delta', 'new_m', 'new_v') for n in WEIGHTS])
```
